```python
import math
import jax
import jax.numpy as jnp
from jax import lax
import numpy as np

D_MODEL = 1024
BATCH = 8
SEQ = 2048
DEPTH = 4
DEC_BATCH = 32
DEC_SEQ = 32
PAST_LEN = 4096

CHUNK = 64
N_MIXERS = 3
N_CONV_LAYERS = (DEPTH + 2) // 3
N_RWKV_LAYERS = (DEPTH + 1) // 3
N_ATTN_LAYERS = DEPTH // 3

CONV_WIDTH = 31

RWKV_HEAD = 64
RWKV_HEADS = D_MODEL // RWKV_HEAD
DECAY_LORA = 64
ICLR_LORA = 64
GATE_LORA = 128
RWKV_LNX_EPS = 64e-5

ATTN_HEADS = 16
KV_HEADS = 2
HEAD_DIM = 64
GROUP = ATTN_HEADS // KV_HEADS
WINDOW = 128
WINDOW_CHUNKS = WINDOW // CHUNK
BAND = (WINDOW_CHUNKS + 1) * CHUNK
N_BUCKETS = 32
MAX_DISTANCE = 128

N_EXPERTS = 32
TOP_K = 4
D_EXPERT = D_MODEL
SWIGLU_ALPHA = 1.702
SWIGLU_LIMIT = 7.0
MOE_BLOCK = 256

PLE_DIM = 256
DEEPNORM_ALPHA = (2 * DEPTH) ** 0.25
DEEPNORM_BETA = (8 * DEPTH) ** -0.25
LN_EPS = 1e-5

kernel_name = 'hybrid_streaming_encoder_step'


def layer_norm(x, g, b, eps=LN_EPS):
    xf = x.astype(jnp.float32)
    mu = jnp.mean(xf, axis=-1, keepdims=True)
    var = jnp.mean(jnp.square(xf - mu), axis=-1, keepdims=True)
    y = (xf - mu) * lax.rsqrt(var + eps) * g.astype(jnp.float32) + b.astype(jnp.float32)
    return y.astype(x.dtype)


def conv_module(x, conv_state, w_in, b_in, w_dw, b_dw, ln_g, ln_b, w_out, b_out):
    a, gate = jnp.split(x @ w_in + b_in, 2, axis=-1)
    u = a * jax.nn.sigmoid(gate)
    u_ext = jnp.concatenate([conv_state.astype(u.dtype), u], axis=1)
    y = lax.conv_general_dilated(u_ext, w_dw[:, None, :], window_strides=(1,), padding='VALID',
                                 dimension_numbers=('NWC', 'WIO', 'NWC'),
                                 feature_group_count=D_MODEL) + b_dw
    y = jax.nn.silu(layer_norm(y, ln_g, ln_b))
    return y @ w_out + b_out, u_ext[:, -(CONV_WIDTH - 1):]


def _wkv7_step(S, inp):
    r, w, k, v, a, b = inp
    sa = jnp.einsum('bhvk,bhk->bhv', S, a)
    S = S * w[:, :, None, :] + sa[..., None] * b[:, :, None, :] + v[..., None] * k[:, :, None, :]
    return S, jnp.einsum('bhvk,bhk->bhv', S, r)


def rwkv7_time_mix(x, shift, wkv, mu, w_rkv, w0, w1, w2, a0, a1, a2, g1, g2,
                   k_k, k_a, r_k, lnx_g, lnx_b, w_o):
    B, T, D = x.shape
    x_prev = jnp.concatenate([shift[:, None, :].astype(x.dtype), x[:, :-1]], axis=1)
    xx = x_prev - x
    mix = lambda n: x + xx * mu[n]
    xr, xw, xk, xv, xa, xg = mix(0), mix(1), mix(2), mix(3), mix(4), mix(5)
    r = xr @ w_rkv[0]
    k = xk @ w_rkv[1]
    v = xv @ w_rkv[2]
    log_w = -jax.nn.softplus(-(w0 + jnp.tanh(xw @ w1) @ w2)) - 0.5
    a = jax.nn.sigmoid(a0 + (xa @ a1) @ a2)
    g = jax.nn.sigmoid(xg @ g1) @ g2
    heads = lambda t: t.reshape(B, T, RWKV_HEADS, RWKV_HEAD).astype(jnp.float32)
    kk = heads(k * k_k)
    kk = kk * lax.rsqrt(jnp.maximum(jnp.sum(kk * kk, axis=-1, keepdims=True), 1e-24))
    k = k * (1.0 + (a - 1.0) * k_a)
    rh, kh, vh = heads(r), heads(k), heads(v)
    decay = jnp.exp(-jnp.exp(heads(log_w)))
    seq = tuple(jnp.moveaxis(t, 1, 0) for t in (rh, decay, kh, vh, -kk, kk * heads(a)))
    S_final, o = lax.scan(_wkv7_step, wkv.astype(jnp.float32), seq)
    o = jnp.moveaxis(o, 0, 1)
    mo = jnp.mean(o, axis=-1, keepdims=True)
    vo = jnp.mean(jnp.square(o - mo), axis=-1, keepdims=True)
    o = ((o - mo) * lax.rsqrt(vo + RWKV_LNX_EPS)).reshape(B, T, D)
    o = o * lnx_g.astype(jnp.float32) + lnx_b.astype(jnp.float32)
    bonus = jnp.sum(rh * kh * r_k.astype(jnp.float32), axis=-1, keepdims=True) * vh
    o = (o + bonus.reshape(B, T, D)).astype(x.dtype)
    return (o * g) @ w_o, x[:, -1], S_final.astype(x.dtype)


def t5_bucket(rel):
    half = N_BUCKETS // 2
    max_exact = half // 2
    ret = jnp.where(rel > 0, half, 0)
    n = jnp.abs(rel)
    nf = jnp.maximum(n, 1).astype(jnp.float32)
    large = max_exact + (jnp.log(nf / max_exact) / math.log(MAX_DISTANCE / max_exact)
                         * (half - max_exact)).astype(jnp.int32)
    large = jnp.minimum(large, half - 1)
    return ret + jnp.where(n < max_exact, n, large)


def t5_bias(rel_bias, n_q, n_k):
    rel = jnp.arange(n_k)[None, :] - WINDOW - jnp.arange(n_q)[:, None]
    return jnp.transpose(rel_bias[t5_bucket(rel)], (2, 0, 1))


def qkv_project(x, w_qkv, b_qkv):
    B, T, _ = x.shape
    h = x @ w_qkv + b_qkv
    q, k, v = jnp.split(h, [ATTN_HEADS * HEAD_DIM, (ATTN_HEADS + KV_HEADS) * HEAD_DIM], axis=-1)
    return (q.reshape(B, T, KV_HEADS, GROUP, HEAD_DIM),
            k.reshape(B, T, KV_HEADS, HEAD_DIM), v.reshape(B, T, KV_HEADS, HEAD_DIM))


def sink_attention(q, k, v, bias, mask, sinks):
    Q, K = q.shape[2], k.shape[2]
    logits = jnp.einsum('bnqhgd,bnkhd->bhgnqk', q.astype(jnp.float32), k.astype(jnp.float32)) * HEAD_DIM ** -0.5
    logits = logits + bias.reshape(KV_HEADS, GROUP, 1, Q, K).astype(jnp.float32)
    if mask is not None:
        logits = jnp.where(mask, logits, -1e30)
    sink = sinks.reshape(KV_HEADS, GROUP, 1, 1, 1).astype(jnp.float32)
    m = jnp.maximum(jnp.max(logits, axis=-1, keepdims=True), sink)
    p = jnp.exp(logits - m)
    p = p / (jnp.sum(p, axis=-1, keepdims=True) + jnp.exp(sink - m))
    return jnp.einsum('bhgnqk,bnkhd->bnqhgd', p.astype(v.dtype), v)


def swa_prompt(x, w_qkv, b_qkv, sinks, rel_bias, w_o, b_o):
    B, T, _ = x.shape
    n_c = T // CHUNK
    q, k, v = qkv_project(x, w_qkv, b_qkv)
    qb = q.reshape(B, n_c, CHUNK, KV_HEADS, GROUP, HEAD_DIM)

    def band(t):
        tp = jnp.concatenate([jnp.zeros((B, WINDOW, KV_HEADS, HEAD_DIM), t.dtype), t], axis=1)
        tc = tp.reshape(B, n_c + WINDOW_CHUNKS, CHUNK, KV_HEADS, HEAD_DIM)
        return jnp.concatenate([tc[:, o:o + n_c] for o in range(WINDOW_CHUNKS + 1)], axis=2)

    key_pos = (jnp.arange(n_c)[:, None] - WINDOW_CHUNKS) * CHUNK + jnp.arange(BAND)[None, :]
    mask = (key_pos >= 0)[:, None, :]
    o = sink_attention(qb, band(k), band(v), t5_bias(rel_bias, CHUNK, BAND), mask, sinks)
    y = o.reshape(B, T, ATTN_HEADS * HEAD_DIM) @ w_o + b_o
    return y, k[:, -WINDOW:], v[:, -WINDOW:]


def swa_sample(x, cache_k, cache_v, w_qkv, b_qkv, sinks, rel_bias, w_o, b_o):
    B, T, _ = x.shape
    q, k, v = qkv_project(x, w_qkv, b_qkv)
    k_all = jnp.concatenate([cache_k.astype(k.dtype), k], axis=1)
    v_all = jnp.concatenate([cache_v.astype(v.dtype), v], axis=1)
    o = sink_attention(q[:, None], k_all[:, None], v_all[:, None], t5_bias(rel_bias, T, WINDOW + T), None, sinks)
    y = o.reshape(B, T, ATTN_HEADS * HEAD_DIM) @ w_o + b_o
    return y, k_all[:, -WINDOW:], v_all[:, -WINDOW:]


def clamped_swiglu(h):
    h_glu, h_lin = jnp.split(h, 2, axis=-1)
    h_glu = jnp.minimum(h_glu, SWIGLU_LIMIT)
    h_lin = jnp.clip(h_lin, -SWIGLU_LIMIT, SWIGLU_LIMIT)
    return h_glu * jax.nn.sigmoid(SWIGLU_ALPHA * h_glu) * (h_lin + 1.0)


def moe_ffn(x, w_router, b_router, w1, b1, w2, b2):
    B, T, D = x.shape
    n_tok = B * T
    xt = x.reshape(n_tok, D)
    logits = (xt @ w_router + b_router).astype(jnp.float32)
    top_val, top_idx = lax.top_k(logits, TOP_K)
    gates = jax.nn.softmax(top_val, axis=-1)
    n_assign = n_tok * TOP_K
    flat_e = top_idx.reshape(-1)
    order = jnp.argsort(flat_e)
    sorted_e = flat_e[order]
    counts = jnp.zeros((N_EXPERTS,), jnp.int32).at[flat_e].add(1)
    padded = (counts + MOE_BLOCK - 1) // MOE_BLOCK * MOE_BLOCK
    start = jnp.cumsum(counts) - counts
    pad_end = jnp.cumsum(padded)
    pad_start = pad_end - padded
    dest = pad_start[sorted_e] + jnp.arange(n_assign, dtype=jnp.int32) - start[sorted_e]
    n_blocks = -(-n_assign // MOE_BLOCK) + N_EXPERTS
    n_rows = n_blocks * MOE_BLOCK
    row_tok = jnp.full((n_rows,), n_tok, jnp.int32).at[dest].set(order // TOP_K)
    row_gate = jnp.zeros((n_rows,), jnp.float32).at[dest].set(gates.reshape(-1)[order])
    block_expert = jnp.minimum(jnp.searchsorted(pad_end, jnp.arange(n_blocks, dtype=jnp.int32) * MOE_BLOCK,
                                                side='right'), N_EXPERTS - 1)
    x_pad = jnp.concatenate([xt, jnp.zeros((1, D), xt.dtype)], axis=0)
    xb = x_pad[row_tok].reshape(n_blocks, MOE_BLOCK, D)

    def expert_block(args):
        xblk, e = args
        return clamped_swiglu(xblk @ w1[e] + b1[e]) @ w2[e] + b2[e]

    yb = lax.map(expert_block, (xb, block_expert))
    y = jax.ops.segment_sum(yb.reshape(n_rows, D) * row_gate[:, None].astype(x.dtype), row_tok,
                            num_segments=n_tok + 1)
    return y[:n_tok].reshape(B, T, D)


def per_layer_embedding(x, p, w_proj, w_gate, b_gate):
    return x + jax.nn.sigmoid(x @ w_gate + b_gate) * (p @ w_proj)


def setup_inputs(seed: int = 0) -> dict:
    key = jax.random.key(seed)
    ks = iter(jax.random.split(key, 64))

    def nrm(shape, scale=1.0):
        return scale * jax.random.normal(next(ks), shape, jnp.float32)

    def unif(shape, lo, hi):
        return jax.random.uniform(next(ks), shape, jnp.float32, lo, hi)

    D, H, N = D_MODEL, RWKV_HEADS, RWKV_HEAD
    nc, nr, na = N_CONV_LAYERS, N_RWKV_LAYERS, N_ATTN_LAYERS
    qkv_w = (ATTN_HEADS + 2 * KV_HEADS) * HEAD_DIM
    beta = DEEPNORM_BETA
    return {
        'x_prompt': nrm((BATCH, SEQ, D)),
        'x_sample': nrm((DEC_BATCH, DEC_SEQ, D)),
        'p_prompt': nrm((DEPTH, BATCH, SEQ, PLE_DIM)),
        'p_sample': nrm((DEPTH, DEC_BATCH, DEC_SEQ, PLE_DIM)),
        'cache_conv': nrm((nc, DEC_BATCH, CONV_WIDTH - 1, D), 0.5),
        'state_rwkv_shift': nrm((nr, DEC_BATCH, D)),
        'state_rwkv_wkv': nrm((nr, DEC_BATCH, H, N, N), 0.2),
        'cache_swa_k': nrm((na, DEC_BATCH, WINDOW, KV_HEADS, HEAD_DIM)),
        'cache_swa_v': nrm((na, DEC_BATCH, WINDOW, KV_HEADS, HEAD_DIM)),
        'conv_w_in': nrm((nc, D, 2 * D), D ** -0.5),
        'conv_b_in': nrm((nc, 2 * D), 0.02),
        'conv_w_dw': nrm((nc, CONV_WIDTH, D), CONV_WIDTH ** -0.5),
        'conv_b_dw': nrm((nc, D), 0.02),
        'conv_ln_g': 1.0 + nrm((nc, D), 0.05),
        'conv_ln_b': nrm((nc, D), 0.02),
        'conv_w_out': nrm((nc, D, D), beta * D ** -0.5),
        'conv_b_out': nrm((nc, D), 0.02),
        'rwkv_mu': unif((nr, 6, D), 0.0, 1.0),
        'rwkv_w_rkv': nrm((nr, 3, D, D), D ** -0.5),
        'rwkv_w0': unif((nr, D), -5.0, 1.0),
        'rwkv_w1': nrm((nr, D, DECAY_LORA), D ** -0.5),
        'rwkv_w2': nrm((nr, DECAY_LORA, D), 0.5 * DECAY_LORA ** -0.5),
        'rwkv_a0': nrm((nr, D), 0.1),
        'rwkv_a1': nrm((nr, D, ICLR_LORA), D ** -0.5),
        'rwkv_a2': nrm((nr, ICLR_LORA, D), 0.5 * ICLR_LORA ** -0.5),
        'rwkv_g1': nrm((nr, D, GATE_LORA), D ** -0.5),
        'rwkv_g2': nrm((nr, GATE_LORA, D), GATE_LORA ** -0.5),
        'rwkv_k_k': 0.85 + nrm((nr, D), 0.05),
        'rwkv_k_a': 1.0 + nrm((nr, D), 0.05),
        'rwkv_r_k': nrm((nr, H, N), 0.1),
        'rwkv_lnx_g': 1.0 + nrm((nr, D), 0.05),
        'rwkv_lnx_b': nrm((nr, D), 0.02),
        'rwkv_w_o': nrm((nr, D, D), beta * D ** -0.5),
        'attn_w_qkv': nrm((na, D, qkv_w), D ** -0.5),
        'attn_b_qkv': nrm((na, qkv_w), 0.02),
        'attn_sinks': nrm((na, ATTN_HEADS), 0.5),
        'attn_w_o': nrm((na, ATTN_HEADS * HEAD_DIM, D), beta * (ATTN_HEADS * HEAD_DIM) ** -0.5),
        'attn_b_o': nrm((na, D), 0.02),
        'rel_bias': nrm((N_BUCKETS, ATTN_HEADS), 0.5),
        'ln_g': 1.0 + nrm((DEPTH, 2, D), 0.05),
        'ln_b': nrm((DEPTH, 2, D), 0.02),
        'moe_w_router': nrm((DEPTH, D, N_EXPERTS), D ** -0.5),
        'moe_b_router': nrm((DEPTH, N_EXPERTS), 0.01),
        'moe_w1': nrm((DEPTH, N_EXPERTS, D, 2 * D_EXPERT), D ** -0.5),
        'moe_b1': nrm((DEPTH, N_EXPERTS, 2 * D_EXPERT), 0.02),
        'moe_w2': nrm((DEPTH, N_EXPERTS, D_EXPERT, D), beta * D_EXPERT ** -0.5),
        'moe_b2': nrm((DEPTH, N_EXPERTS, D), 0.02),
        'ple_w_proj': nrm((DEPTH, PLE_DIM, D), PLE_DIM ** -0.5),
        'ple_w_gate': nrm((DEPTH, D, D), D ** -0.5),
        'ple_b_gate': nrm((DEPTH, D), 0.02),
    }


def reference(x_prompt, x_sample, p_prompt, p_sample, cache_conv, state_rwkv_shift, state_rwkv_wkv,
              cache_swa_k, cache_swa_v,
              conv_w_in, conv_b_in, conv_w_dw, conv_b_dw, conv_ln_g, conv_ln_b, conv_w_out, conv_b_out,
              rwkv_mu, rwkv_w_rkv, rwkv_w0, rwkv_w1, rwkv_w2, rwkv_a0, rwkv_a1, rwkv_a2, rwkv_g1, rwkv_g2,
              rwkv_k_k, rwkv_k_a, rwkv_r_k, rwkv_lnx_g, rwkv_lnx_b, rwkv_w_o,
              attn_w_qkv, attn_b_qkv, attn_sinks, attn_w_o, attn_b_o, rel_bias,
              ln_g, ln_b, moe_w_router, moe_b_router, moe_w1, moe_b1, moe_w2, moe_b2,
              ple_w_proj, ple_w_gate, ple_b_gate):
    xp, xs = x_prompt, x_sample
    bp = xp.shape[0]
    conv_p, conv_s, shift_p, shift_s, wkv_p, wkv_s = [], [], [], [], [], []
    swa_kp, swa_vp, swa_ks, swa_vs = [], [], [], []
    for i in range(DEPTH):
        kind, j = i % N_MIXERS, i // N_MIXERS
        if kind == 0:
            cw = (conv_w_in[j], conv_b_in[j], conv_w_dw[j], conv_b_dw[j], conv_ln_g[j], conv_ln_b[j],
                  conv_w_out[j], conv_b_out[j])
            mix_p, st_p = conv_module(xp, jnp.zeros((bp, CONV_WIDTH - 1, D_MODEL), xp.dtype), *cw)
            mix_s, st_s = conv_module(xs, cache_conv[j], *cw)
            conv_p.append(st_p)
            conv_s.append(st_s)
        elif kind == 1:
            rw = (rwkv_mu[j], rwkv_w_rkv[j], rwkv_w0[j], rwkv_w1[j], rwkv_w2[j], rwkv_a0[j], rwkv_a1[j],
                  rwkv_a2[j], rwkv_g1[j], rwkv_g2[j], rwkv_k_k[j], rwkv_k_a[j], rwkv_r_k[j],
                  rwkv_lnx_g[j], rwkv_lnx_b[j], rwkv_w_o[j])
            mix_p, sh_p, S_p = rwkv7_time_mix(xp, jnp.zeros((bp, D_MODEL), xp.dtype),
                                              jnp.zeros((bp, RWKV_HEADS, RWKV_HEAD, RWKV_HEAD), xp.dtype), *rw)
            mix_s, sh_s, S_s = rwkv7_time_mix(xs, state_rwkv_shift[j], state_rwkv_wkv[j], *rw)
            shift_p.append(sh_p)
            shift_s.append(sh_s)
            wkv_p.append(S_p)
            wkv_s.append(S_s)
        else:
            aw = (attn_w_qkv[j], attn_b_qkv[j], attn_sinks[j], rel_bias, attn_w_o[j], attn_b_o[j])
            mix_p, k_p, v_p = swa_prompt(xp, *aw)
            mix_s, k_s, v_s = swa_sample(xs, cache_swa_k[j], cache_swa_v[j], *aw)
            swa_kp.append(k_p)
            swa_vp.append(v_p)
            swa_ks.append(k_s)
            swa_vs.append(v_s)
        xp = layer_norm(DEEPNORM_ALPHA * xp + mix_p, ln_g[i, 0], ln_b[i, 0])
        xs = layer_norm(DEEPNORM_ALPHA * xs + mix_s, ln_g[i, 0], ln_b[i, 0])
        mw = (moe_w_router[i], moe_b_router[i], moe_w1[i], moe_b1[i], moe_w2[i], moe_b2[i])
        xp = layer_norm(DEEPNORM_ALPHA * xp + moe_ffn(xp, *mw), ln_g[i, 1], ln_b[i, 1])
        xs = layer_norm(DEEPNORM_ALPHA * xs + moe_ffn(xs, *mw), ln_g[i, 1], ln_b[i, 1])
        xp = per_layer_embedding(xp, p_prompt[i], ple_w_proj[i], ple_w_gate[i], ple_b_gate[i])
        xs = per_layer_embedding(xs, p_sample[i], ple_w_proj[i], ple_w_gate[i], ple_b_gate[i])
    return (xp, xs, jnp.stack(conv_p), jnp.stack(conv_s), jnp.stack(shift_p), jnp.stack(shift_s),
            jnp.stack(wkv_p), jnp.stack(wkv_s), jnp.stack(swa_kp), jnp.stack(swa_vp),
            jnp.stack(swa_ks), jnp.stack(swa_vs))
```

```python
import functools
import math

import jax
import jax.numpy as jnp
from jax import lax
from jax.experimental import pallas as pl
from jax.experimental.pallas import tpu as pltpu

F32 = jnp.float32
BF16 = jnp.bfloat16
I32 = jnp.int32

D = 1024
DEPTH = 4
CONV_W = 31
HALO = 32
HEADS = 16
HD = 64
KVH = 2
GROUP = HEADS // KVH
WINDOW = 128
CHUNK = 64
N_BUCKETS = 32
MAX_DISTANCE = 128
N_EXP = 32
TOP_K = 4
EXP_TILE = 256
LANES = 128
LNX_EPS = 64e-5
LN_EPS = 1e-5
ALPHA = (2 * DEPTH) ** 0.25
SWIGLU_ALPHA = 1.702
SWIGLU_LIMIT = 7.0
VMEM_LIMIT = 56 * 1024 * 1024


def _cp(sem):
    return pltpu.CompilerParams(dimension_semantics=sem, vmem_limit_bytes=VMEM_LIMIT)


def _bdot(a, b):
    return jnp.dot(a.astype(BF16), b.astype(BF16), preferred_element_type=F32)


def _split(a):
    hi = a.astype(BF16)
    lo = (a - hi.astype(F32)).astype(BF16)
    return hi, lo


def _split_dot(a, b_exact):
    hi, lo = _split(a)
    return (jnp.dot(hi, b_exact, preferred_element_type=F32)
            + jnp.dot(lo, b_exact, preferred_element_type=F32))


def _ln(x, g, b, eps=LN_EPS):
    mu = jnp.mean(x, axis=-1, keepdims=True)
    xc = x - mu
    var = jnp.mean(xc * xc, axis=-1, keepdims=True)
    return xc * lax.rsqrt(var + eps) * g + b


def _sigmoid(x):
    return 1.0 / (1.0 + jnp.exp(-x))


def _seq_call(body, *, nb, nt, tt, row_off, n_total, ins, outs, scratch, prev=None):
    off = row_off // tt
    in_specs, args = [], []
    for kind, a in ins:
        if kind == 'tok':
            in_specs.append(pl.BlockSpec((tt, a.shape[1]), lambda b, j: (off + b * nt + j, 0)))
        elif kind == 'bat':
            in_specs.append(pl.BlockSpec((1,) + a.shape[1:], lambda b, j: (b, 0, 0)))
        else:
            in_specs.append(pl.BlockSpec(a.shape, lambda b, j, _n=a.ndim: (0,) * _n))
        args.append(a)
    out_specs, out_shapes = [], []
    for kind, tail, dt in outs:
        if kind == 'tok':
            out_specs.append(pl.BlockSpec((tt, tail[0]), lambda b, j: (off + b * nt + j, 0)))
            out_shapes.append(jax.ShapeDtypeStruct((n_total, tail[0]), dt))
        else:
            out_specs.append(pl.BlockSpec((1,) + tuple(tail), lambda b, j: (b, 0, 0)))
            out_shapes.append(jax.ShapeDtypeStruct((nb,) + tuple(tail), dt))
    aliases = {}
    n_prev = 0
    tok_out = [i for i, o in enumerate(outs) if o[0] == 'tok']
    if prev is None:
        prev = [jnp.zeros((n_total, outs[i][1][0]), outs[i][2]) for i in tok_out]
    for p, oi in zip(prev, tok_out):
        aliases[len(args)] = oi
        in_specs.append(pl.BlockSpec(memory_space=pl.ANY))
        args.append(p)
        n_prev += 1
    n_in = len(ins)

    def wrapped(*refs):
        body(*refs[:n_in], *refs[n_in + n_prev:])

    return pl.pallas_call(
        wrapped, grid=(nb, nt), in_specs=in_specs, out_specs=out_specs, out_shape=out_shapes,
        scratch_shapes=scratch, input_output_aliases=aliases,
        compiler_params=_cp(("arbitrary", "arbitrary")))(*args)


CONV_RC = 32
CONV_LC = 512


def _conv_kernel(x_ref, st_ref, win_ref, bin_ref, wdw_ref, bdw_ref, cg_ref, cb_ref, wout_ref,
                 bout_ref, lg_ref, lb_ref, o_ref, so_ref, ubuf, ybuf, *, tt, nt):
    j = pl.program_id(1)

    @pl.when(j == 0)
    def _():
        ubuf[0:HALO, :] = st_ref[0]

    @pl.when(j > 0)
    def _():
        ubuf[0:HALO, :] = ubuf[tt:tt + HALO, :]

    x = x_ref[...]
    h = _bdot(x, win_ref[...]) + bin_ref[...]
    ubuf[HALO:HALO + tt, :] = h[:, :D] * _sigmoid(h[:, D:])
    first = HALO - (CONV_W - 1)
    for r0 in range(0, tt, CONV_RC):
        for c0 in range(0, D, CONV_LC):
            acc = jnp.zeros((CONV_RC, CONV_LC), F32) + bdw_ref[:, c0:c0 + CONV_LC]
            for tap in range(CONV_W):
                acc = acc + (wdw_ref[tap:tap + 1, c0:c0 + CONV_LC]
                             * ubuf[r0 + first + tap:r0 + first + tap + CONV_RC, c0:c0 + CONV_LC])
            ybuf[r0:r0 + CONV_RC, c0:c0 + CONV_LC] = acc
    z = _ln(ybuf[...], cg_ref[...], cb_ref[...])
    z = z * _sigmoid(z)
    mix = _bdot(z, wout_ref[...]) + bout_ref[...]
    o_ref[...] = _ln(ALPHA * x + mix, lg_ref[...], lb_ref[...])

    @pl.when(j == nt - 1)
    def _():
        so_ref[0] = ubuf[tt:tt + HALO, :]


def _conv_mixer(x, state, w, lnp, *, nb, t, tt, row_off, prev):
    nt = t // tt
    n_total = x.shape[0]
    win, bin_, wdw, bdw, cg, cb, wout, bout = w
    ins = [('tok', x), ('bat', state), ('const', win), ('const', bin_), ('const', wdw), ('const', bdw),
           ('const', cg), ('const', cb), ('const', wout), ('const', bout), ('const', lnp[0]), ('const', lnp[1])]
    outs = [('tok', (D,), F32), ('bat', (HALO, D), F32)]
    scratch = [pltpu.VMEM((HALO + tt, D), F32), pltpu.VMEM((tt, D), F32)]
    return _seq_call(functools.partial(_conv_kernel, tt=tt, nt=nt), nb=nb, nt=nt, tt=tt, row_off=row_off,
                     n_total=n_total, ins=ins, outs=outs, scratch=scratch, prev=prev)


def _head_sum(y, hs_ref, hst_ref):
    s = _split_dot(y, hs_ref[...])
    return _split_dot(s, hst_ref[...])


def _rw1_kernel(x_ref, sh_ref, mu_ref, wr_ref, wk_ref, wv_ref, w0_ref, w1_ref, w2_ref, a0_ref, a1_ref,
                a2_ref, g1_ref, g2_ref, kk_ref, ka_ref, hs_ref, hst_ref,
                r_ref, w_ref, k_ref, v_ref, an_ref, b_ref, g_ref, xbuf, *, tt):
    j = pl.program_id(1)

    @pl.when(j == 0)
    def _():
        xbuf[7:8, :] = sh_ref[0]

    @pl.when(j > 0)
    def _():
        xbuf[7:8, :] = xbuf[7 + tt:8 + tt, :]

    x = x_ref[...]
    xbuf[8:8 + tt, :] = x
    xx = xbuf[7:7 + tt, :] - x
    mu = mu_ref[...]
    xr = x + xx * mu[0:1]
    xw = x + xx * mu[1:2]
    xk = x + xx * mu[2:3]
    xv = x + xx * mu[3:4]
    xa = x + xx * mu[4:5]
    xg = x + xx * mu[5:6]
    r = _bdot(xr, wr_ref[...])
    k = _bdot(xk, wk_ref[...])
    v = _bdot(xv, wv_ref[...])
    lw = w0_ref[...] + _bdot(jnp.tanh(_bdot(xw, w1_ref[...])), w2_ref[...])
    z = -lw
    log_w = -(jnp.maximum(z, 0.0) + jnp.log(1.0 + jnp.exp(-jnp.abs(z)))) - 0.5
    a = _sigmoid(a0_ref[...] + _bdot(_bdot(xa, a1_ref[...]), a2_ref[...]))
    g = _bdot(_sigmoid(_bdot(xg, g1_ref[...])), g2_ref[...])
    kk = k * kk_ref[...]
    ss = _head_sum(kk * kk, hs_ref, hst_ref)
    kk = kk * lax.rsqrt(jnp.maximum(ss, 1e-24))
    r_ref[...] = r
    w_ref[...] = jnp.exp(-jnp.exp(log_w))
    k_ref[...] = k * (1.0 + (a - 1.0) * ka_ref[...])
    v_ref[...] = v
    an_ref[...] = -kk
    b_ref[...] = kk * a
    g_ref[...] = g


def _rw1(x, shift, w, hs, hst, *, nb, t, tt, row_off, prev):
    nt = t // tt
    ins = [('tok', x), ('bat', shift)] + [('const', a) for a in w] + [('const', hs), ('const', hst)]
    outs = [('tok', (D,), F32)] * 7
    scratch = [pltpu.VMEM((8 + tt, D), F32)]
    return _seq_call(functools.partial(_rw1_kernel, tt=tt), nb=nb, nt=nt, tt=tt, row_off=row_off,
                     n_total=x.shape[0], ins=ins, outs=outs, scratch=scratch, prev=prev)


def _scan_kernel(r_ref, w_ref, k_ref, v_ref, an_ref, b_ref, a0_ref, s0_ref, o_ref, st_ref, S, sa_buf,
                 *, tc, nc):
    c = pl.program_id(1)

    @pl.when(c == 0)
    def _():
        S[...] = s0_ref[...]

        def init(kk, acc):
            return acc + S[kk] * a0_ref[pl.ds(kk, 1), :]

        sa_buf[...] = lax.fori_loop(0, HD, init, jnp.zeros((HD, LANES), F32))

    def step(t, sa):
        vt = v_ref[t]

        def kbody(kk, acc):
            o_acc, sa_acc = acc
            sk = (S[kk] * w_ref[t, pl.ds(kk, 1), :] + sa * b_ref[t, pl.ds(kk, 1), :]
                  + vt * k_ref[t, pl.ds(kk, 1), :])
            S[kk] = sk
            return (o_acc + sk * r_ref[t, pl.ds(kk, 1), :], sa_acc + sk * an_ref[t, pl.ds(kk, 1), :])

        zero = jnp.zeros((HD, LANES), F32)
        o_acc, sa_next = lax.fori_loop(0, HD, kbody, (zero, zero), unroll=4)
        o_ref[t] = o_acc
        return sa_next

    sa_buf[...] = lax.fori_loop(0, tc, step, sa_buf[...])

    @pl.when(c == nc - 1)
    def _():
        st_ref[...] = S[...]


def _scan(r, w, k, v, an, b, a0, s0, *, tc):
    t, _, lanes = r.shape
    ng, nc = lanes // LANES, t // tc
    seq = pl.BlockSpec((tc, HD, LANES), lambda g, c: (c, 0, g))
    return pl.pallas_call(
        functools.partial(_scan_kernel, tc=tc, nc=nc), grid=(ng, nc),
        in_specs=[seq] * 6 + [pl.BlockSpec((HD, LANES), lambda g, c: (0, g)),
                              pl.BlockSpec((HD, HD, LANES), lambda g, c: (0, 0, g))],
        out_specs=[seq, pl.BlockSpec((HD, HD, LANES), lambda g, c: (0, 0, g))],
        out_shape=[jax.ShapeDtypeStruct((t, HD, lanes), F32), jax.ShapeDtypeStruct((HD, HD, lanes), F32)],
        scratch_shapes=[pltpu.VMEM((HD, HD, LANES), F32), pltpu.VMEM((HD, LANES), F32)],
        compiler_params=_cp(("arbitrary", "arbitrary")))(r, w, k, v, an, b, a0, s0)


def _rw3_kernel(o_ref, r_ref, k_ref, v_ref, g_ref, x_ref, rk_ref, xg_ref, xb_ref, wo_ref, hs_ref, hst_ref,
                lg_ref, lb_ref, out_ref):
    o = o_ref[...]
    mo = _head_sum(o, hs_ref, hst_ref) * (1.0 / HD)
    d = o - mo
    vo = _head_sum(d * d, hs_ref, hst_ref) * (1.0 / HD)
    on = d * lax.rsqrt(vo + LNX_EPS) * xg_ref[...] + xb_ref[...]
    bonus = _head_sum(r_ref[...] * k_ref[...] * rk_ref[...], hs_ref, hst_ref) * v_ref[...]
    mix = _bdot((on + bonus) * g_ref[...], wo_ref[...])
    out_ref[...] = _ln(ALPHA * x_ref[...] + mix, lg_ref[...], lb_ref[...])


def _tok_call(body, n, tt, toks, consts, outs, smem_blocks=(), extra=None):
    in_specs = [pl.BlockSpec((tt, a.shape[1]), lambda i: (i, 0)) for a in toks]
    in_specs += [pl.BlockSpec(a.shape, lambda i, _n=a.ndim: (0,) * _n) for a in consts]
    return pl.pallas_call(
        body, grid=(n // tt,), in_specs=in_specs,
        out_specs=[pl.BlockSpec((tt, w), lambda i: (i, 0)) for w, _ in outs],
        out_shape=[jax.ShapeDtypeStruct((n, w), dt) for w, dt in outs],
        compiler_params=_cp(("arbitrary",)))(*toks, *consts)


def _qkv_kernel(x_ref, w_ref, b_ref, q_ref, k_ref, v_ref):
    h = _bdot(x_ref[...], w_ref[...]) + b_ref[...]
    q_ref[...] = h[:, :HEADS * HD].astype(BF16)
    k_ref[...] = h[:, HEADS * HD:HEADS * HD + KVH * HD]
    v_ref[...] = h[:, HEADS * HD + KVH * HD:]


def _attn_kernel(sink_ref, q_ref, kp_ref, vp_ref, bias_ref, x_ref, wo_ref, bo_ref, lg_ref, lb_ref, o_ref,
                 *, nq, kb, stride, mask_lo):
    c = pl.program_id(1)
    start = pl.multiple_of(c * stride, 8)
    kband = kp_ref[0, pl.ds(start, kb), :].astype(BF16)
    vband = vp_ref[0, pl.ds(start, kb), :].astype(BF16)
    valid = (start + lax.broadcasted_iota(I32, (1, kb), 1)) >= mask_lo
    q = q_ref[...]
    acc = jnp.zeros((nq, D), F32)
    for h in range(HEADS):
        hk = h // GROUP
        qh = q[:, h * HD:(h + 1) * HD]
        kh = kband[:, hk * HD:(hk + 1) * HD]
        vh = vband[:, hk * HD:(hk + 1) * HD]
        logits = lax.dot_general(qh, kh, (((1,), (1,)), ((), ())), preferred_element_type=F32) * HD ** -0.5
        logits = jnp.where(valid, logits + bias_ref[h], -1e30)
        sink = sink_ref[h]
        m = jnp.maximum(jnp.max(logits, axis=-1, keepdims=True), sink)
        p = jnp.exp(logits - m)
        p = p / (jnp.sum(p, axis=-1, keepdims=True) + jnp.exp(sink - m))
        oh = jnp.dot(p.astype(BF16), vh, preferred_element_type=F32)
        acc = acc + jnp.dot(oh.astype(BF16), wo_ref[h * HD:(h + 1) * HD, :], preferred_element_type=F32)
    o_ref[...] = _ln(ALPHA * x_ref[...] + acc + bo_ref[...], lg_ref[...], lb_ref[...])


def _attn(sinks, q, kp, vp, bias, x, wo, bo, lnp, *, nb, nt, nq, kb, stride, mask_lo, row_off, prev):
    off = row_off // nq
    n_total = x.shape[0]
    tok = lambda w: pl.BlockSpec((nq, w), lambda b, c: (off + b * nt + c, 0))
    const = lambda a: pl.BlockSpec(a.shape, lambda b, c, _n=a.ndim: (0,) * _n)
    bat = lambda a: pl.BlockSpec((1,) + a.shape[1:], lambda b, c: (b, 0, 0))
    in_specs = [pl.BlockSpec(memory_space=pltpu.SMEM), tok(D), bat(kp), bat(vp), const(bias), tok(D),
                const(wo), const(bo), const(lnp[0]), const(lnp[1])]
    args = [sinks, q, kp, vp, bias, x, wo, bo, lnp[0], lnp[1]]
    aliases = {}
    if prev is None:
        prev = jnp.zeros((n_total, D), F32)
    aliases[len(args)] = 0
    in_specs.append(pl.BlockSpec(memory_space=pl.ANY))
    args.append(prev)
    n_in = 10

    def wrapped(*refs):
        _attn_kernel(*refs[:n_in], refs[-1], nq=nq, kb=kb, stride=stride, mask_lo=mask_lo)

    return pl.pallas_call(
        wrapped, grid=(nb, nt), in_specs=in_specs, out_specs=tok(D),
        out_shape=jax.ShapeDtypeStruct((n_total, D), F32), input_output_aliases=aliases,
        compiler_params=_cp(("arbitrary", "arbitrary")))(*args)


def _t5_bucket(rel):
    half = N_BUCKETS // 2
    max_exact = half // 2
    ret = jnp.where(rel > 0, half, 0)
    n = jnp.abs(rel)
    nf = jnp.maximum(n, 1).astype(F32)
    large = max_exact + (jnp.log(nf / max_exact) / math.log(MAX_DISTANCE / max_exact)
                         * (half - max_exact)).astype(I32)
    large = jnp.minimum(large, half - 1)
    return ret + jnp.where(n < max_exact, n, large)


def _t5_bias(rel_bias, n_q, n_k):
    rel = jnp.arange(n_k)[None, :] - WINDOW - jnp.arange(n_q)[:, None]
    return jnp.transpose(rel_bias[_t5_bucket(rel)], (2, 0, 1))


def _route_kernel(x_ref, wr_ref, br_ref, idx_ref, gate_ref, rank_ref, cnt_ref, carry, *, tt):
    i = pl.program_id(0)

    @pl.when(i == 0)
    def _():
        carry[...] = jnp.zeros_like(carry)

    xh, xl = _split(x_ref[...])
    wh, wl = _split(wr_ref[...])
    logits = (jnp.dot(xh, wh, preferred_element_type=F32) + jnp.dot(xl, wh, preferred_element_type=F32)
              + jnp.dot(xh, wl, preferred_element_type=F32)) + br_ref[...]
    lane = lax.broadcasted_iota(I32, (tt, N_EXP), 1)
    out_lane = lax.broadcasted_iota(I32, (tt, LANES), 1)
    vals, sels = [], []
    idx_out = jnp.zeros((tt, LANES), I32)
    work = logits
    for k in range(TOP_K):
        m = jnp.max(work, axis=-1, keepdims=True)
        ik = jnp.min(jnp.where(work == m, lane, N_EXP), axis=-1, keepdims=True)
        sel = lane == ik
        vals.append(m)
        sels.append(sel)
        idx_out = jnp.where(out_lane == k, ik, idx_out)
        work = jnp.where(sel, -jnp.inf, work)
    es = [jnp.exp(v - vals[0]) for v in vals]
    den = es[0] + es[1] + es[2] + es[3]
    gate_out = jnp.zeros((tt, LANES), F32)
    for k in range(TOP_K):
        gate_out = jnp.where(out_lane == k, es[k] / den, gate_out)
    onehot = jnp.zeros((tt, N_EXP), F32)
    for sel in sels:
        onehot = onehot + sel.astype(F32)
    tri = (lax.broadcasted_iota(I32, (tt, tt), 0) > lax.broadcasted_iota(I32, (tt, tt), 1)).astype(BF16)
    base = carry[...] + jnp.dot(tri, onehot.astype(BF16), preferred_element_type=F32)
    rank_out = jnp.zeros((tt, LANES), I32)
    for k in range(TOP_K):
        rk = jnp.sum(jnp.where(sels[k], base, 0.0), axis=-1, keepdims=True)
        rank_out = jnp.where(out_lane == k, rk.astype(I32), rank_out)
    carry[...] = carry[...] + jnp.sum(onehot, axis=0, keepdims=True)
    idx_ref[...] = idx_out
    gate_ref[...] = gate_out
    rank_ref[...] = rank_out
    cnt_ref[...] = carry[...]


def _route(x1, wr, br, *, tt):
    n = x1.shape[0]
    tokspec = lambda w: pl.BlockSpec((tt, w), lambda i: (i, 0))
    const = lambda a: pl.BlockSpec(a.shape, lambda i, _n=a.ndim: (0,) * _n)
    return pl.pallas_call(
        functools.partial(_route_kernel, tt=tt), grid=(n // tt,),
        in_specs=[tokspec(D), const(wr), const(br)],
        out_specs=[tokspec(LANES), tokspec(LANES), tokspec(LANES), pl.BlockSpec((1, N_EXP), lambda i: (0, 0))],
        out_shape=[jax.ShapeDtypeStruct((n, LANES), I32), jax.ShapeDtypeStruct((n, LANES), F32),
                   jax.ShapeDtypeStruct((n, LANES), I32), jax.ShapeDtypeStruct((1, N_EXP), F32)],
        scratch_shapes=[pltpu.VMEM((1, N_EXP), F32)],
        compiler_params=_cp(("arbitrary",)))(x1, wr, br)


def _row_copy(src, s, dst, d, sem):
    return pltpu.make_async_copy(src.at[pl.ds(s, 1), :], dst.at[pl.ds(d, 1), :], sem)


def _disp_kernel(idx_ref, rank_ref, ps_ref, x_ref, xs_in, xs_ref, sem, *, tt):
    del xs_in

    def issue(r, carry):
        for k in range(TOP_K):
            dest = ps_ref[idx_ref[r * TOP_K + k]] + rank_ref[r * TOP_K + k]
            _row_copy(x_ref, r, xs_ref, dest, sem).start()
        return carry

    lax.fori_loop(0, tt, issue, 0)

    def drain(r, carry):
        for k in range(TOP_K):
            _row_copy(x_ref, 0, xs_ref, 0, sem).wait()
        return carry

    lax.fori_loop(0, tt, drain, 0)


def _dispatch(idx, rank, pad_start, x1, xs_prev, *, tt):
    n = x1.shape[0]
    smem_tok = pl.BlockSpec((tt * TOP_K,), lambda i: (i,), memory_space=pltpu.SMEM)
    return pl.pallas_call(
        functools.partial(_disp_kernel, tt=tt), grid=(n // tt,),
        in_specs=[smem_tok, smem_tok, pl.BlockSpec(memory_space=pltpu.SMEM),
                  pl.BlockSpec((tt, D), lambda i: (i, 0)), pl.BlockSpec(memory_space=pl.ANY)],
        out_specs=pl.BlockSpec(memory_space=pl.ANY),
        out_shape=jax.ShapeDtypeStruct(xs_prev.shape, F32),
        scratch_shapes=[pltpu.SemaphoreType.DMA],
        input_output_aliases={4: 0},
        compiler_params=_cp(("arbitrary",)))(idx, rank, pad_start, x1, xs_prev)


def _expert_kernel(te_ref, nv_ref, xs_ref, w1_ref, b1_ref, w2_ref, b2_ref, y_ref, w1b, w2b):
    i = pl.program_id(0)
    valid = i < nv_ref[0]
    changed = jnp.logical_or(i == 0, te_ref[i] != te_ref[jnp.maximum(i - 1, 0)])

    @pl.when(jnp.logical_and(valid, changed))
    def _():
        for r0 in range(0, D, 256):
            w1b[r0:r0 + 256, :] = w1_ref[0, r0:r0 + 256, :].astype(BF16)
            w2b[r0:r0 + 256, :] = w2_ref[0, r0:r0 + 256, :].astype(BF16)

    @pl.when(valid)
    def _():
        h = jnp.dot(xs_ref[...].astype(BF16), w1b[...], preferred_element_type=F32) + b1_ref[0]
        glu = jnp.minimum(h[:, :D], SWIGLU_LIMIT)
        lin = jnp.clip(h[:, D:], -SWIGLU_LIMIT, SWIGLU_LIMIT)
        act = glu * _sigmoid(SWIGLU_ALPHA * glu) * (lin + 1.0)
        y_ref[...] = jnp.dot(act.astype(BF16), w2b[...], preferred_element_type=F32) + b2_ref[0]

    @pl.when(jnp.logical_not(valid))
    def _():
        y_ref[...] = jnp.zeros_like(y_ref)


def _experts(tile_expert, n_valid, xs, w1, b1, w2, b2):
    n_tiles = xs.shape[0] // EXP_TILE
    grid_spec = pltpu.PrefetchScalarGridSpec(
        num_scalar_prefetch=2, grid=(n_tiles,),
        in_specs=[pl.BlockSpec((EXP_TILE, D), lambda i, te, nv: (i, 0)),
                  pl.BlockSpec((1, D, 2 * D), lambda i, te, nv: (te[i], 0, 0)),
                  pl.BlockSpec((1, 1, 2 * D), lambda i, te, nv: (te[i], 0, 0)),
                  pl.BlockSpec((1, D, D), lambda i, te, nv: (te[i], 0, 0)),
                  pl.BlockSpec((1, 1, D), lambda i, te, nv: (te[i], 0, 0))],
        out_specs=pl.BlockSpec((EXP_TILE, D), lambda i, te, nv: (i, 0)),
        scratch_shapes=[pltpu.VMEM((D, 2 * D), BF16), pltpu.VMEM((D, D), BF16)])
    return pl.pallas_call(
        _expert_kernel, grid_spec=grid_spec, out_shape=jax.ShapeDtypeStruct(xs.shape, F32),
        compiler_params=_cp(("arbitrary",)))(tile_expert, n_valid, xs, w1, b1, w2, b2)


def _comb_kernel(idx_ref, rank_ref, ps_ref, gate_ref, x1_ref, p_ref, y_ref, lg_ref, lb_ref, wg_ref, bg_ref,
                 wp_ref, o_ref, buf, sem, *, tt):
    def issue(r, carry):
        for k in range(TOP_K):
            src = ps_ref[idx_ref[r * TOP_K + k]] + rank_ref[r * TOP_K + k]
            _row_copy(y_ref, src, buf.at[k], r, sem).start()
        return carry

    lax.fori_loop(0, tt, issue, 0)

    def drain(r, carry):
        for k in range(TOP_K):
            _row_copy(y_ref, 0, buf.at[k], 0, sem).wait()
        return carry

    lax.fori_loop(0, tt, drain, 0)
    gate = gate_ref[...]
    moe = gate[:, 0:1] * buf[0]
    for k in range(1, TOP_K):
        moe = moe + gate[:, k:k + 1] * buf[k]
    x2 = _ln(ALPHA * x1_ref[...] + moe, lg_ref[...], lb_ref[...])
    gt = _sigmoid(_bdot(x2, wg_ref[...]) + bg_ref[...])
    o_ref[...] = x2 + gt * _bdot(p_ref[...], wp_ref[...])


def _combine(idx, rank, pad_start, gate, x1, p, y, lnp, wg, bg, wp, *, tt):
    n = x1.shape[0]
    smem_tok = pl.BlockSpec((tt * TOP_K,), lambda i: (i,), memory_space=pltpu.SMEM)
    tok = lambda w: pl.BlockSpec((tt, w), lambda i: (i, 0))
    const = lambda a: pl.BlockSpec(a.shape, lambda i, _n=a.ndim: (0,) * _n)
    return pl.pallas_call(
        functools.partial(_comb_kernel, tt=tt), grid=(n // tt,),
        in_specs=[smem_tok, smem_tok, pl.BlockSpec(memory_space=pltpu.SMEM), tok(LANES), tok(D),
                  tok(p.shape[1]), pl.BlockSpec(memory_space=pl.ANY), const(lnp[0]), const(lnp[1]),
                  const(wg), const(bg), const(wp)],
        out_specs=tok(D), out_shape=jax.ShapeDtypeStruct((n, D), F32),
        scratch_shapes=[pltpu.VMEM((TOP_K, tt, D), F32), pltpu.SemaphoreType.DMA],
        compiler_params=_cp(("arbitrary",)))(idx, rank, pad_start, gate, x1, p, y, lnp[0], lnp[1], wg, bg, wp)


def _moe_ple(x1, p, xs_buf, wr, br, w1, b1, w2, b2, lnp, wg, bg, wp, *, tt):
    n = x1.shape[0]
    idx, gate, rank, counts = _route(x1, wr, br, tt=tt)
    counts = counts[0].astype(I32)
    padded = (counts + EXP_TILE - 1) // EXP_TILE * EXP_TILE
    pad_end = jnp.cumsum(padded)
    pad_start = pad_end - padded
    n_tiles = xs_buf.shape[0] // EXP_TILE
    n_valid = (pad_end[-1] // EXP_TILE).astype(I32)
    tiles = jnp.minimum(jnp.arange(n_tiles, dtype=I32), n_valid - 1) * EXP_TILE
    tile_expert = jnp.minimum(jnp.searchsorted(pad_end, tiles, side='right'), N_EXP - 1).astype(I32)
    idx_f = idx[:, :TOP_K].reshape(-1)
    rank_f = rank[:, :TOP_K].reshape(-1)
    xs = _dispatch(idx_f, rank_f, pad_start, x1, xs_buf, tt=tt)
    y = _experts(tile_expert, n_valid.reshape(1), xs, w1, b1.reshape(N_EXP, 1, 2 * D), w2,
                 b2.reshape(N_EXP, 1, D))
    return _combine(idx_f, rank_f, pad_start, gate, x1, p, y, lnp, wg, bg, wp, tt=tt), xs


def _to_scan(a, nb, t):
    a = a.reshape(nb, t, HEADS, HD).transpose(1, 3, 0, 2).reshape(t, HD, nb * HEADS)
    pad = (-nb * HEADS) % LANES
    return jnp.pad(a, ((0, 0), (0, 0), (0, pad))) if pad else a


def _from_scan(o, nb, t):
    return o[:, :, :nb * HEADS].reshape(t, HD, nb, HEADS).transpose(2, 0, 3, 1).reshape(nb * t, D)


def _row2(a):
    return a.reshape(1, -1)


def _tile(n):
    for tt in (256, 128, 64, 32, 16, 8):
        if n % tt == 0:
            return tt
    raise ValueError(n)


def kernel(x_prompt, x_sample, p_prompt, p_sample, cache_conv, state_rwkv_shift, state_rwkv_wkv, cache_swa_k, cache_swa_v, conv_w_in, conv_b_in, conv_w_dw, conv_b_dw, conv_ln_g, conv_ln_b, conv_w_out, conv_b_out, rwkv_mu, rwkv_w_rkv, rwkv_w0, rwkv_w1, rwkv_w2, rwkv_a0, rwkv_a1, rwkv_a2, rwkv_g1, rwkv_g2, rwkv_k_k, rwkv_k_a, rwkv_r_k, rwkv_lnx_g, rwkv_lnx_b, rwkv_w_o, attn_w_qkv, attn_b_qkv, attn_sinks, attn_w_o, attn_b_o, rel_bias, ln_g, ln_b, moe_w_router, moe_b_router, moe_w1, moe_b1, moe_w2, moe_b2, ple_w_proj, ple_w_gate, ple_b_gate):
    bp, tp, _ = x_prompt.shape
    bs, ts, _ = x_sample.shape
    n_p, n_s = bp * tp, bs * ts
    n = n_p + n_s
    assert tp % 128 == 0 and n_p % ts == 0 and ts % 8 == 0 and ts <= CHUNK
    tt_tok = _tile(n)
    tt_p = 128
    x = jnp.concatenate([x_prompt.reshape(n_p, D), x_sample.reshape(n_s, D)], axis=0)
    p_all = jnp.concatenate([p_prompt.reshape(DEPTH, n_p, -1), p_sample.reshape(DEPTH, n_s, -1)], axis=1)
    n_rows = (-(-n * TOP_K // EXP_TILE) + N_EXP) * EXP_TILE
    xs_buf = jnp.zeros((n_rows, D), F32)
    head_sel = (jnp.arange(D)[:, None] // HD == jnp.arange(LANES)[None, :]).astype(BF16)
    head_sel_t = head_sel.T
    conv_p, conv_s, shift_p, shift_s, wkv_p, wkv_s = [], [], [], [], [], []
    swa_kp, swa_vp, swa_ks, swa_vs = [], [], [], []
    for i in range(DEPTH):
        kind, j = i % 3, i // 3
        lnp = (_row2(ln_g[i, 0]), _row2(ln_b[i, 0]))
        if kind == 0:
            cw = (conv_w_in[j].astype(BF16), _row2(conv_b_in[j]), conv_w_dw[j], _row2(conv_b_dw[j]),
                  _row2(conv_ln_g[j]), _row2(conv_ln_b[j]), conv_w_out[j].astype(BF16), _row2(conv_b_out[j]))
            st_p = jnp.zeros((bp, HALO, D), F32)
            st_s = jnp.pad(cache_conv[j], ((0, 0), (HALO - (CONV_W - 1), 0), (0, 0)))
            x1, so_p = _conv_mixer(x, st_p, cw, lnp, nb=bp, t=tp, tt=tt_p, row_off=0, prev=None)
            x1, so_s = _conv_mixer(x, st_s, cw, lnp, nb=bs, t=ts, tt=ts, row_off=n_p, prev=[x1])
            conv_p.append(so_p[:, HALO - (CONV_W - 1):])
            conv_s.append(so_s[:, HALO - (CONV_W - 1):])
        elif kind == 1:
            rw = (rwkv_mu[j], rwkv_w_rkv[j, 0].astype(BF16), rwkv_w_rkv[j, 1].astype(BF16),
                  rwkv_w_rkv[j, 2].astype(BF16), _row2(rwkv_w0[j]), rwkv_w1[j].astype(BF16),
                  rwkv_w2[j].astype(BF16), _row2(rwkv_a0[j]), rwkv_a1[j].astype(BF16), rwkv_a2[j].astype(BF16),
                  rwkv_g1[j].astype(BF16), rwkv_g2[j].astype(BF16), _row2(rwkv_k_k[j]), _row2(rwkv_k_a[j]))
            sh_p = jnp.zeros((bp, 1, D), F32)
            sh_s = state_rwkv_shift[j].reshape(bs, 1, D)
            proj = _rw1(x, sh_p, rw, head_sel, head_sel_t, nb=bp, t=tp, tt=tt_p, row_off=0, prev=None)
            proj = _rw1(x, sh_s, rw, head_sel, head_sel_t, nb=bs, t=ts, tt=ts, row_off=n_p, prev=list(proj))
            r, w, k, v, an, b, g = proj
            o_parts, states = [], []
            for (lo, nb_, t_, s0, tc) in ((0, bp, tp, None, 64), (n_p, bs, ts, state_rwkv_wkv[j], ts)):
                sl = slice(lo, lo + nb_ * t_)
                rs, ws, ks, vs, ans, bs_ = (_to_scan(a[sl], nb_, t_) for a in (r, w, k, v, an, b))
                lanes = rs.shape[2]
                if s0 is None:
                    s0l = jnp.zeros((HD, HD, lanes), F32)
                else:
                    s0l = s0.transpose(3, 2, 0, 1).reshape(HD, HD, nb_ * HEADS)
                    s0l = jnp.pad(s0l, ((0, 0), (0, 0), (0, lanes - nb_ * HEADS)))
                a_next = jnp.concatenate([ans[1:], jnp.zeros_like(ans[:1])], axis=0)
                o_l, s_l = _scan(rs, ws, ks, vs, a_next, bs_, ans[0], s0l, tc=tc)
                o_parts.append(_from_scan(o_l, nb_, t_))
                states.append(s_l[:, :, :nb_ * HEADS].reshape(HD, HD, nb_, HEADS).transpose(2, 3, 1, 0))
            o = jnp.concatenate(o_parts, axis=0)
            consts = [_row2(rwkv_r_k[j].reshape(-1)), _row2(rwkv_lnx_g[j]), _row2(rwkv_lnx_b[j]),
                      rwkv_w_o[j].astype(BF16), head_sel, head_sel_t, lnp[0], lnp[1]]
            x1, = _tok_call(_rw3_kernel, n, tt_tok, [o, r, k, v, g, x], consts, [(D, F32)])
            shift_p.append(x[:n_p].reshape(bp, tp, D)[:, -1])
            shift_s.append(x[n_p:].reshape(bs, ts, D)[:, -1])
            wkv_p.append(states[0])
            wkv_s.append(states[1])
        else:
            q, kx, vx = _tok_call(_qkv_kernel, n, tt_tok, [x], [attn_w_qkv[j].astype(BF16), _row2(attn_b_qkv[j])],
                                  [(HEADS * HD, BF16), (KVH * HD, F32), (KVH * HD, F32)])
            wo, bo = attn_w_o[j].astype(BF16), _row2(attn_b_o[j])
            k_p = kx[:n_p].reshape(bp, tp, KVH * HD)
            v_p = vx[:n_p].reshape(bp, tp, KVH * HD)
            zpad = jnp.zeros((bp, WINDOW, KVH * HD), F32)
            nc = tp // CHUNK
            band = WINDOW + CHUNK
            x1 = _attn(attn_sinks[j], q, jnp.concatenate([zpad, k_p], axis=1), jnp.concatenate([zpad, v_p], axis=1),
                       _t5_bias(rel_bias, CHUNK, band), x, wo, bo, lnp, nb=bp, nt=nc, nq=CHUNK, kb=band,
                       stride=CHUNK, mask_lo=WINDOW, row_off=0, prev=None)
            k_all = jnp.concatenate([cache_swa_k[j].reshape(bs, WINDOW, KVH * HD),
                                     kx[n_p:].reshape(bs, ts, KVH * HD)], axis=1)
            v_all = jnp.concatenate([cache_swa_v[j].reshape(bs, WINDOW, KVH * HD),
                                     vx[n_p:].reshape(bs, ts, KVH * HD)], axis=1)
            x1 = _attn(attn_sinks[j], q, k_all, v_all, _t5_bias(rel_bias, ts, WINDOW + ts), x, wo, bo, lnp,
                       nb=bs, nt=1, nq=ts, kb=WINDOW + ts, stride=0, mask_lo=0, row_off=n_p, prev=x1)
            swa_kp.append(k_p[:, -WINDOW:].reshape(bp, WINDOW, KVH, HD))
            swa_vp.append(v_p[:, -WINDOW:].reshape(bp, WINDOW, KVH, HD))
            swa_ks.append(k_all[:, -WINDOW:].reshape(bs, WINDOW, KVH, HD))
            swa_vs.append(v_all[:, -WINDOW:].reshape(bs, WINDOW, KVH, HD))
        x, xs_buf = _moe_ple(x1, p_all[i], xs_buf, moe_w_router[i], _row2(moe_b_router[i]), moe_w1[i], moe_b1[i],
                             moe_w2[i], moe_b2[i], (_row2(ln_g[i, 1]), _row2(ln_b[i, 1])),
                             ple_w_gate[i].astype(BF16), _row2(ple_b_gate[i]), ple_w_proj[i].astype(BF16), tt=tt_tok)
    return (x[:n_p].reshape(bp, tp, D), x[n_p:].reshape(bs, ts, D), jnp.stack(conv_p), jnp.stack(conv_s),
            jnp.stack(shift_p), jnp.stack(shift_s), jnp.stack(wkv_p), jnp.stack(wkv_s), jnp.stack(swa_kp),
            jnp.stack(swa_vp), jnp.stack(swa_ks), jnp.stack(swa_vs))
```

```python
import functools
import math

import jax
import jax.numpy as jnp
from jax import lax
from jax.experimental import pallas as pl
from jax.experimental.pallas import tpu as pltpu

F32 = jnp.float32
BF16 = jnp.bfloat16
I32 = jnp.int32

D = 1024
DEPTH = 4
CONV_W = 31
HALO = 32
HEADS = 16
HD = 64
KVH = 2
GROUP = HEADS // KVH
WINDOW = 128
CHUNK = 64
N_BUCKETS = 32
MAX_DISTANCE = 128
N_EXP = 32
TOP_K = 4
EXP_TILE = 256
LANES = 128
LNX_EPS = 64e-5
LN_EPS = 1e-5
ALPHA = (2 * DEPTH) ** 0.25
SWIGLU_ALPHA = 1.702
SWIGLU_LIMIT = 7.0
VMEM_LIMIT = 56 * 1024 * 1024


def _cp(sem):
    return pltpu.CompilerParams(dimension_semantics=sem, vmem_limit_bytes=VMEM_LIMIT)


def _bdot(a, b):
    return jnp.dot(a.astype(BF16), b.astype(BF16), preferred_element_type=F32)


def _split(a):
    hi = a.astype(BF16)
    lo = (a - hi.astype(F32)).astype(BF16)
    return hi, lo


def _split_dot(a, b_exact):
    hi, lo = _split(a)
    return (jnp.dot(hi, b_exact, preferred_element_type=F32)
            + jnp.dot(lo, b_exact, preferred_element_type=F32))


def _ln(x, g, b, eps=LN_EPS):
    mu = jnp.mean(x, axis=-1, keepdims=True)
    xc = x - mu
    var = jnp.mean(xc * xc, axis=-1, keepdims=True)
    return xc * lax.rsqrt(var + eps) * g + b


def _sigmoid(x):
    return 1.0 / (1.0 + jnp.exp(-x))


def _seq_call(body, *, name, nb, nt, tt, row_off, n_total, ins, outs, scratch, prev=None):
    off = row_off // tt
    in_specs, args = [], []
    for kind, a in ins:
        if kind == 'tok':
            in_specs.append(pl.BlockSpec((tt, a.shape[1]), lambda b, j: (off + b * nt + j, 0)))
        elif kind == 'bat':
            in_specs.append(pl.BlockSpec((1,) + a.shape[1:], lambda b, j: (b, 0, 0)))
        else:
            in_specs.append(pl.BlockSpec(a.shape, lambda b, j, _n=a.ndim: (0,) * _n))
        args.append(a)
    out_specs, out_shapes = [], []
    for kind, tail, dt in outs:
        if kind == 'tok':
            out_specs.append(pl.BlockSpec((tt, tail[0]), lambda b, j: (off + b * nt + j, 0)))
            out_shapes.append(jax.ShapeDtypeStruct((n_total, tail[0]), dt))
        else:
            out_specs.append(pl.BlockSpec((1,) + tuple(tail), lambda b, j: (b, 0, 0)))
            out_shapes.append(jax.ShapeDtypeStruct((nb,) + tuple(tail), dt))
    aliases = {}
    n_prev = 0
    tok_out = [i for i, o in enumerate(outs) if o[0] == 'tok']
    if prev is None:
        prev = [jnp.zeros((n_total, outs[i][1][0]), outs[i][2]) for i in tok_out]
    for p, oi in zip(prev, tok_out):
        aliases[len(args)] = oi
        in_specs.append(pl.BlockSpec(memory_space=pl.ANY))
        args.append(p)
        n_prev += 1
    n_in = len(ins)

    def wrapped(*refs):
        body(*refs[:n_in], *refs[n_in + n_prev:])

    return pl.pallas_call(
        wrapped, grid=(nb, nt), in_specs=in_specs, out_specs=out_specs, out_shape=out_shapes,
        scratch_shapes=scratch, input_output_aliases=aliases, name=name,
        compiler_params=_cp(("arbitrary", "arbitrary")))(*args)


CONV_RC = 32
CONV_LC = 512


def _conv_kernel(x_ref, st_ref, win_ref, bin_ref, wdw_ref, bdw_ref, cg_ref, cb_ref, wout_ref,
                 bout_ref, lg_ref, lb_ref, o_ref, so_ref, ubuf, ybuf, *, tt, nt):
    j = pl.program_id(1)

    @pl.when(j == 0)
    def _():
        ubuf[0:HALO, :] = st_ref[0]

    @pl.when(j > 0)
    def _():
        ubuf[0:HALO, :] = ubuf[tt:tt + HALO, :]

    x = x_ref[...]
    h = _bdot(x, win_ref[...]) + bin_ref[...]
    ubuf[HALO:HALO + tt, :] = h[:, :D] * _sigmoid(h[:, D:])
    first = HALO - (CONV_W - 1)
    for r0 in range(0, tt, CONV_RC):
        for c0 in range(0, D, CONV_LC):
            y = jnp.zeros((CONV_RC, CONV_LC), F32) + bdw_ref[:, c0:c0 + CONV_LC]
            for off in range(8):
                rows = CONV_RC + (8 if off else 0)
                acc = None
                for tap in range(CONV_W):
                    if (first + tap) % 8 != off:
                        continue
                    base = r0 + (first + tap) // 8 * 8
                    term = wdw_ref[tap:tap + 1, c0:c0 + CONV_LC] * ubuf[base:base + rows, c0:c0 + CONV_LC]
                    acc = term if acc is None else acc + term
                y = y + acc[off:off + CONV_RC]
            ybuf[r0:r0 + CONV_RC, c0:c0 + CONV_LC] = y
    z = _ln(ybuf[...], cg_ref[...], cb_ref[...])
    z = z * _sigmoid(z)
    mix = _bdot(z, wout_ref[...]) + bout_ref[...]
    o_ref[...] = _ln(ALPHA * x + mix, lg_ref[...], lb_ref[...])

    @pl.when(j == nt - 1)
    def _():
        so_ref[0] = ubuf[tt:tt + HALO, :]


def _conv_mixer(x, state, w, lnp, *, nb, t, tt, row_off, prev):
    nt = t // tt
    n_total = x.shape[0]
    win, bin_, wdw, bdw, cg, cb, wout, bout = w
    ins = [('tok', x), ('bat', state), ('const', win), ('const', bin_), ('const', wdw), ('const', bdw),
           ('const', cg), ('const', cb), ('const', wout), ('const', bout), ('const', lnp[0]), ('const', lnp[1])]
    outs = [('tok', (D,), F32), ('bat', (HALO, D), F32)]
    scratch = [pltpu.VMEM((HALO + tt, D), F32), pltpu.VMEM((tt, D), F32)]
    return _seq_call(functools.partial(_conv_kernel, tt=tt, nt=nt), name=f"conv_t{t}", nb=nb, nt=nt, tt=tt,
                     row_off=row_off,
                     n_total=n_total, ins=ins, outs=outs, scratch=scratch, prev=prev)


def _head_sum(y, hs_ref, hst_ref):
    s = _split_dot(y, hs_ref[...])
    return _split_dot(s, hst_ref[...])


def _rw1_kernel(x_ref, sh_ref, mu_ref, wr_ref, wk_ref, wv_ref, w0_ref, w1_ref, w2_ref, a0_ref, a1_ref,
                a2_ref, g1_ref, g2_ref, kk_ref, ka_ref, hs_ref, hst_ref,
                r_ref, w_ref, k_ref, v_ref, an_ref, b_ref, g_ref, xbuf, *, tt):
    j = pl.program_id(1)

    @pl.when(j == 0)
    def _():
        xbuf[7:8, :] = sh_ref[0]

    @pl.when(j > 0)
    def _():
        xbuf[7:8, :] = xbuf[7 + tt:8 + tt, :]

    x = x_ref[...]
    xbuf[8:8 + tt, :] = x
    xx = xbuf[7:7 + tt, :] - x
    mu = mu_ref[...]
    xr = x + xx * mu[0:1]
    xw = x + xx * mu[1:2]
    xk = x + xx * mu[2:3]
    xv = x + xx * mu[3:4]
    xa = x + xx * mu[4:5]
    xg = x + xx * mu[5:6]
    r = _bdot(xr, wr_ref[...])
    k = _bdot(xk, wk_ref[...])
    v = _bdot(xv, wv_ref[...])
    lw = w0_ref[...] + _bdot(jnp.tanh(_bdot(xw, w1_ref[...])), w2_ref[...])
    z = -lw
    log_w = -(jnp.maximum(z, 0.0) + jnp.log(1.0 + jnp.exp(-jnp.abs(z)))) - 0.5
    a = _sigmoid(a0_ref[...] + _bdot(_bdot(xa, a1_ref[...]), a2_ref[...]))
    g = _bdot(_sigmoid(_bdot(xg, g1_ref[...])), g2_ref[...])
    kk = k * kk_ref[...]
    ss = _head_sum(kk * kk, hs_ref, hst_ref)
    kk = kk * lax.rsqrt(jnp.maximum(ss, 1e-24))
    r_ref[...] = r
    w_ref[...] = jnp.exp(-jnp.exp(log_w))
    k_ref[...] = k * (1.0 + (a - 1.0) * ka_ref[...])
    v_ref[...] = v
    an_ref[...] = -kk
    b_ref[...] = kk * a
    g_ref[...] = g


def _rw1(x, shift, w, hs, hst, *, nb, t, tt, row_off, prev):
    nt = t // tt
    ins = [('tok', x), ('bat', shift)] + [('const', a) for a in w] + [('const', hs), ('const', hst)]
    outs = [('tok', (D,), F32)] * 7
    scratch = [pltpu.VMEM((8 + tt, D), F32)]
    return _seq_call(functools.partial(_rw1_kernel, tt=tt), name=f"rwkv_proj_t{t}", nb=nb, nt=nt, tt=tt,
                     row_off=row_off,
                     n_total=x.shape[0], ins=ins, outs=outs, scratch=scratch, prev=prev)


def _scan_kernel(r_ref, w_ref, k_ref, v_ref, an_ref, b_ref, a0_ref, s0_ref, o_ref, st_ref, S, sa_buf,
                 *, tc, nc):
    c = pl.program_id(1)

    @pl.when(c == 0)
    def _():
        S[...] = s0_ref[...]

        def init(kk, acc):
            return acc + S[kk] * a0_ref[pl.ds(kk, 1), :]

        sa_buf[...] = lax.fori_loop(0, HD, init, jnp.zeros((HD, LANES), F32))

    def step(t, sa):
        vt = v_ref[t]

        def kbody(kk, acc):
            o_acc, sa_acc = acc
            sk = (S[kk] * w_ref[t, pl.ds(kk, 1), :] + sa * b_ref[t, pl.ds(kk, 1), :]
                  + vt * k_ref[t, pl.ds(kk, 1), :])
            S[kk] = sk
            return (o_acc + sk * r_ref[t, pl.ds(kk, 1), :], sa_acc + sk * an_ref[t, pl.ds(kk, 1), :])

        zero = jnp.zeros((HD, LANES), F32)
        o_acc, sa_next = lax.fori_loop(0, HD, kbody, (zero, zero), unroll=4)
        o_ref[t] = o_acc
        return sa_next

    sa_buf[...] = lax.fori_loop(0, tc, step, sa_buf[...])

    @pl.when(c == nc - 1)
    def _():
        st_ref[...] = S[...]


def _scan(r, w, k, v, an, b, a0, s0, *, tc):
    t, _, lanes = r.shape
    ng, nc = lanes // LANES, t // tc
    seq = pl.BlockSpec((tc, HD, LANES), lambda g, c: (c, 0, g))
    return pl.pallas_call(
        functools.partial(_scan_kernel, tc=tc, nc=nc), grid=(ng, nc),
        in_specs=[seq] * 6 + [pl.BlockSpec((HD, LANES), lambda g, c: (0, g)),
                              pl.BlockSpec((HD, HD, LANES), lambda g, c: (0, 0, g))],
        out_specs=[seq, pl.BlockSpec((HD, HD, LANES), lambda g, c: (0, 0, g))],
        out_shape=[jax.ShapeDtypeStruct((t, HD, lanes), F32), jax.ShapeDtypeStruct((HD, HD, lanes), F32)],
        scratch_shapes=[pltpu.VMEM((HD, HD, LANES), F32), pltpu.VMEM((HD, LANES), F32)],
        name=f"wkv_scan_t{t}", compiler_params=_cp(("arbitrary", "arbitrary")))(r, w, k, v, an, b, a0, s0)


def _rw3_kernel(o_ref, r_ref, k_ref, v_ref, g_ref, x_ref, rk_ref, xg_ref, xb_ref, wo_ref, hs_ref, hst_ref,
                lg_ref, lb_ref, out_ref):
    o = o_ref[...]
    mo = _head_sum(o, hs_ref, hst_ref) * (1.0 / HD)
    d = o - mo
    vo = _head_sum(d * d, hs_ref, hst_ref) * (1.0 / HD)
    on = d * lax.rsqrt(vo + LNX_EPS) * xg_ref[...] + xb_ref[...]
    bonus = _head_sum(r_ref[...] * k_ref[...] * rk_ref[...], hs_ref, hst_ref) * v_ref[...]
    mix = _bdot((on + bonus) * g_ref[...], wo_ref[...])
    out_ref[...] = _ln(ALPHA * x_ref[...] + mix, lg_ref[...], lb_ref[...])


def _head_major_spec(tt):
    return pl.BlockSpec((HEADS, tt, HD), lambda i: (0, i, 0))


def _tok_call(body, name, n, tt, toks, consts, outs):
    in_specs = [_head_major_spec(tt) if a.ndim == 3 else pl.BlockSpec((tt, a.shape[1]), lambda i: (i, 0))
                for a in toks]
    in_specs += [pl.BlockSpec(a.shape, lambda i, _n=a.ndim: (0,) * _n) for a in consts]
    out_specs = [_head_major_spec(tt) if w == 'heads' else pl.BlockSpec((tt, w), lambda i: (i, 0))
                 for w, _ in outs]
    out_shape = [jax.ShapeDtypeStruct((HEADS, n, HD) if w == 'heads' else (n, w), dt) for w, dt in outs]
    return pl.pallas_call(
        body, grid=(n // tt,), in_specs=in_specs, out_specs=out_specs, out_shape=out_shape, name=name,
        compiler_params=_cp(("arbitrary",)))(*toks, *consts)


def _qkv_kernel(x_ref, w_ref, b_ref, q_ref, k_ref, v_ref):
    h = _bdot(x_ref[...], w_ref[...]) + b_ref[...]
    for hd in range(HEADS):
        q_ref[hd] = h[:, hd * HD:(hd + 1) * HD].astype(BF16)
    k_ref[...] = h[:, HEADS * HD:HEADS * HD + KVH * HD]
    v_ref[...] = h[:, HEADS * HD + KVH * HD:]


def _attn_kernel(q_ref, kp_ref, vp_ref, bias_ref, sink_ref, prev_ref, o_ref, *, nq, kb, stride, mask_lo):
    del prev_ref
    c = pl.program_id(1)
    start = pl.multiple_of(c * stride, 8)
    kband = kp_ref[0, pl.ds(start, kb), :].astype(BF16)
    vband = vp_ref[0, pl.ds(start, kb), :].astype(BF16)
    valid = (start + lax.broadcasted_iota(I32, (1, kb), 1)) >= mask_lo
    for g in range(KVH):
        hs = slice(g * GROUP, (g + 1) * GROUP)
        qg = q_ref[hs].reshape(GROUP * nq, HD)
        kh = kband[:, g * HD:(g + 1) * HD]
        vh = vband[:, g * HD:(g + 1) * HD]
        logits = lax.dot_general(qg, kh, (((1,), (1,)), ((), ())), preferred_element_type=F32) * HD ** -0.5
        logits = jnp.where(valid, logits + bias_ref[hs].reshape(GROUP * nq, kb), -1e30)
        sink = sink_ref[hs].reshape(GROUP * nq, 1)
        m = jnp.maximum(jnp.max(logits, axis=-1, keepdims=True), sink)
        p = jnp.exp(logits - m)
        p = p / (jnp.sum(p, axis=-1, keepdims=True) + jnp.exp(sink - m))
        og = jnp.dot(p.astype(BF16), vh, preferred_element_type=F32)
        o_ref[hs] = og.reshape(GROUP, nq, HD).astype(BF16)


def _attn(q, kp, vp, bias, sinks, *, nb, nt, nq, kb, stride, mask_lo, row_off, prev):
    off = row_off // nq
    n_total = q.shape[1]
    heads = pl.BlockSpec((HEADS, nq, HD), lambda b, c: (0, off + b * nt + c, 0))
    const = lambda a: pl.BlockSpec(a.shape, lambda b, c, _n=a.ndim: (0,) * _n)
    bat = lambda a: pl.BlockSpec((1,) + a.shape[1:], lambda b, c: (b, 0, 0))
    sink_tab = jnp.broadcast_to(sinks.reshape(HEADS, 1, 1), (HEADS, nq, 1))
    if prev is None:
        prev = jnp.zeros((HEADS, n_total, HD), BF16)
    return pl.pallas_call(
        functools.partial(_attn_kernel, nq=nq, kb=kb, stride=stride, mask_lo=mask_lo), grid=(nb, nt),
        in_specs=[heads, bat(kp), bat(vp), const(bias), const(sink_tab), pl.BlockSpec(memory_space=pl.ANY)],
        out_specs=heads, out_shape=jax.ShapeDtypeStruct((HEADS, n_total, HD), BF16),
        input_output_aliases={5: 0}, name=f"attn_q{nq}",
        compiler_params=_cp(("arbitrary", "arbitrary")))(q, kp, vp, bias, sink_tab, prev)


def _oproj_kernel(o_ref, x_ref, wo_ref, bo_ref, lg_ref, lb_ref, out_ref):
    acc = jnp.dot(o_ref[0], wo_ref[0:HD, :], preferred_element_type=F32)
    for hd in range(1, HEADS):
        acc = acc + jnp.dot(o_ref[hd], wo_ref[hd * HD:(hd + 1) * HD, :], preferred_element_type=F32)
    out_ref[...] = _ln(ALPHA * x_ref[...] + acc + bo_ref[...], lg_ref[...], lb_ref[...])


def _t5_bucket(rel):
    half = N_BUCKETS // 2
    max_exact = half // 2
    ret = jnp.where(rel > 0, half, 0)
    n = jnp.abs(rel)
    nf = jnp.maximum(n, 1).astype(F32)
    large = max_exact + (jnp.log(nf / max_exact) / math.log(MAX_DISTANCE / max_exact)
                         * (half - max_exact)).astype(I32)
    large = jnp.minimum(large, half - 1)
    return ret + jnp.where(n < max_exact, n, large)


def _t5_bias(rel_bias, n_q, n_k):
    rel = jnp.arange(n_k)[None, :] - WINDOW - jnp.arange(n_q)[:, None]
    return jnp.transpose(rel_bias[_t5_bucket(rel)], (2, 0, 1))


def _route_kernel(x_ref, wr_ref, br_ref, idx_ref, gate_ref, rank_ref, cnt_ref, carry, *, tt):
    i = pl.program_id(0)

    @pl.when(i == 0)
    def _():
        carry[...] = jnp.zeros_like(carry)

    xh, xl = _split(x_ref[...])
    wh, wl = _split(wr_ref[...])
    logits = (jnp.dot(xh, wh, preferred_element_type=F32) + jnp.dot(xl, wh, preferred_element_type=F32)
              + jnp.dot(xh, wl, preferred_element_type=F32)) + br_ref[...]
    lane = lax.broadcasted_iota(I32, (tt, N_EXP), 1)
    out_lane = lax.broadcasted_iota(I32, (tt, LANES), 1)
    vals, sels = [], []
    idx_out = jnp.zeros((tt, LANES), I32)
    work = logits
    for k in range(TOP_K):
        m = jnp.max(work, axis=-1, keepdims=True)
        ik = jnp.min(jnp.where(work == m, lane, N_EXP), axis=-1, keepdims=True)
        sel = lane == ik
        vals.append(m)
        sels.append(sel)
        idx_out = jnp.where(out_lane == k, ik, idx_out)
        work = jnp.where(sel, -jnp.inf, work)
    es = [jnp.exp(v - vals[0]) for v in vals]
    den = es[0] + es[1] + es[2] + es[3]
    gate_out = jnp.zeros((tt, LANES), F32)
    for k in range(TOP_K):
        gate_out = jnp.where(out_lane == k, es[k] / den, gate_out)
    onehot = jnp.zeros((tt, N_EXP), F32)
    for sel in sels:
        onehot = onehot + sel.astype(F32)
    tri = (lax.broadcasted_iota(I32, (tt, tt), 0) > lax.broadcasted_iota(I32, (tt, tt), 1)).astype(BF16)
    base = carry[...] + jnp.dot(tri, onehot.astype(BF16), preferred_element_type=F32)
    rank_out = jnp.zeros((tt, LANES), I32)
    for k in range(TOP_K):
        rk = jnp.sum(jnp.where(sels[k], base, 0.0), axis=-1, keepdims=True)
        rank_out = jnp.where(out_lane == k, rk.astype(I32), rank_out)
    carry[...] = carry[...] + jnp.sum(onehot, axis=0, keepdims=True)
    idx_ref[...] = idx_out
    gate_ref[...] = gate_out
    rank_ref[...] = rank_out
    cnt_ref[...] = carry[...]


def _route(x1, wr, br, *, tt):
    n = x1.shape[0]
    tokspec = lambda w: pl.BlockSpec((tt, w), lambda i: (i, 0))
    const = lambda a: pl.BlockSpec(a.shape, lambda i, _n=a.ndim: (0,) * _n)
    return pl.pallas_call(
        functools.partial(_route_kernel, tt=tt), grid=(n // tt,),
        in_specs=[tokspec(D), const(wr), const(br)],
        out_specs=[tokspec(LANES), tokspec(LANES), tokspec(LANES), pl.BlockSpec((1, N_EXP), lambda i: (0, 0))],
        out_shape=[jax.ShapeDtypeStruct((n, LANES), I32), jax.ShapeDtypeStruct((n, LANES), F32),
                   jax.ShapeDtypeStruct((n, LANES), I32), jax.ShapeDtypeStruct((1, N_EXP), F32)],
        scratch_shapes=[pltpu.VMEM((1, N_EXP), F32)], name="moe_route",
        compiler_params=_cp(("arbitrary",)))(x1, wr, br)


def _row_copy(src, s, dst, d, sem):
    return pltpu.make_async_copy(src.at[pl.ds(s, 1), :], dst.at[pl.ds(d, 1), :], sem)


ROW_UNROLL = 4


def _disp_kernel(dest_ref, x_ref, xs_in, xs_ref, sem, *, tt):
    del xs_in

    def issue(r, carry):
        for k in range(TOP_K):
            _row_copy(x_ref, r, xs_ref, dest_ref[r * TOP_K + k], sem).start(priority=k % 2)
        return carry

    lax.fori_loop(0, tt, issue, 0, unroll=ROW_UNROLL)

    def drain(r, carry):
        for k in range(TOP_K):
            _row_copy(x_ref, 0, xs_ref, 0, sem).wait()
        return carry

    lax.fori_loop(0, tt, drain, 0, unroll=ROW_UNROLL)


def _dispatch(dest, x1, xs_prev, *, tt):
    n = x1.shape[0]
    smem_tok = pl.BlockSpec((tt * TOP_K,), lambda i: (i,), memory_space=pltpu.SMEM)
    return pl.pallas_call(
        functools.partial(_disp_kernel, tt=tt), grid=(n // tt,),
        in_specs=[smem_tok, pl.BlockSpec((tt, D), lambda i: (i, 0)), pl.BlockSpec(memory_space=pl.ANY)],
        out_specs=pl.BlockSpec(memory_space=pl.ANY),
        out_shape=jax.ShapeDtypeStruct(xs_prev.shape, F32),
        scratch_shapes=[pltpu.SemaphoreType.DMA],
        input_output_aliases={2: 0}, name="moe_dispatch",
        compiler_params=_cp(("arbitrary",)))(dest, x1, xs_prev)


def _expert_kernel(te_ref, nv_ref, xs_ref, w1_ref, b1_ref, w2_ref, b2_ref, y_ref, w1b, w2b):
    i = pl.program_id(0)
    valid = i < nv_ref[0]
    changed = jnp.logical_or(i == 0, te_ref[i] != te_ref[jnp.maximum(i - 1, 0)])

    @pl.when(jnp.logical_and(valid, changed))
    def _():
        for r0 in range(0, D, 256):
            w1b[r0:r0 + 256, :] = w1_ref[0, r0:r0 + 256, :].astype(BF16)
            w2b[r0:r0 + 256, :] = w2_ref[0, r0:r0 + 256, :].astype(BF16)

    @pl.when(valid)
    def _():
        h = jnp.dot(xs_ref[...].astype(BF16), w1b[...], preferred_element_type=F32) + b1_ref[0]
        glu = jnp.minimum(h[:, :D], SWIGLU_LIMIT)
        lin = jnp.clip(h[:, D:], -SWIGLU_LIMIT, SWIGLU_LIMIT)
        act = glu * _sigmoid(SWIGLU_ALPHA * glu) * (lin + 1.0)
        y_ref[...] = jnp.dot(act.astype(BF16), w2b[...], preferred_element_type=F32) + b2_ref[0]

    @pl.when(jnp.logical_not(valid))
    def _():
        y_ref[...] = jnp.zeros_like(y_ref)


def _experts(tile_expert, n_valid, xs, w1, b1, w2, b2):
    n_tiles = xs.shape[0] // EXP_TILE
    grid_spec = pltpu.PrefetchScalarGridSpec(
        num_scalar_prefetch=2, grid=(n_tiles,),
        in_specs=[pl.BlockSpec((EXP_TILE, D), lambda i, te, nv: (i, 0)),
                  pl.BlockSpec((1, D, 2 * D), lambda i, te, nv: (te[i], 0, 0)),
                  pl.BlockSpec((1, 1, 2 * D), lambda i, te, nv: (te[i], 0, 0)),
                  pl.BlockSpec((1, D, D), lambda i, te, nv: (te[i], 0, 0)),
                  pl.BlockSpec((1, 1, D), lambda i, te, nv: (te[i], 0, 0))],
        out_specs=pl.BlockSpec((EXP_TILE, D), lambda i, te, nv: (i, 0)),
        scratch_shapes=[pltpu.VMEM((D, 2 * D), BF16), pltpu.VMEM((D, D), BF16)])
    return pl.pallas_call(
        _expert_kernel, grid_spec=grid_spec, out_shape=jax.ShapeDtypeStruct(xs.shape, F32), name="moe_experts",
        compiler_params=_cp(("arbitrary",)))(tile_expert, n_valid, xs, w1, b1, w2, b2)


def _comb_kernel(dest_ref, gate_ref, x1_ref, p_ref, y_ref, lg_ref, lb_ref, wg_ref, bg_ref,
                 wp_ref, o_ref, buf, sem, *, tt):
    def issue(r, carry):
        for k in range(TOP_K):
            _row_copy(y_ref, dest_ref[r * TOP_K + k], buf.at[k], r, sem).start(priority=k % 2)
        return carry

    lax.fori_loop(0, tt, issue, 0, unroll=ROW_UNROLL)

    def drain(r, carry):
        for k in range(TOP_K):
            _row_copy(y_ref, 0, buf.at[k], 0, sem).wait()
        return carry

    lax.fori_loop(0, tt, drain, 0, unroll=ROW_UNROLL)
    gate = gate_ref[...]
    moe = gate[:, 0:1] * buf[0]
    for k in range(1, TOP_K):
        moe = moe + gate[:, k:k + 1] * buf[k]
    x2 = _ln(ALPHA * x1_ref[...] + moe, lg_ref[...], lb_ref[...])
    gt = _sigmoid(_bdot(x2, wg_ref[...]) + bg_ref[...])
    o_ref[...] = x2 + gt * _bdot(p_ref[...], wp_ref[...])


def _combine(dest, gate, x1, p, y, lnp, wg, bg, wp, *, tt):
    n = x1.shape[0]
    smem_tok = pl.BlockSpec((tt * TOP_K,), lambda i: (i,), memory_space=pltpu.SMEM)
    tok = lambda w: pl.BlockSpec((tt, w), lambda i: (i, 0))
    const = lambda a: pl.BlockSpec(a.shape, lambda i, _n=a.ndim: (0,) * _n)
    return pl.pallas_call(
        functools.partial(_comb_kernel, tt=tt), grid=(n // tt,),
        in_specs=[smem_tok, tok(LANES), tok(D), tok(p.shape[1]), pl.BlockSpec(memory_space=pl.ANY),
                  const(lnp[0]), const(lnp[1]), const(wg), const(bg), const(wp)],
        out_specs=tok(D), out_shape=jax.ShapeDtypeStruct((n, D), F32),
        scratch_shapes=[pltpu.VMEM((TOP_K, tt, D), F32), pltpu.SemaphoreType.DMA], name="moe_combine_ple",
        compiler_params=_cp(("arbitrary",)))(dest, gate, x1, p, y, lnp[0], lnp[1], wg, bg, wp)


def _moe_ple(x1, p, xs_buf, wr, br, w1, b1, w2, b2, lnp, wg, bg, wp, *, tt):
    idx, gate, rank, counts = _route(x1, wr, br, tt=tt)
    counts = counts[0].astype(I32)
    padded = (counts + EXP_TILE - 1) // EXP_TILE * EXP_TILE
    pad_end = jnp.cumsum(padded)
    pad_start = pad_end - padded
    n_tiles = xs_buf.shape[0] // EXP_TILE
    n_valid = (pad_end[-1] // EXP_TILE).astype(I32)
    tiles = jnp.minimum(jnp.arange(n_tiles, dtype=I32), n_valid - 1) * EXP_TILE
    tile_expert = jnp.minimum(jnp.sum((tiles[:, None] >= pad_end[None, :]).astype(I32), axis=1), N_EXP - 1)
    experts = jnp.arange(N_EXP, dtype=I32)
    start_of = jnp.sum(jnp.where(idx[:, :TOP_K, None] == experts, pad_start, 0), axis=-1)
    dest = (start_of + rank[:, :TOP_K]).reshape(-1)
    xs = _dispatch(dest, x1, xs_buf, tt=tt)
    y = _experts(tile_expert, n_valid.reshape(1), xs, w1, b1.reshape(N_EXP, 1, 2 * D), w2,
                 b2.reshape(N_EXP, 1, D))
    return _combine(dest, gate, x1, p, y, lnp, wg, bg, wp, tt=tt), xs


def _to_scan(a, nb, t):
    a = a.reshape(nb, t, HEADS, HD).transpose(1, 3, 0, 2).reshape(t, HD, nb * HEADS)
    pad = (-nb * HEADS) % LANES
    return jnp.pad(a, ((0, 0), (0, 0), (0, pad))) if pad else a


def _from_scan(o, nb, t):
    return o[:, :, :nb * HEADS].reshape(t, HD, nb, HEADS).transpose(2, 0, 3, 1).reshape(nb * t, D)


def _row2(a):
    return a.reshape(1, -1)


def _tile(n):
    for tt in (256, 128, 64, 32, 16, 8):
        if n % tt == 0:
            return tt
    raise ValueError(n)


def kernel(x_prompt, x_sample, p_prompt, p_sample, cache_conv, state_rwkv_shift, state_rwkv_wkv, cache_swa_k, cache_swa_v, conv_w_in, conv_b_in, conv_w_dw, conv_b_dw, conv_ln_g, conv_ln_b, conv_w_out, conv_b_out, rwkv_mu, rwkv_w_rkv, rwkv_w0, rwkv_w1, rwkv_w2, rwkv_a0, rwkv_a1, rwkv_a2, rwkv_g1, rwkv_g2, rwkv_k_k, rwkv_k_a, rwkv_r_k, rwkv_lnx_g, rwkv_lnx_b, rwkv_w_o, attn_w_qkv, attn_b_qkv, attn_sinks, attn_w_o, attn_b_o, rel_bias, ln_g, ln_b, moe_w_router, moe_b_router, moe_w1, moe_b1, moe_w2, moe_b2, ple_w_proj, ple_w_gate, ple_b_gate):
    bp, tp, _ = x_prompt.shape
    bs, ts, _ = x_sample.shape
    n_p, n_s = bp * tp, bs * ts
    n = n_p + n_s
    assert tp % 128 == 0 and n_p % ts == 0 and ts % 8 == 0 and ts <= CHUNK
    tt_tok = _tile(n)
    tt_p = 128
    x = jnp.concatenate([x_prompt.reshape(n_p, D), x_sample.reshape(n_s, D)], axis=0)
    p_all = jnp.concatenate([p_prompt.reshape(DEPTH, n_p, -1), p_sample.reshape(DEPTH, n_s, -1)], axis=1)
    n_rows = (-(-n * TOP_K // EXP_TILE) + N_EXP) * EXP_TILE
    xs_buf = jnp.zeros((n_rows, D), F32)
    head_sel = (jnp.arange(D)[:, None] // HD == jnp.arange(LANES)[None, :]).astype(BF16)
    head_sel_t = head_sel.T
    conv_p, conv_s, shift_p, shift_s, wkv_p, wkv_s = [], [], [], [], [], []
    swa_kp, swa_vp, swa_ks, swa_vs = [], [], [], []
    for i in range(DEPTH):
        kind, j = i % 3, i // 3
        lnp = (_row2(ln_g[i, 0]), _row2(ln_b[i, 0]))
        if kind == 0:
            cw = (conv_w_in[j].astype(BF16), _row2(conv_b_in[j]), conv_w_dw[j], _row2(conv_b_dw[j]),
                  _row2(conv_ln_g[j]), _row2(conv_ln_b[j]), conv_w_out[j].astype(BF16), _row2(conv_b_out[j]))
            st_p = jnp.zeros((bp, HALO, D), F32)
            st_s = jnp.pad(cache_conv[j], ((0, 0), (HALO - (CONV_W - 1), 0), (0, 0)))
            x1, so_p = _conv_mixer(x, st_p, cw, lnp, nb=bp, t=tp, tt=tt_p, row_off=0, prev=None)
            x1, so_s = _conv_mixer(x, st_s, cw, lnp, nb=bs, t=ts, tt=ts, row_off=n_p, prev=[x1])
            conv_p.append(so_p[:, HALO - (CONV_W - 1):])
            conv_s.append(so_s[:, HALO - (CONV_W - 1):])
        elif kind == 1:
            rw = (rwkv_mu[j], rwkv_w_rkv[j, 0].astype(BF16), rwkv_w_rkv[j, 1].astype(BF16),
                  rwkv_w_rkv[j, 2].astype(BF16), _row2(rwkv_w0[j]), rwkv_w1[j].astype(BF16),
                  rwkv_w2[j].astype(BF16), _row2(rwkv_a0[j]), rwkv_a1[j].astype(BF16), rwkv_a2[j].astype(BF16),
                  rwkv_g1[j].astype(BF16), rwkv_g2[j].astype(BF16), _row2(rwkv_k_k[j]), _row2(rwkv_k_a[j]))
            sh_p = jnp.zeros((bp, 1, D), F32)
            sh_s = state_rwkv_shift[j].reshape(bs, 1, D)
            proj = _rw1(x, sh_p, rw, head_sel, head_sel_t, nb=bp, t=tp, tt=tt_p, row_off=0, prev=None)
            proj = _rw1(x, sh_s, rw, head_sel, head_sel_t, nb=bs, t=ts, tt=ts, row_off=n_p, prev=list(proj))
            r, w, k, v, an, b, g = proj
            o_parts, states = [], []
            for (lo, nb_, t_, s0, tc) in ((0, bp, tp, None, 64), (n_p, bs, ts, state_rwkv_wkv[j], ts)):
                sl = slice(lo, lo + nb_ * t_)
                rs, ws, ks, vs, ans, bs_ = (_to_scan(a[sl], nb_, t_) for a in (r, w, k, v, an, b))
                lanes = rs.shape[2]
                if s0 is None:
                    s0l = jnp.zeros((HD, HD, lanes), F32)
                else:
                    s0l = s0.transpose(3, 2, 0, 1).reshape(HD, HD, nb_ * HEADS)
                    s0l = jnp.pad(s0l, ((0, 0), (0, 0), (0, lanes - nb_ * HEADS)))
                a_next = jnp.concatenate([ans[1:], jnp.zeros_like(ans[:1])], axis=0)
                o_l, s_l = _scan(rs, ws, ks, vs, a_next, bs_, ans[0], s0l, tc=tc)
                o_parts.append(_from_scan(o_l, nb_, t_))
                states.append(s_l[:, :, :nb_ * HEADS].reshape(HD, HD, nb_, HEADS).transpose(2, 3, 1, 0))
            o = jnp.concatenate(o_parts, axis=0)
            consts = [_row2(rwkv_r_k[j].reshape(-1)), _row2(rwkv_lnx_g[j]), _row2(rwkv_lnx_b[j]),
                      rwkv_w_o[j].astype(BF16), head_sel, head_sel_t, lnp[0], lnp[1]]
            x1, = _tok_call(_rw3_kernel, "rwkv_out", n, tt_tok, [o, r, k, v, g, x], consts, [(D, F32)])
            shift_p.append(x[:n_p].reshape(bp, tp, D)[:, -1])
            shift_s.append(x[n_p:].reshape(bs, ts, D)[:, -1])
            wkv_p.append(states[0])
            wkv_s.append(states[1])
        else:
            q, kx, vx = _tok_call(_qkv_kernel, "attn_qkv", n, tt_tok, [x],
                                  [attn_w_qkv[j].astype(BF16), _row2(attn_b_qkv[j])],
                                  [('heads', BF16), (KVH * HD, F32), (KVH * HD, F32)])
            k_p = kx[:n_p].reshape(bp, tp, KVH * HD)
            v_p = vx[:n_p].reshape(bp, tp, KVH * HD)
            zpad = jnp.zeros((bp, WINDOW, KVH * HD), F32)
            nc = tp // CHUNK
            band = WINDOW + CHUNK
            o = _attn(q, jnp.concatenate([zpad, k_p], axis=1), jnp.concatenate([zpad, v_p], axis=1),
                      _t5_bias(rel_bias, CHUNK, band), attn_sinks[j], nb=bp, nt=nc, nq=CHUNK, kb=band,
                      stride=CHUNK, mask_lo=WINDOW, row_off=0, prev=None)
            k_all = jnp.concatenate([cache_swa_k[j].reshape(bs, WINDOW, KVH * HD),
                                     kx[n_p:].reshape(bs, ts, KVH * HD)], axis=1)
            v_all = jnp.concatenate([cache_swa_v[j].reshape(bs, WINDOW, KVH * HD),
                                     vx[n_p:].reshape(bs, ts, KVH * HD)], axis=1)
            o = _attn(q, k_all, v_all, _t5_bias(rel_bias, ts, WINDOW + ts), attn_sinks[j],
                      nb=bs, nt=1, nq=ts, kb=WINDOW + ts, stride=0, mask_lo=0, row_off=n_p, prev=o)
            x1, = _tok_call(_oproj_kernel, "attn_out", n, tt_tok, [o, x],
                            [attn_w_o[j].astype(BF16), _row2(attn_b_o[j]), lnp[0], lnp[1]], [(D, F32)])
            swa_kp.append(k_p[:, -WINDOW:].reshape(bp, WINDOW, KVH, HD))
            swa_vp.append(v_p[:, -WINDOW:].reshape(bp, WINDOW, KVH, HD))
            swa_ks.append(k_all[:, -WINDOW:].reshape(bs, WINDOW, KVH, HD))
            swa_vs.append(v_all[:, -WINDOW:].reshape(bs, WINDOW, KVH, HD))
        x, xs_buf = _moe_ple(x1, p_all[i], xs_buf, moe_w_router[i], _row2(moe_b_router[i]), moe_w1[i], moe_b1[i],
                             moe_w2[i], moe_b2[i], (_row2(ln_g[i, 1]), _row2(ln_b[i, 1])),
                             ple_w_gate[i].astype(BF16), _row2(ple_b_gate[i]), ple_w_proj[i].astype(BF16), tt=tt_tok)
    return (x[:n_p].reshape(bp, tp, D), x[n_p:].reshape(bs, ts, D), jnp.stack(conv_p), jnp.stack(conv_s),
            jnp.stack(shift_p), jnp.stack(shift_s), jnp.stack(wkv_p), jnp.stack(wkv_s), jnp.stack(swa_kp),
            jnp.stack(swa_vp), jnp.stack(swa_ks), jnp.stack(swa_vs))
```

```python
import functools
import math

import jax
import jax.numpy as jnp
from jax import lax
from jax.experimental import pallas as pl
from jax.experimental.pallas import tpu as pltpu

F32 = jnp.float32
BF16 = jnp.bfloat16
I32 = jnp.int32

D = 1024
DEPTH = 4
CONV_W = 31
HALO = 32
HEADS = 16
HD = 64
KVH = 2
GROUP = HEADS // KVH
WINDOW = 128
CHUNK = 64
N_BUCKETS = 32
MAX_DISTANCE = 128
N_EXP = 32
TOP_K = 4
EXP_TILE = 256
LANES = 128
LNX_EPS = 64e-5
LN_EPS = 1e-5
ALPHA = (2 * DEPTH) ** 0.25
SWIGLU_ALPHA = 1.702
SWIGLU_LIMIT = 7.0
VMEM_LIMIT = 56 * 1024 * 1024


def _cp(sem):
    return pltpu.CompilerParams(dimension_semantics=sem, vmem_limit_bytes=VMEM_LIMIT)


def _bdot(a, b):
    return jnp.dot(a.astype(BF16), b.astype(BF16), preferred_element_type=F32)


def _split(a):
    hi = a.astype(BF16)
    lo = (a - hi.astype(F32)).astype(BF16)
    return hi, lo


def _split_dot(a, b_exact):
    hi, lo = _split(a)
    return (jnp.dot(hi, b_exact, preferred_element_type=F32)
            + jnp.dot(lo, b_exact, preferred_element_type=F32))


def _ln(x, g, b, eps=LN_EPS):
    mu = jnp.mean(x, axis=-1, keepdims=True)
    xc = x - mu
    var = jnp.mean(xc * xc, axis=-1, keepdims=True)
    return xc * lax.rsqrt(var + eps) * g + b


def _sigmoid(x):
    return 1.0 / (1.0 + jnp.exp(-x))


def _seq_call(body, *, name, nb, nt, tt, row_off, n_total, ins, outs, scratch, prev=None):
    off = row_off // tt
    in_specs, args = [], []
    for kind, a in ins:
        if kind == 'tok':
            in_specs.append(pl.BlockSpec((tt, a.shape[1]), lambda b, j: (off + b * nt + j, 0)))
        elif kind == 'own':
            in_specs.append(pl.BlockSpec((tt, a.shape[1]), lambda b, j: (b * nt + j, 0)))
        elif kind == 'tm':
            in_specs.append(pl.BlockSpec((tt, a.shape[1] // nb), lambda b, j: (j, b)))
        elif kind == 'bat':
            in_specs.append(pl.BlockSpec((1,) + a.shape[1:], lambda b, j: (b, 0, 0)))
        else:
            in_specs.append(pl.BlockSpec(a.shape, lambda b, j, _n=a.ndim: (0,) * _n))
        args.append(a)
    out_specs, out_shapes = [], []
    for kind, tail, dt in outs:
        if kind == 'tok':
            out_specs.append(pl.BlockSpec((tt, tail[0]), lambda b, j: (off + b * nt + j, 0)))
            out_shapes.append(jax.ShapeDtypeStruct((n_total, tail[0]), dt))
        elif kind == 'own':
            out_specs.append(pl.BlockSpec((tt, tail[0]), lambda b, j: (b * nt + j, 0)))
            out_shapes.append(jax.ShapeDtypeStruct((nb * nt * tt, tail[0]), dt))
        else:
            out_specs.append(pl.BlockSpec((1,) + tuple(tail), lambda b, j: (b, 0, 0)))
            out_shapes.append(jax.ShapeDtypeStruct((nb,) + tuple(tail), dt))
    aliases = {}
    n_prev = 0
    tok_out = [i for i, o in enumerate(outs) if o[0] == 'tok']
    if prev is None:
        prev = [jnp.zeros((n_total, outs[i][1][0]), outs[i][2]) for i in tok_out]
    for p, oi in zip(prev, tok_out):
        aliases[len(args)] = oi
        in_specs.append(pl.BlockSpec(memory_space=pl.ANY))
        args.append(p)
        n_prev += 1
    n_in = len(ins)

    def wrapped(*refs):
        body(*refs[:n_in], *refs[n_in + n_prev:])

    return pl.pallas_call(
        wrapped, grid=(nb, nt), in_specs=in_specs, out_specs=out_specs, out_shape=out_shapes,
        scratch_shapes=scratch, input_output_aliases=aliases, name=name,
        compiler_params=_cp(("arbitrary", "arbitrary")))(*args)


CONV_RC = 32
CONV_LC = 512


def _conv_kernel(x_ref, st_ref, win_ref, bin_ref, wdw_ref, bdw_ref, cg_ref, cb_ref, wout_ref,
                 bout_ref, lg_ref, lb_ref, o_ref, so_ref, ubuf, ybuf, *, tt, nt):
    j = pl.program_id(1)

    @pl.when(j == 0)
    def _():
        ubuf[0:HALO, :] = st_ref[0]

    @pl.when(j > 0)
    def _():
        ubuf[0:HALO, :] = ubuf[tt:tt + HALO, :]

    x = x_ref[...]
    h = _bdot(x, win_ref[...]) + bin_ref[...]
    ubuf[HALO:HALO + tt, :] = h[:, :D] * _sigmoid(h[:, D:])
    first = HALO - (CONV_W - 1)
    for r0 in range(0, tt, CONV_RC):
        for c0 in range(0, D, CONV_LC):
            y = jnp.zeros((CONV_RC, CONV_LC), F32) + bdw_ref[:, c0:c0 + CONV_LC]
            for off in range(8):
                rows = CONV_RC + (8 if off else 0)
                acc = None
                for tap in range(CONV_W):
                    if (first + tap) % 8 != off:
                        continue
                    base = r0 + (first + tap) // 8 * 8
                    term = wdw_ref[tap:tap + 1, c0:c0 + CONV_LC] * ubuf[base:base + rows, c0:c0 + CONV_LC]
                    acc = term if acc is None else acc + term
                y = y + acc[off:off + CONV_RC]
            ybuf[r0:r0 + CONV_RC, c0:c0 + CONV_LC] = y
    z = _ln(ybuf[...], cg_ref[...], cb_ref[...])
    z = z * _sigmoid(z)
    mix = _bdot(z, wout_ref[...]) + bout_ref[...]
    o_ref[...] = _ln(ALPHA * x + mix, lg_ref[...], lb_ref[...])

    @pl.when(j == nt - 1)
    def _():
        so_ref[0] = ubuf[tt:tt + HALO, :]


def _conv_mixer(x, state, w, lnp, *, nb, t, tt, row_off, prev):
    nt = t // tt
    n_total = x.shape[0]
    win, bin_, wdw, bdw, cg, cb, wout, bout = w
    ins = [('tok', x), ('bat', state), ('const', win), ('const', bin_), ('const', wdw), ('const', bdw),
           ('const', cg), ('const', cb), ('const', wout), ('const', bout), ('const', lnp[0]), ('const', lnp[1])]
    outs = [('tok', (D,), F32), ('bat', (HALO, D), F32)]
    scratch = [pltpu.VMEM((HALO + tt, D), F32), pltpu.VMEM((tt, D), F32)]
    return _seq_call(functools.partial(_conv_kernel, tt=tt, nt=nt), name=f"conv_t{t}", nb=nb, nt=nt, tt=tt,
                     row_off=row_off,
                     n_total=n_total, ins=ins, outs=outs, scratch=scratch, prev=prev)


def _head_sum(y, hs_ref, hst_ref):
    s = _split_dot(y, hs_ref[...])
    return _split_dot(s, hst_ref[...])


def _rw1_kernel(x_ref, sh_ref, mu_ref, wr_ref, wk_ref, wv_ref, w0_ref, w1_ref, w2_ref, a0_ref, a1_ref,
                a2_ref, g1_ref, g2_ref, kk_ref, ka_ref, hs_ref, hst_ref,
                r_ref, w_ref, k_ref, v_ref, an_ref, b_ref, g_ref, xbuf, *, tt):
    j = pl.program_id(1)

    @pl.when(j == 0)
    def _():
        xbuf[7:8, :] = sh_ref[0]

    @pl.when(j > 0)
    def _():
        xbuf[7:8, :] = xbuf[7 + tt:8 + tt, :]

    x = x_ref[...]
    xbuf[8:8 + tt, :] = x
    xx = xbuf[7:7 + tt, :] - x
    mu = mu_ref[...]
    xr = x + xx * mu[0:1]
    xw = x + xx * mu[1:2]
    xk = x + xx * mu[2:3]
    xv = x + xx * mu[3:4]
    xa = x + xx * mu[4:5]
    xg = x + xx * mu[5:6]
    r = _bdot(xr, wr_ref[...])
    k = _bdot(xk, wk_ref[...])
    v = _bdot(xv, wv_ref[...])
    lw = w0_ref[...] + _bdot(jnp.tanh(_bdot(xw, w1_ref[...])), w2_ref[...])
    z = -lw
    log_w = -(jnp.maximum(z, 0.0) + jnp.log(1.0 + jnp.exp(-jnp.abs(z)))) - 0.5
    a = _sigmoid(a0_ref[...] + _bdot(_bdot(xa, a1_ref[...]), a2_ref[...]))
    g = _bdot(_sigmoid(_bdot(xg, g1_ref[...])), g2_ref[...])
    kk = k * kk_ref[...]
    ss = _head_sum(kk * kk, hs_ref, hst_ref)
    kk = kk * lax.rsqrt(jnp.maximum(ss, 1e-24))
    r_ref[...] = r
    w_ref[...] = jnp.exp(-jnp.exp(log_w))
    k_ref[...] = k * (1.0 + (a - 1.0) * ka_ref[...])
    v_ref[...] = v
    an_ref[...] = -kk
    b_ref[...] = kk * a
    g_ref[...] = g


def _rw1(x, shift, w, hs, hst, *, nb, t, tt, row_off):
    nt = t // tt
    prev = []
    ins = [('tok', x), ('bat', shift)] + [('const', a) for a in w] + [('const', hs), ('const', hst)]
    outs = [('own', (D,), F32)] * 7
    scratch = [pltpu.VMEM((8 + tt, D), F32)]
    return _seq_call(functools.partial(_rw1_kernel, tt=tt), name=f"rwkv_proj_t{t}", nb=nb, nt=nt, tt=tt,
                     row_off=row_off,
                     n_total=x.shape[0], ins=ins, outs=outs, scratch=scratch, prev=prev)


def _scan_kernel(r_ref, w_ref, k_ref, v_ref, an_ref, b_ref, nxt_ref, s0_ref, o_ref, st_ref, S, sa_buf,
                 *, tc, nc):
    c = pl.program_id(1)

    @pl.when(c == 0)
    def _():
        S[...] = s0_ref[...]

        def init(kk, acc):
            return acc + S[kk] * an_ref[0, pl.ds(kk, 1), :]

        sa_buf[...] = lax.fori_loop(0, HD, init, jnp.zeros((HD, LANES), F32))

    def step(t, sa, a_next):
        vt = v_ref[t]

        def kbody(kk, acc):
            o_acc, sa_acc = acc
            sk = (S[kk] * w_ref[t, pl.ds(kk, 1), :] + sa * b_ref[t, pl.ds(kk, 1), :]
                  + vt * k_ref[t, pl.ds(kk, 1), :])
            S[kk] = sk
            return (o_acc + sk * r_ref[t, pl.ds(kk, 1), :], sa_acc + sk * a_next(kk))

        zero = jnp.zeros((HD, LANES), F32)
        o_acc, sa_next = lax.fori_loop(0, HD, kbody, (zero, zero), unroll=4)
        o_ref[t] = o_acc
        return sa_next

    sa = lax.fori_loop(0, tc - 1, lambda t, sa: step(t, sa, lambda kk: an_ref[t + 1, pl.ds(kk, 1), :]),
                       sa_buf[...])
    sa_buf[...] = step(tc - 1, sa, lambda kk: nxt_ref[0, pl.ds(kk, 1), :])

    @pl.when(c == nc - 1)
    def _():
        st_ref[...] = S[...]


def _scan(r, w, k, v, an, b, s0, *, tc):
    t, _, lanes = r.shape
    ng, nc = lanes // LANES, t // tc
    seq = pl.BlockSpec((tc, HD, LANES), lambda g, c: (c, 0, g))
    nxt = pl.BlockSpec((1, HD, LANES), lambda g, c: (jnp.minimum((c + 1) * tc, t - 1), 0, g))
    return pl.pallas_call(
        functools.partial(_scan_kernel, tc=tc, nc=nc), grid=(ng, nc),
        in_specs=[seq] * 6 + [nxt, pl.BlockSpec((HD, HD, LANES), lambda g, c: (0, 0, g))],
        out_specs=[seq, pl.BlockSpec((HD, HD, LANES), lambda g, c: (0, 0, g))],
        out_shape=[jax.ShapeDtypeStruct((t, HD, lanes), F32), jax.ShapeDtypeStruct((HD, HD, lanes), F32)],
        scratch_shapes=[pltpu.VMEM((HD, HD, LANES), F32), pltpu.VMEM((HD, LANES), F32)],
        name=f"wkv_scan_t{t}", compiler_params=_cp(("arbitrary", "arbitrary")))(r, w, k, v, an, b, an, s0)


def _post_kernel(o_ref, r_ref, k_ref, v_ref, xg_ref, xb_ref, rk_ref, y_ref, *, tc):
    low = lax.broadcasted_iota(I32, (HD, LANES), 1) < HD

    def norm(t):
        o = o_ref[t]
        d = o - jnp.mean(o, axis=0, keepdims=True)
        vo = jnp.mean(d * d, axis=0, keepdims=True)
        bonus = jnp.sum(r_ref[t] * k_ref[t] * rk_ref[...], axis=0, keepdims=True) * v_ref[t]
        return d * lax.rsqrt(vo + LNX_EPS) * xg_ref[...] + xb_ref[...] + bonus

    def pair(i, carry):
        t = i * 2
        z0, z1 = norm(t), norm(t + 1)
        top = jnp.where(low, z0, pltpu.roll(z1, HD, 1))
        bot = jnp.where(low, pltpu.roll(z0, HD, 1), z1)
        m = jnp.concatenate([top, bot], axis=0).T
        y_ref[t] = jnp.concatenate([m[8 * p:8 * p + 8] for p in range(8)], axis=1)
        y_ref[t + 1] = jnp.concatenate([m[HD + 8 * p:HD + 8 * p + 8] for p in range(8)], axis=1)
        return carry

    lax.fori_loop(0, tc // 2, pair, 0)


def _post(o, r, k, v, xg, xb, rk, *, tc):
    t, _, lanes = o.shape
    ng, nc = lanes // LANES, t // tc
    seq = pl.BlockSpec((tc, HD, LANES), lambda g, c: (c, 0, g))
    const = pl.BlockSpec((HD, LANES), lambda g, c: (0, 0))
    return pl.pallas_call(
        functools.partial(_post_kernel, tc=tc), grid=(ng, nc),
        in_specs=[seq] * 4 + [const] * 3,
        out_specs=pl.BlockSpec((tc, 8, D), lambda g, c: (c, g, 0)),
        out_shape=jax.ShapeDtypeStruct((t, ng * 8, D), F32), name=f"wkv_post_t{t}",
        compiler_params=_cp(("arbitrary", "arbitrary")))(o, r, k, v, xg, xb, rk)


def _rw3_kernel(y_ref, g_ref, x_ref, wo_ref, lg_ref, lb_ref, out_ref):
    mix = _bdot(y_ref[...] * g_ref[...], wo_ref[...])
    out_ref[...] = _ln(ALPHA * x_ref[...] + mix, lg_ref[...], lb_ref[...])


def _head_major_spec(tt):
    return pl.BlockSpec((HEADS, tt, HD), lambda i: (0, i, 0))


def _tok_call(body, name, n, tt, toks, consts, outs):
    in_specs = [_head_major_spec(tt) if a.ndim == 3 else pl.BlockSpec((tt, a.shape[1]), lambda i: (i, 0))
                for a in toks]
    in_specs += [pl.BlockSpec(a.shape, lambda i, _n=a.ndim: (0,) * _n) for a in consts]
    out_specs = [_head_major_spec(tt) if w == 'heads' else pl.BlockSpec((tt, w), lambda i: (i, 0))
                 for w, _ in outs]
    out_shape = [jax.ShapeDtypeStruct((HEADS, n, HD) if w == 'heads' else (n, w), dt) for w, dt in outs]
    return pl.pallas_call(
        body, grid=(n // tt,), in_specs=in_specs, out_specs=out_specs, out_shape=out_shape, name=name,
        compiler_params=_cp(("arbitrary",)))(*toks, *consts)


def _qkv_kernel(x_ref, w_ref, b_ref, q_ref, k_ref, v_ref):
    h = _bdot(x_ref[...], w_ref[...]) + b_ref[...]
    for hd in range(HEADS):
        q_ref[hd] = h[:, hd * HD:(hd + 1) * HD].astype(BF16)
    k_ref[...] = h[:, HEADS * HD:HEADS * HD + KVH * HD]
    v_ref[...] = h[:, HEADS * HD + KVH * HD:]


def _attn_kernel(q_ref, kp_ref, vp_ref, bias_ref, sink_ref, prev_ref, o_ref, *, nq, kb, stride, mask_lo):
    del prev_ref
    c = pl.program_id(1)
    start = pl.multiple_of(c * stride, 8)
    kband = kp_ref[0, pl.ds(start, kb), :].astype(BF16)
    vband = vp_ref[0, pl.ds(start, kb), :].astype(BF16)
    valid = (start + lax.broadcasted_iota(I32, (1, kb), 1)) >= mask_lo
    for g in range(KVH):
        hs = slice(g * GROUP, (g + 1) * GROUP)
        qg = q_ref[hs].reshape(GROUP * nq, HD)
        kh = kband[:, g * HD:(g + 1) * HD]
        vh = vband[:, g * HD:(g + 1) * HD]
        logits = lax.dot_general(qg, kh, (((1,), (1,)), ((), ())), preferred_element_type=F32) * HD ** -0.5
        logits = jnp.where(valid, logits + bias_ref[hs].reshape(GROUP * nq, kb), -1e30)
        sink = sink_ref[hs].reshape(GROUP * nq, 1)
        m = jnp.maximum(jnp.max(logits, axis=-1, keepdims=True), sink)
        p = jnp.exp(logits - m)
        p = p / (jnp.sum(p, axis=-1, keepdims=True) + jnp.exp(sink - m))
        og = jnp.dot(p.astype(BF16), vh, preferred_element_type=F32)
        o_ref[hs] = og.reshape(GROUP, nq, HD).astype(BF16)


def _attn(q, kp, vp, bias, sinks, *, nb, nt, nq, kb, stride, mask_lo, row_off, prev):
    off = row_off // nq
    n_total = q.shape[1]
    heads = pl.BlockSpec((HEADS, nq, HD), lambda b, c: (0, off + b * nt + c, 0))
    const = lambda a: pl.BlockSpec(a.shape, lambda b, c, _n=a.ndim: (0,) * _n)
    bat = lambda a: pl.BlockSpec((1,) + a.shape[1:], lambda b, c: (b, 0, 0))
    sink_tab = jnp.broadcast_to(sinks.reshape(HEADS, 1, 1), (HEADS, nq, 1))
    if prev is None:
        prev = jnp.zeros((HEADS, n_total, HD), BF16)
    return pl.pallas_call(
        functools.partial(_attn_kernel, nq=nq, kb=kb, stride=stride, mask_lo=mask_lo), grid=(nb, nt),
        in_specs=[heads, bat(kp), bat(vp), const(bias), const(sink_tab), pl.BlockSpec(memory_space=pl.ANY)],
        out_specs=heads, out_shape=jax.ShapeDtypeStruct((HEADS, n_total, HD), BF16),
        input_output_aliases={5: 0}, name=f"attn_q{nq}",
        compiler_params=_cp(("arbitrary", "arbitrary")))(q, kp, vp, bias, sink_tab, prev)


def _oproj_kernel(o_ref, x_ref, wo_ref, bo_ref, lg_ref, lb_ref, out_ref):
    acc = jnp.dot(o_ref[0], wo_ref[0:HD, :], preferred_element_type=F32)
    for hd in range(1, HEADS):
        acc = acc + jnp.dot(o_ref[hd], wo_ref[hd * HD:(hd + 1) * HD, :], preferred_element_type=F32)
    out_ref[...] = _ln(ALPHA * x_ref[...] + acc + bo_ref[...], lg_ref[...], lb_ref[...])


def _t5_bucket(rel):
    half = N_BUCKETS // 2
    max_exact = half // 2
    ret = jnp.where(rel > 0, half, 0)
    n = jnp.abs(rel)
    nf = jnp.maximum(n, 1).astype(F32)
    large = max_exact + (jnp.log(nf / max_exact) / math.log(MAX_DISTANCE / max_exact)
                         * (half - max_exact)).astype(I32)
    large = jnp.minimum(large, half - 1)
    return ret + jnp.where(n < max_exact, n, large)


def _t5_bias(rel_bias, n_q, n_k):
    rel = jnp.arange(n_k)[None, :] - WINDOW - jnp.arange(n_q)[:, None]
    return jnp.transpose(rel_bias[_t5_bucket(rel)], (2, 0, 1))


def _route_kernel(x_ref, wr_ref, br_ref, idx_ref, gate_ref, rank_ref, cnt_ref, carry, *, tt):
    i = pl.program_id(0)

    @pl.when(i == 0)
    def _():
        carry[...] = jnp.zeros_like(carry)

    xh, xl = _split(x_ref[...])
    wh, wl = _split(wr_ref[...])
    logits = (jnp.dot(xh, wh, preferred_element_type=F32) + jnp.dot(xl, wh, preferred_element_type=F32)
              + jnp.dot(xh, wl, preferred_element_type=F32)) + br_ref[...]
    lane = lax.broadcasted_iota(I32, (tt, N_EXP), 1)
    out_lane = lax.broadcasted_iota(I32, (tt, LANES), 1)
    vals, sels = [], []
    idx_out = jnp.zeros((tt, LANES), I32)
    work = logits
    for k in range(TOP_K):
        m = jnp.max(work, axis=-1, keepdims=True)
        ik = jnp.min(jnp.where(work == m, lane, N_EXP), axis=-1, keepdims=True)
        sel = lane == ik
        vals.append(m)
        sels.append(sel)
        idx_out = jnp.where(out_lane == k, ik, idx_out)
        work = jnp.where(sel, -jnp.inf, work)
    es = [jnp.exp(v - vals[0]) for v in vals]
    den = es[0] + es[1] + es[2] + es[3]
    gate_out = jnp.zeros((tt, LANES), F32)
    for k in range(TOP_K):
        gate_out = jnp.where(out_lane == k, es[k] / den, gate_out)
    onehot = jnp.zeros((tt, N_EXP), F32)
    for sel in sels:
        onehot = onehot + sel.astype(F32)
    tri = (lax.broadcasted_iota(I32, (tt, tt), 0) > lax.broadcasted_iota(I32, (tt, tt), 1)).astype(BF16)
    base = carry[...] + jnp.dot(tri, onehot.astype(BF16), preferred_element_type=F32)
    rank_out = jnp.zeros((tt, LANES), I32)
    for k in range(TOP_K):
        rk = jnp.sum(jnp.where(sels[k], base, 0.0), axis=-1, keepdims=True)
        rank_out = jnp.where(out_lane == k, rk.astype(I32), rank_out)
    carry[...] = carry[...] + jnp.sum(onehot, axis=0, keepdims=True)
    idx_ref[...] = idx_out
    gate_ref[...] = gate_out
    rank_ref[...] = rank_out
    cnt_ref[...] = carry[...]


def _route(x1, wr, br, *, tt):
    n = x1.shape[0]
    tokspec = lambda w: pl.BlockSpec((tt, w), lambda i: (i, 0))
    const = lambda a: pl.BlockSpec(a.shape, lambda i, _n=a.ndim: (0,) * _n)
    return pl.pallas_call(
        functools.partial(_route_kernel, tt=tt), grid=(n // tt,),
        in_specs=[tokspec(D), const(wr), const(br)],
        out_specs=[tokspec(LANES), tokspec(LANES), tokspec(LANES), pl.BlockSpec((1, N_EXP), lambda i: (0, 0))],
        out_shape=[jax.ShapeDtypeStruct((n, LANES), I32), jax.ShapeDtypeStruct((n, LANES), F32),
                   jax.ShapeDtypeStruct((n, LANES), I32), jax.ShapeDtypeStruct((1, N_EXP), F32)],
        scratch_shapes=[pltpu.VMEM((1, N_EXP), F32)], name="moe_route",
        compiler_params=_cp(("arbitrary",)))(x1, wr, br)


def _row_copy(src, s, dst, d, sem):
    return pltpu.make_async_copy(src.at[pl.ds(s, 1), :], dst.at[pl.ds(d, 1), :], sem)


ROW_UNROLL = 8


def _disp_kernel(dest_ref, x_ref, xs_in, xs_ref, sem, *, tt):
    del xs_in

    def issue(r, carry):
        for k in range(TOP_K):
            _row_copy(x_ref, r, xs_ref, dest_ref[r * TOP_K + k], sem).start(priority=k % 2)
        return carry

    lax.fori_loop(0, tt, issue, 0, unroll=ROW_UNROLL)

    def drain(r, carry):
        for k in range(TOP_K):
            _row_copy(x_ref, 0, xs_ref, 0, sem).wait()
        return carry

    lax.fori_loop(0, tt, drain, 0, unroll=ROW_UNROLL)


def _dispatch(dest, x1, xs_prev, *, tt):
    n = x1.shape[0]
    smem_tok = pl.BlockSpec((tt * TOP_K,), lambda i: (i,), memory_space=pltpu.SMEM)
    return pl.pallas_call(
        functools.partial(_disp_kernel, tt=tt), grid=(n // tt,),
        in_specs=[smem_tok, pl.BlockSpec((tt, D), lambda i: (i, 0)), pl.BlockSpec(memory_space=pl.ANY)],
        out_specs=pl.BlockSpec(memory_space=pl.ANY),
        out_shape=jax.ShapeDtypeStruct(xs_prev.shape, F32),
        scratch_shapes=[pltpu.SemaphoreType.DMA],
        input_output_aliases={2: 0}, name="moe_dispatch",
        compiler_params=_cp(("arbitrary",)))(dest, x1, xs_prev)


def _expert_kernel(te_ref, nv_ref, xs_ref, w1_ref, b1_ref, w2_ref, b2_ref, y_ref, w1b, w2b):
    i = pl.program_id(0)
    valid = i < nv_ref[0]
    changed = jnp.logical_or(i == 0, te_ref[i] != te_ref[jnp.maximum(i - 1, 0)])

    @pl.when(jnp.logical_and(valid, changed))
    def _():
        for r0 in range(0, D, 256):
            w1b[r0:r0 + 256, :] = w1_ref[0, r0:r0 + 256, :].astype(BF16)
            w2b[r0:r0 + 256, :] = w2_ref[0, r0:r0 + 256, :].astype(BF16)

    @pl.when(valid)
    def _():
        h = jnp.dot(xs_ref[...].astype(BF16), w1b[...], preferred_element_type=F32) + b1_ref[0]
        glu = jnp.minimum(h[:, :D], SWIGLU_LIMIT)
        lin = jnp.clip(h[:, D:], -SWIGLU_LIMIT, SWIGLU_LIMIT)
        act = glu * _sigmoid(SWIGLU_ALPHA * glu) * (lin + 1.0)
        y_ref[...] = jnp.dot(act.astype(BF16), w2b[...], preferred_element_type=F32) + b2_ref[0]

    @pl.when(jnp.logical_not(valid))
    def _():
        y_ref[...] = jnp.zeros_like(y_ref)


def _experts(tile_expert, n_valid, xs, w1, b1, w2, b2):
    n_tiles = xs.shape[0] // EXP_TILE
    grid_spec = pltpu.PrefetchScalarGridSpec(
        num_scalar_prefetch=2, grid=(n_tiles,),
        in_specs=[pl.BlockSpec((EXP_TILE, D), lambda i, te, nv: (i, 0)),
                  pl.BlockSpec((1, D, 2 * D), lambda i, te, nv: (te[i], 0, 0)),
                  pl.BlockSpec((1, 1, 2 * D), lambda i, te, nv: (te[i], 0, 0)),
                  pl.BlockSpec((1, D, D), lambda i, te, nv: (te[i], 0, 0)),
                  pl.BlockSpec((1, 1, D), lambda i, te, nv: (te[i], 0, 0))],
        out_specs=pl.BlockSpec((EXP_TILE, D), lambda i, te, nv: (i, 0)),
        scratch_shapes=[pltpu.VMEM((D, 2 * D), BF16), pltpu.VMEM((D, D), BF16)])
    return pl.pallas_call(
        _expert_kernel, grid_spec=grid_spec, out_shape=jax.ShapeDtypeStruct(xs.shape, F32), name="moe_experts",
        compiler_params=_cp(("arbitrary",)))(tile_expert, n_valid, xs, w1, b1, w2, b2)


def _comb_kernel(dest_ref, gate_ref, x1_ref, p_ref, y_ref, lg_ref, lb_ref, wg_ref, bg_ref,
                 wp_ref, o_ref, buf, sem, *, tt):
    def issue(r, carry):
        for k in range(TOP_K):
            _row_copy(y_ref, dest_ref[r * TOP_K + k], buf.at[k], r, sem).start(priority=k % 2)
        return carry

    lax.fori_loop(0, tt, issue, 0, unroll=ROW_UNROLL)

    def drain(r, carry):
        for k in range(TOP_K):
            _row_copy(y_ref, 0, buf.at[k], 0, sem).wait()
        return carry

    lax.fori_loop(0, tt, drain, 0, unroll=ROW_UNROLL)
    gate = gate_ref[...]
    moe = gate[:, 0:1] * buf[0]
    for k in range(1, TOP_K):
        moe = moe + gate[:, k:k + 1] * buf[k]
    x2 = _ln(ALPHA * x1_ref[...] + moe, lg_ref[...], lb_ref[...])
    gt = _sigmoid(_bdot(x2, wg_ref[...]) + bg_ref[...])
    o_ref[...] = x2 + gt * _bdot(p_ref[...], wp_ref[...])


def _combine(dest, gate, x1, p, y, lnp, wg, bg, wp, *, tt):
    n = x1.shape[0]
    smem_tok = pl.BlockSpec((tt * TOP_K,), lambda i: (i,), memory_space=pltpu.SMEM)
    tok = lambda w: pl.BlockSpec((tt, w), lambda i: (i, 0))
    const = lambda a: pl.BlockSpec(a.shape, lambda i, _n=a.ndim: (0,) * _n)
    return pl.pallas_call(
        functools.partial(_comb_kernel, tt=tt), grid=(n // tt,),
        in_specs=[smem_tok, tok(LANES), tok(D), tok(p.shape[1]), pl.BlockSpec(memory_space=pl.ANY),
                  const(lnp[0]), const(lnp[1]), const(wg), const(bg), const(wp)],
        out_specs=tok(D), out_shape=jax.ShapeDtypeStruct((n, D), F32),
        scratch_shapes=[pltpu.VMEM((TOP_K, tt, D), F32), pltpu.SemaphoreType.DMA], name="moe_combine_ple",
        compiler_params=_cp(("arbitrary",)))(dest, gate, x1, p, y, lnp[0], lnp[1], wg, bg, wp)


def _moe_ple(x1, p, xs_buf, wr, br, w1, b1, w2, b2, lnp, wg, bg, wp, *, tt):
    idx, gate, rank, counts = _route(x1, wr, br, tt=tt)
    counts = counts[0].astype(I32)
    padded = (counts + EXP_TILE - 1) // EXP_TILE * EXP_TILE
    pad_end = jnp.cumsum(padded)
    pad_start = pad_end - padded
    n_tiles = xs_buf.shape[0] // EXP_TILE
    n_valid = (pad_end[-1] // EXP_TILE).astype(I32)
    tiles = jnp.minimum(jnp.arange(n_tiles, dtype=I32), n_valid - 1) * EXP_TILE
    tile_expert = jnp.minimum(jnp.sum((tiles[:, None] >= pad_end[None, :]).astype(I32), axis=1), N_EXP - 1)
    experts = jnp.arange(N_EXP, dtype=I32)
    start_of = jnp.sum(jnp.where(idx[:, :TOP_K, None] == experts, pad_start, 0), axis=-1)
    dest = (start_of + rank[:, :TOP_K]).reshape(-1)
    xs = _dispatch(dest, x1, xs_buf, tt=tt)
    y = _experts(tile_expert, n_valid.reshape(1), xs, w1, b1.reshape(N_EXP, 1, 2 * D), w2,
                 b2.reshape(N_EXP, 1, D))
    return _combine(dest, gate, x1, p, y, lnp, wg, bg, wp, tt=tt), xs


def _to_scan(a, nb, t):
    a = a.reshape(nb // 8, 8, t, HEADS // 2, 2, HD).transpose(2, 5, 0, 4, 3, 1)
    return a.reshape(t, HD, nb * HEADS)


def _state_to_scan(s, nb):
    s = s.reshape(nb // 8, 8, HEADS // 2, 2, HD, HD).transpose(5, 4, 0, 3, 2, 1)
    return s.reshape(HD, HD, nb * HEADS)


def _state_from_scan(s, nb):
    s = s.reshape(HD, HD, nb // 8, 2, HEADS // 2, 8).transpose(2, 5, 4, 3, 1, 0)
    return s.reshape(nb, HEADS, HD, HD)


def _head_vec_to_scan(a):
    a = a.reshape(HEADS // 2, 2, HD).transpose(2, 1, 0)
    return jnp.broadcast_to(a[..., None], (HD, 2, HEADS // 2, 8)).reshape(HD, LANES)


def _row2(a):
    return a.reshape(1, -1)


def _tile(n):
    for tt in (256, 128, 64, 32, 16, 8):
        if n % tt == 0:
            return tt
    raise ValueError(n)


def kernel(x_prompt, x_sample, p_prompt, p_sample, cache_conv, state_rwkv_shift, state_rwkv_wkv, cache_swa_k, cache_swa_v, conv_w_in, conv_b_in, conv_w_dw, conv_b_dw, conv_ln_g, conv_ln_b, conv_w_out, conv_b_out, rwkv_mu, rwkv_w_rkv, rwkv_w0, rwkv_w1, rwkv_w2, rwkv_a0, rwkv_a1, rwkv_a2, rwkv_g1, rwkv_g2, rwkv_k_k, rwkv_k_a, rwkv_r_k, rwkv_lnx_g, rwkv_lnx_b, rwkv_w_o, attn_w_qkv, attn_b_qkv, attn_sinks, attn_w_o, attn_b_o, rel_bias, ln_g, ln_b, moe_w_router, moe_b_router, moe_w1, moe_b1, moe_w2, moe_b2, ple_w_proj, ple_w_gate, ple_b_gate):
    bp, tp, _ = x_prompt.shape
    bs, ts, _ = x_sample.shape
    n_p, n_s = bp * tp, bs * ts
    n = n_p + n_s
    assert tp % 128 == 0 and n_p % ts == 0 and ts % 8 == 0 and ts <= CHUNK
    tt_tok = _tile(n)
    tt_p = 128
    x = jnp.concatenate([x_prompt.reshape(n_p, D), x_sample.reshape(n_s, D)], axis=0)
    p_all = jnp.concatenate([p_prompt.reshape(DEPTH, n_p, -1), p_sample.reshape(DEPTH, n_s, -1)], axis=1)
    n_rows = (-(-n * TOP_K // EXP_TILE) + N_EXP) * EXP_TILE
    xs_buf = jnp.zeros((n_rows, D), F32)
    head_sel = (jnp.arange(D)[:, None] // HD == jnp.arange(LANES)[None, :]).astype(BF16)
    head_sel_t = head_sel.T
    conv_p, conv_s, shift_p, shift_s, wkv_p, wkv_s = [], [], [], [], [], []
    swa_kp, swa_vp, swa_ks, swa_vs = [], [], [], []
    for i in range(DEPTH):
        kind, j = i % 3, i // 3
        lnp = (_row2(ln_g[i, 0]), _row2(ln_b[i, 0]))
        if kind == 0:
            cw = (conv_w_in[j].astype(BF16), _row2(conv_b_in[j]), conv_w_dw[j], _row2(conv_b_dw[j]),
                  _row2(conv_ln_g[j]), _row2(conv_ln_b[j]), conv_w_out[j].astype(BF16), _row2(conv_b_out[j]))
            st_p = jnp.zeros((bp, HALO, D), F32)
            st_s = jnp.pad(cache_conv[j], ((0, 0), (HALO - (CONV_W - 1), 0), (0, 0)))
            x1, so_p = _conv_mixer(x, st_p, cw, lnp, nb=bp, t=tp, tt=tt_p, row_off=0, prev=None)
            x1, so_s = _conv_mixer(x, st_s, cw, lnp, nb=bs, t=ts, tt=ts, row_off=n_p, prev=[x1])
            conv_p.append(so_p[:, HALO - (CONV_W - 1):])
            conv_s.append(so_s[:, HALO - (CONV_W - 1):])
        elif kind == 1:
            rw = (rwkv_mu[j], rwkv_w_rkv[j, 0].astype(BF16), rwkv_w_rkv[j, 1].astype(BF16),
                  rwkv_w_rkv[j, 2].astype(BF16), _row2(rwkv_w0[j]), rwkv_w1[j].astype(BF16),
                  rwkv_w2[j].astype(BF16), _row2(rwkv_a0[j]), rwkv_a1[j].astype(BF16), rwkv_a2[j].astype(BF16),
                  rwkv_g1[j].astype(BF16), rwkv_g2[j].astype(BF16), _row2(rwkv_k_k[j]), _row2(rwkv_k_a[j]))
            sh_p = jnp.zeros((bp, 1, D), F32)
            sh_s = state_rwkv_shift[j].reshape(bs, 1, D)
            post_c = [_head_vec_to_scan(a) for a in (rwkv_lnx_g[j], rwkv_lnx_b[j], rwkv_r_k[j].reshape(-1))]
            out_c = [('const', rwkv_w_o[j].astype(BF16)), ('const', lnp[0]), ('const', lnp[1])]
            x1, states = None, []
            for (lo, nb_, t_, tt_, sh, s0, tc) in ((0, bp, tp, tt_p, sh_p, None, 64),
                                                   (n_p, bs, ts, ts, sh_s, state_rwkv_wkv[j], ts)):
                r, w, k, v, an, b, g = _rw1(x, sh, rw, head_sel, head_sel_t, nb=nb_, t=t_, tt=tt_, row_off=lo)
                rs, ws, ks, vs, ans, bs_ = (_to_scan(a, nb_, t_) for a in (r, w, k, v, an, b))
                s0l = jnp.zeros((HD, HD, nb_ * HEADS), F32) if s0 is None else _state_to_scan(s0, nb_)
                o_l, s_l = _scan(rs, ws, ks, vs, ans, bs_, s0l, tc=tc)
                y = _post(o_l, rs, ks, vs, *post_c, tc=tc).reshape(t_, nb_ * D)
                x1, = _seq_call(_rw3_kernel, name=f"rwkv_out_t{t_}", nb=nb_, nt=t_ // tt_, tt=tt_, row_off=lo,
                                n_total=n, ins=[('tm', y), ('own', g), ('tok', x)] + out_c,
                                outs=[('tok', (D,), F32)], scratch=[], prev=None if x1 is None else [x1])
                states.append(_state_from_scan(s_l, nb_))
            shift_p.append(x[:n_p].reshape(bp, tp, D)[:, -1])
            shift_s.append(x[n_p:].reshape(bs, ts, D)[:, -1])
            wkv_p.append(states[0])
            wkv_s.append(states[1])
        else:
            q, kx, vx = _tok_call(_qkv_kernel, "attn_qkv", n, tt_tok, [x],
                                  [attn_w_qkv[j].astype(BF16), _row2(attn_b_qkv[j])],
                                  [('heads', BF16), (KVH * HD, F32), (KVH * HD, F32)])
            k_p = kx[:n_p].reshape(bp, tp, KVH * HD)
            v_p = vx[:n_p].reshape(bp, tp, KVH * HD)
            zpad = jnp.zeros((bp, WINDOW, KVH * HD), F32)
            nc = tp // CHUNK
            band = WINDOW + CHUNK
            o = _attn(q, jnp.concatenate([zpad, k_p], axis=1), jnp.concatenate([zpad, v_p], axis=1),
                      _t5_bias(rel_bias, CHUNK, band), attn_sinks[j], nb=bp, nt=nc, nq=CHUNK, kb=band,
                      stride=CHUNK, mask_lo=WINDOW, row_off=0, prev=None)
            k_all = jnp.concatenate([cache_swa_k[j].reshape(bs, WINDOW, KVH * HD),
                                     kx[n_p:].reshape(bs, ts, KVH * HD)], axis=1)
            v_all = jnp.concatenate([cache_swa_v[j].reshape(bs, WINDOW, KVH * HD),
                                     vx[n_p:].reshape(bs, ts, KVH * HD)], axis=1)
            o = _attn(q, k_all, v_all, _t5_bias(rel_bias, ts, WINDOW + ts), attn_sinks[j],
                      nb=bs, nt=1, nq=ts, kb=WINDOW + ts, stride=0, mask_lo=0, row_off=n_p, prev=o)
            x1, = _tok_call(_oproj_kernel, "attn_out", n, tt_tok, [o, x],
                            [attn_w_o[j].astype(BF16), _row2(attn_b_o[j]), lnp[0], lnp[1]], [(D, F32)])
            swa_kp.append(k_p[:, -WINDOW:].reshape(bp, WINDOW, KVH, HD))
            swa_vp.append(v_p[:, -WINDOW:].reshape(bp, WINDOW, KVH, HD))
            swa_ks.append(k_all[:, -WINDOW:].reshape(bs, WINDOW, KVH, HD))
            swa_vs.append(v_all[:, -WINDOW:].reshape(bs, WINDOW, KVH, HD))
        x, xs_buf = _moe_ple(x1, p_all[i], xs_buf, moe_w_router[i], _row2(moe_b_router[i]), moe_w1[i], moe_b1[i],
                             moe_w2[i], moe_b2[i], (_row2(ln_g[i, 1]), _row2(ln_b[i, 1])),
                             ple_w_gate[i].astype(BF16), _row2(ple_b_gate[i]), ple_w_proj[i].astype(BF16), tt=tt_tok)
    return (x[:n_p].reshape(bp, tp, D), x[n_p:].reshape(bs, ts, D), jnp.stack(conv_p), jnp.stack(conv_s),
            jnp.stack(shift_p), jnp.stack(shift_s), jnp.stack(wkv_p), jnp.stack(wkv_s), jnp.stack(swa_kp),
            jnp.stack(swa_vp), jnp.stack(swa_ks), jnp.stack(swa_vs))
```

```python
import functools
import math

import jax
import jax.numpy as jnp
from jax import lax
from jax.experimental import pallas as pl
from jax.experimental.pallas import tpu as pltpu

F32 = jnp.float32
BF16 = jnp.bfloat16
I32 = jnp.int32

D = 1024
DEPTH = 4
CONV_W = 31
HALO = 32
HEADS = 16
HD = 64
KVH = 2
GROUP = HEADS // KVH
WINDOW = 128
CHUNK = 64
N_BUCKETS = 32
MAX_DISTANCE = 128
N_EXP = 32
TOP_K = 4
EXP_TILE = 512
LANES = 128
LNX_EPS = 64e-5
LN_EPS = 1e-5
ALPHA = (2 * DEPTH) ** 0.25
SWIGLU_ALPHA = 1.702
SWIGLU_LIMIT = 7.0
VMEM_LIMIT = 56 * 1024 * 1024


def _cp(sem):
    return pltpu.CompilerParams(dimension_semantics=sem, vmem_limit_bytes=VMEM_LIMIT)


def _bdot(a, b):
    return jnp.dot(a.astype(BF16), b.astype(BF16), preferred_element_type=F32)


def _split(a):
    hi = a.astype(BF16)
    lo = (a - hi.astype(F32)).astype(BF16)
    return hi, lo


def _split_dot(a, b_exact):
    hi, lo = _split(a)
    return (jnp.dot(hi, b_exact, preferred_element_type=F32)
            + jnp.dot(lo, b_exact, preferred_element_type=F32))


def _ln(x, g, b, eps=LN_EPS):
    mu = jnp.mean(x, axis=-1, keepdims=True)
    xc = x - mu
    var = jnp.mean(xc * xc, axis=-1, keepdims=True)
    return xc * lax.rsqrt(var + eps) * g + b


def _sigmoid(x):
    return 1.0 / (1.0 + jnp.exp(-x))


def _seq_call(body, *, name, nb, nt, tt, row_off, n_total, ins, outs, scratch, prev=None):
    off = row_off // tt
    in_specs, args = [], []
    for kind, a in ins:
        if kind == 'tok':
            in_specs.append(pl.BlockSpec((tt, a.shape[1]), lambda b, j: (off + b * nt + j, 0)))
        elif kind == 'own':
            in_specs.append(pl.BlockSpec((tt, a.shape[1]), lambda b, j: (b * nt + j, 0)))
        elif kind == 'tm':
            in_specs.append(pl.BlockSpec((tt, a.shape[1] // nb), lambda b, j: (j, b)))
        elif kind == 'bat':
            in_specs.append(pl.BlockSpec((1,) + a.shape[1:], lambda b, j: (b, 0, 0)))
        else:
            in_specs.append(pl.BlockSpec(a.shape, lambda b, j, _n=a.ndim: (0,) * _n))
        args.append(a)
    out_specs, out_shapes = [], []
    for kind, tail, dt in outs:
        if kind == 'tok':
            out_specs.append(pl.BlockSpec((tt, tail[0]), lambda b, j: (off + b * nt + j, 0)))
            out_shapes.append(jax.ShapeDtypeStruct((n_total, tail[0]), dt))
        elif kind == 'own':
            out_specs.append(pl.BlockSpec((tt, tail[0]), lambda b, j: (b * nt + j, 0)))
            out_shapes.append(jax.ShapeDtypeStruct((nb * nt * tt, tail[0]), dt))
        elif kind == 'tm':
            out_specs.append(pl.BlockSpec((tt, tail[0]), lambda b, j: (j, b)))
            out_shapes.append(jax.ShapeDtypeStruct((nt * tt, nb * tail[0]), dt))
        else:
            out_specs.append(pl.BlockSpec((1,) + tuple(tail), lambda b, j: (b, 0, 0)))
            out_shapes.append(jax.ShapeDtypeStruct((nb,) + tuple(tail), dt))
    aliases = {}
    n_prev = 0
    tok_out = [i for i, o in enumerate(outs) if o[0] == 'tok']
    if prev is None:
        prev = [jnp.zeros((n_total, outs[i][1][0]), outs[i][2]) for i in tok_out]
    for p, oi in zip(prev, tok_out):
        aliases[len(args)] = oi
        in_specs.append(pl.BlockSpec(memory_space=pl.ANY))
        args.append(p)
        n_prev += 1
    n_in = len(ins)

    def wrapped(*refs):
        body(*refs[:n_in], *refs[n_in + n_prev:])

    return pl.pallas_call(
        wrapped, grid=(nb, nt), in_specs=in_specs, out_specs=out_specs, out_shape=out_shapes,
        scratch_shapes=scratch, input_output_aliases=aliases, name=name,
        compiler_params=_cp(("arbitrary", "arbitrary")))(*args)


CONV_RC = 32
CONV_LC = 512


def _conv_kernel(x_ref, st_ref, win_ref, bin_ref, wdw_ref, bdw_ref, cg_ref, cb_ref, wout_ref,
                 bout_ref, lg_ref, lb_ref, o_ref, so_ref, ubuf, ybuf, *, tt, nt):
    j = pl.program_id(1)

    @pl.when(j == 0)
    def _():
        ubuf[0:HALO, :] = st_ref[0]

    @pl.when(j > 0)
    def _():
        ubuf[0:HALO, :] = ubuf[tt:tt + HALO, :]

    x = x_ref[...]
    h = _bdot(x, win_ref[...]) + bin_ref[...]
    ubuf[HALO:HALO + tt, :] = h[:, :D] * _sigmoid(h[:, D:])
    first = HALO - (CONV_W - 1)
    for r0 in range(0, tt, CONV_RC):
        for c0 in range(0, D, CONV_LC):
            y = jnp.zeros((CONV_RC, CONV_LC), F32) + bdw_ref[:, c0:c0 + CONV_LC]
            for off in range(8):
                rows = CONV_RC + (8 if off else 0)
                acc = None
                for tap in range(CONV_W):
                    if (first + tap) % 8 != off:
                        continue
                    base = r0 + (first + tap) // 8 * 8
                    term = wdw_ref[tap:tap + 1, c0:c0 + CONV_LC] * ubuf[base:base + rows, c0:c0 + CONV_LC]
                    acc = term if acc is None else acc + term
                y = y + acc[off:off + CONV_RC]
            ybuf[r0:r0 + CONV_RC, c0:c0 + CONV_LC] = y
    z = _ln(ybuf[...], cg_ref[...], cb_ref[...])
    z = z * _sigmoid(z)
    mix = _bdot(z, wout_ref[...]) + bout_ref[...]
    o_ref[...] = _ln(ALPHA * x + mix, lg_ref[...], lb_ref[...])

    @pl.when(j == nt - 1)
    def _():
        so_ref[0] = ubuf[tt:tt + HALO, :]


def _conv_mixer(x, state, w, lnp, *, nb, t, tt, row_off, prev):
    nt = t // tt
    n_total = x.shape[0]
    win, bin_, wdw, bdw, cg, cb, wout, bout = w
    ins = [('tok', x), ('bat', state), ('const', win), ('const', bin_), ('const', wdw), ('const', bdw),
           ('const', cg), ('const', cb), ('const', wout), ('const', bout), ('const', lnp[0]), ('const', lnp[1])]
    outs = [('tok', (D,), F32), ('bat', (HALO, D), F32)]
    scratch = [pltpu.VMEM((HALO + tt, D), F32), pltpu.VMEM((tt, D), F32)]
    return _seq_call(functools.partial(_conv_kernel, tt=tt, nt=nt), name=f"conv_t{t}", nb=nb, nt=nt, tt=tt,
                     row_off=row_off,
                     n_total=n_total, ins=ins, outs=outs, scratch=scratch, prev=prev)


def _head_sum(y, hs_ref, hst_ref):
    s = _split_dot(y, hs_ref[...])
    return _split_dot(s, hst_ref[...])


def _rw1_kernel(x_ref, sh_ref, mu_ref, wr_ref, wk_ref, wv_ref, w0_ref, w1_ref, w2_ref, a0_ref, a1_ref,
                a2_ref, g1_ref, g2_ref, kk_ref, ka_ref, hs_ref, hst_ref,
                r_ref, w_ref, k_ref, v_ref, an_ref, b_ref, g_ref, xbuf, *, tt):
    j = pl.program_id(1)

    @pl.when(j == 0)
    def _():
        xbuf[7:8, :] = sh_ref[0]

    @pl.when(j > 0)
    def _():
        xbuf[7:8, :] = xbuf[7 + tt:8 + tt, :]

    x = x_ref[...]
    xbuf[8:8 + tt, :] = x
    xx = xbuf[7:7 + tt, :] - x
    mu = mu_ref[...]
    xr = x + xx * mu[0:1]
    xw = x + xx * mu[1:2]
    xk = x + xx * mu[2:3]
    xv = x + xx * mu[3:4]
    xa = x + xx * mu[4:5]
    xg = x + xx * mu[5:6]
    r = _bdot(xr, wr_ref[...])
    k = _bdot(xk, wk_ref[...])
    v = _bdot(xv, wv_ref[...])
    lw = w0_ref[...] + _bdot(jnp.tanh(_bdot(xw, w1_ref[...])), w2_ref[...])
    z = -lw
    log_w = -(jnp.maximum(z, 0.0) + jnp.log(1.0 + jnp.exp(-jnp.abs(z)))) - 0.5
    a = _sigmoid(a0_ref[...] + _bdot(_bdot(xa, a1_ref[...]), a2_ref[...]))
    g = _bdot(_sigmoid(_bdot(xg, g1_ref[...])), g2_ref[...])
    kk = k * kk_ref[...]
    ss = _head_sum(kk * kk, hs_ref, hst_ref)
    kk = kk * lax.rsqrt(jnp.maximum(ss, 1e-24))
    r_ref[...] = r
    w_ref[...] = jnp.exp(-jnp.exp(log_w))
    k_ref[...] = k * (1.0 + (a - 1.0) * ka_ref[...])
    v_ref[...] = v
    an_ref[...] = -kk
    b_ref[...] = kk * a
    g_ref[...] = g


def _rw1(x, shift, w, hs, hst, *, nb, t, tt, row_off):
    nt = t // tt
    prev = []
    ins = [('tok', x), ('bat', shift)] + [('const', a) for a in w] + [('const', hs), ('const', hst)]
    outs = [('tm', (D,), F32)] * N_SCAN_IN + [('own', (D,), F32)]
    scratch = [pltpu.VMEM((8 + tt, D), F32)]
    return _seq_call(functools.partial(_rw1_kernel, tt=tt), name=f"rwkv_proj_t{t}", nb=nb, nt=nt, tt=tt,
                     row_off=row_off,
                     n_total=x.shape[0], ins=ins, outs=outs, scratch=scratch, prev=prev)


def _scan_kernel(r_ref, w_ref, k_ref, v_ref, an_ref, b_ref, nxt_ref, s0_ref, o_ref, st_ref, S, sa_buf,
                 *, tc, nc):
    c = pl.program_id(1)

    @pl.when(c == 0)
    def _():
        S[...] = s0_ref[...]

        def init(kk, acc):
            return acc + S[kk] * an_ref[0, pl.ds(kk, 1), :]

        sa_buf[...] = lax.fori_loop(0, HD, init, jnp.zeros((HD, LANES), F32))

    def step(t, sa, a_next):
        vt = v_ref[t]

        def kbody(kk, acc):
            o_acc, sa_acc = acc
            sk = (S[kk] * w_ref[t, pl.ds(kk, 1), :] + sa * b_ref[t, pl.ds(kk, 1), :]
                  + vt * k_ref[t, pl.ds(kk, 1), :])
            S[kk] = sk
            return (o_acc + sk * r_ref[t, pl.ds(kk, 1), :], sa_acc + sk * a_next(kk))

        zero = jnp.zeros((HD, LANES), F32)
        o_acc, sa_next = lax.fori_loop(0, HD, kbody, (zero, zero), unroll=4)
        o_ref[t] = o_acc
        return sa_next

    sa = lax.fori_loop(0, tc - 1, lambda t, sa: step(t, sa, lambda kk: an_ref[t + 1, pl.ds(kk, 1), :]),
                       sa_buf[...])
    sa_buf[...] = step(tc - 1, sa, lambda kk: nxt_ref[0, pl.ds(kk, 1), :])

    @pl.when(c == nc - 1)
    def _():
        st_ref[...] = S[...]


def _scan(r, w, k, v, an, b, s0, *, tc):
    t, _, lanes = r.shape
    ng, nc = lanes // LANES, t // tc
    seq = pl.BlockSpec((tc, HD, LANES), lambda g, c: (c, 0, g))
    nxt = pl.BlockSpec((1, HD, LANES), lambda g, c: (jnp.minimum((c + 1) * tc, t - 1), 0, g))
    return pl.pallas_call(
        functools.partial(_scan_kernel, tc=tc, nc=nc), grid=(ng, nc),
        in_specs=[seq] * 6 + [nxt, pl.BlockSpec((HD, HD, LANES), lambda g, c: (0, 0, g))],
        out_specs=[seq, pl.BlockSpec((HD, HD, LANES), lambda g, c: (0, 0, g))],
        out_shape=[jax.ShapeDtypeStruct((t, HD, lanes), F32), jax.ShapeDtypeStruct((HD, HD, lanes), F32)],
        scratch_shapes=[pltpu.VMEM((HD, HD, LANES), F32), pltpu.VMEM((HD, LANES), F32)],
        name=f"wkv_scan_t{t}", compiler_params=_cp(("arbitrary", "arbitrary")))(r, w, k, v, an, b, an, s0)


N_SCAN_IN = 6


def _to_scan_kernel(*refs, tc):
    low = lax.broadcasted_iota(I32, (HD, LANES), 1) < HD
    for src, dst in zip(refs[:N_SCAN_IN], refs[N_SCAN_IN:]):
        def pair(i, carry, src=src, dst=dst):
            t = i * 2
            y0, y1 = src[t], src[t + 1]
            m = jnp.concatenate([y[:, LANES * p:LANES * (p + 1)] for y in (y0, y1) for p in range(8)], axis=0).T
            top, bot = m[:HD], m[HD:]
            dst[t] = jnp.where(low, top, pltpu.roll(bot, HD, 1))
            dst[t + 1] = jnp.where(low, pltpu.roll(top, HD, 1), bot)
            return carry

        lax.fori_loop(0, tc // 2, pair, 0)


def _to_scan_call(arrs, *, tc):
    t, nb, _ = arrs[0].shape
    ng, nc = nb // 8, t // tc
    return pl.pallas_call(
        functools.partial(_to_scan_kernel, tc=tc), grid=(ng, nc),
        in_specs=[pl.BlockSpec((tc, 8, D), lambda g, c: (c, g, 0))] * N_SCAN_IN,
        out_specs=[pl.BlockSpec((tc, HD, LANES), lambda g, c: (c, 0, g))] * N_SCAN_IN,
        out_shape=[jax.ShapeDtypeStruct((t, HD, ng * LANES), F32)] * N_SCAN_IN, name=f"wkv_relayout_t{t}",
        compiler_params=_cp(("arbitrary", "arbitrary")))(*arrs)


def _post_kernel(o_ref, r_ref, k_ref, v_ref, xg_ref, xb_ref, rk_ref, y_ref, *, tc):
    low = lax.broadcasted_iota(I32, (HD, LANES), 1) < HD

    def norm(t):
        o = o_ref[t]
        d = o - jnp.mean(o, axis=0, keepdims=True)
        vo = jnp.mean(d * d, axis=0, keepdims=True)
        bonus = jnp.sum(r_ref[t] * k_ref[t] * rk_ref[...], axis=0, keepdims=True) * v_ref[t]
        return d * lax.rsqrt(vo + LNX_EPS) * xg_ref[...] + xb_ref[...] + bonus

    def pair(i, carry):
        t = i * 2
        z0, z1 = norm(t), norm(t + 1)
        top = jnp.where(low, z0, pltpu.roll(z1, HD, 1))
        bot = jnp.where(low, pltpu.roll(z0, HD, 1), z1)
        m = jnp.concatenate([top, bot], axis=0).T
        y_ref[t] = jnp.concatenate([m[8 * p:8 * p + 8] for p in range(8)], axis=1)
        y_ref[t + 1] = jnp.concatenate([m[HD + 8 * p:HD + 8 * p + 8] for p in range(8)], axis=1)
        return carry

    lax.fori_loop(0, tc // 2, pair, 0)


def _post(o, r, k, v, xg, xb, rk, *, tc):
    t, _, lanes = o.shape
    ng, nc = lanes // LANES, t // tc
    seq = pl.BlockSpec((tc, HD, LANES), lambda g, c: (c, 0, g))
    const = pl.BlockSpec((HD, LANES), lambda g, c: (0, 0))
    return pl.pallas_call(
        functools.partial(_post_kernel, tc=tc), grid=(ng, nc),
        in_specs=[seq] * 4 + [const] * 3,
        out_specs=pl.BlockSpec((tc, 8, D), lambda g, c: (c, g, 0)),
        out_shape=jax.ShapeDtypeStruct((t, ng * 8, D), F32), name=f"wkv_post_t{t}",
        compiler_params=_cp(("arbitrary", "arbitrary")))(o, r, k, v, xg, xb, rk)


def _rw3_kernel(y_ref, g_ref, x_ref, wo_ref, lg_ref, lb_ref, out_ref):
    mix = _bdot(y_ref[...] * g_ref[...], wo_ref[...])
    out_ref[...] = _ln(ALPHA * x_ref[...] + mix, lg_ref[...], lb_ref[...])


def _head_major_spec(tt):
    return pl.BlockSpec((HEADS, tt, HD), lambda i: (0, i, 0))


def _tok_call(body, name, n, tt, toks, consts, outs):
    in_specs = [_head_major_spec(tt) if a.ndim == 3 else pl.BlockSpec((tt, a.shape[1]), lambda i: (i, 0))
                for a in toks]
    in_specs += [pl.BlockSpec(a.shape, lambda i, _n=a.ndim: (0,) * _n) for a in consts]
    out_specs = [_head_major_spec(tt) if w == 'heads' else pl.BlockSpec((tt, w), lambda i: (i, 0))
                 for w, _ in outs]
    out_shape = [jax.ShapeDtypeStruct((HEADS, n, HD) if w == 'heads' else (n, w), dt) for w, dt in outs]
    return pl.pallas_call(
        body, grid=(n // tt,), in_specs=in_specs, out_specs=out_specs, out_shape=out_shape, name=name,
        compiler_params=_cp(("arbitrary",)))(*toks, *consts)


def _qkv_kernel(x_ref, w_ref, b_ref, q_ref, k_ref, v_ref):
    h = _bdot(x_ref[...], w_ref[...]) + b_ref[...]
    for hd in range(HEADS):
        q_ref[hd] = h[:, hd * HD:(hd + 1) * HD].astype(BF16)
    k_ref[...] = h[:, HEADS * HD:HEADS * HD + KVH * HD]
    v_ref[...] = h[:, HEADS * HD + KVH * HD:]


def _attn_kernel(q_ref, kp_ref, vp_ref, bias_ref, sink_ref, prev_ref, o_ref, *, nq, kb, stride, mask_lo):
    del prev_ref
    c = pl.program_id(1)
    start = pl.multiple_of(c * stride, 8)
    kband = kp_ref[0, pl.ds(start, kb), :].astype(BF16)
    vband = vp_ref[0, pl.ds(start, kb), :].astype(BF16)
    valid = (start + lax.broadcasted_iota(I32, (1, kb), 1)) >= mask_lo
    for g in range(KVH):
        hs = slice(g * GROUP, (g + 1) * GROUP)
        qg = q_ref[hs].reshape(GROUP * nq, HD)
        kh = kband[:, g * HD:(g + 1) * HD]
        vh = vband[:, g * HD:(g + 1) * HD]
        logits = lax.dot_general(qg, kh, (((1,), (1,)), ((), ())), preferred_element_type=F32) * HD ** -0.5
        logits = jnp.where(valid, logits + bias_ref[hs].reshape(GROUP * nq, kb), -1e30)
        sink = sink_ref[hs].reshape(GROUP * nq, 1)
        m = jnp.maximum(jnp.max(logits, axis=-1, keepdims=True), sink)
        p = jnp.exp(logits - m)
        p = p / (jnp.sum(p, axis=-1, keepdims=True) + jnp.exp(sink - m))
        og = jnp.dot(p.astype(BF16), vh, preferred_element_type=F32)
        o_ref[hs] = og.reshape(GROUP, nq, HD).astype(BF16)


def _attn(q, kp, vp, bias, sinks, *, nb, nt, nq, kb, stride, mask_lo, row_off, prev):
    off = row_off // nq
    n_total = q.shape[1]
    heads = pl.BlockSpec((HEADS, nq, HD), lambda b, c: (0, off + b * nt + c, 0))
    const = lambda a: pl.BlockSpec(a.shape, lambda b, c, _n=a.ndim: (0,) * _n)
    bat = lambda a: pl.BlockSpec((1,) + a.shape[1:], lambda b, c: (b, 0, 0))
    sink_tab = jnp.broadcast_to(sinks.reshape(HEADS, 1, 1), (HEADS, nq, 1))
    if prev is None:
        prev = jnp.zeros((HEADS, n_total, HD), BF16)
    return pl.pallas_call(
        functools.partial(_attn_kernel, nq=nq, kb=kb, stride=stride, mask_lo=mask_lo), grid=(nb, nt),
        in_specs=[heads, bat(kp), bat(vp), const(bias), const(sink_tab), pl.BlockSpec(memory_space=pl.ANY)],
        out_specs=heads, out_shape=jax.ShapeDtypeStruct((HEADS, n_total, HD), BF16),
        input_output_aliases={5: 0}, name=f"attn_q{nq}",
        compiler_params=_cp(("arbitrary", "arbitrary")))(q, kp, vp, bias, sink_tab, prev)


def _oproj_kernel(o_ref, x_ref, wo_ref, bo_ref, lg_ref, lb_ref, out_ref):
    acc = jnp.dot(o_ref[0], wo_ref[0:HD, :], preferred_element_type=F32)
    for hd in range(1, HEADS):
        acc = acc + jnp.dot(o_ref[hd], wo_ref[hd * HD:(hd + 1) * HD, :], preferred_element_type=F32)
    out_ref[...] = _ln(ALPHA * x_ref[...] + acc + bo_ref[...], lg_ref[...], lb_ref[...])


def _t5_bucket(rel):
    half = N_BUCKETS // 2
    max_exact = half // 2
    ret = jnp.where(rel > 0, half, 0)
    n = jnp.abs(rel)
    nf = jnp.maximum(n, 1).astype(F32)
    large = max_exact + (jnp.log(nf / max_exact) / math.log(MAX_DISTANCE / max_exact)
                         * (half - max_exact)).astype(I32)
    large = jnp.minimum(large, half - 1)
    return ret + jnp.where(n < max_exact, n, large)


def _t5_bias(rel_bias, n_q, n_k):
    rel = jnp.arange(n_k)[None, :] - WINDOW - jnp.arange(n_q)[:, None]
    return jnp.transpose(rel_bias[_t5_bucket(rel)], (2, 0, 1))


def _route_kernel(x_ref, wr_ref, br_ref, idx_ref, gate_ref, rank_ref, cnt_ref, carry, *, tt):
    i = pl.program_id(0)

    @pl.when(i == 0)
    def _():
        carry[...] = jnp.zeros_like(carry)

    logits = lax.dot_general(wr_ref[...].astype(BF16), x_ref[...].astype(BF16), (((1,), (1,)), ((), ())),
                             preferred_element_type=F32) + br_ref[...]
    sub = lax.broadcasted_iota(I32, (N_EXP, tt), 0)
    out_row = lax.broadcasted_iota(I32, (8, tt), 0)
    vals, sels = [], []
    idx_out = jnp.zeros((8, tt), I32)
    work = logits
    for k in range(TOP_K):
        m = jnp.max(work, axis=0, keepdims=True)
        ik = jnp.min(jnp.where(work == m, sub, N_EXP), axis=0, keepdims=True)
        sel = sub == ik
        vals.append(m)
        sels.append(sel)
        idx_out = jnp.where(out_row == k, ik, idx_out)
        work = jnp.where(sel, -jnp.inf, work)
    es = [jnp.exp(v - vals[0]) for v in vals]
    den = es[0] + es[1] + es[2] + es[3]
    gate_out = jnp.zeros((8, tt), F32)
    for k in range(TOP_K):
        gate_out = jnp.where(out_row == k, es[k] / den, gate_out)
    onehot = jnp.zeros((N_EXP, tt), F32)
    for sel in sels:
        onehot = onehot + sel.astype(F32)
    before = (lax.broadcasted_iota(I32, (tt, tt), 0) < lax.broadcasted_iota(I32, (tt, tt), 1)).astype(BF16)
    base = carry[...] + jnp.dot(onehot.astype(BF16), before, preferred_element_type=F32)
    rank_out = jnp.zeros((8, tt), I32)
    for k in range(TOP_K):
        rk = jnp.sum(jnp.where(sels[k], base, 0.0), axis=0, keepdims=True)
        rank_out = jnp.where(out_row == k, rk.astype(I32), rank_out)
    carry[...] = carry[...] + jnp.sum(onehot, axis=1, keepdims=True)
    idx_ref[...] = idx_out
    gate_ref[...] = gate_out
    rank_ref[...] = rank_out
    cnt_ref[...] = carry[...]


def _route(x1, wr_t, br_col, *, tt):
    n = x1.shape[0]
    rows = pl.BlockSpec((8, tt), lambda i: (0, i))
    const = lambda a: pl.BlockSpec(a.shape, lambda i, _n=a.ndim: (0,) * _n)
    return pl.pallas_call(
        functools.partial(_route_kernel, tt=tt), grid=(n // tt,),
        in_specs=[pl.BlockSpec((tt, D), lambda i: (i, 0)), const(wr_t), const(br_col)],
        out_specs=[rows, rows, rows, pl.BlockSpec((N_EXP, 1), lambda i: (0, 0))],
        out_shape=[jax.ShapeDtypeStruct((8, n), I32), jax.ShapeDtypeStruct((8, n), F32),
                   jax.ShapeDtypeStruct((8, n), I32), jax.ShapeDtypeStruct((N_EXP, 1), F32)],
        scratch_shapes=[pltpu.VMEM((N_EXP, 1), F32)], name="moe_route",
        compiler_params=_cp(("arbitrary",)))(x1, wr_t, br_col)


def _row_copy(src, s, dst, d, sem):
    return pltpu.make_async_copy(src.at[pl.ds(s, 1), :], dst.at[pl.ds(d, 1), :], sem)


ROW_UNROLL = 8


def _disp_kernel(dest_ref, x_ref, xs_in, xs_ref, sem, *, tt):
    del xs_in

    def issue(r, carry):
        for k in range(TOP_K):
            _row_copy(x_ref, r, xs_ref, dest_ref[r * TOP_K + k], sem).start(priority=k % 2)
        return carry

    lax.fori_loop(0, tt, issue, 0, unroll=ROW_UNROLL)

    def drain(r, carry):
        for k in range(TOP_K):
            _row_copy(x_ref, 0, xs_ref, 0, sem).wait()
        return carry

    lax.fori_loop(0, tt, drain, 0, unroll=ROW_UNROLL)


def _dispatch(dest, x1, xs_prev, *, tt):
    n = x1.shape[0]
    smem_tok = pl.BlockSpec((tt * TOP_K,), lambda i: (i,), memory_space=pltpu.SMEM)
    return pl.pallas_call(
        functools.partial(_disp_kernel, tt=tt), grid=(n // tt,),
        in_specs=[smem_tok, pl.BlockSpec((tt, D), lambda i: (i, 0)), pl.BlockSpec(memory_space=pl.ANY)],
        out_specs=pl.BlockSpec(memory_space=pl.ANY),
        out_shape=jax.ShapeDtypeStruct(xs_prev.shape, F32),
        scratch_shapes=[pltpu.SemaphoreType.DMA],
        input_output_aliases={2: 0}, name="moe_dispatch",
        compiler_params=_cp(("arbitrary",)))(dest, x1, xs_prev)


def _expert_kernel(te_ref, nv_ref, xs_ref, w1_ref, b1_ref, w2_ref, b2_ref, y_ref, w1b, w2b):
    i = pl.program_id(0)
    valid = i < nv_ref[0]
    changed = jnp.logical_or(i == 0, te_ref[i] != te_ref[jnp.maximum(i - 1, 0)])

    @pl.when(jnp.logical_and(valid, changed))
    def _():
        for r0 in range(0, D, 256):
            w1b[r0:r0 + 256, :] = w1_ref[0, 0, r0:r0 + 256, :].astype(BF16)
            w2b[r0:r0 + 256, :] = w2_ref[0, 0, r0:r0 + 256, :].astype(BF16)

    @pl.when(valid)
    def _():
        h = jnp.dot(xs_ref[...].astype(BF16), w1b[...], preferred_element_type=F32) + b1_ref[0, 0]
        glu = jnp.minimum(h[:, :D], SWIGLU_LIMIT)
        lin = jnp.clip(h[:, D:], -SWIGLU_LIMIT, SWIGLU_LIMIT)
        act = glu * _sigmoid(SWIGLU_ALPHA * glu) * (lin + 1.0)
        y_ref[...] = jnp.dot(act.astype(BF16), w2b[...], preferred_element_type=F32) + b2_ref[0, 0]

    @pl.when(jnp.logical_not(valid))
    def _():
        y_ref[...] = jnp.zeros_like(y_ref)


def _experts(tile_expert, n_valid, xs, w1, b1, w2, b2, layer):
    n_tiles = xs.shape[0] // EXP_TILE
    grid_spec = pltpu.PrefetchScalarGridSpec(
        num_scalar_prefetch=2, grid=(n_tiles,),
        in_specs=[pl.BlockSpec((EXP_TILE, D), lambda i, te, nv: (i, 0)),
                  pl.BlockSpec((1, 1, D, 2 * D), lambda i, te, nv: (layer, te[i], 0, 0)),
                  pl.BlockSpec((1, 1, 1, 2 * D), lambda i, te, nv: (layer, te[i], 0, 0)),
                  pl.BlockSpec((1, 1, D, D), lambda i, te, nv: (layer, te[i], 0, 0)),
                  pl.BlockSpec((1, 1, 1, D), lambda i, te, nv: (layer, te[i], 0, 0))],
        out_specs=pl.BlockSpec((EXP_TILE, D), lambda i, te, nv: (i, 0)),
        scratch_shapes=[pltpu.VMEM((D, 2 * D), BF16), pltpu.VMEM((D, D), BF16)])
    return pl.pallas_call(
        _expert_kernel, grid_spec=grid_spec, out_shape=jax.ShapeDtypeStruct(xs.shape, F32), name="moe_experts",
        compiler_params=_cp(("arbitrary",)))(tile_expert, n_valid, xs, w1, b1, w2, b2)


def _comb_kernel(dest_ref, gate_ref, x1_ref, p_ref, y_ref, lg_ref, lb_ref, wg_ref, bg_ref,
                 wp_ref, o_ref, buf, sem, *, tt):
    def issue(r, carry):
        for k in range(TOP_K):
            _row_copy(y_ref, dest_ref[r * TOP_K + k], buf.at[k], r, sem).start(priority=k % 2)
        return carry

    lax.fori_loop(0, tt, issue, 0, unroll=ROW_UNROLL)

    def drain(r, carry):
        for k in range(TOP_K):
            _row_copy(y_ref, 0, buf.at[k], 0, sem).wait()
        return carry

    lax.fori_loop(0, tt, drain, 0, unroll=ROW_UNROLL)
    gate = gate_ref[...]
    moe = gate[:, 0:1] * buf[0]
    for k in range(1, TOP_K):
        moe = moe + gate[:, k:k + 1] * buf[k]
    x2 = _ln(ALPHA * x1_ref[...] + moe, lg_ref[...], lb_ref[...])
    gt = _sigmoid(_bdot(x2, wg_ref[...]) + bg_ref[...])
    o_ref[...] = x2 + gt * _bdot(p_ref[...], wp_ref[...])


def _combine(dest, gate, x1, p, y, lnp, wg, bg, wp, *, tt):
    n = x1.shape[0]
    smem_tok = pl.BlockSpec((tt * TOP_K,), lambda i: (i,), memory_space=pltpu.SMEM)
    tok = lambda w: pl.BlockSpec((tt, w), lambda i: (i, 0))
    const = lambda a: pl.BlockSpec(a.shape, lambda i, _n=a.ndim: (0,) * _n)
    return pl.pallas_call(
        functools.partial(_comb_kernel, tt=tt), grid=(n // tt,),
        in_specs=[smem_tok, tok(TOP_K), tok(D), tok(p.shape[1]), pl.BlockSpec(memory_space=pl.ANY),
                  const(lnp[0]), const(lnp[1]), const(wg), const(bg), const(wp)],
        out_specs=tok(D), out_shape=jax.ShapeDtypeStruct((n, D), F32),
        scratch_shapes=[pltpu.VMEM((TOP_K, tt, D), F32), pltpu.SemaphoreType.DMA], name="moe_combine_ple",
        compiler_params=_cp(("arbitrary",)))(dest, gate, x1, p, y, lnp[0], lnp[1], wg, bg, wp)


def _moe_ple(x1, p, xs_buf, wr_t, br_col, w1, b1, w2, b2, layer, lnp, wg, bg, wp, *, tt):
    n = x1.shape[0]
    idx, gate, rank, counts = _route(x1, wr_t, br_col, tt=512 if n % 512 == 0 else tt)
    counts = counts[:, 0].astype(I32)
    padded = (counts + EXP_TILE - 1) // EXP_TILE * EXP_TILE
    pad_end = jnp.cumsum(padded)
    pad_start = pad_end - padded
    n_tiles = xs_buf.shape[0] // EXP_TILE
    n_valid = (pad_end[-1] // EXP_TILE).astype(I32)
    tiles = jnp.minimum(jnp.arange(n_tiles, dtype=I32), n_valid - 1) * EXP_TILE
    tile_expert = jnp.minimum(jnp.sum((tiles[:, None] >= pad_end[None, :]).astype(I32), axis=1), N_EXP - 1)
    experts = jnp.arange(N_EXP, dtype=I32)
    start_of = jnp.sum(jnp.where(idx[:TOP_K, :, None] == experts, pad_start, 0), axis=-1)
    dest = (start_of + rank[:TOP_K]).T.reshape(-1)
    xs = _dispatch(dest, x1, xs_buf, tt=tt)
    y = _experts(tile_expert, n_valid.reshape(1), xs, w1, b1, w2, b2, layer)
    return _combine(dest, gate[:TOP_K].T, x1, p, y, lnp, wg, bg, wp, tt=tt), xs


def _state_to_scan(s, nb):
    s = s.reshape(nb // 8, 8, HEADS // 2, 2, HD, HD).transpose(5, 4, 0, 3, 2, 1)
    return s.reshape(HD, HD, nb * HEADS)


def _state_from_scan(s, nb):
    s = s.reshape(HD, HD, nb // 8, 2, HEADS // 2, 8).transpose(2, 5, 4, 3, 1, 0)
    return s.reshape(nb, HEADS, HD, HD)


def _head_vec_to_scan(a):
    a = a.reshape(HEADS // 2, 2, HD).transpose(2, 1, 0)
    return jnp.broadcast_to(a[..., None], (HD, 2, HEADS // 2, 8)).reshape(HD, LANES)


def _row2(a):
    return a.reshape(1, -1)


def _tile(n):
    for tt in (256, 128, 64, 32, 16, 8):
        if n % tt == 0:
            return tt
    raise ValueError(n)


def kernel(x_prompt, x_sample, p_prompt, p_sample, cache_conv, state_rwkv_shift, state_rwkv_wkv, cache_swa_k, cache_swa_v, conv_w_in, conv_b_in, conv_w_dw, conv_b_dw, conv_ln_g, conv_ln_b, conv_w_out, conv_b_out, rwkv_mu, rwkv_w_rkv, rwkv_w0, rwkv_w1, rwkv_w2, rwkv_a0, rwkv_a1, rwkv_a2, rwkv_g1, rwkv_g2, rwkv_k_k, rwkv_k_a, rwkv_r_k, rwkv_lnx_g, rwkv_lnx_b, rwkv_w_o, attn_w_qkv, attn_b_qkv, attn_sinks, attn_w_o, attn_b_o, rel_bias, ln_g, ln_b, moe_w_router, moe_b_router, moe_w1, moe_b1, moe_w2, moe_b2, ple_w_proj, ple_w_gate, ple_b_gate):
    bp, tp, _ = x_prompt.shape
    bs, ts, _ = x_sample.shape
    n_p, n_s = bp * tp, bs * ts
    n = n_p + n_s
    assert tp % 128 == 0 and n_p % ts == 0 and ts % 8 == 0 and ts <= CHUNK
    tt_tok = _tile(n)
    tt_p = 128
    x = jnp.concatenate([x_prompt.reshape(n_p, D), x_sample.reshape(n_s, D)], axis=0)
    p_all = jnp.concatenate([p_prompt.reshape(DEPTH, n_p, -1), p_sample.reshape(DEPTH, n_s, -1)], axis=1)
    n_rows = (-(-n * TOP_K // EXP_TILE) + N_EXP) * EXP_TILE
    xs_buf = jnp.zeros((n_rows, D), F32)
    head_sel = (jnp.arange(D)[:, None] // HD == jnp.arange(LANES)[None, :]).astype(BF16)
    head_sel_t = head_sel.T
    conv_p, conv_s, shift_p, shift_s, wkv_p, wkv_s = [], [], [], [], [], []
    swa_kp, swa_vp, swa_ks, swa_vs = [], [], [], []
    for i in range(DEPTH):
        kind, j = i % 3, i // 3
        lnp = (_row2(ln_g[i, 0]), _row2(ln_b[i, 0]))
        if kind == 0:
            cw = (conv_w_in[j].astype(BF16), _row2(conv_b_in[j]), conv_w_dw[j], _row2(conv_b_dw[j]),
                  _row2(conv_ln_g[j]), _row2(conv_ln_b[j]), conv_w_out[j].astype(BF16), _row2(conv_b_out[j]))
            st_p = jnp.zeros((bp, HALO, D), F32)
            st_s = jnp.pad(cache_conv[j], ((0, 0), (HALO - (CONV_W - 1), 0), (0, 0)))
            x1, so_p = _conv_mixer(x, st_p, cw, lnp, nb=bp, t=tp, tt=tt_p, row_off=0, prev=None)
            x1, so_s = _conv_mixer(x, st_s, cw, lnp, nb=bs, t=ts, tt=ts, row_off=n_p, prev=[x1])
            conv_p.append(so_p[:, HALO - (CONV_W - 1):])
            conv_s.append(so_s[:, HALO - (CONV_W - 1):])
        elif kind == 1:
            rw = (rwkv_mu[j], rwkv_w_rkv[j, 0].astype(BF16), rwkv_w_rkv[j, 1].astype(BF16),
                  rwkv_w_rkv[j, 2].astype(BF16), _row2(rwkv_w0[j]), rwkv_w1[j].astype(BF16),
                  rwkv_w2[j].astype(BF16), _row2(rwkv_a0[j]), rwkv_a1[j].astype(BF16), rwkv_a2[j].astype(BF16),
                  rwkv_g1[j].astype(BF16), rwkv_g2[j].astype(BF16), _row2(rwkv_k_k[j]), _row2(rwkv_k_a[j]))
            sh_p = jnp.zeros((bp, 1, D), F32)
            sh_s = state_rwkv_shift[j].reshape(bs, 1, D)
            post_c = [_head_vec_to_scan(a) for a in (rwkv_lnx_g[j], rwkv_lnx_b[j], rwkv_r_k[j].reshape(-1))]
            out_c = [('const', rwkv_w_o[j].astype(BF16)), ('const', lnp[0]), ('const', lnp[1])]
            x1, states = None, []
            for (lo, nb_, t_, tt_, sh, s0, tc) in ((0, bp, tp, tt_p, sh_p, None, 64),
                                                   (n_p, bs, ts, ts, sh_s, state_rwkv_wkv[j], ts)):
                r, w, k, v, an, b, g = _rw1(x, sh, rw, head_sel, head_sel_t, nb=nb_, t=t_, tt=tt_, row_off=lo)
                rs, ws, ks, vs, ans, bs_ = _to_scan_call([a.reshape(t_, nb_, D) for a in (r, w, k, v, an, b)],
                                                         tc=min(tc, 32))
                s0l = jnp.zeros((HD, HD, nb_ * HEADS), F32) if s0 is None else _state_to_scan(s0, nb_)
                o_l, s_l = _scan(rs, ws, ks, vs, ans, bs_, s0l, tc=tc)
                y = _post(o_l, rs, ks, vs, *post_c, tc=tc).reshape(t_, nb_ * D)
                x1, = _seq_call(_rw3_kernel, name=f"rwkv_out_t{t_}", nb=nb_, nt=t_ // tt_, tt=tt_, row_off=lo,
                                n_total=n, ins=[('tm', y), ('own', g), ('tok', x)] + out_c,
                                outs=[('tok', (D,), F32)], scratch=[], prev=None if x1 is None else [x1])
                states.append(_state_from_scan(s_l, nb_))
            shift_p.append(x[:n_p].reshape(bp, tp, D)[:, -1])
            shift_s.append(x[n_p:].reshape(bs, ts, D)[:, -1])
            wkv_p.append(states[0])
            wkv_s.append(states[1])
        else:
            q, kx, vx = _tok_call(_qkv_kernel, "attn_qkv", n, tt_tok, [x],
                                  [attn_w_qkv[j].astype(BF16), _row2(attn_b_qkv[j])],
                                  [('heads', BF16), (KVH * HD, F32), (KVH * HD, F32)])
            k_p = kx[:n_p].reshape(bp, tp, KVH * HD)
            v_p = vx[:n_p].reshape(bp, tp, KVH * HD)
            zpad = jnp.zeros((bp, WINDOW, KVH * HD), F32)
            nc = tp // CHUNK
            band = WINDOW + CHUNK
            o = _attn(q, jnp.concatenate([zpad, k_p], axis=1), jnp.concatenate([zpad, v_p], axis=1),
                      _t5_bias(rel_bias, CHUNK, band), attn_sinks[j], nb=bp, nt=nc, nq=CHUNK, kb=band,
                      stride=CHUNK, mask_lo=WINDOW, row_off=0, prev=None)
            k_all = jnp.concatenate([cache_swa_k[j].reshape(bs, WINDOW, KVH * HD),
                                     kx[n_p:].reshape(bs, ts, KVH * HD)], axis=1)
            v_all = jnp.concatenate([cache_swa_v[j].reshape(bs, WINDOW, KVH * HD),
                                     vx[n_p:].reshape(bs, ts, KVH * HD)], axis=1)
            o = _attn(q, k_all, v_all, _t5_bias(rel_bias, ts, WINDOW + ts), attn_sinks[j],
                      nb=bs, nt=1, nq=ts, kb=WINDOW + ts, stride=0, mask_lo=0, row_off=n_p, prev=o)
            x1, = _tok_call(_oproj_kernel, "attn_out", n, tt_tok, [o, x],
                            [attn_w_o[j].astype(BF16), _row2(attn_b_o[j]), lnp[0], lnp[1]], [(D, F32)])
            swa_kp.append(k_p[:, -WINDOW:].reshape(bp, WINDOW, KVH, HD))
            swa_vp.append(v_p[:, -WINDOW:].reshape(bp, WINDOW, KVH, HD))
            swa_ks.append(k_all[:, -WINDOW:].reshape(bs, WINDOW, KVH, HD))
            swa_vs.append(v_all[:, -WINDOW:].reshape(bs, WINDOW, KVH, HD))
        x, xs_buf = _moe_ple(x1, p_all[i], xs_buf, moe_w_router[i].T, moe_b_router[i].reshape(N_EXP, 1), moe_w1,
                             moe_b1.reshape(DEPTH, N_EXP, 1, 2 * D), moe_w2, moe_b2.reshape(DEPTH, N_EXP, 1, D), i,
                             (_row2(ln_g[i, 1]), _row2(ln_b[i, 1])),
                             ple_w_gate[i].astype(BF16), _row2(ple_b_gate[i]), ple_w_proj[i].astype(BF16), tt=tt_tok)
    return (x[:n_p].reshape(bp, tp, D), x[n_p:].reshape(bs, ts, D), jnp.stack(conv_p), jnp.stack(conv_s),
            jnp.stack(shift_p), jnp.stack(shift_s), jnp.stack(wkv_p), jnp.stack(wkv_s), jnp.stack(swa_kp),
            jnp.stack(swa_vp), jnp.stack(swa_ks), jnp.stack(swa_vs))
```

```python
import functools
import math

import jax
import jax.numpy as jnp
from jax import lax
from jax.experimental import pallas as pl
from jax.experimental.pallas import tpu as pltpu

F32 = jnp.float32
BF16 = jnp.bfloat16
I32 = jnp.int32

D = 1024
DEPTH = 4
CONV_W = 31
HALO = 32
HEADS = 16
HD = 64
KVH = 2
GROUP = HEADS // KVH
WINDOW = 128
CHUNK = 64
N_BUCKETS = 32
MAX_DISTANCE = 128
N_EXP = 32
TOP_K = 4
EXP_TILE = 512
EXP_COLS = 256
LANES = 128
LNX_EPS = 64e-5
LN_EPS = 1e-5
ALPHA = (2 * DEPTH) ** 0.25
SWIGLU_ALPHA = 1.702
SWIGLU_LIMIT = 7.0
VMEM_LIMIT = 56 * 1024 * 1024


def _cp(sem):
    return pltpu.CompilerParams(dimension_semantics=sem, vmem_limit_bytes=VMEM_LIMIT)


def _bdot(a, b):
    return jnp.dot(a.astype(BF16), b.astype(BF16), preferred_element_type=F32)


def _split(a):
    hi = a.astype(BF16)
    lo = (a - hi.astype(F32)).astype(BF16)
    return hi, lo


def _split_dot(a, b_exact):
    hi, lo = _split(a)
    return (jnp.dot(hi, b_exact, preferred_element_type=F32)
            + jnp.dot(lo, b_exact, preferred_element_type=F32))


def _ln(x, g, b, eps=LN_EPS):
    mu = jnp.mean(x, axis=-1, keepdims=True)
    xc = x - mu
    var = jnp.mean(xc * xc, axis=-1, keepdims=True)
    return xc * lax.rsqrt(var + eps) * g + b


def _sigmoid(x):
    return 1.0 / (1.0 + jnp.exp(-x))


def _seq_call(body, *, name, nb, nt, tt, row_off, n_total, ins, outs, scratch, prev=None):
    off = row_off // tt
    in_specs, args = [], []
    for kind, a in ins:
        if kind == 'tok':
            in_specs.append(pl.BlockSpec((tt, a.shape[1]), lambda b, j: (off + b * nt + j, 0)))
        elif kind == 'own':
            in_specs.append(pl.BlockSpec((tt, a.shape[1]), lambda b, j: (b * nt + j, 0)))
        elif kind == 'tm':
            in_specs.append(pl.BlockSpec((tt, a.shape[1] // nb), lambda b, j: (j, b)))
        elif kind == 'bat':
            in_specs.append(pl.BlockSpec((1,) + a.shape[1:], lambda b, j: (b, 0, 0)))
        else:
            in_specs.append(pl.BlockSpec(a.shape, lambda b, j, _n=a.ndim: (0,) * _n))
        args.append(a)
    out_specs, out_shapes = [], []
    for kind, tail, dt in outs:
        if kind == 'tok':
            out_specs.append(pl.BlockSpec((tt, tail[0]), lambda b, j: (off + b * nt + j, 0)))
            out_shapes.append(jax.ShapeDtypeStruct((n_total, tail[0]), dt))
        elif kind == 'own':
            out_specs.append(pl.BlockSpec((tt, tail[0]), lambda b, j: (b * nt + j, 0)))
            out_shapes.append(jax.ShapeDtypeStruct((nb * nt * tt, tail[0]), dt))
        elif kind == 'tm':
            out_specs.append(pl.BlockSpec((tt, tail[0]), lambda b, j: (j, b)))
            out_shapes.append(jax.ShapeDtypeStruct((nt * tt, nb * tail[0]), dt))
        else:
            out_specs.append(pl.BlockSpec((1,) + tuple(tail), lambda b, j: (b, 0, 0)))
            out_shapes.append(jax.ShapeDtypeStruct((nb,) + tuple(tail), dt))
    aliases = {}
    n_prev = 0
    tok_out = [i for i, o in enumerate(outs) if o[0] == 'tok']
    if prev is None:
        prev = [jnp.zeros((n_total, outs[i][1][0]), outs[i][2]) for i in tok_out]
    for p, oi in zip(prev, tok_out):
        aliases[len(args)] = oi
        in_specs.append(pl.BlockSpec(memory_space=pl.ANY))
        args.append(p)
        n_prev += 1
    n_in = len(ins)

    def wrapped(*refs):
        body(*refs[:n_in], *refs[n_in + n_prev:])

    return pl.pallas_call(
        wrapped, grid=(nb, nt), in_specs=in_specs, out_specs=out_specs, out_shape=out_shapes,
        scratch_shapes=scratch, input_output_aliases=aliases, name=name,
        compiler_params=_cp(("arbitrary", "arbitrary")))(*args)


CONV_RC = 32
CONV_LC = 512


def _conv_kernel(x_ref, st_ref, win_ref, bin_ref, wdw_ref, bdw_ref, cg_ref, cb_ref, wout_ref,
                 bout_ref, lg_ref, lb_ref, o_ref, so_ref, ubuf, ybuf, *, tt, nt):
    j = pl.program_id(1)

    @pl.when(j == 0)
    def _():
        ubuf[0:HALO, :] = st_ref[0]

    @pl.when(j > 0)
    def _():
        ubuf[0:HALO, :] = ubuf[tt:tt + HALO, :]

    x = x_ref[...]
    h = _bdot(x, win_ref[...]) + bin_ref[...]
    ubuf[HALO:HALO + tt, :] = h[:, :D] * _sigmoid(h[:, D:])
    first = HALO - (CONV_W - 1)
    for r0 in range(0, tt, CONV_RC):
        for c0 in range(0, D, CONV_LC):
            y = jnp.zeros((CONV_RC, CONV_LC), F32) + bdw_ref[:, c0:c0 + CONV_LC]
            for off in range(8):
                rows = CONV_RC + (8 if off else 0)
                acc = None
                for tap in range(CONV_W):
                    if (first + tap) % 8 != off:
                        continue
                    base = r0 + (first + tap) // 8 * 8
                    term = wdw_ref[tap:tap + 1, c0:c0 + CONV_LC] * ubuf[base:base + rows, c0:c0 + CONV_LC]
                    acc = term if acc is None else acc + term
                y = y + acc[off:off + CONV_RC]
            ybuf[r0:r0 + CONV_RC, c0:c0 + CONV_LC] = y
    z = _ln(ybuf[...], cg_ref[...], cb_ref[...])
    z = z * _sigmoid(z)
    mix = _bdot(z, wout_ref[...]) + bout_ref[...]
    o_ref[...] = _ln(ALPHA * x + mix, lg_ref[...], lb_ref[...])

    @pl.when(j == nt - 1)
    def _():
        so_ref[0] = ubuf[tt:tt + HALO, :]


def _conv_mixer(x, state, w, lnp, *, nb, t, tt, row_off, prev):
    nt = t // tt
    n_total = x.shape[0]
    win, bin_, wdw, bdw, cg, cb, wout, bout = w
    ins = [('tok', x), ('bat', state), ('const', win), ('const', bin_), ('const', wdw), ('const', bdw),
           ('const', cg), ('const', cb), ('const', wout), ('const', bout), ('const', lnp[0]), ('const', lnp[1])]
    outs = [('tok', (D,), F32), ('bat', (HALO, D), F32)]
    scratch = [pltpu.VMEM((HALO + tt, D), F32), pltpu.VMEM((tt, D), F32)]
    return _seq_call(functools.partial(_conv_kernel, tt=tt, nt=nt), name=f"conv_t{t}", nb=nb, nt=nt, tt=tt,
                     row_off=row_off,
                     n_total=n_total, ins=ins, outs=outs, scratch=scratch, prev=prev)


def _head_sum(y, hs_ref, hst_ref):
    s = _split_dot(y, hs_ref[...])
    return _split_dot(s, hst_ref[...])


def _rw1_kernel(x_ref, sh_ref, mu_ref, wr_ref, wk_ref, wv_ref, w0_ref, w1_ref, w2_ref, a0_ref, a1_ref,
                a2_ref, g1_ref, g2_ref, kk_ref, ka_ref, hs_ref, hst_ref,
                r_ref, w_ref, k_ref, v_ref, an_ref, b_ref, g_ref, xbuf, *, tt):
    j = pl.program_id(1)

    @pl.when(j == 0)
    def _():
        xbuf[7:8, :] = sh_ref[0]

    @pl.when(j > 0)
    def _():
        xbuf[7:8, :] = xbuf[7 + tt:8 + tt, :]

    x = x_ref[...]
    xbuf[8:8 + tt, :] = x
    xx = xbuf[7:7 + tt, :] - x
    mu = mu_ref[...]
    xr = x + xx * mu[0:1]
    xw = x + xx * mu[1:2]
    xk = x + xx * mu[2:3]
    xv = x + xx * mu[3:4]
    xa = x + xx * mu[4:5]
    xg = x + xx * mu[5:6]
    r = _bdot(xr, wr_ref[...])
    k = _bdot(xk, wk_ref[...])
    v = _bdot(xv, wv_ref[...])
    lw = w0_ref[...] + _bdot(jnp.tanh(_bdot(xw, w1_ref[...])), w2_ref[...])
    z = -lw
    log_w = -(jnp.maximum(z, 0.0) + jnp.log(1.0 + jnp.exp(-jnp.abs(z)))) - 0.5
    a = _sigmoid(a0_ref[...] + _bdot(_bdot(xa, a1_ref[...]), a2_ref[...]))
    g = _bdot(_sigmoid(_bdot(xg, g1_ref[...])), g2_ref[...])
    kk = k * kk_ref[...]
    ss = _head_sum(kk * kk, hs_ref, hst_ref)
    kk = kk * lax.rsqrt(jnp.maximum(ss, 1e-24))
    r_ref[...] = r
    w_ref[...] = jnp.exp(-jnp.exp(log_w))
    k_ref[...] = k * (1.0 + (a - 1.0) * ka_ref[...])
    v_ref[...] = v
    an_ref[...] = -kk
    b_ref[...] = kk * a
    g_ref[...] = g


def _rw1(x, shift, w, hs, hst, *, nb, t, tt, row_off):
    nt = t // tt
    prev = []
    ins = [('tok', x), ('bat', shift)] + [('const', a) for a in w] + [('const', hs), ('const', hst)]
    outs = [('tm', (D,), F32)] * N_SCAN_IN + [('own', (D,), F32)]
    scratch = [pltpu.VMEM((8 + tt, D), F32)]
    return _seq_call(functools.partial(_rw1_kernel, tt=tt), name=f"rwkv_proj_t{t}", nb=nb, nt=nt, tt=tt,
                     row_off=row_off,
                     n_total=x.shape[0], ins=ins, outs=outs, scratch=scratch, prev=prev)


def _scan_kernel(r_ref, w_ref, k_ref, v_ref, an_ref, b_ref, nxt_ref, s0_ref, o_ref, st_ref, S, sa_buf,
                 *, tc, nc):
    c = pl.program_id(1)

    @pl.when(c == 0)
    def _():
        S[...] = s0_ref[...]

        def init(kk, acc):
            return acc + S[kk] * an_ref[0, pl.ds(kk, 1), :]

        sa_buf[...] = lax.fori_loop(0, HD, init, jnp.zeros((HD, LANES), F32))

    def step(t, sa, a_next):
        vt = v_ref[t]

        def kbody(kk, acc):
            o_acc, sa_acc = acc
            sk = (S[kk] * w_ref[t, pl.ds(kk, 1), :] + sa * b_ref[t, pl.ds(kk, 1), :]
                  + vt * k_ref[t, pl.ds(kk, 1), :])
            S[kk] = sk
            return (o_acc + sk * r_ref[t, pl.ds(kk, 1), :], sa_acc + sk * a_next(kk))

        zero = jnp.zeros((HD, LANES), F32)
        o_acc, sa_next = lax.fori_loop(0, HD, kbody, (zero, zero), unroll=4)
        o_ref[t] = o_acc
        return sa_next

    sa = lax.fori_loop(0, tc - 1, lambda t, sa: step(t, sa, lambda kk: an_ref[t + 1, pl.ds(kk, 1), :]),
                       sa_buf[...])
    sa_buf[...] = step(tc - 1, sa, lambda kk: nxt_ref[0, pl.ds(kk, 1), :])

    @pl.when(c == nc - 1)
    def _():
        st_ref[...] = S[...]


def _scan(r, w, k, v, an, b, s0, *, tc):
    t, _, lanes = r.shape
    ng, nc = lanes // LANES, t // tc
    seq = pl.BlockSpec((tc, HD, LANES), lambda g, c: (c, 0, g))
    nxt = pl.BlockSpec((1, HD, LANES), lambda g, c: (jnp.minimum((c + 1) * tc, t - 1), 0, g))
    return pl.pallas_call(
        functools.partial(_scan_kernel, tc=tc, nc=nc), grid=(ng, nc),
        in_specs=[seq] * 6 + [nxt, pl.BlockSpec((HD, HD, LANES), lambda g, c: (0, 0, g))],
        out_specs=[seq, pl.BlockSpec((HD, HD, LANES), lambda g, c: (0, 0, g))],
        out_shape=[jax.ShapeDtypeStruct((t, HD, lanes), F32), jax.ShapeDtypeStruct((HD, HD, lanes), F32)],
        scratch_shapes=[pltpu.VMEM((HD, HD, LANES), F32), pltpu.VMEM((HD, LANES), F32)],
        name=f"wkv_scan_t{t}", compiler_params=_cp(("arbitrary", "arbitrary")))(r, w, k, v, an, b, an, s0)


N_SCAN_IN = 6


def _to_scan_kernel(*refs, tc):
    low = lax.broadcasted_iota(I32, (HD, LANES), 1) < HD

    def pair(i, carry):
        t = i * 2
        for src, dst in zip(refs[:N_SCAN_IN], refs[N_SCAN_IN:]):
            y0, y1 = src[t], src[t + 1]
            m = jnp.concatenate([y[:, LANES * p:LANES * (p + 1)] for y in (y0, y1) for p in range(8)], axis=0).T
            top, bot = m[:HD], m[HD:]
            dst[t] = jnp.where(low, top, pltpu.roll(bot, HD, 1))
            dst[t + 1] = jnp.where(low, pltpu.roll(top, HD, 1), bot)
        return carry

    lax.fori_loop(0, tc // 2, pair, 0, unroll=2)


def _to_scan_call(arrs, *, tc):
    t, nb, _ = arrs[0].shape
    ng, nc = nb // 8, t // tc
    return pl.pallas_call(
        functools.partial(_to_scan_kernel, tc=tc), grid=(ng, nc),
        in_specs=[pl.BlockSpec((tc, 8, D), lambda g, c: (c, g, 0))] * N_SCAN_IN,
        out_specs=[pl.BlockSpec((tc, HD, LANES), lambda g, c: (c, 0, g))] * N_SCAN_IN,
        out_shape=[jax.ShapeDtypeStruct((t, HD, ng * LANES), F32)] * N_SCAN_IN, name=f"wkv_relayout_t{t}",
        compiler_params=_cp(("arbitrary", "arbitrary")))(*arrs)


def _post_kernel(o_ref, r_ref, k_ref, v_ref, xg_ref, xb_ref, rk_ref, y_ref, *, tc):
    low = lax.broadcasted_iota(I32, (HD, LANES), 1) < HD

    def norm(t):
        o = o_ref[t]
        d = o - jnp.mean(o, axis=0, keepdims=True)
        vo = jnp.mean(d * d, axis=0, keepdims=True)
        bonus = jnp.sum(r_ref[t] * k_ref[t] * rk_ref[...], axis=0, keepdims=True) * v_ref[t]
        return d * lax.rsqrt(vo + LNX_EPS) * xg_ref[...] + xb_ref[...] + bonus

    def pair(i, carry):
        t = i * 2
        z0, z1 = norm(t), norm(t + 1)
        top = jnp.where(low, z0, pltpu.roll(z1, HD, 1))
        bot = jnp.where(low, pltpu.roll(z0, HD, 1), z1)
        m = jnp.concatenate([top, bot], axis=0).T
        y_ref[t] = jnp.concatenate([m[8 * p:8 * p + 8] for p in range(8)], axis=1)
        y_ref[t + 1] = jnp.concatenate([m[HD + 8 * p:HD + 8 * p + 8] for p in range(8)], axis=1)
        return carry

    lax.fori_loop(0, tc // 2, pair, 0, unroll=4)


def _post(o, r, k, v, xg, xb, rk, *, tc):
    t, _, lanes = o.shape
    ng, nc = lanes // LANES, t // tc
    seq = pl.BlockSpec((tc, HD, LANES), lambda g, c: (c, 0, g))
    const = pl.BlockSpec((HD, LANES), lambda g, c: (0, 0))
    return pl.pallas_call(
        functools.partial(_post_kernel, tc=tc), grid=(ng, nc),
        in_specs=[seq] * 4 + [const] * 3,
        out_specs=pl.BlockSpec((tc, 8, D), lambda g, c: (c, g, 0)),
        out_shape=jax.ShapeDtypeStruct((t, ng * 8, D), F32), name=f"wkv_post_t{t}",
        compiler_params=_cp(("arbitrary", "arbitrary")))(o, r, k, v, xg, xb, rk)


def _rw3_kernel(y_ref, g_ref, x_ref, wo_ref, lg_ref, lb_ref, out_ref):
    mix = _bdot(y_ref[...] * g_ref[...], wo_ref[...])
    out_ref[...] = _ln(ALPHA * x_ref[...] + mix, lg_ref[...], lb_ref[...])


def _head_major_spec(tt):
    return pl.BlockSpec((HEADS, tt, HD), lambda i: (0, i, 0))


def _tok_call(body, name, n, tt, toks, consts, outs):
    in_specs = [_head_major_spec(tt) if a.ndim == 3 else pl.BlockSpec((tt, a.shape[1]), lambda i: (i, 0))
                for a in toks]
    in_specs += [pl.BlockSpec(a.shape, lambda i, _n=a.ndim: (0,) * _n) for a in consts]
    out_specs = [_head_major_spec(tt) if w == 'heads' else pl.BlockSpec((tt, w), lambda i: (i, 0))
                 for w, _ in outs]
    out_shape = [jax.ShapeDtypeStruct((HEADS, n, HD) if w == 'heads' else (n, w), dt) for w, dt in outs]
    return pl.pallas_call(
        body, grid=(n // tt,), in_specs=in_specs, out_specs=out_specs, out_shape=out_shape, name=name,
        compiler_params=_cp(("arbitrary",)))(*toks, *consts)


def _qkv_kernel(x_ref, w_ref, b_ref, q_ref, k_ref, v_ref):
    h = _bdot(x_ref[...], w_ref[...]) + b_ref[...]
    for hd in range(HEADS):
        q_ref[hd] = h[:, hd * HD:(hd + 1) * HD].astype(BF16)
    k_ref[...] = h[:, HEADS * HD:HEADS * HD + KVH * HD]
    v_ref[...] = h[:, HEADS * HD + KVH * HD:]


def _attn_kernel(q_ref, kp_ref, vp_ref, bias_ref, sink_ref, prev_ref, o_ref, *, nq, nsub, kb, mask_lo):
    del prev_ref
    c = pl.program_id(1)
    start = pl.multiple_of(c * (nsub * nq), 8)
    span = kb + (nsub - 1) * nq
    kall = kp_ref[0, pl.ds(start, span), :].astype(BF16)
    vall = vp_ref[0, pl.ds(start, span), :].astype(BF16)
    for s in range(nsub):
        rows = slice(s * nq, (s + 1) * nq)
        valid = (start + s * nq + lax.broadcasted_iota(I32, (1, kb), 1)) >= mask_lo
        for g in range(KVH):
            hs = slice(g * GROUP, (g + 1) * GROUP)
            qg = q_ref[hs, rows, :].reshape(GROUP * nq, HD)
            kh = kall[s * nq:s * nq + kb, g * HD:(g + 1) * HD]
            vh = vall[s * nq:s * nq + kb, g * HD:(g + 1) * HD]
            logits = lax.dot_general(qg, kh, (((1,), (1,)), ((), ())), preferred_element_type=F32) * HD ** -0.5
            logits = jnp.where(valid, logits + bias_ref[hs].reshape(GROUP * nq, kb), -1e30)
            sink = sink_ref[hs].reshape(GROUP * nq, 1)
            m = jnp.maximum(jnp.max(logits, axis=-1, keepdims=True), sink)
            p = jnp.exp(logits - m)
            p = p / (jnp.sum(p, axis=-1, keepdims=True) + jnp.exp(sink - m))
            og = jnp.dot(p.astype(BF16), vh, preferred_element_type=F32)
            o_ref[hs, rows, :] = og.reshape(GROUP, nq, HD).astype(BF16)


def _attn(q, kp, vp, bias, sinks, *, nb, nt, nq, nsub, kb, mask_lo, row_off, prev):
    off = row_off // (nq * nsub)
    n_total = q.shape[1]
    heads = pl.BlockSpec((HEADS, nq * nsub, HD), lambda b, c: (0, off + b * nt + c, 0))
    const = lambda a: pl.BlockSpec(a.shape, lambda b, c, _n=a.ndim: (0,) * _n)
    bat = lambda a: pl.BlockSpec((1,) + a.shape[1:], lambda b, c: (b, 0, 0))
    sink_tab = jnp.broadcast_to(sinks.reshape(HEADS, 1, 1), (HEADS, nq, 1))
    if prev is None:
        prev = jnp.zeros((HEADS, n_total, HD), BF16)
    return pl.pallas_call(
        functools.partial(_attn_kernel, nq=nq, nsub=nsub, kb=kb, mask_lo=mask_lo), grid=(nb, nt),
        in_specs=[heads, bat(kp), bat(vp), const(bias), const(sink_tab), pl.BlockSpec(memory_space=pl.ANY)],
        out_specs=heads, out_shape=jax.ShapeDtypeStruct((HEADS, n_total, HD), BF16),
        input_output_aliases={5: 0}, name=f"attn_q{nq}",
        compiler_params=_cp(("arbitrary", "arbitrary")))(q, kp, vp, bias, sink_tab, prev)


def _oproj_kernel(o_ref, x_ref, wo_ref, bo_ref, lg_ref, lb_ref, out_ref):
    acc = jnp.dot(o_ref[0], wo_ref[0:HD, :], preferred_element_type=F32)
    for hd in range(1, HEADS):
        acc = acc + jnp.dot(o_ref[hd], wo_ref[hd * HD:(hd + 1) * HD, :], preferred_element_type=F32)
    out_ref[...] = _ln(ALPHA * x_ref[...] + acc + bo_ref[...], lg_ref[...], lb_ref[...])


def _t5_bucket(rel):
    half = N_BUCKETS // 2
    max_exact = half // 2
    ret = jnp.where(rel > 0, half, 0)
    n = jnp.abs(rel)
    nf = jnp.maximum(n, 1).astype(F32)
    large = max_exact + (jnp.log(nf / max_exact) / math.log(MAX_DISTANCE / max_exact)
                         * (half - max_exact)).astype(I32)
    large = jnp.minimum(large, half - 1)
    return ret + jnp.where(n < max_exact, n, large)


def _t5_bias(rel_bias, n_q, n_k):
    rel = jnp.arange(n_k)[None, :] - WINDOW - jnp.arange(n_q)[:, None]
    return jnp.transpose(rel_bias[_t5_bucket(rel)], (2, 0, 1))


def _route_kernel(x_ref, wr_ref, br_ref, idx_ref, gate_ref, rank_ref, cnt_ref, carry, *, tt):
    i = pl.program_id(0)

    @pl.when(i == 0)
    def _():
        carry[...] = jnp.zeros_like(carry)

    logits = lax.dot_general(wr_ref[...].astype(BF16), x_ref[...].astype(BF16), (((1,), (1,)), ((), ())),
                             preferred_element_type=F32) + br_ref[...]
    sub = lax.broadcasted_iota(I32, (N_EXP, tt), 0)
    out_row = lax.broadcasted_iota(I32, (8, tt), 0)
    vals, sels = [], []
    idx_out = jnp.zeros((8, tt), I32)
    work = logits
    for k in range(TOP_K):
        m = jnp.max(work, axis=0, keepdims=True)
        ik = jnp.min(jnp.where(work == m, sub, N_EXP), axis=0, keepdims=True)
        sel = sub == ik
        vals.append(m)
        sels.append(sel)
        idx_out = jnp.where(out_row == k, ik, idx_out)
        work = jnp.where(sel, -jnp.inf, work)
    es = [jnp.exp(v - vals[0]) for v in vals]
    den = es[0] + es[1] + es[2] + es[3]
    gate_out = jnp.zeros((8, tt), F32)
    for k in range(TOP_K):
        gate_out = jnp.where(out_row == k, es[k] / den, gate_out)
    onehot = jnp.zeros((N_EXP, tt), F32)
    for sel in sels:
        onehot = onehot + sel.astype(F32)
    before = (lax.broadcasted_iota(I32, (tt, tt), 0) < lax.broadcasted_iota(I32, (tt, tt), 1)).astype(BF16)
    base = carry[...] + jnp.dot(onehot.astype(BF16), before, preferred_element_type=F32)
    rank_out = jnp.zeros((8, tt), I32)
    for k in range(TOP_K):
        rk = jnp.sum(jnp.where(sels[k], base, 0.0), axis=0, keepdims=True)
        rank_out = jnp.where(out_row == k, rk.astype(I32), rank_out)
    carry[...] = carry[...] + jnp.sum(onehot, axis=1, keepdims=True)
    idx_ref[...] = idx_out
    gate_ref[...] = gate_out
    rank_ref[...] = rank_out
    cnt_ref[...] = carry[...]


def _route(x1, wr_t, br_col, *, tt):
    n = x1.shape[0]
    rows = pl.BlockSpec((8, tt), lambda i: (0, i))
    const = lambda a: pl.BlockSpec(a.shape, lambda i, _n=a.ndim: (0,) * _n)
    return pl.pallas_call(
        functools.partial(_route_kernel, tt=tt), grid=(n // tt,),
        in_specs=[pl.BlockSpec((tt, D), lambda i: (i, 0)), const(wr_t), const(br_col)],
        out_specs=[rows, rows, rows, pl.BlockSpec((N_EXP, 1), lambda i: (0, 0))],
        out_shape=[jax.ShapeDtypeStruct((8, n), I32), jax.ShapeDtypeStruct((8, n), F32),
                   jax.ShapeDtypeStruct((8, n), I32), jax.ShapeDtypeStruct((N_EXP, 1), F32)],
        scratch_shapes=[pltpu.VMEM((N_EXP, 1), F32)], name="moe_route",
        compiler_params=_cp(("arbitrary",)))(x1, wr_t, br_col)


def _row_copy(src, s, dst, d, sem):
    return pltpu.make_async_copy(src.at[pl.ds(s, 1), :], dst.at[pl.ds(d, 1), :], sem)


ROW_UNROLL = 8


def _disp_kernel(dest_ref, x_ref, xs_in, xs_ref, sem, *, tt):
    del xs_in

    def issue(r, carry):
        for k in range(TOP_K):
            _row_copy(x_ref, r, xs_ref, dest_ref[r * TOP_K + k], sem).start(priority=k % 2)
        return carry

    lax.fori_loop(0, tt, issue, 0, unroll=ROW_UNROLL)

    def drain(r, carry):
        for k in range(TOP_K):
            _row_copy(x_ref, 0, xs_ref, 0, sem).wait()
        return carry

    lax.fori_loop(0, tt, drain, 0, unroll=ROW_UNROLL)


def _dispatch(dest, x1, xs_prev, *, tt):
    n = x1.shape[0]
    smem_tok = pl.BlockSpec((tt * TOP_K,), lambda i: (i,), memory_space=pltpu.SMEM)
    return pl.pallas_call(
        functools.partial(_disp_kernel, tt=tt), grid=(n // tt,),
        in_specs=[smem_tok, pl.BlockSpec((tt, D), lambda i: (i, 0)), pl.BlockSpec(memory_space=pl.ANY)],
        out_specs=pl.BlockSpec(memory_space=pl.ANY),
        out_shape=jax.ShapeDtypeStruct(xs_prev.shape, F32),
        scratch_shapes=[pltpu.SemaphoreType.DMA],
        input_output_aliases={2: 0}, name="moe_dispatch",
        compiler_params=_cp(("arbitrary",)))(dest, x1, xs_prev)


def _expert_kernel(te_ref, nv_ref, xs_ref, w1_ref, b1_ref, w2_ref, b2_ref, y_ref, w1b, w2b):
    i = pl.program_id(0)
    valid = i < nv_ref[0]
    changed = jnp.logical_or(i == 0, te_ref[i] != te_ref[jnp.maximum(i - 1, 0)])

    @pl.when(jnp.logical_and(valid, changed))
    def _():
        for r0 in range(0, D, 256):
            w1b[r0:r0 + 256, :] = w1_ref[0, 0, r0:r0 + 256, :].astype(BF16)
            w2b[r0:r0 + 256, :] = w2_ref[0, 0, r0:r0 + 256, :].astype(BF16)

    @pl.when(valid)
    def _():
        x = xs_ref[...].astype(BF16)
        acc = None
        for c0 in range(0, D, EXP_COLS):
            hg = jnp.dot(x, w1b[:, c0:c0 + EXP_COLS], preferred_element_type=F32) + b1_ref[0, 0, :, c0:c0 + EXP_COLS]
            hl = (jnp.dot(x, w1b[:, D + c0:D + c0 + EXP_COLS], preferred_element_type=F32)
                  + b1_ref[0, 0, :, D + c0:D + c0 + EXP_COLS])
            glu = jnp.minimum(hg, SWIGLU_LIMIT)
            lin = jnp.clip(hl, -SWIGLU_LIMIT, SWIGLU_LIMIT)
            act = glu * _sigmoid(SWIGLU_ALPHA * glu) * (lin + 1.0)
            part = jnp.dot(act.astype(BF16), w2b[c0:c0 + EXP_COLS, :], preferred_element_type=F32)
            acc = part if acc is None else acc + part
        y_ref[...] = acc + b2_ref[0, 0]

    @pl.when(jnp.logical_not(valid))
    def _():
        y_ref[...] = jnp.zeros_like(y_ref)


def _experts(tile_expert, n_valid, xs, w1, b1, w2, b2, layer):
    n_tiles = xs.shape[0] // EXP_TILE
    grid_spec = pltpu.PrefetchScalarGridSpec(
        num_scalar_prefetch=2, grid=(n_tiles,),
        in_specs=[pl.BlockSpec((EXP_TILE, D), lambda i, te, nv: (i, 0)),
                  pl.BlockSpec((1, 1, D, 2 * D), lambda i, te, nv: (layer, te[i], 0, 0)),
                  pl.BlockSpec((1, 1, 1, 2 * D), lambda i, te, nv: (layer, te[i], 0, 0)),
                  pl.BlockSpec((1, 1, D, D), lambda i, te, nv: (layer, te[i], 0, 0)),
                  pl.BlockSpec((1, 1, 1, D), lambda i, te, nv: (layer, te[i], 0, 0))],
        out_specs=pl.BlockSpec((EXP_TILE, D), lambda i, te, nv: (i, 0)),
        scratch_shapes=[pltpu.VMEM((D, 2 * D), BF16), pltpu.VMEM((D, D), BF16)])
    return pl.pallas_call(
        _expert_kernel, grid_spec=grid_spec, out_shape=jax.ShapeDtypeStruct(xs.shape, F32), name="moe_experts",
        compiler_params=_cp(("arbitrary",)))(tile_expert, n_valid, xs, w1, b1, w2, b2)


def _comb_kernel(dest_ref, gate_ref, x1_ref, p_ref, y_ref, lg_ref, lb_ref, wg_ref, bg_ref,
                 wp_ref, o_ref, buf, sem, *, tt):
    def issue(r, carry):
        for k in range(TOP_K):
            _row_copy(y_ref, dest_ref[r * TOP_K + k], buf.at[k], r, sem).start(priority=k % 2)
        return carry

    lax.fori_loop(0, tt, issue, 0, unroll=ROW_UNROLL)

    def drain(r, carry):
        for k in range(TOP_K):
            _row_copy(y_ref, 0, buf.at[k], 0, sem).wait()
        return carry

    lax.fori_loop(0, tt, drain, 0, unroll=ROW_UNROLL)
    gate = gate_ref[...]
    moe = gate[:, 0:1] * buf[0]
    for k in range(1, TOP_K):
        moe = moe + gate[:, k:k + 1] * buf[k]
    x2 = _ln(ALPHA * x1_ref[...] + moe, lg_ref[...], lb_ref[...])
    gt = _sigmoid(_bdot(x2, wg_ref[...]) + bg_ref[...])
    o_ref[...] = x2 + gt * _bdot(p_ref[...], wp_ref[...])


def _combine(dest, gate, x1, p, y, lnp, wg, bg, wp, *, tt):
    n = x1.shape[0]
    smem_tok = pl.BlockSpec((tt * TOP_K,), lambda i: (i,), memory_space=pltpu.SMEM)
    tok = lambda w: pl.BlockSpec((tt, w), lambda i: (i, 0))
    const = lambda a: pl.BlockSpec(a.shape, lambda i, _n=a.ndim: (0,) * _n)
    return pl.pallas_call(
        functools.partial(_comb_kernel, tt=tt), grid=(n // tt,),
        in_specs=[smem_tok, tok(TOP_K), tok(D), tok(p.shape[1]), pl.BlockSpec(memory_space=pl.ANY),
                  const(lnp[0]), const(lnp[1]), const(wg), const(bg), const(wp)],
        out_specs=tok(D), out_shape=jax.ShapeDtypeStruct((n, D), F32),
        scratch_shapes=[pltpu.VMEM((TOP_K, tt, D), F32), pltpu.SemaphoreType.DMA], name="moe_combine_ple",
        compiler_params=_cp(("arbitrary",)))(dest, gate, x1, p, y, lnp[0], lnp[1], wg, bg, wp)


def _moe_ple(x1, p, xs_buf, wr_t, br_col, w1, b1, w2, b2, layer, lnp, wg, bg, wp, *, tt):
    n = x1.shape[0]
    idx, gate, rank, counts = _route(x1, wr_t, br_col, tt=512 if n % 512 == 0 else tt)
    counts = counts[:, 0].astype(I32)
    padded = (counts + EXP_TILE - 1) // EXP_TILE * EXP_TILE
    pad_end = jnp.cumsum(padded)
    pad_start = pad_end - padded
    n_tiles = xs_buf.shape[0] // EXP_TILE
    n_valid = (pad_end[-1] // EXP_TILE).astype(I32)
    tiles = jnp.minimum(jnp.arange(n_tiles, dtype=I32), n_valid - 1) * EXP_TILE
    tile_expert = jnp.minimum(jnp.sum((tiles[:, None] >= pad_end[None, :]).astype(I32), axis=1), N_EXP - 1)
    experts = jnp.arange(N_EXP, dtype=I32)
    start_of = jnp.sum(jnp.where(idx[:TOP_K, :, None] == experts, pad_start, 0), axis=-1)
    dest = (start_of + rank[:TOP_K]).T.reshape(-1)
    xs = _dispatch(dest, x1, xs_buf, tt=tt)
    y = _experts(tile_expert, n_valid.reshape(1), xs, w1, b1, w2, b2, layer)
    return _combine(dest, gate[:TOP_K].T, x1, p, y, lnp, wg, bg, wp, tt=tt), xs


def _state_to_scan(s, nb):
    s = s.reshape(nb // 8, 8, HEADS // 2, 2, HD, HD).transpose(5, 4, 0, 3, 2, 1)
    return s.reshape(HD, HD, nb * HEADS)


def _state_from_scan(s, nb):
    s = s.reshape(HD, HD, nb // 8, 2, HEADS // 2, 8).transpose(2, 5, 4, 3, 1, 0)
    return s.reshape(nb, HEADS, HD, HD)


def _head_vec_to_scan(a):
    a = a.reshape(HEADS // 2, 2, HD).transpose(2, 1, 0)
    return jnp.broadcast_to(a[..., None], (HD, 2, HEADS // 2, 8)).reshape(HD, LANES)


def _row2(a):
    return a.reshape(1, -1)


def _tile(n):
    for tt in (256, 128, 64, 32, 16, 8):
        if n % tt == 0:
            return tt
    raise ValueError(n)


def kernel(x_prompt, x_sample, p_prompt, p_sample, cache_conv, state_rwkv_shift, state_rwkv_wkv, cache_swa_k, cache_swa_v, conv_w_in, conv_b_in, conv_w_dw, conv_b_dw, conv_ln_g, conv_ln_b, conv_w_out, conv_b_out, rwkv_mu, rwkv_w_rkv, rwkv_w0, rwkv_w1, rwkv_w2, rwkv_a0, rwkv_a1, rwkv_a2, rwkv_g1, rwkv_g2, rwkv_k_k, rwkv_k_a, rwkv_r_k, rwkv_lnx_g, rwkv_lnx_b, rwkv_w_o, attn_w_qkv, attn_b_qkv, attn_sinks, attn_w_o, attn_b_o, rel_bias, ln_g, ln_b, moe_w_router, moe_b_router, moe_w1, moe_b1, moe_w2, moe_b2, ple_w_proj, ple_w_gate, ple_b_gate):
    bp, tp, _ = x_prompt.shape
    bs, ts, _ = x_sample.shape
    n_p, n_s = bp * tp, bs * ts
    n = n_p + n_s
    assert tp % 128 == 0 and n_p % ts == 0 and ts % 8 == 0 and ts <= CHUNK
    tt_tok = _tile(n)
    tt_p = 128
    x = jnp.concatenate([x_prompt.reshape(n_p, D), x_sample.reshape(n_s, D)], axis=0)
    p_all = jnp.concatenate([p_prompt.reshape(DEPTH, n_p, -1), p_sample.reshape(DEPTH, n_s, -1)], axis=1)
    n_rows = (-(-n * TOP_K // EXP_TILE) + N_EXP) * EXP_TILE
    xs_buf = jnp.zeros((n_rows, D), F32)
    head_sel = (jnp.arange(D)[:, None] // HD == jnp.arange(LANES)[None, :]).astype(BF16)
    head_sel_t = head_sel.T
    conv_p, conv_s, shift_p, shift_s, wkv_p, wkv_s = [], [], [], [], [], []
    swa_kp, swa_vp, swa_ks, swa_vs = [], [], [], []
    for i in range(DEPTH):
        kind, j = i % 3, i // 3
        lnp = (_row2(ln_g[i, 0]), _row2(ln_b[i, 0]))
        if kind == 0:
            cw = (conv_w_in[j].astype(BF16), _row2(conv_b_in[j]), conv_w_dw[j], _row2(conv_b_dw[j]),
                  _row2(conv_ln_g[j]), _row2(conv_ln_b[j]), conv_w_out[j].astype(BF16), _row2(conv_b_out[j]))
            st_p = jnp.zeros((bp, HALO, D), F32)
            st_s = jnp.pad(cache_conv[j], ((0, 0), (HALO - (CONV_W - 1), 0), (0, 0)))
            x1, so_p = _conv_mixer(x, st_p, cw, lnp, nb=bp, t=tp, tt=tt_p, row_off=0, prev=None)
            x1, so_s = _conv_mixer(x, st_s, cw, lnp, nb=bs, t=ts, tt=ts, row_off=n_p, prev=[x1])
            conv_p.append(so_p[:, HALO - (CONV_W - 1):])
            conv_s.append(so_s[:, HALO - (CONV_W - 1):])
        elif kind == 1:
            rw = (rwkv_mu[j], rwkv_w_rkv[j, 0].astype(BF16), rwkv_w_rkv[j, 1].astype(BF16),
                  rwkv_w_rkv[j, 2].astype(BF16), _row2(rwkv_w0[j]), rwkv_w1[j].astype(BF16),
                  rwkv_w2[j].astype(BF16), _row2(rwkv_a0[j]), rwkv_a1[j].astype(BF16), rwkv_a2[j].astype(BF16),
                  rwkv_g1[j].astype(BF16), rwkv_g2[j].astype(BF16), _row2(rwkv_k_k[j]), _row2(rwkv_k_a[j]))
            sh_p = jnp.zeros((bp, 1, D), F32)
            sh_s = state_rwkv_shift[j].reshape(bs, 1, D)
            post_c = [_head_vec_to_scan(a) for a in (rwkv_lnx_g[j], rwkv_lnx_b[j], rwkv_r_k[j].reshape(-1))]
            out_c = [('const', rwkv_w_o[j].astype(BF16)), ('const', lnp[0]), ('const', lnp[1])]
            x1, states = None, []
            for (lo, nb_, t_, tt_, sh, s0, tc) in ((0, bp, tp, tt_p, sh_p, None, 64),
                                                   (n_p, bs, ts, ts, sh_s, state_rwkv_wkv[j], ts)):
                r, w, k, v, an, b, g = _rw1(x, sh, rw, head_sel, head_sel_t, nb=nb_, t=t_, tt=tt_, row_off=lo)
                rs, ws, ks, vs, ans, bs_ = _to_scan_call([a.reshape(t_, nb_, D) for a in (r, w, k, v, an, b)],
                                                         tc=min(tc, 32))
                s0l = jnp.zeros((HD, HD, nb_ * HEADS), F32) if s0 is None else _state_to_scan(s0, nb_)
                o_l, s_l = _scan(rs, ws, ks, vs, ans, bs_, s0l, tc=tc)
                y = _post(o_l, rs, ks, vs, *post_c, tc=tc).reshape(t_, nb_ * D)
                x1, = _seq_call(_rw3_kernel, name=f"rwkv_out_t{t_}", nb=nb_, nt=t_ // tt_, tt=tt_, row_off=lo,
                                n_total=n, ins=[('tm', y), ('own', g), ('tok', x)] + out_c,
                                outs=[('tok', (D,), F32)], scratch=[], prev=None if x1 is None else [x1])
                states.append(_state_from_scan(s_l, nb_))
            shift_p.append(x[:n_p].reshape(bp, tp, D)[:, -1])
            shift_s.append(x[n_p:].reshape(bs, ts, D)[:, -1])
            wkv_p.append(states[0])
            wkv_s.append(states[1])
        else:
            q, kx, vx = _tok_call(_qkv_kernel, "attn_qkv", n, tt_tok, [x],
                                  [attn_w_qkv[j].astype(BF16), _row2(attn_b_qkv[j])],
                                  [('heads', BF16), (KVH * HD, F32), (KVH * HD, F32)])
            k_p = kx[:n_p].reshape(bp, tp, KVH * HD)
            v_p = vx[:n_p].reshape(bp, tp, KVH * HD)
            zpad = jnp.zeros((bp, WINDOW, KVH * HD), F32)
            nc = tp // CHUNK
            band = WINDOW + CHUNK
            o = _attn(q, jnp.concatenate([zpad, k_p], axis=1), jnp.concatenate([zpad, v_p], axis=1),
                      _t5_bias(rel_bias, CHUNK, band), attn_sinks[j], nb=bp, nt=nc // 2, nq=CHUNK, nsub=2, kb=band,
                      mask_lo=WINDOW, row_off=0, prev=None)
            k_all = jnp.concatenate([cache_swa_k[j].reshape(bs, WINDOW, KVH * HD),
                                     kx[n_p:].reshape(bs, ts, KVH * HD)], axis=1)
            v_all = jnp.concatenate([cache_swa_v[j].reshape(bs, WINDOW, KVH * HD),
                                     vx[n_p:].reshape(bs, ts, KVH * HD)], axis=1)
            o = _attn(q, k_all, v_all, _t5_bias(rel_bias, ts, WINDOW + ts), attn_sinks[j],
                      nb=bs, nt=1, nq=ts, nsub=1, kb=WINDOW + ts, mask_lo=0, row_off=n_p, prev=o)
            x1, = _tok_call(_oproj_kernel, "attn_out", n, tt_tok, [o, x],
                            [attn_w_o[j].astype(BF16), _row2(attn_b_o[j]), lnp[0], lnp[1]], [(D, F32)])
            swa_kp.append(k_p[:, -WINDOW:].reshape(bp, WINDOW, KVH, HD))
            swa_vp.append(v_p[:, -WINDOW:].reshape(bp, WINDOW, KVH, HD))
            swa_ks.append(k_all[:, -WINDOW:].reshape(bs, WINDOW, KVH, HD))
            swa_vs.append(v_all[:, -WINDOW:].reshape(bs, WINDOW, KVH, HD))
        x, xs_buf = _moe_ple(x1, p_all[i], xs_buf, moe_w_router[i].T, moe_b_router[i].reshape(N_EXP, 1), moe_w1,
                             moe_b1.reshape(DEPTH, N_EXP, 1, 2 * D), moe_w2, moe_b2.reshape(DEPTH, N_EXP, 1, D), i,
                             (_row2(ln_g[i, 1]), _row2(ln_b[i, 1])),
                             ple_w_gate[i].astype(BF16), _row2(ple_b_gate[i]), ple_w_proj[i].astype(BF16), tt=tt_tok)
    return (x[:n_p].reshape(bp, tp, D), x[n_p:].reshape(bs, ts, D), jnp.stack(conv_p), jnp.stack(conv_s),
            jnp.stack(shift_p), jnp.stack(shift_s), jnp.stack(wkv_p), jnp.stack(wkv_s), jnp.stack(swa_kp),
            jnp.stack(swa_vp), jnp.stack(swa_ks), jnp.stack(swa_vs))
```

```python
import functools
import math

import jax
import jax.numpy as jnp
from jax import lax
from jax.experimental import pallas as pl
from jax.experimental.pallas import tpu as pltpu

F32 = jnp.float32
BF16 = jnp.bfloat16
I32 = jnp.int32

D = 1024
DEPTH = 4
CONV_W = 31
HALO = 32
HEADS = 16
HD = 64
KVH = 2
GROUP = HEADS // KVH
WINDOW = 128
CHUNK = 64
N_BUCKETS = 32
MAX_DISTANCE = 128
N_EXP = 32
TOP_K = 4
EXP_TILE = 512
LANES = 128
LNX_EPS = 64e-5
LN_EPS = 1e-5
ALPHA = (2 * DEPTH) ** 0.25
SWIGLU_ALPHA = 1.702
SWIGLU_LIMIT = 7.0
VMEM_LIMIT = 56 * 1024 * 1024


def _cp(sem):
    return pltpu.CompilerParams(dimension_semantics=sem, vmem_limit_bytes=VMEM_LIMIT)


def _bdot(a, b):
    return jnp.dot(a.astype(BF16), b.astype(BF16), preferred_element_type=F32)


def _split(a):
    hi = a.astype(BF16)
    lo = (a - hi.astype(F32)).astype(BF16)
    return hi, lo


def _split_dot(a, b_exact):
    hi, lo = _split(a)
    return (jnp.dot(hi, b_exact, preferred_element_type=F32)
            + jnp.dot(lo, b_exact, preferred_element_type=F32))


def _ln(x, g, b, eps=LN_EPS):
    mu = jnp.mean(x, axis=-1, keepdims=True)
    xc = x - mu
    var = jnp.mean(xc * xc, axis=-1, keepdims=True)
    return xc * lax.rsqrt(var + eps) * g + b


def _sigmoid(x):
    return 1.0 / (1.0 + jnp.exp(-x))


def _seq_call(body, *, name, nb, nt, tt, row_off, n_total, ins, outs, scratch, prev=None):
    off = row_off // tt
    in_specs, args = [], []
    for kind, a in ins:
        if kind == 'tok':
            in_specs.append(pl.BlockSpec((tt, a.shape[1]), lambda b, j: (off + b * nt + j, 0)))
        elif kind == 'own':
            in_specs.append(pl.BlockSpec((tt, a.shape[1]), lambda b, j: (b * nt + j, 0)))
        elif kind == 'tm':
            in_specs.append(pl.BlockSpec((tt, a.shape[1] // nb), lambda b, j: (j, b)))
        elif kind == 'bat':
            in_specs.append(pl.BlockSpec((1,) + a.shape[1:], lambda b, j: (b, 0, 0)))
        else:
            in_specs.append(pl.BlockSpec(a.shape, lambda b, j, _n=a.ndim: (0,) * _n))
        args.append(a)
    out_specs, out_shapes = [], []
    for kind, tail, dt in outs:
        if kind == 'tok':
            out_specs.append(pl.BlockSpec((tt, tail[0]), lambda b, j: (off + b * nt + j, 0)))
            out_shapes.append(jax.ShapeDtypeStruct((n_total, tail[0]), dt))
        elif kind == 'own':
            out_specs.append(pl.BlockSpec((tt, tail[0]), lambda b, j: (b * nt + j, 0)))
            out_shapes.append(jax.ShapeDtypeStruct((nb * nt * tt, tail[0]), dt))
        elif kind == 'tm':
            out_specs.append(pl.BlockSpec((tt, tail[0]), lambda b, j: (j, b)))
            out_shapes.append(jax.ShapeDtypeStruct((nt * tt, nb * tail[0]), dt))
        else:
            out_specs.append(pl.BlockSpec((1,) + tuple(tail), lambda b, j: (b, 0, 0)))
            out_shapes.append(jax.ShapeDtypeStruct((nb,) + tuple(tail), dt))
    aliases = {}
    n_prev = 0
    tok_out = [i for i, o in enumerate(outs) if o[0] == 'tok']
    if prev is None:
        prev = [jnp.zeros((n_total, outs[i][1][0]), outs[i][2]) for i in tok_out]
    for p, oi in zip(prev, tok_out):
        aliases[len(args)] = oi
        in_specs.append(pl.BlockSpec(memory_space=pl.ANY))
        args.append(p)
        n_prev += 1
    n_in = len(ins)

    def wrapped(*refs):
        body(*refs[:n_in], *refs[n_in + n_prev:])

    return pl.pallas_call(
        wrapped, grid=(nb, nt), in_specs=in_specs, out_specs=out_specs, out_shape=out_shapes,
        scratch_shapes=scratch, input_output_aliases=aliases, name=name,
        compiler_params=_cp(("arbitrary", "arbitrary")))(*args)


CONV_RC = 32
CONV_LC = 512


def _conv_kernel(x_ref, st_ref, win_ref, bin_ref, wdw_ref, bdw_ref, cg_ref, cb_ref, wout_ref,
                 bout_ref, lg_ref, lb_ref, o_ref, so_ref, ubuf, ybuf, *, tt, nt):
    j = pl.program_id(1)

    @pl.when(j == 0)
    def _():
        ubuf[0:HALO, :] = st_ref[0]

    @pl.when(j > 0)
    def _():
        ubuf[0:HALO, :] = ubuf[tt:tt + HALO, :]

    x = x_ref[...]
    h = _bdot(x, win_ref[...]) + bin_ref[...]
    ubuf[HALO:HALO + tt, :] = h[:, :D] * _sigmoid(h[:, D:])
    first = HALO - (CONV_W - 1)
    for r0 in range(0, tt, CONV_RC):
        for c0 in range(0, D, CONV_LC):
            y = jnp.zeros((CONV_RC, CONV_LC), F32) + bdw_ref[:, c0:c0 + CONV_LC]
            for off in range(8):
                rows = CONV_RC + (8 if off else 0)
                acc = None
                for tap in range(CONV_W):
                    if (first + tap) % 8 != off:
                        continue
                    base = r0 + (first + tap) // 8 * 8
                    term = wdw_ref[tap:tap + 1, c0:c0 + CONV_LC] * ubuf[base:base + rows, c0:c0 + CONV_LC]
                    acc = term if acc is None else acc + term
                y = y + acc[off:off + CONV_RC]
            ybuf[r0:r0 + CONV_RC, c0:c0 + CONV_LC] = y
    z = _ln(ybuf[...], cg_ref[...], cb_ref[...])
    z = z * _sigmoid(z)
    mix = _bdot(z, wout_ref[...]) + bout_ref[...]
    o_ref[...] = _ln(ALPHA * x + mix, lg_ref[...], lb_ref[...])

    @pl.when(j == nt - 1)
    def _():
        so_ref[0] = ubuf[tt:tt + HALO, :]


def _conv_mixer(x, state, w, lnp, *, nb, t, tt, row_off, prev):
    nt = t // tt
    n_total = x.shape[0]
    win, bin_, wdw, bdw, cg, cb, wout, bout = w
    ins = [('tok', x), ('bat', state), ('const', win), ('const', bin_), ('const', wdw), ('const', bdw),
           ('const', cg), ('const', cb), ('const', wout), ('const', bout), ('const', lnp[0]), ('const', lnp[1])]
    outs = [('tok', (D,), F32), ('bat', (HALO, D), F32)]
    scratch = [pltpu.VMEM((HALO + tt, D), F32), pltpu.VMEM((tt, D), F32)]
    return _seq_call(functools.partial(_conv_kernel, tt=tt, nt=nt), name=f"conv_t{t}", nb=nb, nt=nt, tt=tt,
                     row_off=row_off,
                     n_total=n_total, ins=ins, outs=outs, scratch=scratch, prev=prev)


def _head_sum(y, hs_ref, hst_ref):
    s = _split_dot(y, hs_ref[...])
    return _split_dot(s, hst_ref[...])


def _rw1_kernel(x_ref, sh_ref, mu_ref, wr_ref, wk_ref, wv_ref, w0_ref, w1_ref, w2_ref, a0_ref, a1_ref,
                a2_ref, g1_ref, g2_ref, kk_ref, ka_ref, hs_ref, hst_ref,
                r_ref, w_ref, k_ref, v_ref, an_ref, b_ref, g_ref, xbuf, *, tt):
    j = pl.program_id(1)

    @pl.when(j == 0)
    def _():
        xbuf[7:8, :] = sh_ref[0]

    @pl.when(j > 0)
    def _():
        xbuf[7:8, :] = xbuf[7 + tt:8 + tt, :]

    x = x_ref[...]
    xbuf[8:8 + tt, :] = x
    xx = xbuf[7:7 + tt, :] - x
    mu = mu_ref[...]
    xr = x + xx * mu[0:1]
    xw = x + xx * mu[1:2]
    xk = x + xx * mu[2:3]
    xv = x + xx * mu[3:4]
    xa = x + xx * mu[4:5]
    xg = x + xx * mu[5:6]
    r = _bdot(xr, wr_ref[...])
    k = _bdot(xk, wk_ref[...])
    v = _bdot(xv, wv_ref[...])
    lw = w0_ref[...] + _bdot(jnp.tanh(_bdot(xw, w1_ref[...])), w2_ref[...])
    z = -lw
    log_w = -(jnp.maximum(z, 0.0) + jnp.log(1.0 + jnp.exp(-jnp.abs(z)))) - 0.5
    a = _sigmoid(a0_ref[...] + _bdot(_bdot(xa, a1_ref[...]), a2_ref[...]))
    g = _bdot(_sigmoid(_bdot(xg, g1_ref[...])), g2_ref[...])
    kk = k * kk_ref[...]
    ss = _head_sum(kk * kk, hs_ref, hst_ref)
    kk = kk * lax.rsqrt(jnp.maximum(ss, 1e-24))
    r_ref[...] = r
    w_ref[...] = jnp.exp(-jnp.exp(log_w))
    k_ref[...] = k * (1.0 + (a - 1.0) * ka_ref[...])
    v_ref[...] = v
    an_ref[...] = -kk
    b_ref[...] = kk * a
    g_ref[...] = g


def _rw1(x, shift, w, hs, hst, *, nb, t, tt, row_off):
    nt = t // tt
    prev = []
    ins = [('tok', x), ('bat', shift)] + [('const', a) for a in w] + [('const', hs), ('const', hst)]
    outs = [('tm', (D,), F32)] * N_SCAN_IN + [('own', (D,), F32)]
    scratch = [pltpu.VMEM((8 + tt, D), F32)]
    return _seq_call(functools.partial(_rw1_kernel, tt=tt), name=f"rwkv_proj_t{t}", nb=nb, nt=nt, tt=tt,
                     row_off=row_off,
                     n_total=x.shape[0], ins=ins, outs=outs, scratch=scratch, prev=prev)


def _scan_kernel(r_ref, w_ref, k_ref, v_ref, an_ref, b_ref, nxt_ref, s0_ref, o_ref, st_ref, S, sa_buf,
                 *, tc, nc):
    c = pl.program_id(1)

    @pl.when(c == 0)
    def _():
        S[...] = s0_ref[...]

        def init(kk, acc):
            return acc + S[kk] * an_ref[0, pl.ds(kk, 1), :]

        sa_buf[...] = lax.fori_loop(0, HD, init, jnp.zeros((HD, LANES), F32))

    def step(t, sa, a_next):
        vt = v_ref[t]

        def kbody(kk, acc):
            o_acc, sa_acc = acc
            sk = (S[kk] * w_ref[t, pl.ds(kk, 1), :] + sa * b_ref[t, pl.ds(kk, 1), :]
                  + vt * k_ref[t, pl.ds(kk, 1), :])
            S[kk] = sk
            return (o_acc + sk * r_ref[t, pl.ds(kk, 1), :], sa_acc + sk * a_next(kk))

        zero = jnp.zeros((HD, LANES), F32)
        o_acc, sa_next = lax.fori_loop(0, HD, kbody, (zero, zero), unroll=4)
        o_ref[t] = o_acc
        return sa_next

    sa = lax.fori_loop(0, tc - 1, lambda t, sa: step(t, sa, lambda kk: an_ref[t + 1, pl.ds(kk, 1), :]),
                       sa_buf[...])
    sa_buf[...] = step(tc - 1, sa, lambda kk: nxt_ref[0, pl.ds(kk, 1), :])

    @pl.when(c == nc - 1)
    def _():
        st_ref[...] = S[...]


def _scan(r, w, k, v, an, b, s0, *, tc):
    t, _, lanes = r.shape
    ng, nc = lanes // LANES, t // tc
    seq = pl.BlockSpec((tc, HD, LANES), lambda g, c: (c, 0, g))
    nxt = pl.BlockSpec((1, HD, LANES), lambda g, c: (jnp.minimum((c + 1) * tc, t - 1), 0, g))
    return pl.pallas_call(
        functools.partial(_scan_kernel, tc=tc, nc=nc), grid=(ng, nc),
        in_specs=[seq] * 6 + [nxt, pl.BlockSpec((HD, HD, LANES), lambda g, c: (0, 0, g))],
        out_specs=[seq, pl.BlockSpec((HD, HD, LANES), lambda g, c: (0, 0, g))],
        out_shape=[jax.ShapeDtypeStruct((t, HD, lanes), F32), jax.ShapeDtypeStruct((HD, HD, lanes), F32)],
        scratch_shapes=[pltpu.VMEM((HD, HD, LANES), F32), pltpu.VMEM((HD, LANES), F32)],
        name=f"wkv_scan_t{t}", compiler_params=_cp(("arbitrary", "arbitrary")))(r, w, k, v, an, b, an, s0)


N_SCAN_IN = 6


def _to_scan_kernel(*refs, tc):
    low = lax.broadcasted_iota(I32, (HD, LANES), 1) < HD

    def pair(i, carry):
        t = i * 2
        for src, dst in zip(refs[:N_SCAN_IN], refs[N_SCAN_IN:]):
            y0, y1 = src[t], src[t + 1]
            m = jnp.concatenate([y[:, LANES * p:LANES * (p + 1)] for y in (y0, y1) for p in range(8)], axis=0).T
            top, bot = m[:HD], m[HD:]
            dst[t] = jnp.where(low, top, pltpu.roll(bot, HD, 1))
            dst[t + 1] = jnp.where(low, pltpu.roll(top, HD, 1), bot)
        return carry

    lax.fori_loop(0, tc // 2, pair, 0, unroll=2)


def _to_scan_call(arrs, *, tc):
    t, nb, _ = arrs[0].shape
    ng, nc = nb // 8, t // tc
    return pl.pallas_call(
        functools.partial(_to_scan_kernel, tc=tc), grid=(ng, nc),
        in_specs=[pl.BlockSpec((tc, 8, D), lambda g, c: (c, g, 0))] * N_SCAN_IN,
        out_specs=[pl.BlockSpec((tc, HD, LANES), lambda g, c: (c, 0, g))] * N_SCAN_IN,
        out_shape=[jax.ShapeDtypeStruct((t, HD, ng * LANES), F32)] * N_SCAN_IN, name=f"wkv_relayout_t{t}",
        compiler_params=_cp(("arbitrary", "arbitrary")))(*arrs)


def _post_kernel(o_ref, r_ref, k_ref, v_ref, xg_ref, xb_ref, rk_ref, y_ref, *, tc):
    low = lax.broadcasted_iota(I32, (HD, LANES), 1) < HD

    def norm(t):
        o = o_ref[t]
        d = o - jnp.mean(o, axis=0, keepdims=True)
        vo = jnp.mean(d * d, axis=0, keepdims=True)
        bonus = jnp.sum(r_ref[t] * k_ref[t] * rk_ref[...], axis=0, keepdims=True) * v_ref[t]
        return d * lax.rsqrt(vo + LNX_EPS) * xg_ref[...] + xb_ref[...] + bonus

    def pair(i, carry):
        t = i * 2
        z0, z1 = norm(t), norm(t + 1)
        top = jnp.where(low, z0, pltpu.roll(z1, HD, 1))
        bot = jnp.where(low, pltpu.roll(z0, HD, 1), z1)
        m = jnp.concatenate([top, bot], axis=0).T
        y_ref[t] = jnp.concatenate([m[8 * p:8 * p + 8] for p in range(8)], axis=1)
        y_ref[t + 1] = jnp.concatenate([m[HD + 8 * p:HD + 8 * p + 8] for p in range(8)], axis=1)
        return carry

    lax.fori_loop(0, tc // 2, pair, 0, unroll=4)


def _post(o, r, k, v, xg, xb, rk, *, tc):
    t, _, lanes = o.shape
    ng, nc = lanes // LANES, t // tc
    seq = pl.BlockSpec((tc, HD, LANES), lambda g, c: (c, 0, g))
    const = pl.BlockSpec((HD, LANES), lambda g, c: (0, 0))
    return pl.pallas_call(
        functools.partial(_post_kernel, tc=tc), grid=(ng, nc),
        in_specs=[seq] * 4 + [const] * 3,
        out_specs=pl.BlockSpec((tc, 8, D), lambda g, c: (c, g, 0)),
        out_shape=jax.ShapeDtypeStruct((t, ng * 8, D), F32), name=f"wkv_post_t{t}",
        compiler_params=_cp(("arbitrary", "arbitrary")))(o, r, k, v, xg, xb, rk)


def _rw3_kernel(y_ref, g_ref, x_ref, wo_ref, lg_ref, lb_ref, out_ref):
    mix = _bdot(y_ref[...] * g_ref[...], wo_ref[...])
    out_ref[...] = _ln(ALPHA * x_ref[...] + mix, lg_ref[...], lb_ref[...])


def _head_major_spec(tt):
    return pl.BlockSpec((HEADS, tt, HD), lambda i: (0, i, 0))


def _tok_call(body, name, n, tt, toks, consts, outs):
    in_specs = [_head_major_spec(tt) if a.ndim == 3 else pl.BlockSpec((tt, a.shape[1]), lambda i: (i, 0))
                for a in toks]
    in_specs += [pl.BlockSpec(a.shape, lambda i, _n=a.ndim: (0,) * _n) for a in consts]
    out_specs = [_head_major_spec(tt) if w == 'heads' else pl.BlockSpec((tt, w), lambda i: (i, 0))
                 for w, _ in outs]
    out_shape = [jax.ShapeDtypeStruct((HEADS, n, HD) if w == 'heads' else (n, w), dt) for w, dt in outs]
    return pl.pallas_call(
        body, grid=(n // tt,), in_specs=in_specs, out_specs=out_specs, out_shape=out_shape, name=name,
        compiler_params=_cp(("arbitrary",)))(*toks, *consts)


def _qkv_kernel(x_ref, w_ref, b_ref, q_ref, k_ref, v_ref):
    h = _bdot(x_ref[...], w_ref[...]) + b_ref[...]
    for hd in range(HEADS):
        q_ref[hd] = h[:, hd * HD:(hd + 1) * HD].astype(BF16)
    k_ref[...] = h[:, HEADS * HD:HEADS * HD + KVH * HD]
    v_ref[...] = h[:, HEADS * HD + KVH * HD:]


def _attn_kernel(q_ref, kp_ref, vp_ref, bias_ref, sink_ref, prev_ref, o_ref, *, nq, nsub, kb, mask_lo):
    del prev_ref
    c = pl.program_id(1)
    start = pl.multiple_of(c * (nsub * nq), 8)
    span = kb + (nsub - 1) * nq
    kall = kp_ref[0, pl.ds(start, span), :].astype(BF16)
    vall = vp_ref[0, pl.ds(start, span), :].astype(BF16)
    for s in range(nsub):
        rows = slice(s * nq, (s + 1) * nq)
        valid = (start + s * nq + lax.broadcasted_iota(I32, (1, kb), 1)) >= mask_lo
        for g in range(KVH):
            hs = slice(g * GROUP, (g + 1) * GROUP)
            qg = q_ref[hs, rows, :].reshape(GROUP * nq, HD)
            kh = kall[s * nq:s * nq + kb, g * HD:(g + 1) * HD]
            vh = vall[s * nq:s * nq + kb, g * HD:(g + 1) * HD]
            logits = lax.dot_general(qg, kh, (((1,), (1,)), ((), ())), preferred_element_type=F32) * HD ** -0.5
            logits = jnp.where(valid, logits + bias_ref[hs].reshape(GROUP * nq, kb), -1e30)
            sink = sink_ref[hs].reshape(GROUP * nq, 1)
            m = jnp.maximum(jnp.max(logits, axis=-1, keepdims=True), sink)
            p = jnp.exp(logits - m)
            p = p / (jnp.sum(p, axis=-1, keepdims=True) + jnp.exp(sink - m))
            og = jnp.dot(p.astype(BF16), vh, preferred_element_type=F32)
            o_ref[hs, rows, :] = og.reshape(GROUP, nq, HD).astype(BF16)


def _attn(q, kp, vp, bias, sinks, *, nb, nt, nq, nsub, kb, mask_lo, row_off, prev):
    off = row_off // (nq * nsub)
    n_total = q.shape[1]
    heads = pl.BlockSpec((HEADS, nq * nsub, HD), lambda b, c: (0, off + b * nt + c, 0))
    const = lambda a: pl.BlockSpec(a.shape, lambda b, c, _n=a.ndim: (0,) * _n)
    bat = lambda a: pl.BlockSpec((1,) + a.shape[1:], lambda b, c: (b, 0, 0))
    sink_tab = jnp.broadcast_to(sinks.reshape(HEADS, 1, 1), (HEADS, nq, 1))
    if prev is None:
        prev = jnp.zeros((HEADS, n_total, HD), BF16)
    return pl.pallas_call(
        functools.partial(_attn_kernel, nq=nq, nsub=nsub, kb=kb, mask_lo=mask_lo), grid=(nb, nt),
        in_specs=[heads, bat(kp), bat(vp), const(bias), const(sink_tab), pl.BlockSpec(memory_space=pl.ANY)],
        out_specs=heads, out_shape=jax.ShapeDtypeStruct((HEADS, n_total, HD), BF16),
        input_output_aliases={5: 0}, name=f"attn_q{nq}",
        compiler_params=_cp(("arbitrary", "arbitrary")))(q, kp, vp, bias, sink_tab, prev)


def _oproj_kernel(o_ref, x_ref, wo_ref, bo_ref, lg_ref, lb_ref, out_ref):
    acc = jnp.dot(o_ref[0], wo_ref[0:HD, :], preferred_element_type=F32)
    for hd in range(1, HEADS):
        acc = acc + jnp.dot(o_ref[hd], wo_ref[hd * HD:(hd + 1) * HD, :], preferred_element_type=F32)
    out_ref[...] = _ln(ALPHA * x_ref[...] + acc + bo_ref[...], lg_ref[...], lb_ref[...])


def _t5_bucket(rel):
    half = N_BUCKETS // 2
    max_exact = half // 2
    ret = jnp.where(rel > 0, half, 0)
    n = jnp.abs(rel)
    nf = jnp.maximum(n, 1).astype(F32)
    large = max_exact + (jnp.log(nf / max_exact) / math.log(MAX_DISTANCE / max_exact)
                         * (half - max_exact)).astype(I32)
    large = jnp.minimum(large, half - 1)
    return ret + jnp.where(n < max_exact, n, large)


def _t5_bias(rel_bias, n_q, n_k):
    rel = jnp.arange(n_k)[None, :] - WINDOW - jnp.arange(n_q)[:, None]
    return jnp.transpose(rel_bias[_t5_bucket(rel)], (2, 0, 1))


def _route_kernel(x_ref, wr_ref, br_ref, idx_ref, gate_ref, rank_ref, cnt_ref, carry, *, tt):
    i = pl.program_id(0)

    @pl.when(i == 0)
    def _():
        carry[...] = jnp.zeros_like(carry)

    logits = lax.dot_general(wr_ref[...].astype(BF16), x_ref[...].astype(BF16), (((1,), (1,)), ((), ())),
                             preferred_element_type=F32) + br_ref[...]
    sub = lax.broadcasted_iota(I32, (N_EXP, tt), 0)
    out_row = lax.broadcasted_iota(I32, (8, tt), 0)
    vals, sels = [], []
    idx_out = jnp.zeros((8, tt), I32)
    work = logits
    for k in range(TOP_K):
        m = jnp.max(work, axis=0, keepdims=True)
        ik = jnp.min(jnp.where(work == m, sub, N_EXP), axis=0, keepdims=True)
        sel = sub == ik
        vals.append(m)
        sels.append(sel)
        idx_out = jnp.where(out_row == k, ik, idx_out)
        work = jnp.where(sel, -jnp.inf, work)
    es = [jnp.exp(v - vals[0]) for v in vals]
    den = es[0] + es[1] + es[2] + es[3]
    gate_out = jnp.zeros((8, tt), F32)
    for k in range(TOP_K):
        gate_out = jnp.where(out_row == k, es[k] / den, gate_out)
    onehot = jnp.zeros((N_EXP, tt), F32)
    for sel in sels:
        onehot = onehot + sel.astype(F32)
    before = (lax.broadcasted_iota(I32, (tt, tt), 0) < lax.broadcasted_iota(I32, (tt, tt), 1)).astype(BF16)
    base = carry[...] + jnp.dot(onehot.astype(BF16), before, preferred_element_type=F32)
    rank_out = jnp.zeros((8, tt), I32)
    for k in range(TOP_K):
        rk = jnp.sum(jnp.where(sels[k], base, 0.0), axis=0, keepdims=True)
        rank_out = jnp.where(out_row == k, rk.astype(I32), rank_out)
    carry[...] = carry[...] + jnp.sum(onehot, axis=1, keepdims=True)
    idx_ref[...] = idx_out
    gate_ref[...] = gate_out
    rank_ref[...] = rank_out
    cnt_ref[...] = carry[...]


def _route(x1, wr_t, br_col, *, tt):
    n = x1.shape[0]
    rows = pl.BlockSpec((8, tt), lambda i: (0, i))
    const = lambda a: pl.BlockSpec(a.shape, lambda i, _n=a.ndim: (0,) * _n)
    return pl.pallas_call(
        functools.partial(_route_kernel, tt=tt), grid=(n // tt,),
        in_specs=[pl.BlockSpec((tt, D), lambda i: (i, 0)), const(wr_t), const(br_col)],
        out_specs=[rows, rows, rows, pl.BlockSpec((N_EXP, 1), lambda i: (0, 0))],
        out_shape=[jax.ShapeDtypeStruct((8, n), I32), jax.ShapeDtypeStruct((8, n), F32),
                   jax.ShapeDtypeStruct((8, n), I32), jax.ShapeDtypeStruct((N_EXP, 1), F32)],
        scratch_shapes=[pltpu.VMEM((N_EXP, 1), F32)], name="moe_route",
        compiler_params=_cp(("arbitrary",)))(x1, wr_t, br_col)


SUB = D // LANES
assert SUB == 8


def _rows_to_tiles(dst_ref, x, n):
    for s in range(SUB):
        dst_ref[pl.ds(s, n, stride=SUB), :] = x[:, LANES * s:LANES * (s + 1)]


def _tiles_to_rows(src_ref, n):
    return jnp.concatenate([src_ref[pl.ds(s, n, stride=SUB), :] for s in range(SUB)], axis=1)


def _tile_copy(src, s, dst, d, sem):
    return pltpu.make_async_copy(src.at[pl.ds(pl.multiple_of(s * SUB, SUB), SUB), :],
                                 dst.at[pl.ds(pl.multiple_of(d * SUB, SUB), SUB), :], sem)


ROW_UNROLL = 8


def _disp_kernel(dest_ref, x_ref, xs_in, xs_ref, xt, sem, *, tt):
    del xs_in
    _rows_to_tiles(xt, x_ref[...], tt)

    def issue(r, carry):
        for k in range(TOP_K):
            _tile_copy(xt, r, xs_ref, dest_ref[r * TOP_K + k], sem).start(priority=k % 2)
        return carry

    lax.fori_loop(0, tt, issue, 0, unroll=ROW_UNROLL)

    def drain(r, carry):
        for k in range(TOP_K):
            _tile_copy(xt, 0, xs_ref, 0, sem).wait()
        return carry

    lax.fori_loop(0, tt, drain, 0, unroll=ROW_UNROLL)


def _dispatch(dest, x1, xs_prev, *, tt):
    n = x1.shape[0]
    smem_tok = pl.BlockSpec((tt * TOP_K,), lambda i: (i,), memory_space=pltpu.SMEM)
    return pl.pallas_call(
        functools.partial(_disp_kernel, tt=tt), grid=(n // tt,),
        in_specs=[smem_tok, pl.BlockSpec((tt, D), lambda i: (i, 0)), pl.BlockSpec(memory_space=pl.ANY)],
        out_specs=pl.BlockSpec(memory_space=pl.ANY),
        out_shape=jax.ShapeDtypeStruct(xs_prev.shape, F32),
        scratch_shapes=[pltpu.VMEM((tt * SUB, LANES), F32), pltpu.SemaphoreType.DMA],
        input_output_aliases={2: 0}, name="moe_dispatch",
        compiler_params=_cp(("arbitrary",)))(dest, x1, xs_prev)


def _expert_kernel(te_ref, nv_ref, xs_ref, w1_ref, b1_ref, w2_ref, b2_ref, y_ref, w1b, w2b):
    i = pl.program_id(0)
    valid = i < nv_ref[0]
    changed = jnp.logical_or(i == 0, te_ref[i] != te_ref[jnp.maximum(i - 1, 0)])

    @pl.when(jnp.logical_and(valid, changed))
    def _():
        for r0 in range(0, D, 256):
            w1b[r0:r0 + 256, :] = w1_ref[0, 0, r0:r0 + 256, :].astype(BF16)
            w2b[r0:r0 + 256, :] = w2_ref[0, 0, r0:r0 + 256, :].astype(BF16)

    @pl.when(valid)
    def _():
        x = _tiles_to_rows(xs_ref, EXP_TILE).astype(BF16)
        h = jnp.dot(x, w1b[...], preferred_element_type=F32) + b1_ref[0, 0]
        glu = jnp.minimum(h[:, :D], SWIGLU_LIMIT)
        lin = jnp.clip(h[:, D:], -SWIGLU_LIMIT, SWIGLU_LIMIT)
        act = glu * _sigmoid(SWIGLU_ALPHA * glu) * (lin + 1.0)
        y = jnp.dot(act.astype(BF16), w2b[...], preferred_element_type=F32) + b2_ref[0, 0]
        _rows_to_tiles(y_ref, y, EXP_TILE)

    @pl.when(jnp.logical_not(valid))
    def _():
        y_ref[...] = jnp.zeros_like(y_ref)


def _experts(tile_expert, n_valid, xs, w1, b1, w2, b2, layer):
    n_tiles = xs.shape[0] // (EXP_TILE * SUB)
    grid_spec = pltpu.PrefetchScalarGridSpec(
        num_scalar_prefetch=2, grid=(n_tiles,),
        in_specs=[pl.BlockSpec((EXP_TILE * SUB, LANES), lambda i, te, nv: (i, 0)),
                  pl.BlockSpec((1, 1, D, 2 * D), lambda i, te, nv: (layer, te[i], 0, 0)),
                  pl.BlockSpec((1, 1, 1, 2 * D), lambda i, te, nv: (layer, te[i], 0, 0)),
                  pl.BlockSpec((1, 1, D, D), lambda i, te, nv: (layer, te[i], 0, 0)),
                  pl.BlockSpec((1, 1, 1, D), lambda i, te, nv: (layer, te[i], 0, 0))],
        out_specs=pl.BlockSpec((EXP_TILE * SUB, LANES), lambda i, te, nv: (i, 0)),
        scratch_shapes=[pltpu.VMEM((D, 2 * D), BF16), pltpu.VMEM((D, D), BF16)])
    return pl.pallas_call(
        _expert_kernel, grid_spec=grid_spec, out_shape=jax.ShapeDtypeStruct(xs.shape, F32), name="moe_experts",
        compiler_params=_cp(("arbitrary",)))(tile_expert, n_valid, xs, w1, b1, w2, b2)


def _comb_kernel(dest_ref, gate_ref, x1_ref, p_ref, y_ref, lg_ref, lb_ref, wg_ref, bg_ref,
                 wp_ref, o_ref, buf, sem, *, tt):
    def issue(r, carry):
        for k in range(TOP_K):
            _tile_copy(y_ref, dest_ref[r * TOP_K + k], buf.at[k], r, sem).start(priority=k % 2)
        return carry

    lax.fori_loop(0, tt, issue, 0, unroll=ROW_UNROLL)

    def drain(r, carry):
        for k in range(TOP_K):
            _tile_copy(y_ref, 0, buf.at[k], 0, sem).wait()
        return carry

    lax.fori_loop(0, tt, drain, 0, unroll=ROW_UNROLL)
    gate = gate_ref[...]
    moe = gate[:, 0:1] * _tiles_to_rows(buf.at[0], tt)
    for k in range(1, TOP_K):
        moe = moe + gate[:, k:k + 1] * _tiles_to_rows(buf.at[k], tt)
    x2 = _ln(ALPHA * x1_ref[...] + moe, lg_ref[...], lb_ref[...])
    gt = _sigmoid(_bdot(x2, wg_ref[...]) + bg_ref[...])
    o_ref[...] = x2 + gt * _bdot(p_ref[...], wp_ref[...])


def _combine(dest, gate, x1, p, y, lnp, wg, bg, wp, *, tt):
    n = x1.shape[0]
    smem_tok = pl.BlockSpec((tt * TOP_K,), lambda i: (i,), memory_space=pltpu.SMEM)
    tok = lambda w: pl.BlockSpec((tt, w), lambda i: (i, 0))
    const = lambda a: pl.BlockSpec(a.shape, lambda i, _n=a.ndim: (0,) * _n)
    return pl.pallas_call(
        functools.partial(_comb_kernel, tt=tt), grid=(n // tt,),
        in_specs=[smem_tok, tok(TOP_K), tok(D), tok(p.shape[1]), pl.BlockSpec(memory_space=pl.ANY),
                  const(lnp[0]), const(lnp[1]), const(wg), const(bg), const(wp)],
        out_specs=tok(D), out_shape=jax.ShapeDtypeStruct((n, D), F32),
        scratch_shapes=[pltpu.VMEM((TOP_K, tt * SUB, LANES), F32), pltpu.SemaphoreType.DMA],
        name="moe_combine_ple",
        compiler_params=_cp(("arbitrary",)))(dest, gate, x1, p, y, lnp[0], lnp[1], wg, bg, wp)


def _moe_ple(x1, p, xs_buf, wr_t, br_col, w1, b1, w2, b2, layer, lnp, wg, bg, wp, *, tt):
    n = x1.shape[0]
    idx, gate, rank, counts = _route(x1, wr_t, br_col, tt=512 if n % 512 == 0 else tt)
    counts = counts[:, 0].astype(I32)
    padded = (counts + EXP_TILE - 1) // EXP_TILE * EXP_TILE
    pad_end = jnp.cumsum(padded)
    pad_start = pad_end - padded
    n_tiles = xs_buf.shape[0] // (EXP_TILE * SUB)
    n_valid = (pad_end[-1] // EXP_TILE).astype(I32)
    tiles = jnp.minimum(jnp.arange(n_tiles, dtype=I32), n_valid - 1) * EXP_TILE
    tile_expert = jnp.minimum(jnp.sum((tiles[:, None] >= pad_end[None, :]).astype(I32), axis=1), N_EXP - 1)
    experts = jnp.arange(N_EXP, dtype=I32)
    start_of = jnp.sum(jnp.where(idx[:TOP_K, :, None] == experts, pad_start, 0), axis=-1)
    dest = (start_of + rank[:TOP_K]).T.reshape(-1)
    xs = _dispatch(dest, x1, xs_buf, tt=tt)
    y = _experts(tile_expert, n_valid.reshape(1), xs, w1, b1, w2, b2, layer)
    return _combine(dest, gate[:TOP_K].T, x1, p, y, lnp, wg, bg, wp, tt=tt), xs


def _state_to_scan(s, nb):
    s = s.reshape(nb // 8, 8, HEADS // 2, 2, HD, HD).transpose(5, 4, 0, 3, 2, 1)
    return s.reshape(HD, HD, nb * HEADS)


def _state_from_scan(s, nb):
    s = s.reshape(HD, HD, nb // 8, 2, HEADS // 2, 8).transpose(2, 5, 4, 3, 1, 0)
    return s.reshape(nb, HEADS, HD, HD)


def _head_vec_to_scan(a):
    a = a.reshape(HEADS // 2, 2, HD).transpose(2, 1, 0)
    return jnp.broadcast_to(a[..., None], (HD, 2, HEADS // 2, 8)).reshape(HD, LANES)


def _row2(a):
    return a.reshape(1, -1)


def _tile(n):
    for tt in (256, 128, 64, 32, 16, 8):
        if n % tt == 0:
            return tt
    raise ValueError(n)


def kernel(x_prompt, x_sample, p_prompt, p_sample, cache_conv, state_rwkv_shift, state_rwkv_wkv, cache_swa_k, cache_swa_v, conv_w_in, conv_b_in, conv_w_dw, conv_b_dw, conv_ln_g, conv_ln_b, conv_w_out, conv_b_out, rwkv_mu, rwkv_w_rkv, rwkv_w0, rwkv_w1, rwkv_w2, rwkv_a0, rwkv_a1, rwkv_a2, rwkv_g1, rwkv_g2, rwkv_k_k, rwkv_k_a, rwkv_r_k, rwkv_lnx_g, rwkv_lnx_b, rwkv_w_o, attn_w_qkv, attn_b_qkv, attn_sinks, attn_w_o, attn_b_o, rel_bias, ln_g, ln_b, moe_w_router, moe_b_router, moe_w1, moe_b1, moe_w2, moe_b2, ple_w_proj, ple_w_gate, ple_b_gate):
    bp, tp, _ = x_prompt.shape
    bs, ts, _ = x_sample.shape
    n_p, n_s = bp * tp, bs * ts
    n = n_p + n_s
    assert tp % 128 == 0 and n_p % ts == 0 and ts % 8 == 0 and ts <= CHUNK
    tt_tok = _tile(n)
    tt_p = 128
    x = jnp.concatenate([x_prompt.reshape(n_p, D), x_sample.reshape(n_s, D)], axis=0)
    p_all = jnp.concatenate([p_prompt.reshape(DEPTH, n_p, -1), p_sample.reshape(DEPTH, n_s, -1)], axis=1)
    n_rows = (-(-n * TOP_K // EXP_TILE) + N_EXP) * EXP_TILE
    xs_buf = jnp.zeros((n_rows * SUB, LANES), F32)
    head_sel = (jnp.arange(D)[:, None] // HD == jnp.arange(LANES)[None, :]).astype(BF16)
    head_sel_t = head_sel.T
    conv_p, conv_s, shift_p, shift_s, wkv_p, wkv_s = [], [], [], [], [], []
    swa_kp, swa_vp, swa_ks, swa_vs = [], [], [], []
    for i in range(DEPTH):
        kind, j = i % 3, i // 3
        lnp = (_row2(ln_g[i, 0]), _row2(ln_b[i, 0]))
        if kind == 0:
            cw = (conv_w_in[j].astype(BF16), _row2(conv_b_in[j]), conv_w_dw[j], _row2(conv_b_dw[j]),
                  _row2(conv_ln_g[j]), _row2(conv_ln_b[j]), conv_w_out[j].astype(BF16), _row2(conv_b_out[j]))
            st_p = jnp.zeros((bp, HALO, D), F32)
            st_s = jnp.pad(cache_conv[j], ((0, 0), (HALO - (CONV_W - 1), 0), (0, 0)))
            x1, so_p = _conv_mixer(x, st_p, cw, lnp, nb=bp, t=tp, tt=tt_p, row_off=0, prev=None)
            x1, so_s = _conv_mixer(x, st_s, cw, lnp, nb=bs, t=ts, tt=ts, row_off=n_p, prev=[x1])
            conv_p.append(so_p[:, HALO - (CONV_W - 1):])
            conv_s.append(so_s[:, HALO - (CONV_W - 1):])
        elif kind == 1:
            rw = (rwkv_mu[j], rwkv_w_rkv[j, 0].astype(BF16), rwkv_w_rkv[j, 1].astype(BF16),
                  rwkv_w_rkv[j, 2].astype(BF16), _row2(rwkv_w0[j]), rwkv_w1[j].astype(BF16),
                  rwkv_w2[j].astype(BF16), _row2(rwkv_a0[j]), rwkv_a1[j].astype(BF16), rwkv_a2[j].astype(BF16),
                  rwkv_g1[j].astype(BF16), rwkv_g2[j].astype(BF16), _row2(rwkv_k_k[j]), _row2(rwkv_k_a[j]))
            sh_p = jnp.zeros((bp, 1, D), F32)
            sh_s = state_rwkv_shift[j].reshape(bs, 1, D)
            post_c = [_head_vec_to_scan(a) for a in (rwkv_lnx_g[j], rwkv_lnx_b[j], rwkv_r_k[j].reshape(-1))]
            out_c = [('const', rwkv_w_o[j].astype(BF16)), ('const', lnp[0]), ('const', lnp[1])]
            x1, states = None, []
            for (lo, nb_, t_, tt_, sh, s0, tc) in ((0, bp, tp, tt_p, sh_p, None, 64),
                                                   (n_p, bs, ts, ts, sh_s, state_rwkv_wkv[j], ts)):
                r, w, k, v, an, b, g = _rw1(x, sh, rw, head_sel, head_sel_t, nb=nb_, t=t_, tt=tt_, row_off=lo)
                rs, ws, ks, vs, ans, bs_ = _to_scan_call([a.reshape(t_, nb_, D) for a in (r, w, k, v, an, b)],
                                                         tc=min(tc, 32))
                s0l = jnp.zeros((HD, HD, nb_ * HEADS), F32) if s0 is None else _state_to_scan(s0, nb_)
                o_l, s_l = _scan(rs, ws, ks, vs, ans, bs_, s0l, tc=tc)
                y = _post(o_l, rs, ks, vs, *post_c, tc=tc).reshape(t_, nb_ * D)
                x1, = _seq_call(_rw3_kernel, name=f"rwkv_out_t{t_}", nb=nb_, nt=t_ // tt_, tt=tt_, row_off=lo,
                                n_total=n, ins=[('tm', y), ('own', g), ('tok', x)] + out_c,
                                outs=[('tok', (D,), F32)], scratch=[], prev=None if x1 is None else [x1])
                states.append(_state_from_scan(s_l, nb_))
            shift_p.append(x[:n_p].reshape(bp, tp, D)[:, -1])
            shift_s.append(x[n_p:].reshape(bs, ts, D)[:, -1])
            wkv_p.append(states[0])
            wkv_s.append(states[1])
        else:
            q, kx, vx = _tok_call(_qkv_kernel, "attn_qkv", n, tt_tok, [x],
                                  [attn_w_qkv[j].astype(BF16), _row2(attn_b_qkv[j])],
                                  [('heads', BF16), (KVH * HD, F32), (KVH * HD, F32)])
            k_p = kx[:n_p].reshape(bp, tp, KVH * HD)
            v_p = vx[:n_p].reshape(bp, tp, KVH * HD)
            zpad = jnp.zeros((bp, WINDOW, KVH * HD), F32)
            nc = tp // CHUNK
            band = WINDOW + CHUNK
            o = _attn(q, jnp.concatenate([zpad, k_p], axis=1), jnp.concatenate([zpad, v_p], axis=1),
                      _t5_bias(rel_bias, CHUNK, band), attn_sinks[j], nb=bp, nt=nc, nq=CHUNK, nsub=1, kb=band,
                      mask_lo=WINDOW, row_off=0, prev=None)
            k_all = jnp.concatenate([cache_swa_k[j].reshape(bs, WINDOW, KVH * HD),
                                     kx[n_p:].reshape(bs, ts, KVH * HD)], axis=1)
            v_all = jnp.concatenate([cache_swa_v[j].reshape(bs, WINDOW, KVH * HD),
                                     vx[n_p:].reshape(bs, ts, KVH * HD)], axis=1)
            o = _attn(q, k_all, v_all, _t5_bias(rel_bias, ts, WINDOW + ts), attn_sinks[j],
                      nb=bs, nt=1, nq=ts, nsub=1, kb=WINDOW + ts, mask_lo=0, row_off=n_p, prev=o)
            x1, = _tok_call(_oproj_kernel, "attn_out", n, tt_tok, [o, x],
                            [attn_w_o[j].astype(BF16), _row2(attn_b_o[j]), lnp[0], lnp[1]], [(D, F32)])
            swa_kp.append(k_p[:, -WINDOW:].reshape(bp, WINDOW, KVH, HD))
            swa_vp.append(v_p[:, -WINDOW:].reshape(bp, WINDOW, KVH, HD))
            swa_ks.append(k_all[:, -WINDOW:].reshape(bs, WINDOW, KVH, HD))
            swa_vs.append(v_all[:, -WINDOW:].reshape(bs, WINDOW, KVH, HD))
        x, xs_buf = _moe_ple(x1, p_all[i], xs_buf, moe_w_router[i].T, moe_b_router[i].reshape(N_EXP, 1), moe_w1,
                             moe_b1.reshape(DEPTH, N_EXP, 1, 2 * D), moe_w2, moe_b2.reshape(DEPTH, N_EXP, 1, D), i,
                             (_row2(ln_g[i, 1]), _row2(ln_b[i, 1])),
                             ple_w_gate[i].astype(BF16), _row2(ple_b_gate[i]), ple_w_proj[i].astype(BF16), tt=tt_tok)
    return (x[:n_p].reshape(bp, tp, D), x[n_p:].reshape(bs, ts, D), jnp.stack(conv_p), jnp.stack(conv_s),
            jnp.stack(shift_p), jnp.stack(shift_s), jnp.stack(wkv_p), jnp.stack(wkv_s), jnp.stack(swa_kp),
            jnp.stack(swa_vp), jnp.stack(swa_ks), jnp.stack(swa_vs))
```

```python
import functools
import math

import jax
import jax.numpy as jnp
from jax import lax
from jax.experimental import pallas as pl
from jax.experimental.pallas import tpu as pltpu

F32 = jnp.float32
BF16 = jnp.bfloat16
I32 = jnp.int32

D = 1024
DEPTH = 4
CONV_W = 31
HALO = 32
HEADS = 16
HD = 64
KVH = 2
GROUP = HEADS // KVH
WINDOW = 128
CHUNK = 64
N_BUCKETS = 32
MAX_DISTANCE = 128
N_EXP = 32
TOP_K = 4
EXP_TILE = 512
LANES = 128
LNX_EPS = 64e-5
LN_EPS = 1e-5
ALPHA = (2 * DEPTH) ** 0.25
SWIGLU_ALPHA = 1.702
SWIGLU_LIMIT = 7.0
VMEM_LIMIT = 56 * 1024 * 1024


def _cp(sem):
    return pltpu.CompilerParams(dimension_semantics=sem, vmem_limit_bytes=VMEM_LIMIT)


def _bdot(a, b):
    return jnp.dot(a.astype(BF16), b.astype(BF16), preferred_element_type=F32)


def _split(a):
    hi = a.astype(BF16)
    lo = (a - hi.astype(F32)).astype(BF16)
    return hi, lo


def _split_dot(a, b_exact):
    hi, lo = _split(a)
    return (jnp.dot(hi, b_exact, preferred_element_type=F32)
            + jnp.dot(lo, b_exact, preferred_element_type=F32))


def _ln(x, g, b, eps=LN_EPS):
    mu = jnp.mean(x, axis=-1, keepdims=True)
    xc = x - mu
    var = jnp.mean(xc * xc, axis=-1, keepdims=True)
    return xc * lax.rsqrt(var + eps) * g + b


def _sigmoid(x):
    return 1.0 / (1.0 + jnp.exp(-x))


def _seq_call(body, *, name, nb, nt, tt, row_off, n_total, ins, outs, scratch, prev=None):
    off = row_off // tt
    in_specs, args = [], []
    for kind, a in ins:
        if kind == 'tok':
            in_specs.append(pl.BlockSpec((tt, a.shape[1]), lambda b, j: (off + b * nt + j, 0)))
        elif kind == 'own':
            in_specs.append(pl.BlockSpec((tt, a.shape[1]), lambda b, j: (b * nt + j, 0)))
        elif kind == 'tm':
            in_specs.append(pl.BlockSpec((tt, a.shape[1] // nb), lambda b, j: (j, b)))
        elif kind == 'bat':
            in_specs.append(pl.BlockSpec((1,) + a.shape[1:], lambda b, j: (b, 0, 0)))
        else:
            in_specs.append(pl.BlockSpec(a.shape, lambda b, j, _n=a.ndim: (0,) * _n))
        args.append(a)
    out_specs, out_shapes = [], []
    for kind, tail, dt in outs:
        if kind == 'tok':
            out_specs.append(pl.BlockSpec((tt, tail[0]), lambda b, j: (off + b * nt + j, 0)))
            out_shapes.append(jax.ShapeDtypeStruct((n_total, tail[0]), dt))
        elif kind == 'own':
            out_specs.append(pl.BlockSpec((tt, tail[0]), lambda b, j: (b * nt + j, 0)))
            out_shapes.append(jax.ShapeDtypeStruct((nb * nt * tt, tail[0]), dt))
        elif kind == 'tm':
            out_specs.append(pl.BlockSpec((tt, tail[0]), lambda b, j: (j, b)))
            out_shapes.append(jax.ShapeDtypeStruct((nt * tt, nb * tail[0]), dt))
        else:
            out_specs.append(pl.BlockSpec((1,) + tuple(tail), lambda b, j: (b, 0, 0)))
            out_shapes.append(jax.ShapeDtypeStruct((nb,) + tuple(tail), dt))
    aliases = {}
    n_prev = 0
    tok_out = [i for i, o in enumerate(outs) if o[0] == 'tok']
    if prev is None:
        prev = [jnp.zeros((n_total, outs[i][1][0]), outs[i][2]) for i in tok_out]
    for p, oi in zip(prev, tok_out):
        aliases[len(args)] = oi
        in_specs.append(pl.BlockSpec(memory_space=pl.ANY))
        args.append(p)
        n_prev += 1
    n_in = len(ins)

    def wrapped(*refs):
        body(*refs[:n_in], *refs[n_in + n_prev:])

    return pl.pallas_call(
        wrapped, grid=(nb, nt), in_specs=in_specs, out_specs=out_specs, out_shape=out_shapes,
        scratch_shapes=scratch, input_output_aliases=aliases, name=name,
        compiler_params=_cp(("arbitrary", "arbitrary")))(*args)


CONV_RC = 64
CONV_LC = 256


def _conv_kernel(x_ref, st_ref, win_ref, bin_ref, wdw_ref, bdw_ref, cg_ref, cb_ref, wout_ref,
                 bout_ref, lg_ref, lb_ref, o_ref, so_ref, ubuf, ybuf, *, tt, nt):
    j = pl.program_id(1)

    @pl.when(j == 0)
    def _():
        ubuf[0:HALO, :] = st_ref[0]

    @pl.when(j > 0)
    def _():
        ubuf[0:HALO, :] = ubuf[tt:tt + HALO, :]

    x = x_ref[...]
    h = _bdot(x, win_ref[...]) + bin_ref[...]
    ubuf[HALO:HALO + tt, :] = h[:, :D] * _sigmoid(h[:, D:])
    first = HALO - (CONV_W - 1)
    rc = min(CONV_RC, tt)
    for r0 in range(0, tt, rc):
        for c0 in range(0, D, CONV_LC):
            y = jnp.zeros((rc, CONV_LC), F32) + bdw_ref[:, c0:c0 + CONV_LC]
            for off in range(8):
                rows = rc + (8 if off else 0)
                acc = None
                for tap in range(CONV_W):
                    if (first + tap) % 8 != off:
                        continue
                    base = r0 + (first + tap) // 8 * 8
                    term = wdw_ref[tap:tap + 1, c0:c0 + CONV_LC] * ubuf[base:base + rows, c0:c0 + CONV_LC]
                    acc = term if acc is None else acc + term
                y = y + acc[off:off + rc]
            ybuf[r0:r0 + rc, c0:c0 + CONV_LC] = y
    z = _ln(ybuf[...], cg_ref[...], cb_ref[...])
    z = z * _sigmoid(z)
    mix = _bdot(z, wout_ref[...]) + bout_ref[...]
    o_ref[...] = _ln(ALPHA * x + mix, lg_ref[...], lb_ref[...])

    @pl.when(j == nt - 1)
    def _():
        so_ref[0] = ubuf[tt:tt + HALO, :]


def _conv_mixer(x, state, w, lnp, *, nb, t, tt, row_off, n_total, prev):
    nt = t // tt
    win, bin_, wdw, bdw, cg, cb, wout, bout = w
    ins = [('tok' if x.shape[0] == n_total else 'own', x), ('bat', state), ('const', win), ('const', bin_),
           ('const', wdw), ('const', bdw),
           ('const', cg), ('const', cb), ('const', wout), ('const', bout), ('const', lnp[0]), ('const', lnp[1])]
    outs = [('tok', (D,), F32), ('bat', (HALO, D), F32)]
    scratch = [pltpu.VMEM((HALO + tt, D), F32), pltpu.VMEM((tt, D), F32)]
    return _seq_call(functools.partial(_conv_kernel, tt=tt, nt=nt), name=f"conv_t{t}", nb=nb, nt=nt, tt=tt,
                     row_off=row_off,
                     n_total=n_total, ins=ins, outs=outs, scratch=scratch, prev=prev)


def _head_sum(y, hs_ref, hst_ref):
    s = _split_dot(y, hs_ref[...])
    return _split_dot(s, hst_ref[...])


def _rw1_kernel(x_ref, sh_ref, mu_ref, wr_ref, wk_ref, wv_ref, w0_ref, w1_ref, w2_ref, a0_ref, a1_ref,
                a2_ref, g1_ref, g2_ref, kk_ref, ka_ref, hs_ref, hst_ref,
                r_ref, w_ref, k_ref, v_ref, an_ref, b_ref, g_ref, xbuf, *, tt):
    j = pl.program_id(1)

    @pl.when(j == 0)
    def _():
        xbuf[7:8, :] = sh_ref[0]

    @pl.when(j > 0)
    def _():
        xbuf[7:8, :] = xbuf[7 + tt:8 + tt, :]

    x = x_ref[...]
    xbuf[8:8 + tt, :] = x
    xx = xbuf[7:7 + tt, :] - x
    mu = mu_ref[...]
    xr = x + xx * mu[0:1]
    xw = x + xx * mu[1:2]
    xk = x + xx * mu[2:3]
    xv = x + xx * mu[3:4]
    xa = x + xx * mu[4:5]
    xg = x + xx * mu[5:6]
    r = _bdot(xr, wr_ref[...])
    k = _bdot(xk, wk_ref[...])
    v = _bdot(xv, wv_ref[...])
    lw = w0_ref[...] + _bdot(jnp.tanh(_bdot(xw, w1_ref[...])), w2_ref[...])
    z = -lw
    log_w = -(jnp.maximum(z, 0.0) + jnp.log(1.0 + jnp.exp(-jnp.abs(z)))) - 0.5
    a = _sigmoid(a0_ref[...] + _bdot(_bdot(xa, a1_ref[...]), a2_ref[...]))
    g = _bdot(_sigmoid(_bdot(xg, g1_ref[...])), g2_ref[...])
    kk = k * kk_ref[...]
    ss = _head_sum(kk * kk, hs_ref, hst_ref)
    kk = kk * lax.rsqrt(jnp.maximum(ss, 1e-24))
    r_ref[...] = r
    w_ref[...] = jnp.exp(-jnp.exp(log_w))
    k_ref[...] = k * (1.0 + (a - 1.0) * ka_ref[...])
    v_ref[...] = v
    an_ref[...] = -kk
    b_ref[...] = kk * a
    g_ref[...] = g


def _rw1(x, shift, w, hs, hst, *, nb, t, tt, row_off):
    nt = t // tt
    prev = []
    ins = [('tok', x), ('bat', shift)] + [('const', a) for a in w] + [('const', hs), ('const', hst)]
    outs = [('tm', (D,), F32)] * N_SCAN_IN + [('own', (D,), F32)]
    scratch = [pltpu.VMEM((8 + tt, D), F32)]
    return _seq_call(functools.partial(_rw1_kernel, tt=tt), name=f"rwkv_proj_t{t}", nb=nb, nt=nt, tt=tt,
                     row_off=row_off,
                     n_total=x.shape[0], ins=ins, outs=outs, scratch=scratch, prev=prev)


def _scan_kernel(r_ref, w_ref, k_ref, v_ref, an_ref, b_ref, nxt_ref, s0_ref, o_ref, st_ref, S, sa_buf,
                 *, tc, nc):
    c = pl.program_id(1)

    @pl.when(c == 0)
    def _():
        S[...] = s0_ref[...]

        def init(kk, acc):
            return acc + S[kk] * an_ref[0, pl.ds(kk, 1), :]

        sa_buf[...] = lax.fori_loop(0, HD, init, jnp.zeros((HD, LANES), F32))

    def step(t, sa, a_next):
        vt = v_ref[t]

        def kbody(kk, acc):
            o_acc, sa_acc = acc
            sk = (S[kk] * w_ref[t, pl.ds(kk, 1), :] + sa * b_ref[t, pl.ds(kk, 1), :]
                  + vt * k_ref[t, pl.ds(kk, 1), :])
            S[kk] = sk
            return (o_acc + sk * r_ref[t, pl.ds(kk, 1), :], sa_acc + sk * a_next(kk))

        zero = jnp.zeros((HD, LANES), F32)
        o_acc, sa_next = lax.fori_loop(0, HD, kbody, (zero, zero), unroll=4)
        o_ref[t] = o_acc
        return sa_next

    sa = lax.fori_loop(0, tc - 1, lambda t, sa: step(t, sa, lambda kk: an_ref[t + 1, pl.ds(kk, 1), :]),
                       sa_buf[...])
    sa_buf[...] = step(tc - 1, sa, lambda kk: nxt_ref[0, pl.ds(kk, 1), :])

    @pl.when(c == nc - 1)
    def _():
        st_ref[...] = S[...]


def _scan(r, w, k, v, an, b, s0, *, tc):
    t, _, lanes = r.shape
    ng, nc = lanes // LANES, t // tc
    seq = pl.BlockSpec((tc, HD, LANES), lambda g, c: (c, 0, g))
    nxt = pl.BlockSpec((1, HD, LANES), lambda g, c: (jnp.minimum((c + 1) * tc, t - 1), 0, g))
    return pl.pallas_call(
        functools.partial(_scan_kernel, tc=tc, nc=nc), grid=(ng, nc),
        in_specs=[seq] * 6 + [nxt, pl.BlockSpec((HD, HD, LANES), lambda g, c: (0, 0, g))],
        out_specs=[seq, pl.BlockSpec((HD, HD, LANES), lambda g, c: (0, 0, g))],
        out_shape=[jax.ShapeDtypeStruct((t, HD, lanes), F32), jax.ShapeDtypeStruct((HD, HD, lanes), F32)],
        scratch_shapes=[pltpu.VMEM((HD, HD, LANES), F32), pltpu.VMEM((HD, LANES), F32)],
        name=f"wkv_scan_t{t}", compiler_params=_cp(("arbitrary", "arbitrary")))(r, w, k, v, an, b, an, s0)


N_SCAN_IN = 6


def _to_scan_kernel(*refs, tc):
    low = lax.broadcasted_iota(I32, (HD, LANES), 1) < HD

    def pair(i, carry):
        t = i * 2
        for src, dst in zip(refs[:N_SCAN_IN], refs[N_SCAN_IN:]):
            y0, y1 = src[t], src[t + 1]
            m = jnp.concatenate([y[:, LANES * p:LANES * (p + 1)] for y in (y0, y1) for p in range(8)], axis=0).T
            top, bot = m[:HD], m[HD:]
            dst[t] = jnp.where(low, top, pltpu.roll(bot, HD, 1))
            dst[t + 1] = jnp.where(low, pltpu.roll(top, HD, 1), bot)
        return carry

    lax.fori_loop(0, tc // 2, pair, 0, unroll=2)


def _to_scan_call(arrs, *, tc):
    t, nb, _ = arrs[0].shape
    ng, nc = nb // 8, t // tc
    return pl.pallas_call(
        functools.partial(_to_scan_kernel, tc=tc), grid=(ng, nc),
        in_specs=[pl.BlockSpec((tc, 8, D), lambda g, c: (c, g, 0))] * N_SCAN_IN,
        out_specs=[pl.BlockSpec((tc, HD, LANES), lambda g, c: (c, 0, g))] * N_SCAN_IN,
        out_shape=[jax.ShapeDtypeStruct((t, HD, ng * LANES), F32)] * N_SCAN_IN, name=f"wkv_relayout_t{t}",
        compiler_params=_cp(("arbitrary", "arbitrary")))(*arrs)


def _post_kernel(o_ref, r_ref, k_ref, v_ref, xg_ref, xb_ref, rk_ref, y_ref, *, tc):
    low = lax.broadcasted_iota(I32, (HD, LANES), 1) < HD

    def norm(t):
        o = o_ref[t]
        d = o - jnp.mean(o, axis=0, keepdims=True)
        vo = jnp.mean(d * d, axis=0, keepdims=True)
        bonus = jnp.sum(r_ref[t] * k_ref[t] * rk_ref[...], axis=0, keepdims=True) * v_ref[t]
        return d * lax.rsqrt(vo + LNX_EPS) * xg_ref[...] + xb_ref[...] + bonus

    def pair(i, carry):
        t = i * 2
        z0, z1 = norm(t), norm(t + 1)
        top = jnp.where(low, z0, pltpu.roll(z1, HD, 1))
        bot = jnp.where(low, pltpu.roll(z0, HD, 1), z1)
        m = jnp.concatenate([top, bot], axis=0).T
        y_ref[t] = jnp.concatenate([m[8 * p:8 * p + 8] for p in range(8)], axis=1)
        y_ref[t + 1] = jnp.concatenate([m[HD + 8 * p:HD + 8 * p + 8] for p in range(8)], axis=1)
        return carry

    lax.fori_loop(0, tc // 2, pair, 0, unroll=4)


def _post(o, r, k, v, xg, xb, rk, *, tc):
    t, _, lanes = o.shape
    ng, nc = lanes // LANES, t // tc
    seq = pl.BlockSpec((tc, HD, LANES), lambda g, c: (c, 0, g))
    const = pl.BlockSpec((HD, LANES), lambda g, c: (0, 0))
    return pl.pallas_call(
        functools.partial(_post_kernel, tc=tc), grid=(ng, nc),
        in_specs=[seq] * 4 + [const] * 3,
        out_specs=pl.BlockSpec((tc, 8, D), lambda g, c: (c, g, 0)),
        out_shape=jax.ShapeDtypeStruct((t, ng * 8, D), F32), name=f"wkv_post_t{t}",
        compiler_params=_cp(("arbitrary", "arbitrary")))(o, r, k, v, xg, xb, rk)


def _rw3_kernel(y_ref, g_ref, x_ref, wo_ref, lg_ref, lb_ref, out_ref):
    mix = _bdot(y_ref[...] * g_ref[...], wo_ref[...])
    out_ref[...] = _ln(ALPHA * x_ref[...] + mix, lg_ref[...], lb_ref[...])


def _head_major_spec(tt):
    return pl.BlockSpec((HEADS, tt, HD), lambda i: (0, i, 0))


def _tok_call(body, name, n, tt, toks, consts, outs):
    in_specs = [_head_major_spec(tt) if a.ndim == 3 else pl.BlockSpec((tt, a.shape[1]), lambda i: (i, 0))
                for a in toks]
    in_specs += [pl.BlockSpec(a.shape, lambda i, _n=a.ndim: (0,) * _n) for a in consts]
    out_specs = [_head_major_spec(tt) if w == 'heads' else pl.BlockSpec((tt, w), lambda i: (i, 0))
                 for w, _ in outs]
    out_shape = [jax.ShapeDtypeStruct((HEADS, n, HD) if w == 'heads' else (n, w), dt) for w, dt in outs]
    return pl.pallas_call(
        body, grid=(n // tt,), in_specs=in_specs, out_specs=out_specs, out_shape=out_shape, name=name,
        compiler_params=_cp(("arbitrary",)))(*toks, *consts)


def _qkv_kernel(x_ref, w_ref, b_ref, q_ref, k_ref, v_ref):
    h = _bdot(x_ref[...], w_ref[...]) + b_ref[...]
    for hd in range(HEADS):
        q_ref[hd] = h[:, hd * HD:(hd + 1) * HD].astype(BF16)
    k_ref[...] = h[:, HEADS * HD:HEADS * HD + KVH * HD]
    v_ref[...] = h[:, HEADS * HD + KVH * HD:]


def _attn_kernel(q_ref, kp_ref, vp_ref, bias_ref, sink_ref, prev_ref, o_ref, *, nq, nsub, kb, mask_lo):
    del prev_ref
    c = pl.program_id(1)
    start = pl.multiple_of(c * (nsub * nq), 8)
    span = kb + (nsub - 1) * nq
    kall = kp_ref[0, pl.ds(start, span), :].astype(BF16)
    vall = vp_ref[0, pl.ds(start, span), :].astype(BF16)
    for s in range(nsub):
        rows = slice(s * nq, (s + 1) * nq)
        valid = (start + s * nq + lax.broadcasted_iota(I32, (1, kb), 1)) >= mask_lo
        for g in range(KVH):
            hs = slice(g * GROUP, (g + 1) * GROUP)
            qg = q_ref[hs, rows, :].reshape(GROUP * nq, HD)
            kh = kall[s * nq:s * nq + kb, g * HD:(g + 1) * HD]
            vh = vall[s * nq:s * nq + kb, g * HD:(g + 1) * HD]
            logits = lax.dot_general(qg, kh, (((1,), (1,)), ((), ())), preferred_element_type=F32) * HD ** -0.5
            logits = jnp.where(valid, logits + bias_ref[hs].reshape(GROUP * nq, kb), -1e30)
            sink = sink_ref[hs].reshape(GROUP * nq, 1)
            m = jnp.maximum(jnp.max(logits, axis=-1, keepdims=True), sink)
            p = jnp.exp(logits - m)
            p = p / (jnp.sum(p, axis=-1, keepdims=True) + jnp.exp(sink - m))
            og = jnp.dot(p.astype(BF16), vh, preferred_element_type=F32)
            o_ref[hs, rows, :] = og.reshape(GROUP, nq, HD).astype(BF16)


def _attn(q, kp, vp, bias, sinks, *, nb, nt, nq, nsub, kb, mask_lo, row_off, prev):
    off = row_off // (nq * nsub)
    n_total = q.shape[1]
    heads = pl.BlockSpec((HEADS, nq * nsub, HD), lambda b, c: (0, off + b * nt + c, 0))
    const = lambda a: pl.BlockSpec(a.shape, lambda b, c, _n=a.ndim: (0,) * _n)
    bat = lambda a: pl.BlockSpec((1,) + a.shape[1:], lambda b, c: (b, 0, 0))
    sink_tab = jnp.broadcast_to(sinks.reshape(HEADS, 1, 1), (HEADS, nq, 1))
    if prev is None:
        prev = jnp.zeros((HEADS, n_total, HD), BF16)
    return pl.pallas_call(
        functools.partial(_attn_kernel, nq=nq, nsub=nsub, kb=kb, mask_lo=mask_lo), grid=(nb, nt),
        in_specs=[heads, bat(kp), bat(vp), const(bias), const(sink_tab), pl.BlockSpec(memory_space=pl.ANY)],
        out_specs=heads, out_shape=jax.ShapeDtypeStruct((HEADS, n_total, HD), BF16),
        input_output_aliases={5: 0}, name=f"attn_q{nq}",
        compiler_params=_cp(("arbitrary", "arbitrary")))(q, kp, vp, bias, sink_tab, prev)


def _oproj_kernel(o_ref, x_ref, wo_ref, bo_ref, lg_ref, lb_ref, out_ref):
    acc = jnp.dot(o_ref[0], wo_ref[0:HD, :], preferred_element_type=F32)
    for hd in range(1, HEADS):
        acc = acc + jnp.dot(o_ref[hd], wo_ref[hd * HD:(hd + 1) * HD, :], preferred_element_type=F32)
    out_ref[...] = _ln(ALPHA * x_ref[...] + acc + bo_ref[...], lg_ref[...], lb_ref[...])


def _t5_bucket(rel):
    half = N_BUCKETS // 2
    max_exact = half // 2
    ret = jnp.where(rel > 0, half, 0)
    n = jnp.abs(rel)
    nf = jnp.maximum(n, 1).astype(F32)
    large = max_exact + (jnp.log(nf / max_exact) / math.log(MAX_DISTANCE / max_exact)
                         * (half - max_exact)).astype(I32)
    large = jnp.minimum(large, half - 1)
    return ret + jnp.where(n < max_exact, n, large)


def _t5_bias(rel_bias, n_q, n_k):
    rel = jnp.arange(n_k)[None, :] - WINDOW - jnp.arange(n_q)[:, None]
    onehot = (_t5_bucket(rel)[..., None] == jnp.arange(N_BUCKETS)).astype(F32)
    return jnp.einsum('qkn,nh->hqk', onehot, rel_bias, precision=lax.Precision.HIGHEST)


def _route_kernel(x_ref, wr_ref, br_ref, idx_ref, gate_ref, rank_ref, cnt_ref, carry, *, tt):
    i = pl.program_id(0)

    @pl.when(i == 0)
    def _():
        carry[...] = jnp.zeros_like(carry)

    logits = lax.dot_general(wr_ref[...].astype(BF16), x_ref[...].astype(BF16), (((1,), (1,)), ((), ())),
                             preferred_element_type=F32) + br_ref[...]
    sub = lax.broadcasted_iota(I32, (N_EXP, tt), 0)
    out_row = lax.broadcasted_iota(I32, (8, tt), 0)
    vals, sels = [], []
    idx_out = jnp.zeros((8, tt), I32)
    work = logits
    for k in range(TOP_K):
        m = jnp.max(work, axis=0, keepdims=True)
        ik = jnp.min(jnp.where(work == m, sub, N_EXP), axis=0, keepdims=True)
        sel = sub == ik
        vals.append(m)
        sels.append(sel)
        idx_out = jnp.where(out_row == k, ik, idx_out)
        work = jnp.where(sel, -jnp.inf, work)
    es = [jnp.exp(v - vals[0]) for v in vals]
    den = es[0] + es[1] + es[2] + es[3]
    gate_out = jnp.zeros((8, tt), F32)
    for k in range(TOP_K):
        gate_out = jnp.where(out_row == k, es[k] / den, gate_out)
    onehot = jnp.zeros((N_EXP, tt), F32)
    for sel in sels:
        onehot = onehot + sel.astype(F32)
    before = (lax.broadcasted_iota(I32, (tt, tt), 0) < lax.broadcasted_iota(I32, (tt, tt), 1)).astype(BF16)
    base = carry[...] + jnp.dot(onehot.astype(BF16), before, preferred_element_type=F32)
    rank_out = jnp.zeros((8, tt), I32)
    for k in range(TOP_K):
        rk = jnp.sum(jnp.where(sels[k], base, 0.0), axis=0, keepdims=True)
        rank_out = jnp.where(out_row == k, rk.astype(I32), rank_out)
    carry[...] = carry[...] + jnp.sum(onehot, axis=1, keepdims=True)
    idx_ref[...] = idx_out
    gate_ref[...] = gate_out
    rank_ref[...] = rank_out
    cnt_ref[...] = carry[...]


def _route(x1, wr_t, br_col, *, tt):
    n = x1.shape[0]
    rows = pl.BlockSpec((8, tt), lambda i: (0, i))
    const = lambda a: pl.BlockSpec(a.shape, lambda i, _n=a.ndim: (0,) * _n)
    return pl.pallas_call(
        functools.partial(_route_kernel, tt=tt), grid=(n // tt,),
        in_specs=[pl.BlockSpec((tt, D), lambda i: (i, 0)), const(wr_t), const(br_col)],
        out_specs=[rows, rows, rows, pl.BlockSpec((N_EXP, 1), lambda i: (0, 0))],
        out_shape=[jax.ShapeDtypeStruct((8, n), I32), jax.ShapeDtypeStruct((8, n), F32),
                   jax.ShapeDtypeStruct((8, n), I32), jax.ShapeDtypeStruct((N_EXP, 1), F32)],
        scratch_shapes=[pltpu.VMEM((N_EXP, 1), F32)], name="moe_route",
        compiler_params=_cp(("arbitrary",)))(x1, wr_t, br_col)


SUB = D // LANES
assert SUB == 8


def _rows_to_tiles(dst_ref, x, n):
    for s in range(SUB):
        dst_ref[pl.ds(s, n, stride=SUB), :] = x[:, LANES * s:LANES * (s + 1)]


def _tiles_to_rows(src_ref, n):
    return jnp.concatenate([src_ref[pl.ds(s, n, stride=SUB), :] for s in range(SUB)], axis=1)


def _tile_copy(src, s, dst, d, sem):
    return pltpu.make_async_copy(src.at[pl.ds(pl.multiple_of(s * SUB, SUB), SUB), :],
                                 dst.at[pl.ds(pl.multiple_of(d * SUB, SUB), SUB), :], sem)


ROW_UNROLL = 8


def _disp_kernel(dest_ref, x_ref, xs_in, xs_ref, xt, sem, *, tt):
    del xs_in
    _rows_to_tiles(xt, x_ref[...], tt)

    def issue(r, carry):
        for k in range(TOP_K):
            _tile_copy(xt, r, xs_ref, dest_ref[r * TOP_K + k], sem).start(priority=k % 2)
        return carry

    lax.fori_loop(0, tt, issue, 0, unroll=ROW_UNROLL)

    def drain(r, carry):
        for k in range(TOP_K):
            _tile_copy(xt, 0, xs_ref, 0, sem).wait()
        return carry

    lax.fori_loop(0, tt, drain, 0, unroll=ROW_UNROLL)


def _dispatch(dest, x1, xs_prev, *, tt):
    n = x1.shape[0]
    smem_tok = pl.BlockSpec((tt * TOP_K,), lambda i: (i,), memory_space=pltpu.SMEM)
    return pl.pallas_call(
        functools.partial(_disp_kernel, tt=tt), grid=(n // tt,),
        in_specs=[smem_tok, pl.BlockSpec((tt, D), lambda i: (i, 0)), pl.BlockSpec(memory_space=pl.ANY)],
        out_specs=pl.BlockSpec(memory_space=pl.ANY),
        out_shape=jax.ShapeDtypeStruct(xs_prev.shape, F32),
        scratch_shapes=[pltpu.VMEM((tt * SUB, LANES), F32), pltpu.SemaphoreType.DMA],
        input_output_aliases={2: 0}, name="moe_dispatch",
        compiler_params=_cp(("arbitrary",)))(dest, x1, xs_prev)


def _expert_kernel(te_ref, nv_ref, xs_ref, w1_ref, b1_ref, w2_ref, b2_ref, y_ref, w1b, w2b):
    i = pl.program_id(0)
    valid = i < nv_ref[0]
    changed = jnp.logical_or(i == 0, te_ref[i] != te_ref[jnp.maximum(i - 1, 0)])

    @pl.when(jnp.logical_and(valid, changed))
    def _():
        for r0 in range(0, D, 256):
            w1b[r0:r0 + 256, :] = w1_ref[0, 0, r0:r0 + 256, :].astype(BF16)
            w2b[r0:r0 + 256, :] = w2_ref[0, 0, r0:r0 + 256, :].astype(BF16)

    @pl.when(valid)
    def _():
        x = _tiles_to_rows(xs_ref, EXP_TILE).astype(BF16)
        h = jnp.dot(x, w1b[...], preferred_element_type=F32) + b1_ref[0, 0]
        glu = jnp.minimum(h[:, :D], SWIGLU_LIMIT)
        lin = jnp.clip(h[:, D:], -SWIGLU_LIMIT, SWIGLU_LIMIT)
        act = glu * _sigmoid(SWIGLU_ALPHA * glu) * (lin + 1.0)
        y = jnp.dot(act.astype(BF16), w2b[...], preferred_element_type=F32) + b2_ref[0, 0]
        _rows_to_tiles(y_ref, y, EXP_TILE)

    @pl.when(jnp.logical_not(valid))
    def _():
        y_ref[...] = jnp.zeros_like(y_ref)


def _experts(tile_expert, n_valid, xs, w1, b1, w2, b2, layer):
    n_tiles = xs.shape[0] // (EXP_TILE * SUB)
    grid_spec = pltpu.PrefetchScalarGridSpec(
        num_scalar_prefetch=2, grid=(n_tiles,),
        in_specs=[pl.BlockSpec((EXP_TILE * SUB, LANES), lambda i, te, nv: (i, 0)),
                  pl.BlockSpec((1, 1, D, 2 * D), lambda i, te, nv: (layer, te[i], 0, 0)),
                  pl.BlockSpec((1, 1, 1, 2 * D), lambda i, te, nv: (layer, te[i], 0, 0)),
                  pl.BlockSpec((1, 1, D, D), lambda i, te, nv: (layer, te[i], 0, 0)),
                  pl.BlockSpec((1, 1, 1, D), lambda i, te, nv: (layer, te[i], 0, 0))],
        out_specs=pl.BlockSpec((EXP_TILE * SUB, LANES), lambda i, te, nv: (i, 0)),
        scratch_shapes=[pltpu.VMEM((D, 2 * D), BF16), pltpu.VMEM((D, D), BF16)])
    return pl.pallas_call(
        _expert_kernel, grid_spec=grid_spec, out_shape=jax.ShapeDtypeStruct(xs.shape, F32), name="moe_experts",
        compiler_params=_cp(("arbitrary",)))(tile_expert, n_valid, xs, w1, b1, w2, b2)


def _comb_kernel(dest_ref, gate_ref, x1_ref, p_ref, y_ref, lg_ref, lb_ref, wg_ref, bg_ref,
                 wp_ref, o_ref, buf, sem, *, tt):
    def issue(r, carry):
        for k in range(TOP_K):
            _tile_copy(y_ref, dest_ref[r * TOP_K + k], buf.at[k], r, sem).start(priority=k % 2)
        return carry

    lax.fori_loop(0, tt, issue, 0, unroll=ROW_UNROLL)

    def drain(r, carry):
        for k in range(TOP_K):
            _tile_copy(y_ref, 0, buf.at[k], 0, sem).wait()
        return carry

    lax.fori_loop(0, tt, drain, 0, unroll=ROW_UNROLL)
    gate = gate_ref[...]
    moe = gate[:, 0:1] * _tiles_to_rows(buf.at[0], tt)
    for k in range(1, TOP_K):
        moe = moe + gate[:, k:k + 1] * _tiles_to_rows(buf.at[k], tt)
    x2 = _ln(ALPHA * x1_ref[...] + moe, lg_ref[...], lb_ref[...])
    gt = _sigmoid(_bdot(x2, wg_ref[...]) + bg_ref[...])
    o_ref[...] = x2 + gt * _bdot(p_ref[...], wp_ref[...])


def _combine(dest, gate, x1, p, y, lnp, wg, bg, wp, *, tt):
    n = x1.shape[0]
    smem_tok = pl.BlockSpec((tt * TOP_K,), lambda i: (i,), memory_space=pltpu.SMEM)
    tok = lambda w: pl.BlockSpec((tt, w), lambda i: (i, 0))
    const = lambda a: pl.BlockSpec(a.shape, lambda i, _n=a.ndim: (0,) * _n)
    return pl.pallas_call(
        functools.partial(_comb_kernel, tt=tt), grid=(n // tt,),
        in_specs=[smem_tok, tok(TOP_K), tok(D), tok(p.shape[1]), pl.BlockSpec(memory_space=pl.ANY),
                  const(lnp[0]), const(lnp[1]), const(wg), const(bg), const(wp)],
        out_specs=tok(D), out_shape=jax.ShapeDtypeStruct((n, D), F32),
        scratch_shapes=[pltpu.VMEM((TOP_K, tt * SUB, LANES), F32), pltpu.SemaphoreType.DMA],
        name="moe_combine_ple",
        compiler_params=_cp(("arbitrary",)))(dest, gate, x1, p, y, lnp[0], lnp[1], wg, bg, wp)


def _moe_ple(x1, p, xs_buf, wr_t, br_col, w1, b1, w2, b2, layer, lnp, wg, bg, wp, *, tt):
    n = x1.shape[0]
    idx, gate, rank, counts = _route(x1, wr_t, br_col, tt=512 if n % 512 == 0 else tt)
    counts = counts[:, 0].astype(I32)
    padded = (counts + EXP_TILE - 1) // EXP_TILE * EXP_TILE
    pad_end = jnp.cumsum(padded)
    pad_start = pad_end - padded
    n_tiles = xs_buf.shape[0] // (EXP_TILE * SUB)
    n_valid = (pad_end[-1] // EXP_TILE).astype(I32)
    tiles = jnp.minimum(jnp.arange(n_tiles, dtype=I32), n_valid - 1) * EXP_TILE
    tile_expert = jnp.minimum(jnp.sum((tiles[:, None] >= pad_end[None, :]).astype(I32), axis=1), N_EXP - 1)
    experts = jnp.arange(N_EXP, dtype=I32)
    start_of = jnp.sum(jnp.where(idx[:TOP_K, :, None] == experts, pad_start, 0), axis=-1)
    dest = (start_of + rank[:TOP_K]).T.reshape(-1)
    xs = _dispatch(dest, x1, xs_buf, tt=tt)
    y = _experts(tile_expert, n_valid.reshape(1), xs, w1, b1, w2, b2, layer)
    return _combine(dest, gate[:TOP_K].T, x1, p, y, lnp, wg, bg, wp, tt=tt), xs


def _state_to_scan(s, nb):
    s = s.reshape(nb // 8, 8, HEADS // 2, 2, HD, HD).transpose(5, 4, 0, 3, 2, 1)
    return s.reshape(HD, HD, nb * HEADS)


def _state_from_scan(s, nb):
    s = s.reshape(HD, HD, nb // 8, 2, HEADS // 2, 8).transpose(2, 5, 4, 3, 1, 0)
    return s.reshape(nb, HEADS, HD, HD)


def _head_vec_to_scan(a):
    a = a.reshape(HEADS // 2, 2, HD).transpose(2, 1, 0)
    return jnp.broadcast_to(a[..., None], (HD, 2, HEADS // 2, 8)).reshape(HD, LANES)


def _row2(a):
    return a.reshape(1, -1)


def _tile(n):
    for tt in (256, 128, 64, 32, 16, 8):
        if n % tt == 0:
            return tt
    raise ValueError(n)


def kernel(x_prompt, x_sample, p_prompt, p_sample, cache_conv, state_rwkv_shift, state_rwkv_wkv, cache_swa_k, cache_swa_v, conv_w_in, conv_b_in, conv_w_dw, conv_b_dw, conv_ln_g, conv_ln_b, conv_w_out, conv_b_out, rwkv_mu, rwkv_w_rkv, rwkv_w0, rwkv_w1, rwkv_w2, rwkv_a0, rwkv_a1, rwkv_a2, rwkv_g1, rwkv_g2, rwkv_k_k, rwkv_k_a, rwkv_r_k, rwkv_lnx_g, rwkv_lnx_b, rwkv_w_o, attn_w_qkv, attn_b_qkv, attn_sinks, attn_w_o, attn_b_o, rel_bias, ln_g, ln_b, moe_w_router, moe_b_router, moe_w1, moe_b1, moe_w2, moe_b2, ple_w_proj, ple_w_gate, ple_b_gate):
    bp, tp, _ = x_prompt.shape
    bs, ts, _ = x_sample.shape
    n_p, n_s = bp * tp, bs * ts
    n = n_p + n_s
    assert tp % 128 == 0 and n_p % ts == 0 and ts % 8 == 0 and ts <= CHUNK
    tt_tok = _tile(n)
    tt_p = 128
    x = None
    p_all = jnp.concatenate([p_prompt.reshape(DEPTH, n_p, -1), p_sample.reshape(DEPTH, n_s, -1)], axis=1)
    n_rows = (-(-n * TOP_K // EXP_TILE) + N_EXP) * EXP_TILE
    xs_buf = jnp.zeros((n_rows * SUB, LANES), F32)
    head_sel = (jnp.arange(D)[:, None] // HD == jnp.arange(LANES)[None, :]).astype(BF16)
    head_sel_t = head_sel.T
    conv_p, conv_s, shift_p, shift_s, wkv_p, wkv_s = [], [], [], [], [], []
    swa_kp, swa_vp, swa_ks, swa_vs = [], [], [], []
    for i in range(DEPTH):
        kind, j = i % 3, i // 3
        lnp = (_row2(ln_g[i, 0]), _row2(ln_b[i, 0]))
        if kind == 0:
            cw = (conv_w_in[j].astype(BF16), _row2(conv_b_in[j]), conv_w_dw[j], _row2(conv_b_dw[j]),
                  _row2(conv_ln_g[j]), _row2(conv_ln_b[j]), conv_w_out[j].astype(BF16), _row2(conv_b_out[j]))
            st_p = jnp.zeros((bp, HALO, D), F32)
            st_s = jnp.pad(cache_conv[j], ((0, 0), (HALO - (CONV_W - 1), 0), (0, 0)))
            xin_p, xin_s = (x_prompt.reshape(n_p, D), x_sample.reshape(n_s, D)) if i == 0 else (x, x)
            x1, so_p = _conv_mixer(xin_p, st_p, cw, lnp, nb=bp, t=tp, tt=tt_p, row_off=0, n_total=n, prev=None)
            x1, so_s = _conv_mixer(xin_s, st_s, cw, lnp, nb=bs, t=ts, tt=ts, row_off=n_p, n_total=n, prev=[x1])
            conv_p.append(so_p[:, HALO - (CONV_W - 1):])
            conv_s.append(so_s[:, HALO - (CONV_W - 1):])
        elif kind == 1:
            rw = (rwkv_mu[j], rwkv_w_rkv[j, 0].astype(BF16), rwkv_w_rkv[j, 1].astype(BF16),
                  rwkv_w_rkv[j, 2].astype(BF16), _row2(rwkv_w0[j]), rwkv_w1[j].astype(BF16),
                  rwkv_w2[j].astype(BF16), _row2(rwkv_a0[j]), rwkv_a1[j].astype(BF16), rwkv_a2[j].astype(BF16),
                  rwkv_g1[j].astype(BF16), rwkv_g2[j].astype(BF16), _row2(rwkv_k_k[j]), _row2(rwkv_k_a[j]))
            sh_p = jnp.zeros((bp, 1, D), F32)
            sh_s = state_rwkv_shift[j].reshape(bs, 1, D)
            post_c = [_head_vec_to_scan(a) for a in (rwkv_lnx_g[j], rwkv_lnx_b[j], rwkv_r_k[j].reshape(-1))]
            out_c = [('const', rwkv_w_o[j].astype(BF16)), ('const', lnp[0]), ('const', lnp[1])]
            x1, states = None, []
            for (lo, nb_, t_, tt_, sh, s0, tc) in ((0, bp, tp, tt_p, sh_p, None, 64),
                                                   (n_p, bs, ts, ts, sh_s, state_rwkv_wkv[j], ts)):
                r, w, k, v, an, b, g = _rw1(x, sh, rw, head_sel, head_sel_t, nb=nb_, t=t_, tt=tt_, row_off=lo)
                rs, ws, ks, vs, ans, bs_ = _to_scan_call([a.reshape(t_, nb_, D) for a in (r, w, k, v, an, b)],
                                                         tc=min(tc, 32))
                s0l = jnp.zeros((HD, HD, nb_ * HEADS), F32) if s0 is None else _state_to_scan(s0, nb_)
                o_l, s_l = _scan(rs, ws, ks, vs, ans, bs_, s0l, tc=tc)
                y = _post(o_l, rs, ks, vs, *post_c, tc=tc).reshape(t_, nb_ * D)
                x1, = _seq_call(_rw3_kernel, name=f"rwkv_out_t{t_}", nb=nb_, nt=t_ // tt_, tt=tt_, row_off=lo,
                                n_total=n, ins=[('tm', y), ('own', g), ('tok', x)] + out_c,
                                outs=[('tok', (D,), F32)], scratch=[], prev=None if x1 is None else [x1])
                states.append(_state_from_scan(s_l, nb_))
            shift_p.append(x[tp - 1:n_p:tp])
            shift_s.append(x[n_p + ts - 1::ts])
            wkv_p.append(states[0])
            wkv_s.append(states[1])
        else:
            q, kx, vx = _tok_call(_qkv_kernel, "attn_qkv", n, tt_tok, [x],
                                  [attn_w_qkv[j].astype(BF16), _row2(attn_b_qkv[j])],
                                  [('heads', BF16), (KVH * HD, F32), (KVH * HD, F32)])
            k_p = kx[:n_p].reshape(bp, tp, KVH * HD)
            v_p = vx[:n_p].reshape(bp, tp, KVH * HD)
            zpad = jnp.zeros((bp, WINDOW, KVH * HD), F32)
            nc = tp // CHUNK
            band = WINDOW + CHUNK
            o = _attn(q, jnp.concatenate([zpad, k_p], axis=1), jnp.concatenate([zpad, v_p], axis=1),
                      _t5_bias(rel_bias, CHUNK, band), attn_sinks[j], nb=bp, nt=nc, nq=CHUNK, nsub=1, kb=band,
                      mask_lo=WINDOW, row_off=0, prev=None)
            k_all = jnp.concatenate([cache_swa_k[j].reshape(bs, WINDOW, KVH * HD),
                                     kx[n_p:].reshape(bs, ts, KVH * HD)], axis=1)
            v_all = jnp.concatenate([cache_swa_v[j].reshape(bs, WINDOW, KVH * HD),
                                     vx[n_p:].reshape(bs, ts, KVH * HD)], axis=1)
            o = _attn(q, k_all, v_all, _t5_bias(rel_bias, ts, WINDOW + ts), attn_sinks[j],
                      nb=bs, nt=1, nq=ts, nsub=1, kb=WINDOW + ts, mask_lo=0, row_off=n_p, prev=o)
            x1, = _tok_call(_oproj_kernel, "attn_out", n, tt_tok, [o, x],
                            [attn_w_o[j].astype(BF16), _row2(attn_b_o[j]), lnp[0], lnp[1]], [(D, F32)])
            swa_kp.append(k_p[:, -WINDOW:].reshape(bp, WINDOW, KVH, HD))
            swa_vp.append(v_p[:, -WINDOW:].reshape(bp, WINDOW, KVH, HD))
            swa_ks.append(k_all[:, -WINDOW:].reshape(bs, WINDOW, KVH, HD))
            swa_vs.append(v_all[:, -WINDOW:].reshape(bs, WINDOW, KVH, HD))
        x, xs_buf = _moe_ple(x1, p_all[i], xs_buf, moe_w_router[i].T, moe_b_router[i].reshape(N_EXP, 1), moe_w1,
                             moe_b1.reshape(DEPTH, N_EXP, 1, 2 * D), moe_w2, moe_b2.reshape(DEPTH, N_EXP, 1, D), i,
                             (_row2(ln_g[i, 1]), _row2(ln_b[i, 1])),
                             ple_w_gate[i].astype(BF16), _row2(ple_b_gate[i]), ple_w_proj[i].astype(BF16), tt=tt_tok)
    return (x[:n_p].reshape(bp, tp, D), x[n_p:].reshape(bs, ts, D), jnp.stack(conv_p), jnp.stack(conv_s),
            jnp.stack(shift_p), jnp.stack(shift_s), jnp.stack(wkv_p), jnp.stack(wkv_s), jnp.stack(swa_kp),
            jnp.stack(swa_vp), jnp.stack(swa_ks), jnp.stack(swa_vs))
```

```python
import functools
import math

import jax
import jax.numpy as jnp
from jax import lax
from jax.experimental import pallas as pl
from jax.experimental.pallas import tpu as pltpu

F32 = jnp.float32
BF16 = jnp.bfloat16
I32 = jnp.int32

D = 1024
DEPTH = 4
CONV_W = 31
HALO = 32
HEADS = 16
HD = 64
KVH = 2
GROUP = HEADS // KVH
WINDOW = 128
CHUNK = 64
N_BUCKETS = 32
MAX_DISTANCE = 128
N_EXP = 32
TOP_K = 4
EXP_TILE = 512
LANES = 128
LNX_EPS = 64e-5
LN_EPS = 1e-5
ALPHA = (2 * DEPTH) ** 0.25
SWIGLU_ALPHA = 1.702
SWIGLU_LIMIT = 7.0
VMEM_LIMIT = 56 * 1024 * 1024


def _cp(sem):
    return pltpu.CompilerParams(dimension_semantics=sem, vmem_limit_bytes=VMEM_LIMIT)


def _bdot(a, b):
    return jnp.dot(a.astype(BF16), b.astype(BF16), preferred_element_type=F32)


def _split(a):
    hi = a.astype(BF16)
    lo = (a - hi.astype(F32)).astype(BF16)
    return hi, lo


def _split_dot(a, b_exact):
    hi, lo = _split(a)
    return (jnp.dot(hi, b_exact, preferred_element_type=F32)
            + jnp.dot(lo, b_exact, preferred_element_type=F32))


def _ln(x, g, b, eps=LN_EPS):
    mu = jnp.mean(x, axis=-1, keepdims=True)
    xc = x - mu
    var = jnp.mean(xc * xc, axis=-1, keepdims=True)
    return xc * lax.rsqrt(var + eps) * g + b


def _sigmoid(x):
    return 1.0 / (1.0 + jnp.exp(-x))


def _seq_call(body, *, name, nb, nt, tt, row_off, n_total, ins, outs, scratch, prev=None):
    off = row_off // tt
    in_specs, args = [], []
    for kind, a in ins:
        if kind == 'tok':
            in_specs.append(pl.BlockSpec((tt, a.shape[1]), lambda b, j: (off + b * nt + j, 0)))
        elif kind == 'own':
            in_specs.append(pl.BlockSpec((tt, a.shape[1]), lambda b, j: (b * nt + j, 0)))
        elif kind == 'tm':
            in_specs.append(pl.BlockSpec((tt, a.shape[1] // nb), lambda b, j: (j, b)))
        elif kind == 'bat':
            in_specs.append(pl.BlockSpec((1,) + a.shape[1:], lambda b, j: (b, 0, 0)))
        else:
            in_specs.append(pl.BlockSpec(a.shape, lambda b, j, _n=a.ndim: (0,) * _n))
        args.append(a)
    out_specs, out_shapes = [], []
    for kind, tail, dt in outs:
        if kind == 'tok':
            out_specs.append(pl.BlockSpec((tt, tail[0]), lambda b, j: (off + b * nt + j, 0)))
            out_shapes.append(jax.ShapeDtypeStruct((n_total, tail[0]), dt))
        elif kind == 'own':
            out_specs.append(pl.BlockSpec((tt, tail[0]), lambda b, j: (b * nt + j, 0)))
            out_shapes.append(jax.ShapeDtypeStruct((nb * nt * tt, tail[0]), dt))
        elif kind == 'tm':
            out_specs.append(pl.BlockSpec((tt, tail[0]), lambda b, j: (j, b)))
            out_shapes.append(jax.ShapeDtypeStruct((nt * tt, nb * tail[0]), dt))
        else:
            out_specs.append(pl.BlockSpec((1,) + tuple(tail), lambda b, j: (b, 0, 0)))
            out_shapes.append(jax.ShapeDtypeStruct((nb,) + tuple(tail), dt))
    aliases = {}
    n_prev = 0
    tok_out = [i for i, o in enumerate(outs) if o[0] == 'tok']
    if prev is None:
        prev = [jnp.zeros((n_total, outs[i][1][0]), outs[i][2]) for i in tok_out]
    for p, oi in zip(prev, tok_out):
        aliases[len(args)] = oi
        in_specs.append(pl.BlockSpec(memory_space=pl.ANY))
        args.append(p)
        n_prev += 1
    n_in = len(ins)

    def wrapped(*refs):
        body(*refs[:n_in], *refs[n_in + n_prev:])

    return pl.pallas_call(
        wrapped, grid=(nb, nt), in_specs=in_specs, out_specs=out_specs, out_shape=out_shapes,
        scratch_shapes=scratch, input_output_aliases=aliases, name=name,
        compiler_params=_cp(("arbitrary", "arbitrary")))(*args)


CONV_RC = 64
CONV_LC = 256


def _conv_kernel(x_ref, st_ref, win_ref, bin_ref, wdw_ref, bdw_ref, cg_ref, cb_ref, wout_ref,
                 bout_ref, lg_ref, lb_ref, o_ref, so_ref, ubuf, ybuf, *, tt, nt):
    j = pl.program_id(1)

    @pl.when(j == 0)
    def _():
        ubuf[0:HALO, :] = st_ref[0].astype(BF16).astype(F32)

    @pl.when(j > 0)
    def _():
        ubuf[0:HALO, :] = ubuf[tt:tt + HALO, :]

    x = x_ref[...]
    h = _bdot(x, win_ref[...]) + bin_ref[...]
    u = h[:, :D] * _sigmoid(h[:, D:])
    ubuf[HALO:HALO + tt, :] = u.astype(BF16).astype(F32)

    @pl.when(j == nt - 1)
    def _():
        so_ref[0] = u[tt - HALO:tt, :]
    first = HALO - (CONV_W - 1)
    rc = min(CONV_RC, tt)
    for r0 in range(0, tt, rc):
        for c0 in range(0, D, CONV_LC):
            y = jnp.zeros((rc, CONV_LC), F32) + bdw_ref[:, c0:c0 + CONV_LC]
            for off in range(8):
                rows = rc + (8 if off else 0)
                acc = None
                for tap in range(CONV_W):
                    if (first + tap) % 8 != off:
                        continue
                    base = r0 + (first + tap) // 8 * 8
                    term = wdw_ref[tap:tap + 1, c0:c0 + CONV_LC] * ubuf[base:base + rows, c0:c0 + CONV_LC]
                    acc = term if acc is None else acc + term
                y = y + acc[off:off + rc]
            ybuf[r0:r0 + rc, c0:c0 + CONV_LC] = y
    z = _ln(ybuf[...], cg_ref[...], cb_ref[...])
    z = z * _sigmoid(z)
    mix = _bdot(z, wout_ref[...]) + bout_ref[...]
    o_ref[...] = _ln(ALPHA * x + mix, lg_ref[...], lb_ref[...])


def _conv_mixer(x, state, w, lnp, *, nb, t, tt, row_off, n_total, prev):
    nt = t // tt
    win, bin_, wdw, bdw, cg, cb, wout, bout = w
    ins = [('tok' if x.shape[0] == n_total else 'own', x), ('bat', state), ('const', win), ('const', bin_),
           ('const', wdw), ('const', bdw),
           ('const', cg), ('const', cb), ('const', wout), ('const', bout), ('const', lnp[0]), ('const', lnp[1])]
    outs = [('tok', (D,), F32), ('bat', (HALO, D), F32)]
    scratch = [pltpu.VMEM((HALO + tt, D), F32), pltpu.VMEM((tt, D), F32)]
    return _seq_call(functools.partial(_conv_kernel, tt=tt, nt=nt), name=f"conv_t{t}", nb=nb, nt=nt, tt=tt,
                     row_off=row_off,
                     n_total=n_total, ins=ins, outs=outs, scratch=scratch, prev=prev)


def _head_sum(y, hs_ref, hst_ref):
    s = _split_dot(y, hs_ref[...])
    return _split_dot(s, hst_ref[...])


def _rw1_kernel(x_ref, sh_ref, mu_ref, wr_ref, wk_ref, wv_ref, w0_ref, w1_ref, w2_ref, a0_ref, a1_ref,
                a2_ref, g1_ref, g2_ref, kk_ref, ka_ref, hs_ref, hst_ref,
                r_ref, w_ref, k_ref, v_ref, an_ref, b_ref, g_ref, xbuf, *, tt):
    j = pl.program_id(1)

    @pl.when(j == 0)
    def _():
        xbuf[7:8, :] = sh_ref[0]

    @pl.when(j > 0)
    def _():
        xbuf[7:8, :] = xbuf[7 + tt:8 + tt, :]

    x = x_ref[...]
    xbuf[8:8 + tt, :] = x
    xx = xbuf[7:7 + tt, :] - x
    mu = mu_ref[...]
    xr = x + xx * mu[0:1]
    xw = x + xx * mu[1:2]
    xk = x + xx * mu[2:3]
    xv = x + xx * mu[3:4]
    xa = x + xx * mu[4:5]
    xg = x + xx * mu[5:6]
    r = _bdot(xr, wr_ref[...])
    k = _bdot(xk, wk_ref[...])
    v = _bdot(xv, wv_ref[...])
    lw = w0_ref[...] + _bdot(jnp.tanh(_bdot(xw, w1_ref[...])), w2_ref[...])
    z = -lw
    log_w = -(jnp.maximum(z, 0.0) + jnp.log(1.0 + jnp.exp(-jnp.abs(z)))) - 0.5
    a = _sigmoid(a0_ref[...] + _bdot(_bdot(xa, a1_ref[...]), a2_ref[...]))
    g = _bdot(_sigmoid(_bdot(xg, g1_ref[...])), g2_ref[...])
    kk = k * kk_ref[...]
    ss = _head_sum(kk * kk, hs_ref, hst_ref)
    kk = kk * lax.rsqrt(jnp.maximum(ss, 1e-24))
    r_ref[...] = r
    w_ref[...] = jnp.exp(-jnp.exp(log_w))
    k_ref[...] = k * (1.0 + (a - 1.0) * ka_ref[...])
    v_ref[...] = v
    an_ref[...] = -kk
    b_ref[...] = kk * a
    g_ref[...] = g


def _rw1(x, shift, w, hs, hst, *, nb, t, tt, row_off):
    nt = t // tt
    prev = []
    ins = [('tok', x), ('bat', shift)] + [('const', a) for a in w] + [('const', hs), ('const', hst)]
    outs = [('tm', (D,), F32)] * N_SCAN_IN + [('own', (D,), F32)]
    scratch = [pltpu.VMEM((8 + tt, D), F32)]
    return _seq_call(functools.partial(_rw1_kernel, tt=tt), name=f"rwkv_proj_t{t}", nb=nb, nt=nt, tt=tt,
                     row_off=row_off,
                     n_total=x.shape[0], ins=ins, outs=outs, scratch=scratch, prev=prev)


def _scan_kernel(r_ref, w_ref, k_ref, v_ref, an_ref, b_ref, nxt_ref, s0_ref, o_ref, st_ref, S, sa_buf,
                 *, tc, nc):
    c = pl.program_id(1)

    @pl.when(c == 0)
    def _():
        S[...] = s0_ref[...]

        def init(kk, acc):
            return acc + S[kk] * an_ref[0, pl.ds(kk, 1), :]

        sa_buf[...] = lax.fori_loop(0, HD, init, jnp.zeros((HD, LANES), F32))

    def step(t, sa, a_next):
        vt = v_ref[t]

        def kbody(kk, acc):
            o_acc, sa_acc = acc
            sk = (S[kk] * w_ref[t, pl.ds(kk, 1), :] + sa * b_ref[t, pl.ds(kk, 1), :]
                  + vt * k_ref[t, pl.ds(kk, 1), :])
            S[kk] = sk
            return (o_acc + sk * r_ref[t, pl.ds(kk, 1), :], sa_acc + sk * a_next(kk))

        zero = jnp.zeros((HD, LANES), F32)
        o_acc, sa_next = lax.fori_loop(0, HD, kbody, (zero, zero), unroll=4)
        o_ref[t] = o_acc
        return sa_next

    sa = lax.fori_loop(0, tc - 1, lambda t, sa: step(t, sa, lambda kk: an_ref[t + 1, pl.ds(kk, 1), :]),
                       sa_buf[...])
    sa_buf[...] = step(tc - 1, sa, lambda kk: nxt_ref[0, pl.ds(kk, 1), :])

    @pl.when(c == nc - 1)
    def _():
        st_ref[...] = S[...]


def _scan(r, w, k, v, an, b, s0, *, tc):
    t, _, lanes = r.shape
    ng, nc = lanes // LANES, t // tc
    seq = pl.BlockSpec((tc, HD, LANES), lambda g, c: (c, 0, g))
    nxt = pl.BlockSpec((1, HD, LANES), lambda g, c: (jnp.minimum((c + 1) * tc, t - 1), 0, g))
    return pl.pallas_call(
        functools.partial(_scan_kernel, tc=tc, nc=nc), grid=(ng, nc),
        in_specs=[seq] * 6 + [nxt, pl.BlockSpec((HD, HD, LANES), lambda g, c: (0, 0, g))],
        out_specs=[seq, pl.BlockSpec((HD, HD, LANES), lambda g, c: (0, 0, g))],
        out_shape=[jax.ShapeDtypeStruct((t, HD, lanes), F32), jax.ShapeDtypeStruct((HD, HD, lanes), F32)],
        scratch_shapes=[pltpu.VMEM((HD, HD, LANES), F32), pltpu.VMEM((HD, LANES), F32)],
        name=f"wkv_scan_t{t}", compiler_params=_cp(("arbitrary", "arbitrary")))(r, w, k, v, an, b, an, s0)


N_SCAN_IN = 6


def _to_scan_kernel(*refs, tc):
    low = lax.broadcasted_iota(I32, (HD, LANES), 1) < HD

    def pair(i, carry):
        t = i * 2
        for src, dst in zip(refs[:N_SCAN_IN], refs[N_SCAN_IN:]):
            y0, y1 = src[t], src[t + 1]
            m = jnp.concatenate([y[:, LANES * p:LANES * (p + 1)] for y in (y0, y1) for p in range(8)], axis=0).T
            top, bot = m[:HD], m[HD:]
            dst[t] = jnp.where(low, top, pltpu.roll(bot, HD, 1))
            dst[t + 1] = jnp.where(low, pltpu.roll(top, HD, 1), bot)
        return carry

    lax.fori_loop(0, tc // 2, pair, 0, unroll=2)


def _to_scan_call(arrs, *, tc):
    t, nb, _ = arrs[0].shape
    ng, nc = nb // 8, t // tc
    return pl.pallas_call(
        functools.partial(_to_scan_kernel, tc=tc), grid=(ng, nc),
        in_specs=[pl.BlockSpec((tc, 8, D), lambda g, c: (c, g, 0))] * N_SCAN_IN,
        out_specs=[pl.BlockSpec((tc, HD, LANES), lambda g, c: (c, 0, g))] * N_SCAN_IN,
        out_shape=[jax.ShapeDtypeStruct((t, HD, ng * LANES), F32)] * N_SCAN_IN, name=f"wkv_relayout_t{t}",
        compiler_params=_cp(("arbitrary", "arbitrary")))(*arrs)


def _post_kernel(o_ref, r_ref, k_ref, v_ref, xg_ref, xb_ref, rk_ref, y_ref, *, tc):
    low = lax.broadcasted_iota(I32, (HD, LANES), 1) < HD

    def norm(t):
        o = o_ref[t]
        d = o - jnp.mean(o, axis=0, keepdims=True)
        vo = jnp.mean(d * d, axis=0, keepdims=True)
        bonus = jnp.sum(r_ref[t] * k_ref[t] * rk_ref[...], axis=0, keepdims=True) * v_ref[t]
        return d * lax.rsqrt(vo + LNX_EPS) * xg_ref[...] + xb_ref[...] + bonus

    def pair(i, carry):
        t = i * 2
        z0, z1 = norm(t), norm(t + 1)
        top = jnp.where(low, z0, pltpu.roll(z1, HD, 1))
        bot = jnp.where(low, pltpu.roll(z0, HD, 1), z1)
        m = jnp.concatenate([top, bot], axis=0).T
        y_ref[t] = jnp.concatenate([m[8 * p:8 * p + 8] for p in range(8)], axis=1)
        y_ref[t + 1] = jnp.concatenate([m[HD + 8 * p:HD + 8 * p + 8] for p in range(8)], axis=1)
        return carry

    lax.fori_loop(0, tc // 2, pair, 0, unroll=4)


def _post(o, r, k, v, xg, xb, rk, *, tc):
    t, _, lanes = o.shape
    ng, nc = lanes // LANES, t // tc
    seq = pl.BlockSpec((tc, HD, LANES), lambda g, c: (c, 0, g))
    const = pl.BlockSpec((HD, LANES), lambda g, c: (0, 0))
    return pl.pallas_call(
        functools.partial(_post_kernel, tc=tc), grid=(ng, nc),
        in_specs=[seq] * 4 + [const] * 3,
        out_specs=pl.BlockSpec((tc, 8, D), lambda g, c: (c, g, 0)),
        out_shape=jax.ShapeDtypeStruct((t, ng * 8, D), F32), name=f"wkv_post_t{t}",
        compiler_params=_cp(("arbitrary", "arbitrary")))(o, r, k, v, xg, xb, rk)


def _rw3_kernel(y_ref, g_ref, x_ref, wo_ref, lg_ref, lb_ref, out_ref):
    mix = _bdot(y_ref[...] * g_ref[...], wo_ref[...])
    out_ref[...] = _ln(ALPHA * x_ref[...] + mix, lg_ref[...], lb_ref[...])


def _head_major_spec(tt):
    return pl.BlockSpec((HEADS, tt, HD), lambda i: (0, i, 0))


def _tok_call(body, name, n, tt, toks, consts, outs):
    in_specs = [_head_major_spec(tt) if a.ndim == 3 else pl.BlockSpec((tt, a.shape[1]), lambda i: (i, 0))
                for a in toks]
    in_specs += [pl.BlockSpec(a.shape, lambda i, _n=a.ndim: (0,) * _n) for a in consts]
    out_specs = [_head_major_spec(tt) if w == 'heads' else pl.BlockSpec((tt, w), lambda i: (i, 0))
                 for w, _ in outs]
    out_shape = [jax.ShapeDtypeStruct((HEADS, n, HD) if w == 'heads' else (n, w), dt) for w, dt in outs]
    return pl.pallas_call(
        body, grid=(n // tt,), in_specs=in_specs, out_specs=out_specs, out_shape=out_shape, name=name,
        compiler_params=_cp(("arbitrary",)))(*toks, *consts)


def _qkv_kernel(x_ref, w_ref, b_ref, q_ref, k_ref, v_ref):
    h = _bdot(x_ref[...], w_ref[...]) + b_ref[...]
    for hd in range(HEADS):
        q_ref[hd] = h[:, hd * HD:(hd + 1) * HD].astype(BF16)
    k_ref[...] = h[:, HEADS * HD:HEADS * HD + KVH * HD]
    v_ref[...] = h[:, HEADS * HD + KVH * HD:]


def _attn_kernel(q_ref, kp_ref, vp_ref, bias_ref, sink_ref, prev_ref, o_ref, *, nq, nsub, kb, mask_lo):
    del prev_ref
    c = pl.program_id(1)
    start = pl.multiple_of(c * (nsub * nq), 8)
    span = kb + (nsub - 1) * nq
    kall = kp_ref[0, pl.ds(start, span), :].astype(BF16)
    vall = vp_ref[0, pl.ds(start, span), :].astype(BF16)
    for s in range(nsub):
        rows = slice(s * nq, (s + 1) * nq)
        valid = (start + s * nq + lax.broadcasted_iota(I32, (1, kb), 1)) >= mask_lo
        for g in range(KVH):
            hs = slice(g * GROUP, (g + 1) * GROUP)
            qg = q_ref[hs, rows, :].reshape(GROUP * nq, HD)
            kh = kall[s * nq:s * nq + kb, g * HD:(g + 1) * HD]
            vh = vall[s * nq:s * nq + kb, g * HD:(g + 1) * HD]
            logits = lax.dot_general(qg, kh, (((1,), (1,)), ((), ())), preferred_element_type=F32) * HD ** -0.5
            logits = jnp.where(valid, logits + bias_ref[hs].reshape(GROUP * nq, kb), -1e30)
            sink = sink_ref[hs].reshape(GROUP * nq, 1)
            m = jnp.maximum(jnp.max(logits, axis=-1, keepdims=True), sink)
            p = jnp.exp(logits - m)
            p = p / (jnp.sum(p, axis=-1, keepdims=True) + jnp.exp(sink - m))
            og = jnp.dot(p.astype(BF16), vh, preferred_element_type=F32)
            o_ref[hs, rows, :] = og.reshape(GROUP, nq, HD).astype(BF16)


def _attn(q, kp, vp, bias, sinks, *, nb, nt, nq, nsub, kb, mask_lo, row_off, prev):
    off = row_off // (nq * nsub)
    n_total = q.shape[1]
    heads = pl.BlockSpec((HEADS, nq * nsub, HD), lambda b, c: (0, off + b * nt + c, 0))
    const = lambda a: pl.BlockSpec(a.shape, lambda b, c, _n=a.ndim: (0,) * _n)
    bat = lambda a: pl.BlockSpec((1,) + a.shape[1:], lambda b, c: (b, 0, 0))
    sink_tab = jnp.broadcast_to(sinks.reshape(HEADS, 1, 1), (HEADS, nq, 1))
    if prev is None:
        prev = jnp.zeros((HEADS, n_total, HD), BF16)
    return pl.pallas_call(
        functools.partial(_attn_kernel, nq=nq, nsub=nsub, kb=kb, mask_lo=mask_lo), grid=(nb, nt),
        in_specs=[heads, bat(kp), bat(vp), const(bias), const(sink_tab), pl.BlockSpec(memory_space=pl.ANY)],
        out_specs=heads, out_shape=jax.ShapeDtypeStruct((HEADS, n_total, HD), BF16),
        input_output_aliases={5: 0}, name=f"attn_q{nq}",
        compiler_params=_cp(("arbitrary", "arbitrary")))(q, kp, vp, bias, sink_tab, prev)


def _oproj_kernel(o_ref, x_ref, wo_ref, bo_ref, lg_ref, lb_ref, out_ref):
    acc = jnp.dot(o_ref[0], wo_ref[0:HD, :], preferred_element_type=F32)
    for hd in range(1, HEADS):
        acc = acc + jnp.dot(o_ref[hd], wo_ref[hd * HD:(hd + 1) * HD, :], preferred_element_type=F32)
    out_ref[...] = _ln(ALPHA * x_ref[...] + acc + bo_ref[...], lg_ref[...], lb_ref[...])


def _t5_bucket(rel):
    half = N_BUCKETS // 2
    max_exact = half // 2
    ret = jnp.where(rel > 0, half, 0)
    n = jnp.abs(rel)
    nf = jnp.maximum(n, 1).astype(F32)
    large = max_exact + (jnp.log(nf / max_exact) / math.log(MAX_DISTANCE / max_exact)
                         * (half - max_exact)).astype(I32)
    large = jnp.minimum(large, half - 1)
    return ret + jnp.where(n < max_exact, n, large)


def _t5_bias(rel_bias, n_q, n_k):
    rel = jnp.arange(n_k)[None, :] - WINDOW - jnp.arange(n_q)[:, None]
    onehot = (_t5_bucket(rel)[..., None] == jnp.arange(N_BUCKETS)).astype(F32)
    return jnp.einsum('qkn,nh->hqk', onehot, rel_bias, precision=lax.Precision.HIGHEST)


def _route_kernel(x_ref, wr_ref, br_ref, idx_ref, gate_ref, rank_ref, cnt_ref, carry, *, tt):
    i = pl.program_id(0)

    @pl.when(i == 0)
    def _():
        carry[...] = jnp.zeros_like(carry)

    logits = lax.dot_general(wr_ref[...].astype(BF16), x_ref[...].astype(BF16), (((1,), (1,)), ((), ())),
                             preferred_element_type=F32) + br_ref[...]
    sub = lax.broadcasted_iota(I32, (N_EXP, tt), 0)
    out_row = lax.broadcasted_iota(I32, (8, tt), 0)
    vals, sels = [], []
    idx_out = jnp.zeros((8, tt), I32)
    work = logits
    for k in range(TOP_K):
        m = jnp.max(work, axis=0, keepdims=True)
        ik = jnp.min(jnp.where(work == m, sub, N_EXP), axis=0, keepdims=True)
        sel = sub == ik
        vals.append(m)
        sels.append(sel)
        idx_out = jnp.where(out_row == k, ik, idx_out)
        work = jnp.where(sel, -jnp.inf, work)
    es = [jnp.exp(v - vals[0]) for v in vals]
    den = es[0] + es[1] + es[2] + es[3]
    gate_out = jnp.zeros((8, tt), F32)
    for k in range(TOP_K):
        gate_out = jnp.where(out_row == k, es[k] / den, gate_out)
    onehot = jnp.zeros((N_EXP, tt), F32)
    for sel in sels:
        onehot = onehot + sel.astype(F32)
    before = (lax.broadcasted_iota(I32, (tt, tt), 0) < lax.broadcasted_iota(I32, (tt, tt), 1)).astype(BF16)
    base = carry[...] + jnp.dot(onehot.astype(BF16), before, preferred_element_type=F32)
    rank_out = jnp.zeros((8, tt), I32)
    for k in range(TOP_K):
        rk = jnp.sum(jnp.where(sels[k], base, 0.0), axis=0, keepdims=True)
        rank_out = jnp.where(out_row == k, rk.astype(I32), rank_out)
    carry[...] = carry[...] + jnp.sum(onehot, axis=1, keepdims=True)
    idx_ref[...] = idx_out
    gate_ref[...] = gate_out
    rank_ref[...] = rank_out
    cnt_ref[...] = carry[...]


def _route(x1, wr_t, br_col, *, tt):
    n = x1.shape[0]
    rows = pl.BlockSpec((8, tt), lambda i: (0, i))
    const = lambda a: pl.BlockSpec(a.shape, lambda i, _n=a.ndim: (0,) * _n)
    return pl.pallas_call(
        functools.partial(_route_kernel, tt=tt), grid=(n // tt,),
        in_specs=[pl.BlockSpec((tt, D), lambda i: (i, 0)), const(wr_t), const(br_col)],
        out_specs=[rows, rows, rows, pl.BlockSpec((N_EXP, 1), lambda i: (0, 0))],
        out_shape=[jax.ShapeDtypeStruct((8, n), I32), jax.ShapeDtypeStruct((8, n), F32),
                   jax.ShapeDtypeStruct((8, n), I32), jax.ShapeDtypeStruct((N_EXP, 1), F32)],
        scratch_shapes=[pltpu.VMEM((N_EXP, 1), F32)], name="moe_route",
        compiler_params=_cp(("arbitrary",)))(x1, wr_t, br_col)


SUB = D // LANES
assert SUB == 8


def _rows_to_tiles(dst_ref, x, n):
    for s in range(SUB):
        dst_ref[pl.ds(s, n, stride=SUB), :] = x[:, LANES * s:LANES * (s + 1)]


def _tiles_to_rows(src_ref, n):
    return jnp.concatenate([src_ref[pl.ds(s, n, stride=SUB), :] for s in range(SUB)], axis=1)


def _tile_copy(src, s, dst, d, sem):
    return pltpu.make_async_copy(src.at[pl.ds(pl.multiple_of(s * SUB, SUB), SUB), :],
                                 dst.at[pl.ds(pl.multiple_of(d * SUB, SUB), SUB), :], sem)


ROW_UNROLL = 8


def _drain_rows(src, dst, sem, tt):
    def drain(r, carry):
        for k in range(TOP_K):
            _tile_copy(src, 0, dst, 0, sem).wait()
        return carry

    lax.fori_loop(0, tt, drain, 0, unroll=ROW_UNROLL)


def _disp_kernel(dest_ref, x_ref, xs_in, xs_ref, xt, sem, *, tt, nt):
    del xs_in
    i = pl.program_id(0)
    slot = lax.rem(i, 2)
    stage = xt.at[slot]
    _rows_to_tiles(stage, x_ref[...], tt)

    def issue(r, carry):
        for k in range(TOP_K):
            _tile_copy(stage, r, xs_ref, dest_ref[r * TOP_K + k], sem.at[slot]).start(priority=k % 2)
        return carry

    lax.fori_loop(0, tt, issue, 0, unroll=ROW_UNROLL)

    @pl.when(i > 0)
    def _():
        _drain_rows(xt.at[1 - slot], xs_ref, sem.at[1 - slot], tt)

    @pl.when(i == nt - 1)
    def _():
        _drain_rows(stage, xs_ref, sem.at[slot], tt)


def _dispatch(dest, x1, xs_prev, *, tt):
    n = x1.shape[0]
    smem_tok = pl.BlockSpec((tt * TOP_K,), lambda i: (i,), memory_space=pltpu.SMEM)
    return pl.pallas_call(
        functools.partial(_disp_kernel, tt=tt, nt=n // tt), grid=(n // tt,),
        in_specs=[smem_tok, pl.BlockSpec((tt, D), lambda i: (i, 0)), pl.BlockSpec(memory_space=pl.ANY)],
        out_specs=pl.BlockSpec(memory_space=pl.ANY),
        out_shape=jax.ShapeDtypeStruct(xs_prev.shape, F32),
        scratch_shapes=[pltpu.VMEM((2, tt * SUB, LANES), F32), pltpu.SemaphoreType.DMA((2,))],
        input_output_aliases={2: 0}, name="moe_dispatch",
        compiler_params=_cp(("arbitrary",)))(dest, x1, xs_prev)


def _expert_kernel(te_ref, nv_ref, xs_ref, w1_ref, b1_ref, w2_ref, b2_ref, y_ref, w1b, w2b):
    i = pl.program_id(0)
    valid = i < nv_ref[0]
    changed = jnp.logical_or(i == 0, te_ref[i] != te_ref[jnp.maximum(i - 1, 0)])

    @pl.when(jnp.logical_and(valid, changed))
    def _():
        for r0 in range(0, D, 256):
            w1b[r0:r0 + 256, :] = w1_ref[0, 0, r0:r0 + 256, :].astype(BF16)
            w2b[r0:r0 + 256, :] = w2_ref[0, 0, r0:r0 + 256, :].astype(BF16)

    @pl.when(valid)
    def _():
        x = _tiles_to_rows(xs_ref, EXP_TILE).astype(BF16)
        h = jnp.dot(x, w1b[...], preferred_element_type=F32) + b1_ref[0, 0]
        glu = jnp.minimum(h[:, :D], SWIGLU_LIMIT)
        lin = jnp.clip(h[:, D:], -SWIGLU_LIMIT, SWIGLU_LIMIT)
        act = glu * _sigmoid(SWIGLU_ALPHA * glu) * (lin + 1.0)
        y = jnp.dot(act.astype(BF16), w2b[...], preferred_element_type=F32) + b2_ref[0, 0]
        _rows_to_tiles(y_ref, y, EXP_TILE)

    @pl.when(jnp.logical_not(valid))
    def _():
        y_ref[...] = jnp.zeros_like(y_ref)


def _experts(tile_expert, n_valid, xs, w1, b1, w2, b2, layer):
    n_tiles = xs.shape[0] // (EXP_TILE * SUB)
    grid_spec = pltpu.PrefetchScalarGridSpec(
        num_scalar_prefetch=2, grid=(n_tiles,),
        in_specs=[pl.BlockSpec((EXP_TILE * SUB, LANES), lambda i, te, nv: (i, 0)),
                  pl.BlockSpec((1, 1, D, 2 * D), lambda i, te, nv: (layer, te[i], 0, 0)),
                  pl.BlockSpec((1, 1, 1, 2 * D), lambda i, te, nv: (layer, te[i], 0, 0)),
                  pl.BlockSpec((1, 1, D, D), lambda i, te, nv: (layer, te[i], 0, 0)),
                  pl.BlockSpec((1, 1, 1, D), lambda i, te, nv: (layer, te[i], 0, 0))],
        out_specs=pl.BlockSpec((EXP_TILE * SUB, LANES), lambda i, te, nv: (i, 0)),
        scratch_shapes=[pltpu.VMEM((D, 2 * D), BF16), pltpu.VMEM((D, D), BF16)])
    return pl.pallas_call(
        _expert_kernel, grid_spec=grid_spec, out_shape=jax.ShapeDtypeStruct(xs.shape, F32), name="moe_experts",
        compiler_params=_cp(("arbitrary",)))(tile_expert, n_valid, xs, w1, b1, w2, b2)


def _comb_kernel(dest_ref, next_ref, gate_ref, x1_ref, p_ref, y_ref, lg_ref, lb_ref, wg_ref, bg_ref,
                 wp_ref, o_ref, buf, sem, *, tt, nt):
    i = pl.program_id(0)
    slot = lax.rem(i, 2)

    def gather(idx_ref, s):
        def issue(r, carry):
            for k in range(TOP_K):
                _tile_copy(y_ref, idx_ref[r * TOP_K + k], buf.at[s, k], r, sem.at[s]).start(priority=k % 2)
            return carry

        lax.fori_loop(0, tt, issue, 0, unroll=ROW_UNROLL)

    @pl.when(i == 0)
    def _():
        gather(dest_ref, slot)

    @pl.when(i + 1 < nt)
    def _():
        gather(next_ref, 1 - slot)

    _drain_rows(y_ref, buf.at[slot, 0], sem.at[slot], tt)
    gate = gate_ref[...]
    moe = gate[:, 0:1] * _tiles_to_rows(buf.at[slot, 0], tt)
    for k in range(1, TOP_K):
        moe = moe + gate[:, k:k + 1] * _tiles_to_rows(buf.at[slot, k], tt)
    x2 = _ln(ALPHA * x1_ref[...] + moe, lg_ref[...], lb_ref[...])
    gt = _sigmoid(_bdot(x2, wg_ref[...]) + bg_ref[...])
    o_ref[...] = x2 + gt * _bdot(p_ref[...], wp_ref[...])


def _combine(dest, gate, x1, p, y, lnp, wg, bg, wp, *, tt):
    n = x1.shape[0]
    smem_tok = pl.BlockSpec((tt * TOP_K,), lambda i: (i,), memory_space=pltpu.SMEM)
    tok = lambda w: pl.BlockSpec((tt, w), lambda i: (i, 0))
    const = lambda a: pl.BlockSpec(a.shape, lambda i, _n=a.ndim: (0,) * _n)
    nt = n // tt
    smem_next = pl.BlockSpec((tt * TOP_K,), lambda i: (jnp.minimum(i + 1, nt - 1),), memory_space=pltpu.SMEM)
    return pl.pallas_call(
        functools.partial(_comb_kernel, tt=tt, nt=nt), grid=(nt,),
        in_specs=[smem_tok, smem_next, tok(TOP_K), tok(D), tok(p.shape[1]), pl.BlockSpec(memory_space=pl.ANY),
                  const(lnp[0]), const(lnp[1]), const(wg), const(bg), const(wp)],
        out_specs=tok(D), out_shape=jax.ShapeDtypeStruct((n, D), F32),
        scratch_shapes=[pltpu.VMEM((2, TOP_K, tt * SUB, LANES), F32), pltpu.SemaphoreType.DMA((2,))],
        name="moe_combine_ple",
        compiler_params=_cp(("arbitrary",)))(dest, dest, gate, x1, p, y, lnp[0], lnp[1], wg, bg, wp)


def _moe_ple(x1, p, xs_buf, wr_t, br_col, w1, b1, w2, b2, layer, lnp, wg, bg, wp, *, tt):
    n = x1.shape[0]
    idx, gate, rank, counts = _route(x1, wr_t, br_col, tt=512 if n % 512 == 0 else tt)
    counts = counts[:, 0].astype(I32)
    padded = (counts + EXP_TILE - 1) // EXP_TILE * EXP_TILE
    pad_end = jnp.cumsum(padded)
    pad_start = pad_end - padded
    n_tiles = xs_buf.shape[0] // (EXP_TILE * SUB)
    n_valid = (pad_end[-1] // EXP_TILE).astype(I32)
    tiles = jnp.minimum(jnp.arange(n_tiles, dtype=I32), n_valid - 1) * EXP_TILE
    tile_expert = jnp.minimum(jnp.sum((tiles[:, None] >= pad_end[None, :]).astype(I32), axis=1), N_EXP - 1)
    experts = jnp.arange(N_EXP, dtype=I32)
    start_of = jnp.sum(jnp.where(idx[:TOP_K, :, None] == experts, pad_start, 0), axis=-1)
    dest = (start_of + rank[:TOP_K]).T.reshape(-1)
    xs = _dispatch(dest, x1, xs_buf, tt=tt)
    y = _experts(tile_expert, n_valid.reshape(1), xs, w1, b1, w2, b2, layer)
    return _combine(dest, gate[:TOP_K].T, x1, p, y, lnp, wg, bg, wp, tt=tt), xs


def _state_to_scan(s, nb):
    s = s.reshape(nb // 8, 8, HEADS // 2, 2, HD, HD).transpose(5, 4, 0, 3, 2, 1)
    return s.reshape(HD, HD, nb * HEADS)


def _state_from_scan(s, nb):
    s = s.reshape(HD, HD, nb // 8, 2, HEADS // 2, 8).transpose(2, 5, 4, 3, 1, 0)
    return s.reshape(nb, HEADS, HD, HD)


def _head_vec_to_scan(a):
    a = a.reshape(HEADS // 2, 2, HD).transpose(2, 1, 0)
    return jnp.broadcast_to(a[..., None], (HD, 2, HEADS // 2, 8)).reshape(HD, LANES)


def _row2(a):
    return a.reshape(1, -1)


def _tile(n):
    for tt in (256, 128, 64, 32, 16, 8):
        if n % tt == 0:
            return tt
    raise ValueError(n)


def kernel(x_prompt, x_sample, p_prompt, p_sample, cache_conv, state_rwkv_shift, state_rwkv_wkv, cache_swa_k, cache_swa_v, conv_w_in, conv_b_in, conv_w_dw, conv_b_dw, conv_ln_g, conv_ln_b, conv_w_out, conv_b_out, rwkv_mu, rwkv_w_rkv, rwkv_w0, rwkv_w1, rwkv_w2, rwkv_a0, rwkv_a1, rwkv_a2, rwkv_g1, rwkv_g2, rwkv_k_k, rwkv_k_a, rwkv_r_k, rwkv_lnx_g, rwkv_lnx_b, rwkv_w_o, attn_w_qkv, attn_b_qkv, attn_sinks, attn_w_o, attn_b_o, rel_bias, ln_g, ln_b, moe_w_router, moe_b_router, moe_w1, moe_b1, moe_w2, moe_b2, ple_w_proj, ple_w_gate, ple_b_gate):
    bp, tp, _ = x_prompt.shape
    bs, ts, _ = x_sample.shape
    n_p, n_s = bp * tp, bs * ts
    n = n_p + n_s
    assert tp % 128 == 0 and n_p % ts == 0 and ts % 8 == 0 and HALO <= ts <= CHUNK
    tt_tok = _tile(n)
    tt_p = 128
    x = None
    p_all = jnp.concatenate([p_prompt.reshape(DEPTH, n_p, -1), p_sample.reshape(DEPTH, n_s, -1)], axis=1)
    n_rows = (-(-n * TOP_K // EXP_TILE) + N_EXP) * EXP_TILE
    xs_buf = jnp.zeros((n_rows * SUB, LANES), F32)
    head_sel = (jnp.arange(D)[:, None] // HD == jnp.arange(LANES)[None, :]).astype(BF16)
    head_sel_t = head_sel.T
    conv_p, conv_s, shift_p, shift_s, wkv_p, wkv_s = [], [], [], [], [], []
    swa_kp, swa_vp, swa_ks, swa_vs = [], [], [], []
    for i in range(DEPTH):
        kind, j = i % 3, i // 3
        lnp = (_row2(ln_g[i, 0]), _row2(ln_b[i, 0]))
        if kind == 0:
            cw = (conv_w_in[j].astype(BF16), _row2(conv_b_in[j]), conv_w_dw[j].astype(BF16).astype(F32),
                  _row2(conv_b_dw[j]),
                  _row2(conv_ln_g[j]), _row2(conv_ln_b[j]), conv_w_out[j].astype(BF16), _row2(conv_b_out[j]))
            st_p = jnp.zeros((bp, HALO, D), F32)
            st_s = jnp.pad(cache_conv[j], ((0, 0), (HALO - (CONV_W - 1), 0), (0, 0)))
            xin_p, xin_s = (x_prompt.reshape(n_p, D), x_sample.reshape(n_s, D)) if i == 0 else (x, x)
            x1, so_p = _conv_mixer(xin_p, st_p, cw, lnp, nb=bp, t=tp, tt=tt_p, row_off=0, n_total=n, prev=None)
            x1, so_s = _conv_mixer(xin_s, st_s, cw, lnp, nb=bs, t=ts, tt=ts, row_off=n_p, n_total=n, prev=[x1])
            conv_p.append(so_p[:, HALO - (CONV_W - 1):])
            conv_s.append(so_s[:, HALO - (CONV_W - 1):])
        elif kind == 1:
            rw = (rwkv_mu[j], rwkv_w_rkv[j, 0].astype(BF16), rwkv_w_rkv[j, 1].astype(BF16),
                  rwkv_w_rkv[j, 2].astype(BF16), _row2(rwkv_w0[j]), rwkv_w1[j].astype(BF16),
                  rwkv_w2[j].astype(BF16), _row2(rwkv_a0[j]), rwkv_a1[j].astype(BF16), rwkv_a2[j].astype(BF16),
                  rwkv_g1[j].astype(BF16), rwkv_g2[j].astype(BF16), _row2(rwkv_k_k[j]), _row2(rwkv_k_a[j]))
            sh_p = jnp.zeros((bp, 1, D), F32)
            sh_s = state_rwkv_shift[j].reshape(bs, 1, D)
            post_c = [_head_vec_to_scan(a) for a in (rwkv_lnx_g[j], rwkv_lnx_b[j], rwkv_r_k[j].reshape(-1))]
            out_c = [('const', rwkv_w_o[j].astype(BF16)), ('const', lnp[0]), ('const', lnp[1])]
            x1, states = None, []
            for (lo, nb_, t_, tt_, sh, s0, tc) in ((0, bp, tp, tt_p, sh_p, None, 64),
                                                   (n_p, bs, ts, ts, sh_s, state_rwkv_wkv[j], ts)):
                r, w, k, v, an, b, g = _rw1(x, sh, rw, head_sel, head_sel_t, nb=nb_, t=t_, tt=tt_, row_off=lo)
                rs, ws, ks, vs, ans, bs_ = _to_scan_call([a.reshape(t_, nb_, D) for a in (r, w, k, v, an, b)],
                                                         tc=min(tc, 32))
                s0l = jnp.zeros((HD, HD, nb_ * HEADS), F32) if s0 is None else _state_to_scan(s0, nb_)
                o_l, s_l = _scan(rs, ws, ks, vs, ans, bs_, s0l, tc=tc)
                y = _post(o_l, rs, ks, vs, *post_c, tc=tc).reshape(t_, nb_ * D)
                x1, = _seq_call(_rw3_kernel, name=f"rwkv_out_t{t_}", nb=nb_, nt=t_ // tt_, tt=tt_, row_off=lo,
                                n_total=n, ins=[('tm', y), ('own', g), ('tok', x)] + out_c,
                                outs=[('tok', (D,), F32)], scratch=[], prev=None if x1 is None else [x1])
                states.append(_state_from_scan(s_l, nb_))
            shift_p.append(x[tp - 1:n_p:tp])
            shift_s.append(x[n_p + ts - 1::ts])
            wkv_p.append(states[0])
            wkv_s.append(states[1])
        else:
            q, kx, vx = _tok_call(_qkv_kernel, "attn_qkv", n, tt_tok, [x],
                                  [attn_w_qkv[j].astype(BF16), _row2(attn_b_qkv[j])],
                                  [('heads', BF16), (KVH * HD, F32), (KVH * HD, F32)])
            k_p = kx[:n_p].reshape(bp, tp, KVH * HD)
            v_p = vx[:n_p].reshape(bp, tp, KVH * HD)
            zpad = jnp.zeros((bp, WINDOW, KVH * HD), F32)
            nc = tp // CHUNK
            band = WINDOW + CHUNK
            o = _attn(q, jnp.concatenate([zpad, k_p], axis=1), jnp.concatenate([zpad, v_p], axis=1),
                      _t5_bias(rel_bias, CHUNK, band), attn_sinks[j], nb=bp, nt=nc, nq=CHUNK, nsub=1, kb=band,
                      mask_lo=WINDOW, row_off=0, prev=None)
            k_all = jnp.concatenate([cache_swa_k[j].reshape(bs, WINDOW, KVH * HD),
                                     kx[n_p:].reshape(bs, ts, KVH * HD)], axis=1)
            v_all = jnp.concatenate([cache_swa_v[j].reshape(bs, WINDOW, KVH * HD),
                                     vx[n_p:].reshape(bs, ts, KVH * HD)], axis=1)
            o = _attn(q, k_all, v_all, _t5_bias(rel_bias, ts, WINDOW + ts), attn_sinks[j],
                      nb=bs, nt=1, nq=ts, nsub=1, kb=WINDOW + ts, mask_lo=0, row_off=n_p, prev=o)
            x1, = _tok_call(_oproj_kernel, "attn_out", n, tt_tok, [o, x],
                            [attn_w_o[j].astype(BF16), _row2(attn_b_o[j]), lnp[0], lnp[1]], [(D, F32)])
            swa_kp.append(k_p[:, -WINDOW:].reshape(bp, WINDOW, KVH, HD))
            swa_vp.append(v_p[:, -WINDOW:].reshape(bp, WINDOW, KVH, HD))
            swa_ks.append(k_all[:, -WINDOW:].reshape(bs, WINDOW, KVH, HD))
            swa_vs.append(v_all[:, -WINDOW:].reshape(bs, WINDOW, KVH, HD))
        x, xs_buf = _moe_ple(x1, p_all[i], xs_buf, moe_w_router[i].T, moe_b_router[i].reshape(N_EXP, 1), moe_w1,
                             moe_b1.reshape(DEPTH, N_EXP, 1, 2 * D), moe_w2, moe_b2.reshape(DEPTH, N_EXP, 1, D), i,
                             (_row2(ln_g[i, 1]), _row2(ln_b[i, 1])),
                             ple_w_gate[i].astype(BF16), _row2(ple_b_gate[i]), ple_w_proj[i].astype(BF16), tt=tt_tok)
    return (x[:n_p].reshape(bp, tp, D), x[n_p:].reshape(bs, ts, D), jnp.stack(conv_p), jnp.stack(conv_s),
            jnp.stack(shift_p), jnp.stack(shift_s), jnp.stack(wkv_p), jnp.stack(wkv_s), jnp.stack(swa_kp),
            jnp.stack(swa_vp), jnp.stack(swa_ks), jnp.stack(swa_vs))
```

```python
import functools
import math

import jax
import jax.numpy as jnp
from jax import lax
from jax.experimental import pallas as pl
from jax.experimental.pallas import tpu as pltpu

F32 = jnp.float32
BF16 = jnp.bfloat16
I32 = jnp.int32

D = 1024
DEPTH = 4
CONV_W = 31
HALO = 32
HEADS = 16
HD = 64
KVH = 2
GROUP = HEADS // KVH
WINDOW = 128
CHUNK = 64
N_BUCKETS = 32
MAX_DISTANCE = 128
N_EXP = 32
TOP_K = 4
EXP_TILE = 512
LANES = 128
LNX_EPS = 64e-5
LN_EPS = 1e-5
ALPHA = (2 * DEPTH) ** 0.25
SWIGLU_ALPHA = 1.702
SWIGLU_LIMIT = 7.0
VMEM_LIMIT = 56 * 1024 * 1024


def _cp(sem):
    return pltpu.CompilerParams(dimension_semantics=sem, vmem_limit_bytes=VMEM_LIMIT)


def _bdot(a, b):
    return jnp.dot(a.astype(BF16), b.astype(BF16), preferred_element_type=F32)


def _split(a):
    hi = a.astype(BF16)
    lo = (a - hi.astype(F32)).astype(BF16)
    return hi, lo


def _split_dot(a, b_exact):
    hi, lo = _split(a)
    return (jnp.dot(hi, b_exact, preferred_element_type=F32)
            + jnp.dot(lo, b_exact, preferred_element_type=F32))


def _ln(x, g, b, eps=LN_EPS):
    mu = jnp.mean(x, axis=-1, keepdims=True)
    xc = x - mu
    var = jnp.mean(xc * xc, axis=-1, keepdims=True)
    return xc * lax.rsqrt(var + eps) * g + b


def _sigmoid(x):
    return 1.0 / (1.0 + jnp.exp(-x))


def _seq_call(body, *, name, nb, nt, tt, row_off, n_total, ins, outs, scratch, prev=None):
    off = row_off // tt
    in_specs, args = [], []
    for kind, a in ins:
        if kind == 'tok':
            in_specs.append(pl.BlockSpec((tt, a.shape[1]), lambda b, j: (off + b * nt + j, 0)))
        elif kind == 'own':
            in_specs.append(pl.BlockSpec((tt, a.shape[1]), lambda b, j: (b * nt + j, 0)))
        elif kind == 'tm':
            in_specs.append(pl.BlockSpec((tt, a.shape[1] // nb), lambda b, j: (j, b)))
        elif kind == 'bat':
            in_specs.append(pl.BlockSpec((1,) + a.shape[1:], lambda b, j: (b, 0, 0)))
        else:
            in_specs.append(pl.BlockSpec(a.shape, lambda b, j, _n=a.ndim: (0,) * _n))
        args.append(a)
    out_specs, out_shapes = [], []
    for kind, tail, dt in outs:
        if kind == 'tok':
            out_specs.append(pl.BlockSpec((tt, tail[0]), lambda b, j: (off + b * nt + j, 0)))
            out_shapes.append(jax.ShapeDtypeStruct((n_total, tail[0]), dt))
        elif kind == 'own':
            out_specs.append(pl.BlockSpec((tt, tail[0]), lambda b, j: (b * nt + j, 0)))
            out_shapes.append(jax.ShapeDtypeStruct((nb * nt * tt, tail[0]), dt))
        elif kind == 'tm':
            out_specs.append(pl.BlockSpec((tt, tail[0]), lambda b, j: (j, b)))
            out_shapes.append(jax.ShapeDtypeStruct((nt * tt, nb * tail[0]), dt))
        else:
            out_specs.append(pl.BlockSpec((1,) + tuple(tail), lambda b, j: (b, 0, 0)))
            out_shapes.append(jax.ShapeDtypeStruct((nb,) + tuple(tail), dt))
    aliases = {}
    n_prev = 0
    tok_out = [i for i, o in enumerate(outs) if o[0] == 'tok']
    if prev is None:
        prev = [jnp.zeros((n_total, outs[i][1][0]), outs[i][2]) for i in tok_out]
    for p, oi in zip(prev, tok_out):
        aliases[len(args)] = oi
        in_specs.append(pl.BlockSpec(memory_space=pl.ANY))
        args.append(p)
        n_prev += 1
    n_in = len(ins)

    def wrapped(*refs):
        body(*refs[:n_in], *refs[n_in + n_prev:])

    return pl.pallas_call(
        wrapped, grid=(nb, nt), in_specs=in_specs, out_specs=out_specs, out_shape=out_shapes,
        scratch_shapes=scratch, input_output_aliases=aliases, name=name,
        compiler_params=_cp(("arbitrary", "arbitrary")))(*args)


CONV_RC = 64
CONV_LC = 256


def _conv_kernel(x_ref, st_ref, win_ref, bin_ref, wdw_ref, bdw_ref, cg_ref, cb_ref, wout_ref,
                 bout_ref, lg_ref, lb_ref, o_ref, so_ref, ubuf, ybuf, *, tt, nt):
    j = pl.program_id(1)

    @pl.when(j == 0)
    def _():
        ubuf[0:HALO, :] = st_ref[0].astype(BF16).astype(F32)

    @pl.when(j > 0)
    def _():
        ubuf[0:HALO, :] = ubuf[tt:tt + HALO, :]

    x = x_ref[...]
    h = _bdot(x, win_ref[...]) + bin_ref[...]
    u = h[:, :D] * _sigmoid(h[:, D:])
    ubuf[HALO:HALO + tt, :] = u.astype(BF16).astype(F32)

    @pl.when(j == nt - 1)
    def _():
        so_ref[0] = u[tt - HALO:tt, :]
    first = HALO - (CONV_W - 1)
    rc = min(CONV_RC, tt)
    for r0 in range(0, tt, rc):
        for c0 in range(0, D, CONV_LC):
            y = jnp.zeros((rc, CONV_LC), F32) + bdw_ref[:, c0:c0 + CONV_LC]
            for off in range(8):
                rows = rc + (8 if off else 0)
                acc = None
                for tap in range(CONV_W):
                    if (first + tap) % 8 != off:
                        continue
                    base = r0 + (first + tap) // 8 * 8
                    term = wdw_ref[tap:tap + 1, c0:c0 + CONV_LC] * ubuf[base:base + rows, c0:c0 + CONV_LC]
                    acc = term if acc is None else acc + term
                y = y + acc[off:off + rc]
            ybuf[r0:r0 + rc, c0:c0 + CONV_LC] = y
    z = _ln(ybuf[...], cg_ref[...], cb_ref[...])
    z = z * _sigmoid(z)
    mix = _bdot(z, wout_ref[...]) + bout_ref[...]
    o_ref[...] = _ln(ALPHA * x + mix, lg_ref[...], lb_ref[...])


def _conv_mixer(x, state, w, lnp, *, nb, t, tt, row_off, n_total, prev):
    nt = t // tt
    win, bin_, wdw, bdw, cg, cb, wout, bout = w
    ins = [('tok' if x.shape[0] == n_total else 'own', x), ('bat', state), ('const', win), ('const', bin_),
           ('const', wdw), ('const', bdw),
           ('const', cg), ('const', cb), ('const', wout), ('const', bout), ('const', lnp[0]), ('const', lnp[1])]
    outs = [('tok', (D,), F32), ('bat', (HALO, D), F32)]
    scratch = [pltpu.VMEM((HALO + tt, D), F32), pltpu.VMEM((tt, D), F32)]
    return _seq_call(functools.partial(_conv_kernel, tt=tt, nt=nt), name=f"conv_t{t}", nb=nb, nt=nt, tt=tt,
                     row_off=row_off,
                     n_total=n_total, ins=ins, outs=outs, scratch=scratch, prev=prev)


def _head_sum(y, hs_ref, hst_ref):
    s = _split_dot(y, hs_ref[...])
    return _split_dot(s, hst_ref[...])


def _rw1_kernel(x_ref, sh_ref, mu_ref, wr_ref, wk_ref, wv_ref, w0_ref, w1_ref, w2_ref, a0_ref, a1_ref,
                a2_ref, g1_ref, g2_ref, kk_ref, ka_ref, hs_ref, hst_ref,
                r_ref, w_ref, k_ref, v_ref, an_ref, b_ref, g_ref, xbuf, *, tt):
    j = pl.program_id(1)

    @pl.when(j == 0)
    def _():
        xbuf[7:8, :] = sh_ref[0]

    @pl.when(j > 0)
    def _():
        xbuf[7:8, :] = xbuf[7 + tt:8 + tt, :]

    x = x_ref[...]
    xbuf[8:8 + tt, :] = x
    xx = xbuf[7:7 + tt, :] - x
    mu = mu_ref[...]
    xr = x + xx * mu[0:1]
    xw = x + xx * mu[1:2]
    xk = x + xx * mu[2:3]
    xv = x + xx * mu[3:4]
    xa = x + xx * mu[4:5]
    xg = x + xx * mu[5:6]
    r = _bdot(xr, wr_ref[...])
    k = _bdot(xk, wk_ref[...])
    v = _bdot(xv, wv_ref[...])
    lw = w0_ref[...] + _bdot(jnp.tanh(_bdot(xw, w1_ref[...])), w2_ref[...])
    z = -lw
    log_w = -(jnp.maximum(z, 0.0) + jnp.log(1.0 + jnp.exp(-jnp.abs(z)))) - 0.5
    a = _sigmoid(a0_ref[...] + _bdot(_bdot(xa, a1_ref[...]), a2_ref[...]))
    g = _bdot(_sigmoid(_bdot(xg, g1_ref[...])), g2_ref[...])
    kk = k * kk_ref[...]
    ss = _head_sum(kk * kk, hs_ref, hst_ref)
    kk = kk * lax.rsqrt(jnp.maximum(ss, 1e-24))
    r_ref[...] = r
    w_ref[...] = jnp.exp(-jnp.exp(log_w))
    k_ref[...] = k * (1.0 + (a - 1.0) * ka_ref[...])
    v_ref[...] = v
    an_ref[...] = -kk
    b_ref[...] = kk * a
    g_ref[...] = g


def _rw1(x, shift, w, hs, hst, *, nb, t, tt, row_off):
    nt = t // tt
    prev = []
    ins = [('tok', x), ('bat', shift)] + [('const', a) for a in w] + [('const', hs), ('const', hst)]
    outs = [('own', (D,), F32)] * (N_SCAN_IN + 1)
    scratch = [pltpu.VMEM((8 + tt, D), F32)]
    return _seq_call(functools.partial(_rw1_kernel, tt=tt), name=f"rwkv_proj_t{t}", nb=nb, nt=nt, tt=tt,
                     row_off=row_off,
                     n_total=x.shape[0], ins=ins, outs=outs, scratch=scratch, prev=prev)


def _scan_kernel(r_ref, w_ref, k_ref, v_ref, an_ref, b_ref, nxt_ref, s0_ref, o_ref, st_ref, S, sa_buf,
                 *, tc, nc):
    c = pl.program_id(1)

    @pl.when(c == 0)
    def _():
        S[...] = s0_ref[...]

        def init(kk, acc):
            return acc + S[kk] * an_ref[0, pl.ds(kk, 1), :]

        sa_buf[...] = lax.fori_loop(0, HD, init, jnp.zeros((HD, LANES), F32))

    def step(t, sa, a_next):
        vt = v_ref[t]

        def kbody(kk, acc):
            o_acc, sa_acc = acc
            sk = (S[kk] * w_ref[t, pl.ds(kk, 1), :] + sa * b_ref[t, pl.ds(kk, 1), :]
                  + vt * k_ref[t, pl.ds(kk, 1), :])
            S[kk] = sk
            return (o_acc + sk * r_ref[t, pl.ds(kk, 1), :], sa_acc + sk * a_next(kk))

        zero = jnp.zeros((HD, LANES), F32)
        o_acc, sa_next = lax.fori_loop(0, HD, kbody, (zero, zero), unroll=4)
        o_ref[t] = o_acc
        return sa_next

    sa = lax.fori_loop(0, tc - 1, lambda t, sa: step(t, sa, lambda kk: an_ref[t + 1, pl.ds(kk, 1), :]),
                       sa_buf[...])
    sa_buf[...] = step(tc - 1, sa, lambda kk: nxt_ref[0, pl.ds(kk, 1), :])

    @pl.when(c == nc - 1)
    def _():
        st_ref[...] = S[...]


def _scan(r, w, k, v, an, b, s0, *, tc):
    t, _, lanes = r.shape
    ng, nc = lanes // LANES, t // tc
    seq = pl.BlockSpec((tc, HD, LANES), lambda g, c: (c, 0, g))
    nxt = pl.BlockSpec((1, HD, LANES), lambda g, c: (jnp.minimum((c + 1) * tc, t - 1), 0, g))
    return pl.pallas_call(
        functools.partial(_scan_kernel, tc=tc, nc=nc), grid=(ng, nc),
        in_specs=[seq] * 6 + [nxt, pl.BlockSpec((HD, HD, LANES), lambda g, c: (0, 0, g))],
        out_specs=[seq, pl.BlockSpec((HD, HD, LANES), lambda g, c: (0, 0, g))],
        out_shape=[jax.ShapeDtypeStruct((t, HD, lanes), F32), jax.ShapeDtypeStruct((HD, HD, lanes), F32)],
        scratch_shapes=[pltpu.VMEM((HD, HD, LANES), F32), pltpu.VMEM((HD, LANES), F32)],
        name=f"wkv_scan_t{t}", compiler_params=_cp(("arbitrary", "arbitrary")))(r, w, k, v, an, b, an, s0)


N_SCAN_IN = 6


def _to_scan_kernel(*refs, tc):
    low = lax.broadcasted_iota(I32, (HD, LANES), 1) < HD

    def pair(i, carry):
        t = i * 2
        for src, dst in zip(refs[:N_SCAN_IN], refs[N_SCAN_IN:]):
            y0, y1 = src[:, t, :], src[:, t + 1, :]
            m = jnp.concatenate([y[:, LANES * p:LANES * (p + 1)] for y in (y0, y1) for p in range(8)], axis=0).T
            top, bot = m[:HD], m[HD:]
            dst[t] = jnp.where(low, top, pltpu.roll(bot, HD, 1))
            dst[t + 1] = jnp.where(low, pltpu.roll(top, HD, 1), bot)
        return carry

    lax.fori_loop(0, tc // 2, pair, 0, unroll=2)


def _to_scan_call(arrs, *, tc):
    nb, t, _ = arrs[0].shape
    ng, nc = nb // 8, t // tc
    return pl.pallas_call(
        functools.partial(_to_scan_kernel, tc=tc), grid=(ng, nc),
        in_specs=[pl.BlockSpec((8, tc, D), lambda g, c: (g, c, 0))] * N_SCAN_IN,
        out_specs=[pl.BlockSpec((tc, HD, LANES), lambda g, c: (c, 0, g))] * N_SCAN_IN,
        out_shape=[jax.ShapeDtypeStruct((t, HD, ng * LANES), F32)] * N_SCAN_IN, name=f"wkv_relayout_t{t}",
        compiler_params=_cp(("arbitrary", "arbitrary")))(*arrs)


def _post_kernel(o_ref, r_ref, k_ref, v_ref, xg_ref, xb_ref, rk_ref, y_ref, *, tc):
    low = lax.broadcasted_iota(I32, (HD, LANES), 1) < HD

    def norm(t):
        o = o_ref[t]
        d = o - jnp.mean(o, axis=0, keepdims=True)
        vo = jnp.mean(d * d, axis=0, keepdims=True)
        bonus = jnp.sum(r_ref[t] * k_ref[t] * rk_ref[...], axis=0, keepdims=True) * v_ref[t]
        return d * lax.rsqrt(vo + LNX_EPS) * xg_ref[...] + xb_ref[...] + bonus

    def pair(i, carry):
        t = i * 2
        z0, z1 = norm(t), norm(t + 1)
        top = jnp.where(low, z0, pltpu.roll(z1, HD, 1))
        bot = jnp.where(low, pltpu.roll(z0, HD, 1), z1)
        m = jnp.concatenate([top, bot], axis=0).T
        y_ref[t] = jnp.concatenate([m[8 * p:8 * p + 8] for p in range(8)], axis=1)
        y_ref[t + 1] = jnp.concatenate([m[HD + 8 * p:HD + 8 * p + 8] for p in range(8)], axis=1)
        return carry

    lax.fori_loop(0, tc // 2, pair, 0, unroll=4)


def _post(o, r, k, v, xg, xb, rk, *, tc):
    t, _, lanes = o.shape
    ng, nc = lanes // LANES, t // tc
    seq = pl.BlockSpec((tc, HD, LANES), lambda g, c: (c, 0, g))
    const = pl.BlockSpec((HD, LANES), lambda g, c: (0, 0))
    return pl.pallas_call(
        functools.partial(_post_kernel, tc=tc), grid=(ng, nc),
        in_specs=[seq] * 4 + [const] * 3,
        out_specs=pl.BlockSpec((tc, 8, D), lambda g, c: (c, g, 0)),
        out_shape=jax.ShapeDtypeStruct((t, ng * 8, D), F32), name=f"wkv_post_t{t}",
        compiler_params=_cp(("arbitrary", "arbitrary")))(o, r, k, v, xg, xb, rk)


def _rw3_kernel(y_ref, g_ref, x_ref, wo_ref, lg_ref, lb_ref, out_ref):
    mix = _bdot(y_ref[...] * g_ref[...], wo_ref[...])
    out_ref[...] = _ln(ALPHA * x_ref[...] + mix, lg_ref[...], lb_ref[...])


def _head_major_spec(tt):
    return pl.BlockSpec((HEADS, tt, HD), lambda i: (0, i, 0))


def _tok_call(body, name, n, tt, toks, consts, outs):
    in_specs = [_head_major_spec(tt) if a.ndim == 3 else pl.BlockSpec((tt, a.shape[1]), lambda i: (i, 0))
                for a in toks]
    in_specs += [pl.BlockSpec(a.shape, lambda i, _n=a.ndim: (0,) * _n) for a in consts]
    out_specs = [_head_major_spec(tt) if w == 'heads' else pl.BlockSpec((tt, w), lambda i: (i, 0))
                 for w, _ in outs]
    out_shape = [jax.ShapeDtypeStruct((HEADS, n, HD) if w == 'heads' else (n, w), dt) for w, dt in outs]
    return pl.pallas_call(
        body, grid=(n // tt,), in_specs=in_specs, out_specs=out_specs, out_shape=out_shape, name=name,
        compiler_params=_cp(("arbitrary",)))(*toks, *consts)


def _qkv_kernel(x_ref, w_ref, b_ref, q_ref, k_ref, v_ref):
    h = _bdot(x_ref[...], w_ref[...]) + b_ref[...]
    for hd in range(HEADS):
        q_ref[hd] = h[:, hd * HD:(hd + 1) * HD].astype(BF16)
    k_ref[...] = h[:, HEADS * HD:HEADS * HD + KVH * HD]
    v_ref[...] = h[:, HEADS * HD + KVH * HD:]


def _attn_kernel(q_ref, kp_ref, vp_ref, bias_ref, sink_ref, prev_ref, o_ref, *, nq, nsub, kb, mask_lo):
    del prev_ref
    c = pl.program_id(1)
    start = pl.multiple_of(c * (nsub * nq), 8)
    span = kb + (nsub - 1) * nq
    kall = kp_ref[0, pl.ds(start, span), :].astype(BF16)
    vall = vp_ref[0, pl.ds(start, span), :].astype(BF16)
    for s in range(nsub):
        rows = slice(s * nq, (s + 1) * nq)
        valid = (start + s * nq + lax.broadcasted_iota(I32, (1, kb), 1)) >= mask_lo
        for g in range(KVH):
            hs = slice(g * GROUP, (g + 1) * GROUP)
            qg = q_ref[hs, rows, :].reshape(GROUP * nq, HD)
            kh = kall[s * nq:s * nq + kb, g * HD:(g + 1) * HD]
            vh = vall[s * nq:s * nq + kb, g * HD:(g + 1) * HD]
            logits = lax.dot_general(qg, kh, (((1,), (1,)), ((), ())), preferred_element_type=F32) * HD ** -0.5
            logits = jnp.where(valid, logits + bias_ref[hs].reshape(GROUP * nq, kb), -1e30)
            sink = sink_ref[hs].reshape(GROUP * nq, 1)
            m = jnp.maximum(jnp.max(logits, axis=-1, keepdims=True), sink)
            p = jnp.exp(logits - m)
            p = p / (jnp.sum(p, axis=-1, keepdims=True) + jnp.exp(sink - m))
            og = jnp.dot(p.astype(BF16), vh, preferred_element_type=F32)
            o_ref[hs, rows, :] = og.reshape(GROUP, nq, HD).astype(BF16)


def _attn(q, kp, vp, bias, sinks, *, nb, nt, nq, nsub, kb, mask_lo, row_off, prev):
    off = row_off // (nq * nsub)
    n_total = q.shape[1]
    heads = pl.BlockSpec((HEADS, nq * nsub, HD), lambda b, c: (0, off + b * nt + c, 0))
    const = lambda a: pl.BlockSpec(a.shape, lambda b, c, _n=a.ndim: (0,) * _n)
    bat = lambda a: pl.BlockSpec((1,) + a.shape[1:], lambda b, c: (b, 0, 0))
    sink_tab = jnp.broadcast_to(sinks.reshape(HEADS, 1, 1), (HEADS, nq, 1))
    if prev is None:
        prev = jnp.zeros((HEADS, n_total, HD), BF16)
    return pl.pallas_call(
        functools.partial(_attn_kernel, nq=nq, nsub=nsub, kb=kb, mask_lo=mask_lo), grid=(nb, nt),
        in_specs=[heads, bat(kp), bat(vp), const(bias), const(sink_tab), pl.BlockSpec(memory_space=pl.ANY)],
        out_specs=heads, out_shape=jax.ShapeDtypeStruct((HEADS, n_total, HD), BF16),
        input_output_aliases={5: 0}, name=f"attn_q{nq}",
        compiler_params=_cp(("arbitrary", "arbitrary")))(q, kp, vp, bias, sink_tab, prev)


def _oproj_kernel(o_ref, x_ref, wo_ref, bo_ref, lg_ref, lb_ref, out_ref):
    acc = jnp.dot(o_ref[0], wo_ref[0:HD, :], preferred_element_type=F32)
    for hd in range(1, HEADS):
        acc = acc + jnp.dot(o_ref[hd], wo_ref[hd * HD:(hd + 1) * HD, :], preferred_element_type=F32)
    out_ref[...] = _ln(ALPHA * x_ref[...] + acc + bo_ref[...], lg_ref[...], lb_ref[...])


def _t5_bucket(rel):
    half = N_BUCKETS // 2
    max_exact = half // 2
    ret = jnp.where(rel > 0, half, 0)
    n = jnp.abs(rel)
    nf = jnp.maximum(n, 1).astype(F32)
    large = max_exact + (jnp.log(nf / max_exact) / math.log(MAX_DISTANCE / max_exact)
                         * (half - max_exact)).astype(I32)
    large = jnp.minimum(large, half - 1)
    return ret + jnp.where(n < max_exact, n, large)


def _t5_bias(rel_bias, n_q, n_k):
    rel = jnp.arange(n_k)[None, :] - WINDOW - jnp.arange(n_q)[:, None]
    onehot = (_t5_bucket(rel)[..., None] == jnp.arange(N_BUCKETS)).astype(F32)
    return jnp.einsum('qkn,nh->hqk', onehot, rel_bias, precision=lax.Precision.HIGHEST)


def _route_kernel(x_ref, wr_ref, br_ref, idx_ref, gate_ref, rank_ref, cnt_ref, carry, *, tt):
    i = pl.program_id(0)

    @pl.when(i == 0)
    def _():
        carry[...] = jnp.zeros_like(carry)

    logits = lax.dot_general(wr_ref[...].astype(BF16), x_ref[...].astype(BF16), (((1,), (1,)), ((), ())),
                             preferred_element_type=F32) + br_ref[...]
    sub = lax.broadcasted_iota(I32, (N_EXP, tt), 0)
    out_row = lax.broadcasted_iota(I32, (8, tt), 0)
    vals, sels = [], []
    idx_out = jnp.zeros((8, tt), I32)
    work = logits
    for k in range(TOP_K):
        m = jnp.max(work, axis=0, keepdims=True)
        ik = jnp.min(jnp.where(work == m, sub, N_EXP), axis=0, keepdims=True)
        sel = sub == ik
        vals.append(m)
        sels.append(sel)
        idx_out = jnp.where(out_row == k, ik, idx_out)
        work = jnp.where(sel, -jnp.inf, work)
    es = [jnp.exp(v - vals[0]) for v in vals]
    den = es[0] + es[1] + es[2] + es[3]
    gate_out = jnp.zeros((8, tt), F32)
    for k in range(TOP_K):
        gate_out = jnp.where(out_row == k, es[k] / den, gate_out)
    onehot = jnp.zeros((N_EXP, tt), F32)
    for sel in sels:
        onehot = onehot + sel.astype(F32)
    before = (lax.broadcasted_iota(I32, (tt, tt), 0) < lax.broadcasted_iota(I32, (tt, tt), 1)).astype(BF16)
    base = carry[...] + jnp.dot(onehot.astype(BF16), before, preferred_element_type=F32)
    rank_out = jnp.zeros((8, tt), I32)
    for k in range(TOP_K):
        rk = jnp.sum(jnp.where(sels[k], base, 0.0), axis=0, keepdims=True)
        rank_out = jnp.where(out_row == k, rk.astype(I32), rank_out)
    carry[...] = carry[...] + jnp.sum(onehot, axis=1, keepdims=True)
    idx_ref[...] = idx_out
    gate_ref[...] = gate_out
    rank_ref[...] = rank_out
    cnt_ref[...] = carry[...]


def _route(x1, wr_t, br_col, *, tt):
    n = x1.shape[0]
    rows = pl.BlockSpec((8, tt), lambda i: (0, i))
    const = lambda a: pl.BlockSpec(a.shape, lambda i, _n=a.ndim: (0,) * _n)
    return pl.pallas_call(
        functools.partial(_route_kernel, tt=tt), grid=(n // tt,),
        in_specs=[pl.BlockSpec((tt, D), lambda i: (i, 0)), const(wr_t), const(br_col)],
        out_specs=[rows, rows, rows, pl.BlockSpec((N_EXP, 1), lambda i: (0, 0))],
        out_shape=[jax.ShapeDtypeStruct((8, n), I32), jax.ShapeDtypeStruct((8, n), F32),
                   jax.ShapeDtypeStruct((8, n), I32), jax.ShapeDtypeStruct((N_EXP, 1), F32)],
        scratch_shapes=[pltpu.VMEM((N_EXP, 1), F32)], name="moe_route",
        compiler_params=_cp(("arbitrary",)))(x1, wr_t, br_col)


SUB = D // LANES
assert SUB == 8


def _rows_to_tiles(dst_ref, x, n):
    for s in range(SUB):
        dst_ref[pl.ds(s, n, stride=SUB), :] = x[:, LANES * s:LANES * (s + 1)]


def _tiles_to_rows(src_ref, n):
    return jnp.concatenate([src_ref[pl.ds(s, n, stride=SUB), :] for s in range(SUB)], axis=1)


def _tile_copy(src, s, dst, d, sem):
    return pltpu.make_async_copy(src.at[pl.ds(pl.multiple_of(s * SUB, SUB), SUB), :],
                                 dst.at[pl.ds(pl.multiple_of(d * SUB, SUB), SUB), :], sem)


ROW_UNROLL = 16


def _drain_rows(src, dst, sem, tt):
    def drain(r, carry):
        for k in range(TOP_K):
            _tile_copy(src, 0, dst, 0, sem).wait()
        return carry

    lax.fori_loop(0, tt, drain, 0, unroll=ROW_UNROLL)


def _disp_kernel(dest_ref, x_ref, xs_in, xs_ref, xt, sem, *, tt, nt):
    del xs_in
    i = pl.program_id(0)
    slot = lax.rem(i, 2)
    stage = xt.at[slot]
    _rows_to_tiles(stage, x_ref[...], tt)

    def issue(r, carry):
        for k in range(TOP_K):
            _tile_copy(stage, r, xs_ref, dest_ref[r * TOP_K + k], sem.at[slot]).start(priority=k % 2)
        return carry

    lax.fori_loop(0, tt, issue, 0, unroll=ROW_UNROLL)

    @pl.when(i > 0)
    def _():
        _drain_rows(xt.at[1 - slot], xs_ref, sem.at[1 - slot], tt)

    @pl.when(i == nt - 1)
    def _():
        _drain_rows(stage, xs_ref, sem.at[slot], tt)


def _dispatch(dest, x1, xs_prev, *, tt):
    n = x1.shape[0]
    smem_tok = pl.BlockSpec((tt * TOP_K,), lambda i: (i,), memory_space=pltpu.SMEM)
    return pl.pallas_call(
        functools.partial(_disp_kernel, tt=tt, nt=n // tt), grid=(n // tt,),
        in_specs=[smem_tok, pl.BlockSpec((tt, D), lambda i: (i, 0)), pl.BlockSpec(memory_space=pl.ANY)],
        out_specs=pl.BlockSpec(memory_space=pl.ANY),
        out_shape=jax.ShapeDtypeStruct(xs_prev.shape, F32),
        scratch_shapes=[pltpu.VMEM((2, tt * SUB, LANES), F32), pltpu.SemaphoreType.DMA((2,))],
        input_output_aliases={2: 0}, name="moe_dispatch",
        compiler_params=_cp(("arbitrary",)))(dest, x1, xs_prev)


def _expert_kernel(te_ref, nv_ref, xs_ref, w1_ref, b1_ref, w2_ref, b2_ref, y_ref, w1b, w2b):
    i = pl.program_id(0)
    valid = i < nv_ref[0]
    changed = jnp.logical_or(i == 0, te_ref[i] != te_ref[jnp.maximum(i - 1, 0)])

    @pl.when(jnp.logical_and(valid, changed))
    def _():
        for r0 in range(0, D, 256):
            w1b[r0:r0 + 256, :] = w1_ref[0, 0, r0:r0 + 256, :].astype(BF16)
            w2b[r0:r0 + 256, :] = w2_ref[0, 0, r0:r0 + 256, :].astype(BF16)

    @pl.when(valid)
    def _():
        x = _tiles_to_rows(xs_ref, EXP_TILE).astype(BF16)
        h = jnp.dot(x, w1b[...], preferred_element_type=F32) + b1_ref[0, 0]
        glu = jnp.minimum(h[:, :D], SWIGLU_LIMIT)
        lin = jnp.clip(h[:, D:], -SWIGLU_LIMIT, SWIGLU_LIMIT)
        act = glu * _sigmoid(SWIGLU_ALPHA * glu) * (lin + 1.0)
        y = jnp.dot(act.astype(BF16), w2b[...], preferred_element_type=F32) + b2_ref[0, 0]
        _rows_to_tiles(y_ref, y, EXP_TILE)

    @pl.when(jnp.logical_not(valid))
    def _():
        y_ref[...] = jnp.zeros_like(y_ref)


def _experts(tile_expert, n_valid, xs, w1, b1, w2, b2, layer):
    n_tiles = xs.shape[0] // (EXP_TILE * SUB)
    grid_spec = pltpu.PrefetchScalarGridSpec(
        num_scalar_prefetch=2, grid=(n_tiles,),
        in_specs=[pl.BlockSpec((EXP_TILE * SUB, LANES), lambda i, te, nv: (i, 0)),
                  pl.BlockSpec((1, 1, D, 2 * D), lambda i, te, nv: (layer, te[i], 0, 0)),
                  pl.BlockSpec((1, 1, 1, 2 * D), lambda i, te, nv: (layer, te[i], 0, 0)),
                  pl.BlockSpec((1, 1, D, D), lambda i, te, nv: (layer, te[i], 0, 0)),
                  pl.BlockSpec((1, 1, 1, D), lambda i, te, nv: (layer, te[i], 0, 0))],
        out_specs=pl.BlockSpec((EXP_TILE * SUB, LANES), lambda i, te, nv: (i, 0)),
        scratch_shapes=[pltpu.VMEM((D, 2 * D), BF16), pltpu.VMEM((D, D), BF16)])
    return pl.pallas_call(
        _expert_kernel, grid_spec=grid_spec, out_shape=jax.ShapeDtypeStruct(xs.shape, F32), name="moe_experts",
        compiler_params=_cp(("arbitrary",)))(tile_expert, n_valid, xs, w1, b1, w2, b2)


def _comb_kernel(dest_ref, next_ref, gate_ref, x1_ref, p_ref, y_ref, lg_ref, lb_ref, wg_ref, bg_ref,
                 wp_ref, o_ref, buf, sem, *, tt, nt):
    i = pl.program_id(0)
    slot = lax.rem(i, 2)

    def gather(idx_ref, s):
        def issue(r, carry):
            for k in range(TOP_K):
                _tile_copy(y_ref, idx_ref[r * TOP_K + k], buf.at[s, k], r, sem.at[s]).start(priority=k % 2)
            return carry

        lax.fori_loop(0, tt, issue, 0, unroll=ROW_UNROLL)

    @pl.when(i == 0)
    def _():
        gather(dest_ref, slot)

    @pl.when(i + 1 < nt)
    def _():
        gather(next_ref, 1 - slot)

    _drain_rows(y_ref, buf.at[slot, 0], sem.at[slot], tt)
    gate = gate_ref[...]
    moe = gate[:, 0:1] * _tiles_to_rows(buf.at[slot, 0], tt)
    for k in range(1, TOP_K):
        moe = moe + gate[:, k:k + 1] * _tiles_to_rows(buf.at[slot, k], tt)
    x2 = _ln(ALPHA * x1_ref[...] + moe, lg_ref[...], lb_ref[...])
    gt = _sigmoid(_bdot(x2, wg_ref[...]) + bg_ref[...])
    o_ref[...] = x2 + gt * _bdot(p_ref[...], wp_ref[...])


def _combine(dest, gate, x1, p, y, lnp, wg, bg, wp, *, tt):
    n = x1.shape[0]
    smem_tok = pl.BlockSpec((tt * TOP_K,), lambda i: (i,), memory_space=pltpu.SMEM)
    tok = lambda w: pl.BlockSpec((tt, w), lambda i: (i, 0))
    const = lambda a: pl.BlockSpec(a.shape, lambda i, _n=a.ndim: (0,) * _n)
    nt = n // tt
    smem_next = pl.BlockSpec((tt * TOP_K,), lambda i: (jnp.minimum(i + 1, nt - 1),), memory_space=pltpu.SMEM)
    return pl.pallas_call(
        functools.partial(_comb_kernel, tt=tt, nt=nt), grid=(nt,),
        in_specs=[smem_tok, smem_next, tok(TOP_K), tok(D), tok(p.shape[1]), pl.BlockSpec(memory_space=pl.ANY),
                  const(lnp[0]), const(lnp[1]), const(wg), const(bg), const(wp)],
        out_specs=tok(D), out_shape=jax.ShapeDtypeStruct((n, D), F32),
        scratch_shapes=[pltpu.VMEM((2, TOP_K, tt * SUB, LANES), F32), pltpu.SemaphoreType.DMA((2,))],
        name="moe_combine_ple",
        compiler_params=_cp(("arbitrary",)))(dest, dest, gate, x1, p, y, lnp[0], lnp[1], wg, bg, wp)


def _moe_ple(x1, p, xs_buf, wr_t, br_col, w1, b1, w2, b2, layer, lnp, wg, bg, wp, *, tt):
    n = x1.shape[0]
    idx, gate, rank, counts = _route(x1, wr_t, br_col, tt=512 if n % 512 == 0 else tt)
    counts = counts[:, 0].astype(I32)
    padded = (counts + EXP_TILE - 1) // EXP_TILE * EXP_TILE
    pad_end = jnp.cumsum(padded)
    pad_start = pad_end - padded
    n_tiles = xs_buf.shape[0] // (EXP_TILE * SUB)
    n_valid = (pad_end[-1] // EXP_TILE).astype(I32)
    tiles = jnp.minimum(jnp.arange(n_tiles, dtype=I32), n_valid - 1) * EXP_TILE
    tile_expert = jnp.minimum(jnp.sum((tiles[:, None] >= pad_end[None, :]).astype(I32), axis=1), N_EXP - 1)
    experts = jnp.arange(N_EXP, dtype=I32)
    start_of = jnp.sum(jnp.where(idx[:TOP_K, :, None] == experts, pad_start, 0), axis=-1)
    dest = (start_of + rank[:TOP_K]).T.reshape(-1)
    xs = _dispatch(dest, x1, xs_buf, tt=tt)
    y = _experts(tile_expert, n_valid.reshape(1), xs, w1, b1, w2, b2, layer)
    return _combine(dest, gate[:TOP_K].T, x1, p, y, lnp, wg, bg, wp, tt=tt), xs


def _state_to_scan(s, nb):
    s = s.reshape(nb // 8, 8, HEADS // 2, 2, HD, HD).transpose(5, 4, 0, 3, 2, 1)
    return s.reshape(HD, HD, nb * HEADS)


def _state_from_scan(s, nb):
    s = s.reshape(HD, HD, nb // 8, 2, HEADS // 2, 8).transpose(2, 5, 4, 3, 1, 0)
    return s.reshape(nb, HEADS, HD, HD)


def _head_vec_to_scan(a):
    a = a.reshape(HEADS // 2, 2, HD).transpose(2, 1, 0)
    return jnp.broadcast_to(a[..., None], (HD, 2, HEADS // 2, 8)).reshape(HD, LANES)


def _row2(a):
    return a.reshape(1, -1)


def _tile(n):
    for tt in (256, 128, 64, 32, 16, 8):
        if n % tt == 0:
            return tt
    raise ValueError(n)


def kernel(x_prompt, x_sample, p_prompt, p_sample, cache_conv, state_rwkv_shift, state_rwkv_wkv, cache_swa_k, cache_swa_v, conv_w_in, conv_b_in, conv_w_dw, conv_b_dw, conv_ln_g, conv_ln_b, conv_w_out, conv_b_out, rwkv_mu, rwkv_w_rkv, rwkv_w0, rwkv_w1, rwkv_w2, rwkv_a0, rwkv_a1, rwkv_a2, rwkv_g1, rwkv_g2, rwkv_k_k, rwkv_k_a, rwkv_r_k, rwkv_lnx_g, rwkv_lnx_b, rwkv_w_o, attn_w_qkv, attn_b_qkv, attn_sinks, attn_w_o, attn_b_o, rel_bias, ln_g, ln_b, moe_w_router, moe_b_router, moe_w1, moe_b1, moe_w2, moe_b2, ple_w_proj, ple_w_gate, ple_b_gate):
    bp, tp, _ = x_prompt.shape
    bs, ts, _ = x_sample.shape
    n_p, n_s = bp * tp, bs * ts
    n = n_p + n_s
    assert tp % 128 == 0 and n_p % ts == 0 and ts % 8 == 0 and HALO <= ts <= CHUNK
    tt_tok = _tile(n)
    tt_p = 128
    x = None
    p_all = jnp.concatenate([p_prompt.reshape(DEPTH, n_p, -1), p_sample.reshape(DEPTH, n_s, -1)], axis=1)
    n_rows = (-(-n * TOP_K // EXP_TILE) + N_EXP) * EXP_TILE
    xs_buf = jnp.zeros((n_rows * SUB, LANES), F32)
    head_sel = (jnp.arange(D)[:, None] // HD == jnp.arange(LANES)[None, :]).astype(BF16)
    head_sel_t = head_sel.T
    conv_p, conv_s, shift_p, shift_s, wkv_p, wkv_s = [], [], [], [], [], []
    swa_kp, swa_vp, swa_ks, swa_vs = [], [], [], []
    for i in range(DEPTH):
        kind, j = i % 3, i // 3
        lnp = (_row2(ln_g[i, 0]), _row2(ln_b[i, 0]))
        if kind == 0:
            cw = (conv_w_in[j].astype(BF16), _row2(conv_b_in[j]), conv_w_dw[j].astype(BF16).astype(F32),
                  _row2(conv_b_dw[j]),
                  _row2(conv_ln_g[j]), _row2(conv_ln_b[j]), conv_w_out[j].astype(BF16), _row2(conv_b_out[j]))
            st_p = jnp.zeros((bp, HALO, D), F32)
            st_s = jnp.pad(cache_conv[j], ((0, 0), (HALO - (CONV_W - 1), 0), (0, 0)))
            xin_p, xin_s = (x_prompt.reshape(n_p, D), x_sample.reshape(n_s, D)) if i == 0 else (x, x)
            x1, so_p = _conv_mixer(xin_p, st_p, cw, lnp, nb=bp, t=tp, tt=tt_p, row_off=0, n_total=n, prev=None)
            x1, so_s = _conv_mixer(xin_s, st_s, cw, lnp, nb=bs, t=ts, tt=ts, row_off=n_p, n_total=n, prev=[x1])
            conv_p.append(so_p[:, HALO - (CONV_W - 1):])
            conv_s.append(so_s[:, HALO - (CONV_W - 1):])
        elif kind == 1:
            rw = (rwkv_mu[j], rwkv_w_rkv[j, 0].astype(BF16), rwkv_w_rkv[j, 1].astype(BF16),
                  rwkv_w_rkv[j, 2].astype(BF16), _row2(rwkv_w0[j]), rwkv_w1[j].astype(BF16),
                  rwkv_w2[j].astype(BF16), _row2(rwkv_a0[j]), rwkv_a1[j].astype(BF16), rwkv_a2[j].astype(BF16),
                  rwkv_g1[j].astype(BF16), rwkv_g2[j].astype(BF16), _row2(rwkv_k_k[j]), _row2(rwkv_k_a[j]))
            sh_p = jnp.zeros((bp, 1, D), F32)
            sh_s = state_rwkv_shift[j].reshape(bs, 1, D)
            post_c = [_head_vec_to_scan(a) for a in (rwkv_lnx_g[j], rwkv_lnx_b[j], rwkv_r_k[j].reshape(-1))]
            out_c = [('const', rwkv_w_o[j].astype(BF16)), ('const', lnp[0]), ('const', lnp[1])]
            x1, states = None, []
            for (lo, nb_, t_, tt_, sh, s0, tc) in ((0, bp, tp, tt_p, sh_p, None, 64),
                                                   (n_p, bs, ts, ts, sh_s, state_rwkv_wkv[j], ts)):
                r, w, k, v, an, b, g = _rw1(x, sh, rw, head_sel, head_sel_t, nb=nb_, t=t_, tt=tt_, row_off=lo)
                rs, ws, ks, vs, ans, bs_ = _to_scan_call([a.reshape(nb_, t_, D) for a in (r, w, k, v, an, b)],
                                                         tc=min(tc, 32))
                s0l = jnp.zeros((HD, HD, nb_ * HEADS), F32) if s0 is None else _state_to_scan(s0, nb_)
                o_l, s_l = _scan(rs, ws, ks, vs, ans, bs_, s0l, tc=tc)
                y = _post(o_l, rs, ks, vs, *post_c, tc=tc).reshape(t_, nb_ * D)
                x1, = _seq_call(_rw3_kernel, name=f"rwkv_out_t{t_}", nb=nb_, nt=t_ // tt_, tt=tt_, row_off=lo,
                                n_total=n, ins=[('tm', y), ('own', g), ('tok', x)] + out_c,
                                outs=[('tok', (D,), F32)], scratch=[], prev=None if x1 is None else [x1])
                states.append(_state_from_scan(s_l, nb_))
            shift_p.append(x[tp - 1:n_p:tp])
            shift_s.append(x[n_p + ts - 1::ts])
            wkv_p.append(states[0])
            wkv_s.append(states[1])
        else:
            q, kx, vx = _tok_call(_qkv_kernel, "attn_qkv", n, tt_tok, [x],
                                  [attn_w_qkv[j].astype(BF16), _row2(attn_b_qkv[j])],
                                  [('heads', BF16), (KVH * HD, F32), (KVH * HD, F32)])
            k_p = kx[:n_p].reshape(bp, tp, KVH * HD)
            v_p = vx[:n_p].reshape(bp, tp, KVH * HD)
            zpad = jnp.zeros((bp, WINDOW, KVH * HD), F32)
            nc = tp // CHUNK
            band = WINDOW + CHUNK
            o = _attn(q, jnp.concatenate([zpad, k_p], axis=1), jnp.concatenate([zpad, v_p], axis=1),
                      _t5_bias(rel_bias, CHUNK, band), attn_sinks[j], nb=bp, nt=nc, nq=CHUNK, nsub=1, kb=band,
                      mask_lo=WINDOW, row_off=0, prev=None)
            k_all = jnp.concatenate([cache_swa_k[j].reshape(bs, WINDOW, KVH * HD),
                                     kx[n_p:].reshape(bs, ts, KVH * HD)], axis=1)
            v_all = jnp.concatenate([cache_swa_v[j].reshape(bs, WINDOW, KVH * HD),
                                     vx[n_p:].reshape(bs, ts, KVH * HD)], axis=1)
            o = _attn(q, k_all, v_all, _t5_bias(rel_bias, ts, WINDOW + ts), attn_sinks[j],
                      nb=bs, nt=1, nq=ts, nsub=1, kb=WINDOW + ts, mask_lo=0, row_off=n_p, prev=o)
            x1, = _tok_call(_oproj_kernel, "attn_out", n, tt_tok, [o, x],
                            [attn_w_o[j].astype(BF16), _row2(attn_b_o[j]), lnp[0], lnp[1]], [(D, F32)])
            swa_kp.append(k_p[:, -WINDOW:].reshape(bp, WINDOW, KVH, HD))
            swa_vp.append(v_p[:, -WINDOW:].reshape(bp, WINDOW, KVH, HD))
            swa_ks.append(k_all[:, -WINDOW:].reshape(bs, WINDOW, KVH, HD))
            swa_vs.append(v_all[:, -WINDOW:].reshape(bs, WINDOW, KVH, HD))
        x, xs_buf = _moe_ple(x1, p_all[i], xs_buf, moe_w_router[i].T, moe_b_router[i].reshape(N_EXP, 1), moe_w1,
                             moe_b1.reshape(DEPTH, N_EXP, 1, 2 * D), moe_w2, moe_b2.reshape(DEPTH, N_EXP, 1, D), i,
                             (_row2(ln_g[i, 1]), _row2(ln_b[i, 1])),
                             ple_w_gate[i].astype(BF16), _row2(ple_b_gate[i]), ple_w_proj[i].astype(BF16), tt=tt_tok)
    return (x[:n_p].reshape(bp, tp, D), x[n_p:].reshape(bs, ts, D), jnp.stack(conv_p), jnp.stack(conv_s),
            jnp.stack(shift_p), jnp.stack(shift_s), jnp.stack(wkv_p), jnp.stack(wkv_s), jnp.stack(swa_kp),
            jnp.stack(swa_vp), jnp.stack(swa_ks), jnp.stack(swa_vs))
```

```python
import functools
import math

import jax
import jax.numpy as jnp
from jax import lax
from jax.experimental import pallas as pl
from jax.experimental.pallas import tpu as pltpu

F32 = jnp.float32
BF16 = jnp.bfloat16
I32 = jnp.int32

D = 1024
DEPTH = 4
CONV_W = 31
HALO = 32
HEADS = 16
HD = 64
KVH = 2
GROUP = HEADS // KVH
WINDOW = 128
CHUNK = 64
N_BUCKETS = 32
MAX_DISTANCE = 128
N_EXP = 32
TOP_K = 4
EXP_TILE = 512
LANES = 128
LNX_EPS = 64e-5
LN_EPS = 1e-5
ALPHA = (2 * DEPTH) ** 0.25
SWIGLU_ALPHA = 1.702
SWIGLU_LIMIT = 7.0
VMEM_LIMIT = 56 * 1024 * 1024


def _cp(sem):
    return pltpu.CompilerParams(dimension_semantics=sem, vmem_limit_bytes=VMEM_LIMIT)


def _bdot(a, b):
    return jnp.dot(a.astype(BF16), b.astype(BF16), preferred_element_type=F32)


def _split(a):
    hi = a.astype(BF16)
    lo = (a - hi.astype(F32)).astype(BF16)
    return hi, lo


def _split_dot(a, b_exact):
    hi, lo = _split(a)
    return (jnp.dot(hi, b_exact, preferred_element_type=F32)
            + jnp.dot(lo, b_exact, preferred_element_type=F32))


def _ln(x, g, b, eps=LN_EPS):
    mu = jnp.mean(x, axis=-1, keepdims=True)
    xc = x - mu
    var = jnp.mean(xc * xc, axis=-1, keepdims=True)
    return xc * lax.rsqrt(var + eps) * g + b


def _sigmoid(x):
    return 1.0 / (1.0 + jnp.exp(-x))


def _seq_call(body, *, name, nb, nt, tt, row_off, n_total, ins, outs, scratch, prev=None):
    off = row_off // tt
    in_specs, args = [], []
    for kind, a in ins:
        if kind == 'tok':
            in_specs.append(pl.BlockSpec((tt, a.shape[1]), lambda b, j: (off + b * nt + j, 0)))
        elif kind == 'own':
            in_specs.append(pl.BlockSpec((tt, a.shape[1]), lambda b, j: (b * nt + j, 0)))
        elif kind == 'tm':
            in_specs.append(pl.BlockSpec((tt, a.shape[1] // nb), lambda b, j: (j, b)))
        elif kind == 'bat':
            in_specs.append(pl.BlockSpec((1,) + a.shape[1:], lambda b, j: (b, 0, 0)))
        else:
            in_specs.append(pl.BlockSpec(a.shape, lambda b, j, _n=a.ndim: (0,) * _n))
        args.append(a)
    out_specs, out_shapes = [], []
    for kind, tail, dt in outs:
        if kind == 'tok':
            out_specs.append(pl.BlockSpec((tt, tail[0]), lambda b, j: (off + b * nt + j, 0)))
            out_shapes.append(jax.ShapeDtypeStruct((n_total, tail[0]), dt))
        elif kind == 'own':
            out_specs.append(pl.BlockSpec((tt, tail[0]), lambda b, j: (b * nt + j, 0)))
            out_shapes.append(jax.ShapeDtypeStruct((nb * nt * tt, tail[0]), dt))
        elif kind == 'tm':
            out_specs.append(pl.BlockSpec((tt, tail[0]), lambda b, j: (j, b)))
            out_shapes.append(jax.ShapeDtypeStruct((nt * tt, nb * tail[0]), dt))
        else:
            out_specs.append(pl.BlockSpec((1,) + tuple(tail), lambda b, j: (b, 0, 0)))
            out_shapes.append(jax.ShapeDtypeStruct((nb,) + tuple(tail), dt))
    aliases = {}
    n_prev = 0
    tok_out = [i for i, o in enumerate(outs) if o[0] == 'tok']
    if prev is None:
        prev = [jnp.zeros((n_total, outs[i][1][0]), outs[i][2]) for i in tok_out]
    for p, oi in zip(prev, tok_out):
        aliases[len(args)] = oi
        in_specs.append(pl.BlockSpec(memory_space=pl.ANY))
        args.append(p)
        n_prev += 1
    n_in = len(ins)

    def wrapped(*refs):
        body(*refs[:n_in], *refs[n_in + n_prev:])

    return pl.pallas_call(
        wrapped, grid=(nb, nt), in_specs=in_specs, out_specs=out_specs, out_shape=out_shapes,
        scratch_shapes=scratch, input_output_aliases=aliases, name=name,
        compiler_params=_cp(("arbitrary", "arbitrary")))(*args)


CONV_RC = 64
CONV_LC = 256


def _conv_kernel(x_ref, st_ref, win_ref, bin_ref, wdw_ref, bdw_ref, cg_ref, cb_ref, wout_ref,
                 bout_ref, lg_ref, lb_ref, o_ref, so_ref, ubuf, ybuf, *, tt, nt):
    j = pl.program_id(1)

    @pl.when(j == 0)
    def _():
        ubuf[0:HALO, :] = st_ref[0].astype(BF16).astype(F32)

    @pl.when(j > 0)
    def _():
        ubuf[0:HALO, :] = ubuf[tt:tt + HALO, :]

    x = x_ref[...]
    h = _bdot(x, win_ref[...]) + bin_ref[...]
    u = h[:, :D] * _sigmoid(h[:, D:])
    ubuf[HALO:HALO + tt, :] = u.astype(BF16).astype(F32)

    @pl.when(j == nt - 1)
    def _():
        so_ref[0] = u[tt - HALO:tt, :]
    first = HALO - (CONV_W - 1)
    rc = min(CONV_RC, tt)
    for r0 in range(0, tt, rc):
        for c0 in range(0, D, CONV_LC):
            y = jnp.zeros((rc, CONV_LC), F32) + bdw_ref[:, c0:c0 + CONV_LC]
            for off in range(8):
                rows = rc + (8 if off else 0)
                acc = None
                for tap in range(CONV_W):
                    if (first + tap) % 8 != off:
                        continue
                    base = r0 + (first + tap) // 8 * 8
                    term = wdw_ref[tap:tap + 1, c0:c0 + CONV_LC] * ubuf[base:base + rows, c0:c0 + CONV_LC]
                    acc = term if acc is None else acc + term
                y = y + acc[off:off + rc]
            ybuf[r0:r0 + rc, c0:c0 + CONV_LC] = y
    z = _ln(ybuf[...], cg_ref[...], cb_ref[...])
    z = z * _sigmoid(z)
    mix = _bdot(z, wout_ref[...]) + bout_ref[...]
    o_ref[...] = _ln(ALPHA * x + mix, lg_ref[...], lb_ref[...])


def _conv_mixer(x, state, w, lnp, *, nb, t, tt, row_off, n_total, prev):
    nt = t // tt
    win, bin_, wdw, bdw, cg, cb, wout, bout = w
    ins = [('tok' if x.shape[0] == n_total else 'own', x), ('bat', state), ('const', win), ('const', bin_),
           ('const', wdw), ('const', bdw),
           ('const', cg), ('const', cb), ('const', wout), ('const', bout), ('const', lnp[0]), ('const', lnp[1])]
    outs = [('tok', (D,), F32), ('bat', (HALO, D), F32)]
    scratch = [pltpu.VMEM((HALO + tt, D), F32), pltpu.VMEM((tt, D), F32)]
    return _seq_call(functools.partial(_conv_kernel, tt=tt, nt=nt), name=f"conv_t{t}", nb=nb, nt=nt, tt=tt,
                     row_off=row_off,
                     n_total=n_total, ins=ins, outs=outs, scratch=scratch, prev=prev)


def _head_sum(y, hs_ref, hst_ref):
    s = _split_dot(y, hs_ref[...])
    return _split_dot(s, hst_ref[...])


def _rw1_kernel(x_ref, sh_ref, mu_ref, wr_ref, wk_ref, wv_ref, w0_ref, w1_ref, w2_ref, a0_ref, a1_ref,
                a2_ref, g1_ref, g2_ref, kk_ref, ka_ref, hs_ref, hst_ref,
                r_ref, w_ref, k_ref, v_ref, an_ref, b_ref, g_ref, xbuf, *, tt):
    j = pl.program_id(1)

    @pl.when(j == 0)
    def _():
        xbuf[7:8, :] = sh_ref[0]

    @pl.when(j > 0)
    def _():
        xbuf[7:8, :] = xbuf[7 + tt:8 + tt, :]

    x = x_ref[...]
    xbuf[8:8 + tt, :] = x
    xx = xbuf[7:7 + tt, :] - x
    mu = mu_ref[...]
    xr = x + xx * mu[0:1]
    xw = x + xx * mu[1:2]
    xk = x + xx * mu[2:3]
    xv = x + xx * mu[3:4]
    xa = x + xx * mu[4:5]
    xg = x + xx * mu[5:6]
    r = _bdot(xr, wr_ref[...])
    k = _bdot(xk, wk_ref[...])
    v = _bdot(xv, wv_ref[...])
    lw = w0_ref[...] + _bdot(jnp.tanh(_bdot(xw, w1_ref[...])), w2_ref[...])
    z = -lw
    log_w = -(jnp.maximum(z, 0.0) + jnp.log(1.0 + jnp.exp(-jnp.abs(z)))) - 0.5
    a = _sigmoid(a0_ref[...] + _bdot(_bdot(xa, a1_ref[...]), a2_ref[...]))
    g = _bdot(_sigmoid(_bdot(xg, g1_ref[...])), g2_ref[...])
    kk = k * kk_ref[...]
    ss = _head_sum(kk * kk, hs_ref, hst_ref)
    kk = kk * lax.rsqrt(jnp.maximum(ss, 1e-24))
    r_ref[...] = r
    w_ref[...] = jnp.exp(-jnp.exp(log_w))
    k_ref[...] = k * (1.0 + (a - 1.0) * ka_ref[...])
    v_ref[...] = v
    an_ref[...] = -kk
    b_ref[...] = kk * a
    g_ref[...] = g


def _rw1(x, shift, w, hs, hst, *, nb, t, tt, row_off):
    nt = t // tt
    prev = []
    ins = [('tok', x), ('bat', shift)] + [('const', a) for a in w] + [('const', hs), ('const', hst)]
    outs = [('own', (D,), F32)] * (N_SCAN_IN + 1)
    scratch = [pltpu.VMEM((8 + tt, D), F32)]
    return _seq_call(functools.partial(_rw1_kernel, tt=tt), name=f"rwkv_proj_t{t}", nb=nb, nt=nt, tt=tt,
                     row_off=row_off,
                     n_total=x.shape[0], ins=ins, outs=outs, scratch=scratch, prev=prev)


def _scan_kernel(r_ref, w_ref, k_ref, v_ref, an_ref, b_ref, nxt_ref, s0_ref, o_ref, st_ref, S, sa_buf,
                 *, tc, nc):
    c = pl.program_id(1)

    @pl.when(c == 0)
    def _():
        S[...] = s0_ref[...]

        def init(kk, acc):
            return acc + S[kk] * an_ref[0, pl.ds(kk, 1), :]

        sa_buf[...] = lax.fori_loop(0, HD, init, jnp.zeros((HD, LANES), F32))

    def step(t, sa, a_next):
        vt = v_ref[t]

        def kbody(kk, acc):
            o_acc, sa_acc = acc
            sk = (S[kk] * w_ref[t, pl.ds(kk, 1), :] + sa * b_ref[t, pl.ds(kk, 1), :]
                  + vt * k_ref[t, pl.ds(kk, 1), :])
            S[kk] = sk
            return (o_acc + sk * r_ref[t, pl.ds(kk, 1), :], sa_acc + sk * a_next(kk))

        zero = jnp.zeros((HD, LANES), F32)
        o_acc, sa_next = lax.fori_loop(0, HD, kbody, (zero, zero), unroll=8)
        o_ref[t] = o_acc
        return sa_next

    sa = lax.fori_loop(0, tc - 1, lambda t, sa: step(t, sa, lambda kk: an_ref[t + 1, pl.ds(kk, 1), :]),
                       sa_buf[...])
    sa_buf[...] = step(tc - 1, sa, lambda kk: nxt_ref[0, pl.ds(kk, 1), :])

    @pl.when(c == nc - 1)
    def _():
        st_ref[...] = S[...]


def _scan(r, w, k, v, an, b, s0, *, tc):
    t, _, lanes = r.shape
    ng, nc = lanes // LANES, t // tc
    seq = pl.BlockSpec((tc, HD, LANES), lambda g, c: (c, 0, g))
    nxt = pl.BlockSpec((1, HD, LANES), lambda g, c: (jnp.minimum((c + 1) * tc, t - 1), 0, g))
    return pl.pallas_call(
        functools.partial(_scan_kernel, tc=tc, nc=nc), grid=(ng, nc),
        in_specs=[seq] * 6 + [nxt, pl.BlockSpec((HD, HD, LANES), lambda g, c: (0, 0, g))],
        out_specs=[seq, pl.BlockSpec((HD, HD, LANES), lambda g, c: (0, 0, g))],
        out_shape=[jax.ShapeDtypeStruct((t, HD, lanes), F32), jax.ShapeDtypeStruct((HD, HD, lanes), F32)],
        scratch_shapes=[pltpu.VMEM((HD, HD, LANES), F32), pltpu.VMEM((HD, LANES), F32)],
        name=f"wkv_scan_t{t}", compiler_params=_cp(("arbitrary", "arbitrary")))(r, w, k, v, an, b, an, s0)


N_SCAN_IN = 6


def _to_scan_kernel(*refs, tc):
    low = lax.broadcasted_iota(I32, (HD, LANES), 1) < HD

    def pair(i, carry):
        t = i * 2
        for src, dst in zip(refs[:N_SCAN_IN], refs[N_SCAN_IN:]):
            y0, y1 = src[:, t, :], src[:, t + 1, :]
            m = jnp.concatenate([y[:, LANES * p:LANES * (p + 1)] for y in (y0, y1) for p in range(8)], axis=0).T
            top, bot = m[:HD], m[HD:]
            dst[t] = jnp.where(low, top, pltpu.roll(bot, HD, 1))
            dst[t + 1] = jnp.where(low, pltpu.roll(top, HD, 1), bot)
        return carry

    lax.fori_loop(0, tc // 2, pair, 0, unroll=2)


def _to_scan_call(arrs, *, tc):
    nb, t, _ = arrs[0].shape
    ng, nc = nb // 8, t // tc
    return pl.pallas_call(
        functools.partial(_to_scan_kernel, tc=tc), grid=(ng, nc),
        in_specs=[pl.BlockSpec((8, tc, D), lambda g, c: (g, c, 0))] * N_SCAN_IN,
        out_specs=[pl.BlockSpec((tc, HD, LANES), lambda g, c: (c, 0, g))] * N_SCAN_IN,
        out_shape=[jax.ShapeDtypeStruct((t, HD, ng * LANES), F32)] * N_SCAN_IN, name=f"wkv_relayout_t{t}",
        compiler_params=_cp(("arbitrary", "arbitrary")))(*arrs)


def _post_kernel(o_ref, r_ref, k_ref, v_ref, xg_ref, xb_ref, rk_ref, y_ref, *, tc):
    low = lax.broadcasted_iota(I32, (HD, LANES), 1) < HD

    def norm(t):
        o = o_ref[t]
        d = o - jnp.mean(o, axis=0, keepdims=True)
        vo = jnp.mean(d * d, axis=0, keepdims=True)
        bonus = jnp.sum(r_ref[t] * k_ref[t] * rk_ref[...], axis=0, keepdims=True) * v_ref[t]
        return d * lax.rsqrt(vo + LNX_EPS) * xg_ref[...] + xb_ref[...] + bonus

    def pair(i, carry):
        t = i * 2
        z0, z1 = norm(t), norm(t + 1)
        top = jnp.where(low, z0, pltpu.roll(z1, HD, 1))
        bot = jnp.where(low, pltpu.roll(z0, HD, 1), z1)
        m = jnp.concatenate([top, bot], axis=0).T
        y_ref[t] = jnp.concatenate([m[8 * p:8 * p + 8] for p in range(8)], axis=1)
        y_ref[t + 1] = jnp.concatenate([m[HD + 8 * p:HD + 8 * p + 8] for p in range(8)], axis=1)
        return carry

    lax.fori_loop(0, tc // 2, pair, 0, unroll=4)


def _post(o, r, k, v, xg, xb, rk, *, tc):
    t, _, lanes = o.shape
    ng, nc = lanes // LANES, t // tc
    seq = pl.BlockSpec((tc, HD, LANES), lambda g, c: (c, 0, g))
    const = pl.BlockSpec((HD, LANES), lambda g, c: (0, 0))
    return pl.pallas_call(
        functools.partial(_post_kernel, tc=tc), grid=(ng, nc),
        in_specs=[seq] * 4 + [const] * 3,
        out_specs=pl.BlockSpec((tc, 8, D), lambda g, c: (c, g, 0)),
        out_shape=jax.ShapeDtypeStruct((t, ng * 8, D), F32), name=f"wkv_post_t{t}",
        compiler_params=_cp(("arbitrary", "arbitrary")))(o, r, k, v, xg, xb, rk)


def _rw3_kernel(y_ref, g_ref, x_ref, wo_ref, lg_ref, lb_ref, out_ref):
    mix = _bdot(y_ref[...] * g_ref[...], wo_ref[...])
    out_ref[...] = _ln(ALPHA * x_ref[...] + mix, lg_ref[...], lb_ref[...])


def _head_major_spec(tt):
    return pl.BlockSpec((HEADS, tt, HD), lambda i: (0, i, 0))


def _tok_call(body, name, n, tt, toks, consts, outs):
    in_specs = [_head_major_spec(tt) if a.ndim == 3 else pl.BlockSpec((tt, a.shape[1]), lambda i: (i, 0))
                for a in toks]
    in_specs += [pl.BlockSpec(a.shape, lambda i, _n=a.ndim: (0,) * _n) for a in consts]
    out_specs = [_head_major_spec(tt) if w == 'heads' else pl.BlockSpec((tt, w), lambda i: (i, 0))
                 for w, _ in outs]
    out_shape = [jax.ShapeDtypeStruct((HEADS, n, HD) if w == 'heads' else (n, w), dt) for w, dt in outs]
    return pl.pallas_call(
        body, grid=(n // tt,), in_specs=in_specs, out_specs=out_specs, out_shape=out_shape, name=name,
        compiler_params=_cp(("arbitrary",)))(*toks, *consts)


def _qkv_kernel(x_ref, w_ref, b_ref, q_ref, k_ref, v_ref):
    h = _bdot(x_ref[...], w_ref[...]) + b_ref[...]
    for hd in range(HEADS):
        q_ref[hd] = h[:, hd * HD:(hd + 1) * HD].astype(BF16)
    k_ref[...] = h[:, HEADS * HD:HEADS * HD + KVH * HD]
    v_ref[...] = h[:, HEADS * HD + KVH * HD:]


def _attn_kernel(q_ref, kp_ref, vp_ref, bias_ref, sink_ref, prev_ref, o_ref, *, nq, nsub, kb, mask_lo):
    del prev_ref
    c = pl.program_id(1)
    start = pl.multiple_of(c * (nsub * nq), 8)
    span = kb + (nsub - 1) * nq
    kall = kp_ref[0, pl.ds(start, span), :].astype(BF16)
    vall = vp_ref[0, pl.ds(start, span), :].astype(BF16)
    for s in range(nsub):
        rows = slice(s * nq, (s + 1) * nq)
        valid = (start + s * nq + lax.broadcasted_iota(I32, (1, kb), 1)) >= mask_lo
        for g in range(KVH):
            hs = slice(g * GROUP, (g + 1) * GROUP)
            qg = q_ref[hs, rows, :].reshape(GROUP * nq, HD)
            kh = kall[s * nq:s * nq + kb, g * HD:(g + 1) * HD]
            vh = vall[s * nq:s * nq + kb, g * HD:(g + 1) * HD]
            logits = lax.dot_general(qg, kh, (((1,), (1,)), ((), ())), preferred_element_type=F32) * HD ** -0.5
            logits = jnp.where(valid, logits + bias_ref[hs].reshape(GROUP * nq, kb), -1e30)
            sink = sink_ref[hs].reshape(GROUP * nq, 1)
            m = jnp.maximum(jnp.max(logits, axis=-1, keepdims=True), sink)
            p = jnp.exp(logits - m)
            p = p / (jnp.sum(p, axis=-1, keepdims=True) + jnp.exp(sink - m))
            og = jnp.dot(p.astype(BF16), vh, preferred_element_type=F32)
            o_ref[hs, rows, :] = og.reshape(GROUP, nq, HD).astype(BF16)


def _attn(q, kp, vp, bias, sinks, *, nb, nt, nq, nsub, kb, mask_lo, row_off, prev):
    off = row_off // (nq * nsub)
    n_total = q.shape[1]
    heads = pl.BlockSpec((HEADS, nq * nsub, HD), lambda b, c: (0, off + b * nt + c, 0))
    const = lambda a: pl.BlockSpec(a.shape, lambda b, c, _n=a.ndim: (0,) * _n)
    bat = lambda a: pl.BlockSpec((1,) + a.shape[1:], lambda b, c: (b, 0, 0))
    sink_tab = jnp.broadcast_to(sinks.reshape(HEADS, 1, 1), (HEADS, nq, 1))
    if prev is None:
        prev = jnp.zeros((HEADS, n_total, HD), BF16)
    return pl.pallas_call(
        functools.partial(_attn_kernel, nq=nq, nsub=nsub, kb=kb, mask_lo=mask_lo), grid=(nb, nt),
        in_specs=[heads, bat(kp), bat(vp), const(bias), const(sink_tab), pl.BlockSpec(memory_space=pl.ANY)],
        out_specs=heads, out_shape=jax.ShapeDtypeStruct((HEADS, n_total, HD), BF16),
        input_output_aliases={5: 0}, name=f"attn_q{nq}",
        compiler_params=_cp(("arbitrary", "arbitrary")))(q, kp, vp, bias, sink_tab, prev)


def _oproj_kernel(o_ref, x_ref, wo_ref, bo_ref, lg_ref, lb_ref, out_ref):
    acc = jnp.dot(o_ref[0], wo_ref[0:HD, :], preferred_element_type=F32)
    for hd in range(1, HEADS):
        acc = acc + jnp.dot(o_ref[hd], wo_ref[hd * HD:(hd + 1) * HD, :], preferred_element_type=F32)
    out_ref[...] = _ln(ALPHA * x_ref[...] + acc + bo_ref[...], lg_ref[...], lb_ref[...])


def _t5_bucket(rel):
    half = N_BUCKETS // 2
    max_exact = half // 2
    ret = jnp.where(rel > 0, half, 0)
    n = jnp.abs(rel)
    nf = jnp.maximum(n, 1).astype(F32)
    large = max_exact + (jnp.log(nf / max_exact) / math.log(MAX_DISTANCE / max_exact)
                         * (half - max_exact)).astype(I32)
    large = jnp.minimum(large, half - 1)
    return ret + jnp.where(n < max_exact, n, large)


def _t5_bias(rel_bias, n_q, n_k):
    rel = jnp.arange(n_k)[None, :] - WINDOW - jnp.arange(n_q)[:, None]
    onehot = (_t5_bucket(rel)[..., None] == jnp.arange(N_BUCKETS)).astype(F32)
    return jnp.einsum('qkn,nh->hqk', onehot, rel_bias, precision=lax.Precision.HIGHEST)


def _route_kernel(x_ref, wr_ref, br_ref, idx_ref, gate_ref, rank_ref, cnt_ref, carry, *, tt):
    i = pl.program_id(0)

    @pl.when(i == 0)
    def _():
        carry[...] = jnp.zeros_like(carry)

    logits = lax.dot_general(wr_ref[...].astype(BF16), x_ref[...].astype(BF16), (((1,), (1,)), ((), ())),
                             preferred_element_type=F32) + br_ref[...]
    sub = lax.broadcasted_iota(I32, (N_EXP, tt), 0)
    out_row = lax.broadcasted_iota(I32, (8, tt), 0)
    vals, sels = [], []
    idx_out = jnp.zeros((8, tt), I32)
    work = logits
    for k in range(TOP_K):
        m = jnp.max(work, axis=0, keepdims=True)
        ik = jnp.min(jnp.where(work == m, sub, N_EXP), axis=0, keepdims=True)
        sel = sub == ik
        vals.append(m)
        sels.append(sel)
        idx_out = jnp.where(out_row == k, ik, idx_out)
        work = jnp.where(sel, -jnp.inf, work)
    es = [jnp.exp(v - vals[0]) for v in vals]
    den = es[0] + es[1] + es[2] + es[3]
    gate_out = jnp.zeros((8, tt), F32)
    for k in range(TOP_K):
        gate_out = jnp.where(out_row == k, es[k] / den, gate_out)
    onehot = jnp.zeros((N_EXP, tt), F32)
    for sel in sels:
        onehot = onehot + sel.astype(F32)
    before = (lax.broadcasted_iota(I32, (tt, tt), 0) < lax.broadcasted_iota(I32, (tt, tt), 1)).astype(BF16)
    base = carry[...] + jnp.dot(onehot.astype(BF16), before, preferred_element_type=F32)
    rank_out = jnp.zeros((8, tt), I32)
    for k in range(TOP_K):
        rk = jnp.sum(jnp.where(sels[k], base, 0.0), axis=0, keepdims=True)
        rank_out = jnp.where(out_row == k, rk.astype(I32), rank_out)
    carry[...] = carry[...] + jnp.sum(onehot, axis=1, keepdims=True)
    idx_ref[...] = idx_out
    gate_ref[...] = gate_out
    rank_ref[...] = rank_out
    cnt_ref[...] = carry[...]


def _route(x1, wr_t, br_col, *, tt):
    n = x1.shape[0]
    rows = pl.BlockSpec((8, tt), lambda i: (0, i))
    const = lambda a: pl.BlockSpec(a.shape, lambda i, _n=a.ndim: (0,) * _n)
    return pl.pallas_call(
        functools.partial(_route_kernel, tt=tt), grid=(n // tt,),
        in_specs=[pl.BlockSpec((tt, D), lambda i: (i, 0)), const(wr_t), const(br_col)],
        out_specs=[rows, rows, rows, pl.BlockSpec((N_EXP, 1), lambda i: (0, 0))],
        out_shape=[jax.ShapeDtypeStruct((8, n), I32), jax.ShapeDtypeStruct((8, n), F32),
                   jax.ShapeDtypeStruct((8, n), I32), jax.ShapeDtypeStruct((N_EXP, 1), F32)],
        scratch_shapes=[pltpu.VMEM((N_EXP, 1), F32)], name="moe_route",
        compiler_params=_cp(("arbitrary",)))(x1, wr_t, br_col)


SUB = D // LANES
assert SUB == 8


def _rows_to_tiles(dst_ref, x, n):
    for s in range(SUB):
        dst_ref[pl.ds(s, n, stride=SUB), :] = x[:, LANES * s:LANES * (s + 1)]


def _tiles_to_rows(src_ref, n):
    return jnp.concatenate([src_ref[pl.ds(s, n, stride=SUB), :] for s in range(SUB)], axis=1)


def _tile_copy(src, s, dst, d, sem):
    return pltpu.make_async_copy(src.at[pl.ds(pl.multiple_of(s * SUB, SUB), SUB), :],
                                 dst.at[pl.ds(pl.multiple_of(d * SUB, SUB), SUB), :], sem)


ROW_UNROLL = 16


def _drain_rows(src, dst, sem, tt):
    def drain(r, carry):
        for k in range(TOP_K):
            _tile_copy(src, 0, dst, 0, sem).wait()
        return carry

    lax.fori_loop(0, tt, drain, 0, unroll=ROW_UNROLL)


def _disp_kernel(dest_ref, x_ref, xs_in, xs_ref, xt, sem, *, tt, nt):
    del xs_in
    i = pl.program_id(0)
    slot = lax.rem(i, 2)
    stage = xt.at[slot]
    _rows_to_tiles(stage, x_ref[...], tt)

    def issue(r, carry):
        for k in range(TOP_K):
            _tile_copy(stage, r, xs_ref, dest_ref[r * TOP_K + k], sem.at[slot]).start(priority=k % 2)
        return carry

    lax.fori_loop(0, tt, issue, 0, unroll=ROW_UNROLL)

    @pl.when(i > 0)
    def _():
        _drain_rows(xt.at[1 - slot], xs_ref, sem.at[1 - slot], tt)

    @pl.when(i == nt - 1)
    def _():
        _drain_rows(stage, xs_ref, sem.at[slot], tt)


def _dispatch(dest, x1, xs_prev, *, tt):
    n = x1.shape[0]
    smem_tok = pl.BlockSpec((tt * TOP_K,), lambda i: (i,), memory_space=pltpu.SMEM)
    return pl.pallas_call(
        functools.partial(_disp_kernel, tt=tt, nt=n // tt), grid=(n // tt,),
        in_specs=[smem_tok, pl.BlockSpec((tt, D), lambda i: (i, 0)), pl.BlockSpec(memory_space=pl.ANY)],
        out_specs=pl.BlockSpec(memory_space=pl.ANY),
        out_shape=jax.ShapeDtypeStruct(xs_prev.shape, F32),
        scratch_shapes=[pltpu.VMEM((2, tt * SUB, LANES), F32), pltpu.SemaphoreType.DMA((2,))],
        input_output_aliases={2: 0}, name="moe_dispatch",
        compiler_params=_cp(("arbitrary",)))(dest, x1, xs_prev)


def _expert_kernel(te_ref, nv_ref, xs_ref, w1_ref, b1_ref, w2_ref, b2_ref, y_ref, w1b, w2b):
    i = pl.program_id(0)
    valid = i < nv_ref[0]
    changed = jnp.logical_or(i == 0, te_ref[i] != te_ref[jnp.maximum(i - 1, 0)])

    @pl.when(jnp.logical_and(valid, changed))
    def _():
        for r0 in range(0, D, 256):
            w1b[r0:r0 + 256, :] = w1_ref[0, 0, r0:r0 + 256, :].astype(BF16)
            w2b[r0:r0 + 256, :] = w2_ref[0, 0, r0:r0 + 256, :].astype(BF16)

    @pl.when(valid)
    def _():
        x = _tiles_to_rows(xs_ref, EXP_TILE).astype(BF16)
        h = jnp.dot(x, w1b[...], preferred_element_type=F32) + b1_ref[0, 0]
        glu = jnp.minimum(h[:, :D], SWIGLU_LIMIT)
        lin = jnp.clip(h[:, D:], -SWIGLU_LIMIT, SWIGLU_LIMIT)
        act = glu * _sigmoid(SWIGLU_ALPHA * glu) * (lin + 1.0)
        y = jnp.dot(act.astype(BF16), w2b[...], preferred_element_type=F32) + b2_ref[0, 0]
        _rows_to_tiles(y_ref, y, EXP_TILE)

    @pl.when(jnp.logical_not(valid))
    def _():
        y_ref[...] = jnp.zeros_like(y_ref)


def _experts(tile_expert, n_valid, xs, w1, b1, w2, b2, layer):
    n_tiles = xs.shape[0] // (EXP_TILE * SUB)
    grid_spec = pltpu.PrefetchScalarGridSpec(
        num_scalar_prefetch=2, grid=(n_tiles,),
        in_specs=[pl.BlockSpec((EXP_TILE * SUB, LANES), lambda i, te, nv: (i, 0)),
                  pl.BlockSpec((1, 1, D, 2 * D), lambda i, te, nv: (layer, te[i], 0, 0)),
                  pl.BlockSpec((1, 1, 1, 2 * D), lambda i, te, nv: (layer, te[i], 0, 0)),
                  pl.BlockSpec((1, 1, D, D), lambda i, te, nv: (layer, te[i], 0, 0)),
                  pl.BlockSpec((1, 1, 1, D), lambda i, te, nv: (layer, te[i], 0, 0))],
        out_specs=pl.BlockSpec((EXP_TILE * SUB, LANES), lambda i, te, nv: (i, 0)),
        scratch_shapes=[pltpu.VMEM((D, 2 * D), BF16), pltpu.VMEM((D, D), BF16)])
    return pl.pallas_call(
        _expert_kernel, grid_spec=grid_spec, out_shape=jax.ShapeDtypeStruct(xs.shape, F32), name="moe_experts",
        compiler_params=_cp(("arbitrary",)))(tile_expert, n_valid, xs, w1, b1, w2, b2)


def _comb_kernel(dest_ref, next_ref, gate_ref, x1_ref, p_ref, y_ref, lg_ref, lb_ref, wg_ref, bg_ref,
                 wp_ref, o_ref, buf, sem, *, tt, nt):
    i = pl.program_id(0)
    slot = lax.rem(i, 2)

    def gather(idx_ref, s):
        def issue(r, carry):
            for k in range(TOP_K):
                _tile_copy(y_ref, idx_ref[r * TOP_K + k], buf.at[s, k], r, sem.at[s]).start(priority=k % 2)
            return carry

        lax.fori_loop(0, tt, issue, 0, unroll=ROW_UNROLL)

    @pl.when(i == 0)
    def _():
        gather(dest_ref, slot)

    @pl.when(i + 1 < nt)
    def _():
        gather(next_ref, 1 - slot)

    _drain_rows(y_ref, buf.at[slot, 0], sem.at[slot], tt)
    gate = gate_ref[...]
    moe = gate[:, 0:1] * _tiles_to_rows(buf.at[slot, 0], tt)
    for k in range(1, TOP_K):
        moe = moe + gate[:, k:k + 1] * _tiles_to_rows(buf.at[slot, k], tt)
    x2 = _ln(ALPHA * x1_ref[...] + moe, lg_ref[...], lb_ref[...])
    gt = _sigmoid(_bdot(x2, wg_ref[...]) + bg_ref[...])
    o_ref[...] = x2 + gt * _bdot(p_ref[...], wp_ref[...])


def _combine(dest, gate, x1, p, y, lnp, wg, bg, wp, *, tt):
    n = x1.shape[0]
    smem_tok = pl.BlockSpec((tt * TOP_K,), lambda i: (i,), memory_space=pltpu.SMEM)
    tok = lambda w: pl.BlockSpec((tt, w), lambda i: (i, 0))
    const = lambda a: pl.BlockSpec(a.shape, lambda i, _n=a.ndim: (0,) * _n)
    nt = n // tt
    smem_next = pl.BlockSpec((tt * TOP_K,), lambda i: (jnp.minimum(i + 1, nt - 1),), memory_space=pltpu.SMEM)
    return pl.pallas_call(
        functools.partial(_comb_kernel, tt=tt, nt=nt), grid=(nt,),
        in_specs=[smem_tok, smem_next, tok(TOP_K), tok(D), tok(p.shape[1]), pl.BlockSpec(memory_space=pl.ANY),
                  const(lnp[0]), const(lnp[1]), const(wg), const(bg), const(wp)],
        out_specs=tok(D), out_shape=jax.ShapeDtypeStruct((n, D), F32),
        scratch_shapes=[pltpu.VMEM((2, TOP_K, tt * SUB, LANES), F32), pltpu.SemaphoreType.DMA((2,))],
        name="moe_combine_ple",
        compiler_params=_cp(("arbitrary",)))(dest, dest, gate, x1, p, y, lnp[0], lnp[1], wg, bg, wp)


def _moe_ple(x1, p, xs_buf, wr_t, br_col, w1, b1, w2, b2, layer, lnp, wg, bg, wp, *, tt):
    n = x1.shape[0]
    idx, gate, rank, counts = _route(x1, wr_t, br_col, tt=512 if n % 512 == 0 else tt)
    counts = counts[:, 0].astype(I32)
    padded = (counts + EXP_TILE - 1) // EXP_TILE * EXP_TILE
    pad_end = jnp.cumsum(padded)
    pad_start = pad_end - padded
    n_tiles = xs_buf.shape[0] // (EXP_TILE * SUB)
    n_valid = (pad_end[-1] // EXP_TILE).astype(I32)
    tiles = jnp.minimum(jnp.arange(n_tiles, dtype=I32), n_valid - 1) * EXP_TILE
    tile_expert = jnp.minimum(jnp.sum((tiles[:, None] >= pad_end[None, :]).astype(I32), axis=1), N_EXP - 1)
    experts = jnp.arange(N_EXP, dtype=I32)
    start_of = jnp.sum(jnp.where(idx[:TOP_K, :, None] == experts, pad_start, 0), axis=-1)
    dest = (start_of + rank[:TOP_K]).T.reshape(-1)
    xs = _dispatch(dest, x1, xs_buf, tt=tt)
    y = _experts(tile_expert, n_valid.reshape(1), xs, w1, b1, w2, b2, layer)
    return _combine(dest, gate[:TOP_K].T, x1, p, y, lnp, wg, bg, wp, tt=tt), xs


def _state_to_scan(s, nb):
    s = s.reshape(nb // 8, 8, HEADS // 2, 2, HD, HD).transpose(5, 4, 0, 3, 2, 1)
    return s.reshape(HD, HD, nb * HEADS)


def _state_from_scan(s, nb):
    s = s.reshape(HD, HD, nb // 8, 2, HEADS // 2, 8).transpose(2, 5, 4, 3, 1, 0)
    return s.reshape(nb, HEADS, HD, HD)


def _head_vec_to_scan(a):
    a = a.reshape(HEADS // 2, 2, HD).transpose(2, 1, 0)
    return jnp.broadcast_to(a[..., None], (HD, 2, HEADS // 2, 8)).reshape(HD, LANES)


def _row2(a):
    return a.reshape(1, -1)


def _tile(n):
    for tt in (256, 128, 64, 32, 16, 8):
        if n % tt == 0:
            return tt
    raise ValueError(n)


def kernel(x_prompt, x_sample, p_prompt, p_sample, cache_conv, state_rwkv_shift, state_rwkv_wkv, cache_swa_k, cache_swa_v, conv_w_in, conv_b_in, conv_w_dw, conv_b_dw, conv_ln_g, conv_ln_b, conv_w_out, conv_b_out, rwkv_mu, rwkv_w_rkv, rwkv_w0, rwkv_w1, rwkv_w2, rwkv_a0, rwkv_a1, rwkv_a2, rwkv_g1, rwkv_g2, rwkv_k_k, rwkv_k_a, rwkv_r_k, rwkv_lnx_g, rwkv_lnx_b, rwkv_w_o, attn_w_qkv, attn_b_qkv, attn_sinks, attn_w_o, attn_b_o, rel_bias, ln_g, ln_b, moe_w_router, moe_b_router, moe_w1, moe_b1, moe_w2, moe_b2, ple_w_proj, ple_w_gate, ple_b_gate):
    bp, tp, _ = x_prompt.shape
    bs, ts, _ = x_sample.shape
    n_p, n_s = bp * tp, bs * ts
    n = n_p + n_s
    assert tp % 128 == 0 and n_p % ts == 0 and ts % 8 == 0 and HALO <= ts <= CHUNK
    tt_tok = _tile(n)
    tt_p = 256 if tp % 256 == 0 else 128
    x = None
    p_all = jnp.concatenate([p_prompt.reshape(DEPTH, n_p, -1), p_sample.reshape(DEPTH, n_s, -1)], axis=1)
    n_rows = (-(-n * TOP_K // EXP_TILE) + N_EXP) * EXP_TILE
    xs_buf = jnp.zeros((n_rows * SUB, LANES), F32)
    head_sel = (jnp.arange(D)[:, None] // HD == jnp.arange(LANES)[None, :]).astype(BF16)
    head_sel_t = head_sel.T
    conv_p, conv_s, shift_p, shift_s, wkv_p, wkv_s = [], [], [], [], [], []
    swa_kp, swa_vp, swa_ks, swa_vs = [], [], [], []
    for i in range(DEPTH):
        kind, j = i % 3, i // 3
        lnp = (_row2(ln_g[i, 0]), _row2(ln_b[i, 0]))
        if kind == 0:
            cw = (conv_w_in[j].astype(BF16), _row2(conv_b_in[j]), conv_w_dw[j].astype(BF16).astype(F32),
                  _row2(conv_b_dw[j]),
                  _row2(conv_ln_g[j]), _row2(conv_ln_b[j]), conv_w_out[j].astype(BF16), _row2(conv_b_out[j]))
            st_p = jnp.zeros((bp, HALO, D), F32)
            st_s = jnp.pad(cache_conv[j], ((0, 0), (HALO - (CONV_W - 1), 0), (0, 0)))
            xin_p, xin_s = (x_prompt.reshape(n_p, D), x_sample.reshape(n_s, D)) if i == 0 else (x, x)
            x1, so_p = _conv_mixer(xin_p, st_p, cw, lnp, nb=bp, t=tp, tt=tt_p, row_off=0, n_total=n, prev=None)
            x1, so_s = _conv_mixer(xin_s, st_s, cw, lnp, nb=bs, t=ts, tt=ts, row_off=n_p, n_total=n, prev=[x1])
            conv_p.append(so_p[:, HALO - (CONV_W - 1):])
            conv_s.append(so_s[:, HALO - (CONV_W - 1):])
        elif kind == 1:
            rw = (rwkv_mu[j], rwkv_w_rkv[j, 0].astype(BF16), rwkv_w_rkv[j, 1].astype(BF16),
                  rwkv_w_rkv[j, 2].astype(BF16), _row2(rwkv_w0[j]), rwkv_w1[j].astype(BF16),
                  rwkv_w2[j].astype(BF16), _row2(rwkv_a0[j]), rwkv_a1[j].astype(BF16), rwkv_a2[j].astype(BF16),
                  rwkv_g1[j].astype(BF16), rwkv_g2[j].astype(BF16), _row2(rwkv_k_k[j]), _row2(rwkv_k_a[j]))
            sh_p = jnp.zeros((bp, 1, D), F32)
            sh_s = state_rwkv_shift[j].reshape(bs, 1, D)
            post_c = [_head_vec_to_scan(a) for a in (rwkv_lnx_g[j], rwkv_lnx_b[j], rwkv_r_k[j].reshape(-1))]
            out_c = [('const', rwkv_w_o[j].astype(BF16)), ('const', lnp[0]), ('const', lnp[1])]
            x1, states = None, []
            for (lo, nb_, t_, tt_, sh, s0, tc) in ((0, bp, tp, tt_p, sh_p, None, 64),
                                                   (n_p, bs, ts, ts, sh_s, state_rwkv_wkv[j], ts)):
                r, w, k, v, an, b, g = _rw1(x, sh, rw, head_sel, head_sel_t, nb=nb_, t=t_, tt=tt_, row_off=lo)
                rs, ws, ks, vs, ans, bs_ = _to_scan_call([a.reshape(nb_, t_, D) for a in (r, w, k, v, an, b)],
                                                         tc=min(tc, 32))
                s0l = jnp.zeros((HD, HD, nb_ * HEADS), F32) if s0 is None else _state_to_scan(s0, nb_)
                o_l, s_l = _scan(rs, ws, ks, vs, ans, bs_, s0l, tc=tc)
                y = _post(o_l, rs, ks, vs, *post_c, tc=tc).reshape(t_, nb_ * D)
                x1, = _seq_call(_rw3_kernel, name=f"rwkv_out_t{t_}", nb=nb_, nt=t_ // tt_, tt=tt_, row_off=lo,
                                n_total=n, ins=[('tm', y), ('own', g), ('tok', x)] + out_c,
                                outs=[('tok', (D,), F32)], scratch=[], prev=None if x1 is None else [x1])
                states.append(_state_from_scan(s_l, nb_))
            shift_p.append(x[tp - 1:n_p:tp])
            shift_s.append(x[n_p + ts - 1::ts])
            wkv_p.append(states[0])
            wkv_s.append(states[1])
        else:
            q, kx, vx = _tok_call(_qkv_kernel, "attn_qkv", n, tt_tok, [x],
                                  [attn_w_qkv[j].astype(BF16), _row2(attn_b_qkv[j])],
                                  [('heads', BF16), (KVH * HD, F32), (KVH * HD, F32)])
            k_p = kx[:n_p].reshape(bp, tp, KVH * HD)
            v_p = vx[:n_p].reshape(bp, tp, KVH * HD)
            zpad = jnp.zeros((bp, WINDOW, KVH * HD), F32)
            nc = tp // CHUNK
            band = WINDOW + CHUNK
            o = _attn(q, jnp.concatenate([zpad, k_p], axis=1), jnp.concatenate([zpad, v_p], axis=1),
                      _t5_bias(rel_bias, CHUNK, band), attn_sinks[j], nb=bp, nt=nc, nq=CHUNK, nsub=1, kb=band,
                      mask_lo=WINDOW, row_off=0, prev=None)
            k_all = jnp.concatenate([cache_swa_k[j].reshape(bs, WINDOW, KVH * HD),
                                     kx[n_p:].reshape(bs, ts, KVH * HD)], axis=1)
            v_all = jnp.concatenate([cache_swa_v[j].reshape(bs, WINDOW, KVH * HD),
                                     vx[n_p:].reshape(bs, ts, KVH * HD)], axis=1)
            o = _attn(q, k_all, v_all, _t5_bias(rel_bias, ts, WINDOW + ts), attn_sinks[j],
                      nb=bs, nt=1, nq=ts, nsub=1, kb=WINDOW + ts, mask_lo=0, row_off=n_p, prev=o)
            x1, = _tok_call(_oproj_kernel, "attn_out", n, tt_tok, [o, x],
                            [attn_w_o[j].astype(BF16), _row2(attn_b_o[j]), lnp[0], lnp[1]], [(D, F32)])
            swa_kp.append(k_p[:, -WINDOW:].reshape(bp, WINDOW, KVH, HD))
            swa_vp.append(v_p[:, -WINDOW:].reshape(bp, WINDOW, KVH, HD))
            swa_ks.append(k_all[:, -WINDOW:].reshape(bs, WINDOW, KVH, HD))
            swa_vs.append(v_all[:, -WINDOW:].reshape(bs, WINDOW, KVH, HD))
        x, xs_buf = _moe_ple(x1, p_all[i], xs_buf, moe_w_router[i].T, moe_b_router[i].reshape(N_EXP, 1), moe_w1,
                             moe_b1.reshape(DEPTH, N_EXP, 1, 2 * D), moe_w2, moe_b2.reshape(DEPTH, N_EXP, 1, D), i,
                             (_row2(ln_g[i, 1]), _row2(ln_b[i, 1])),
                             ple_w_gate[i].astype(BF16), _row2(ple_b_gate[i]), ple_w_proj[i].astype(BF16), tt=tt_tok)
    return (x[:n_p].reshape(bp, tp, D), x[n_p:].reshape(bs, ts, D), jnp.stack(conv_p), jnp.stack(conv_s),
            jnp.stack(shift_p), jnp.stack(shift_s), jnp.stack(wkv_p), jnp.stack(wkv_s), jnp.stack(swa_kp),
            jnp.stack(swa_vp), jnp.stack(swa_ks), jnp.stack(swa_vs))
```

```python
import functools
import math

import jax
import jax.numpy as jnp
from jax import lax
from jax.experimental import pallas as pl
from jax.experimental.pallas import tpu as pltpu

F32 = jnp.float32
BF16 = jnp.bfloat16
I32 = jnp.int32

D = 1024
DEPTH = 4
CONV_W = 31
HALO = 32
HEADS = 16
HD = 64
KVH = 2
GROUP = HEADS // KVH
WINDOW = 128
CHUNK = 64
N_BUCKETS = 32
MAX_DISTANCE = 128
N_EXP = 32
TOP_K = 4
EXP_TILE = 512
LANES = 128
LNX_EPS = 64e-5
LN_EPS = 1e-5
ALPHA = (2 * DEPTH) ** 0.25
SWIGLU_ALPHA = 1.702
SWIGLU_LIMIT = 7.0
VMEM_LIMIT = 56 * 1024 * 1024


def _cp(sem):
    return pltpu.CompilerParams(dimension_semantics=sem, vmem_limit_bytes=VMEM_LIMIT)


def _bdot(a, b):
    return jnp.dot(a.astype(BF16), b.astype(BF16), preferred_element_type=F32)


def _split(a):
    hi = a.astype(BF16)
    lo = (a - hi.astype(F32)).astype(BF16)
    return hi, lo


def _split_dot(a, b_exact):
    hi, lo = _split(a)
    return (jnp.dot(hi, b_exact, preferred_element_type=F32)
            + jnp.dot(lo, b_exact, preferred_element_type=F32))


def _ln(x, g, b, eps=LN_EPS):
    mu = jnp.mean(x, axis=-1, keepdims=True)
    xc = x - mu
    var = jnp.mean(xc * xc, axis=-1, keepdims=True)
    return xc * lax.rsqrt(var + eps) * g + b


def _sigmoid(x):
    return 1.0 / (1.0 + jnp.exp(-x))


def _seq_call(body, *, name, nb, nt, tt, row_off, n_total, ins, outs, scratch, prev=None):
    off = row_off // tt
    in_specs, args = [], []
    for kind, a in ins:
        if kind == 'tok':
            in_specs.append(pl.BlockSpec((tt, a.shape[1]), lambda b, j: (off + b * nt + j, 0)))
        elif kind == 'own':
            in_specs.append(pl.BlockSpec((tt, a.shape[1]), lambda b, j: (b * nt + j, 0)))
        elif kind == 'tm':
            in_specs.append(pl.BlockSpec((tt, a.shape[1] // nb), lambda b, j: (j, b)))
        elif kind == 'bat':
            in_specs.append(pl.BlockSpec((1,) + a.shape[1:], lambda b, j: (b, 0, 0)))
        else:
            in_specs.append(pl.BlockSpec(a.shape, lambda b, j, _n=a.ndim: (0,) * _n))
        args.append(a)
    out_specs, out_shapes = [], []
    for kind, tail, dt in outs:
        if kind == 'tok':
            out_specs.append(pl.BlockSpec((tt, tail[0]), lambda b, j: (off + b * nt + j, 0)))
            out_shapes.append(jax.ShapeDtypeStruct((n_total, tail[0]), dt))
        elif kind == 'own':
            out_specs.append(pl.BlockSpec((tt, tail[0]), lambda b, j: (b * nt + j, 0)))
            out_shapes.append(jax.ShapeDtypeStruct((nb * nt * tt, tail[0]), dt))
        elif kind == 'tm':
            out_specs.append(pl.BlockSpec((tt, tail[0]), lambda b, j: (j, b)))
            out_shapes.append(jax.ShapeDtypeStruct((nt * tt, nb * tail[0]), dt))
        else:
            out_specs.append(pl.BlockSpec((1,) + tuple(tail), lambda b, j: (b, 0, 0)))
            out_shapes.append(jax.ShapeDtypeStruct((nb,) + tuple(tail), dt))
    aliases = {}
    n_prev = 0
    tok_out = [i for i, o in enumerate(outs) if o[0] == 'tok']
    if prev is None:
        prev = [jnp.zeros((n_total, outs[i][1][0]), outs[i][2]) for i in tok_out]
    for p, oi in zip(prev, tok_out):
        aliases[len(args)] = oi
        in_specs.append(pl.BlockSpec(memory_space=pl.ANY))
        args.append(p)
        n_prev += 1
    n_in = len(ins)

    def wrapped(*refs):
        body(*refs[:n_in], *refs[n_in + n_prev:])

    return pl.pallas_call(
        wrapped, grid=(nb, nt), in_specs=in_specs, out_specs=out_specs, out_shape=out_shapes,
        scratch_shapes=scratch, input_output_aliases=aliases, name=name,
        compiler_params=_cp(("arbitrary", "arbitrary")))(*args)


CONV_RC = 64
CONV_LC = 256


def _conv_kernel(x_ref, st_ref, win_ref, bin_ref, wdw_ref, bdw_ref, cg_ref, cb_ref, wout_ref,
                 bout_ref, lg_ref, lb_ref, o_ref, so_ref, ubuf, ybuf, *, tt, nt):
    j = pl.program_id(1)

    @pl.when(j == 0)
    def _():
        ubuf[0:HALO, :] = st_ref[0].astype(BF16).astype(F32)

    @pl.when(j > 0)
    def _():
        ubuf[0:HALO, :] = ubuf[tt:tt + HALO, :]

    x = x_ref[...]
    h = _bdot(x, win_ref[...]) + bin_ref[...]
    u = h[:, :D] * _sigmoid(h[:, D:])
    ubuf[HALO:HALO + tt, :] = u.astype(BF16).astype(F32)

    @pl.when(j == nt - 1)
    def _():
        so_ref[0] = u[tt - HALO:tt, :]
    first = HALO - (CONV_W - 1)
    rc = min(CONV_RC, tt)
    for r0 in range(0, tt, rc):
        for c0 in range(0, D, CONV_LC):
            y = jnp.zeros((rc, CONV_LC), F32) + bdw_ref[:, c0:c0 + CONV_LC]
            for off in range(8):
                rows = rc + (8 if off else 0)
                acc = None
                for tap in range(CONV_W):
                    if (first + tap) % 8 != off:
                        continue
                    base = r0 + (first + tap) // 8 * 8
                    term = wdw_ref[tap:tap + 1, c0:c0 + CONV_LC] * ubuf[base:base + rows, c0:c0 + CONV_LC]
                    acc = term if acc is None else acc + term
                y = y + acc[off:off + rc]
            ybuf[r0:r0 + rc, c0:c0 + CONV_LC] = y
    z = _ln(ybuf[...], cg_ref[...], cb_ref[...])
    z = z * _sigmoid(z)
    mix = _bdot(z, wout_ref[...]) + bout_ref[...]
    o_ref[...] = _ln(ALPHA * x + mix, lg_ref[...], lb_ref[...])


def _conv_mixer(x, state, w, lnp, *, nb, t, tt, row_off, n_total, prev):
    nt = t // tt
    win, bin_, wdw, bdw, cg, cb, wout, bout = w
    ins = [('tok' if x.shape[0] == n_total else 'own', x), ('bat', state), ('const', win), ('const', bin_),
           ('const', wdw), ('const', bdw),
           ('const', cg), ('const', cb), ('const', wout), ('const', bout), ('const', lnp[0]), ('const', lnp[1])]
    outs = [('tok', (D,), F32), ('bat', (HALO, D), F32)]
    scratch = [pltpu.VMEM((HALO + tt, D), F32), pltpu.VMEM((tt, D), F32)]
    return _seq_call(functools.partial(_conv_kernel, tt=tt, nt=nt), name=f"conv_t{t}", nb=nb, nt=nt, tt=tt,
                     row_off=row_off,
                     n_total=n_total, ins=ins, outs=outs, scratch=scratch, prev=prev)


def _head_sum(y, hs_ref, hst_ref):
    s = _split_dot(y, hs_ref[...])
    return _split_dot(s, hst_ref[...])


def _rw1_kernel(x_ref, sh_ref, mu_ref, wr_ref, wk_ref, wv_ref, w0_ref, w1_ref, w2_ref, a0_ref, a1_ref,
                a2_ref, g1_ref, g2_ref, kk_ref, ka_ref, hs_ref, hst_ref,
                r_ref, w_ref, k_ref, v_ref, an_ref, b_ref, g_ref, xbuf, *, tt):
    j = pl.program_id(1)

    @pl.when(j == 0)
    def _():
        xbuf[7:8, :] = sh_ref[0]

    @pl.when(j > 0)
    def _():
        xbuf[7:8, :] = xbuf[7 + tt:8 + tt, :]

    x = x_ref[...]
    xbuf[8:8 + tt, :] = x
    xx = xbuf[7:7 + tt, :] - x
    mu = mu_ref[...]
    xr = x + xx * mu[0:1]
    xw = x + xx * mu[1:2]
    xk = x + xx * mu[2:3]
    xv = x + xx * mu[3:4]
    xa = x + xx * mu[4:5]
    xg = x + xx * mu[5:6]
    r = _bdot(xr, wr_ref[...])
    k = _bdot(xk, wk_ref[...])
    v = _bdot(xv, wv_ref[...])
    lw = w0_ref[...] + _bdot(jnp.tanh(_bdot(xw, w1_ref[...])), w2_ref[...])
    z = -lw
    log_w = -(jnp.maximum(z, 0.0) + jnp.log(1.0 + jnp.exp(-jnp.abs(z)))) - 0.5
    a = _sigmoid(a0_ref[...] + _bdot(_bdot(xa, a1_ref[...]), a2_ref[...]))
    g = _bdot(_sigmoid(_bdot(xg, g1_ref[...])), g2_ref[...])
    kk = k * kk_ref[...]
    ss = _head_sum(kk * kk, hs_ref, hst_ref)
    kk = kk * lax.rsqrt(jnp.maximum(ss, 1e-24))
    r_ref[...] = r
    w_ref[...] = jnp.exp(-jnp.exp(log_w))
    k_ref[...] = k * (1.0 + (a - 1.0) * ka_ref[...])
    v_ref[...] = v
    an_ref[...] = -kk
    b_ref[...] = kk * a
    g_ref[...] = g


def _rw1(x, shift, w, hs, hst, *, nb, t, tt, row_off):
    nt = t // tt
    prev = []
    ins = [('tok', x), ('bat', shift)] + [('const', a) for a in w] + [('const', hs), ('const', hst)]
    outs = [('own', (D,), F32)] * (N_SCAN_IN + 1)
    scratch = [pltpu.VMEM((8 + tt, D), F32)]
    return _seq_call(functools.partial(_rw1_kernel, tt=tt), name=f"rwkv_proj_t{t}", nb=nb, nt=nt, tt=tt,
                     row_off=row_off,
                     n_total=x.shape[0], ins=ins, outs=outs, scratch=scratch, prev=prev)


def _scan_kernel(r_ref, w_ref, k_ref, v_ref, an_ref, b_ref, nxt_ref, s0_ref, o_ref, st_ref, S, sa_buf,
                 *, tc, nc):
    c = pl.program_id(1)

    @pl.when(c == 0)
    def _():
        S[...] = s0_ref[...]

        def init(kk, acc):
            return acc + S[kk] * an_ref[0, pl.ds(kk, 1), :]

        sa_buf[...] = lax.fori_loop(0, HD, init, jnp.zeros((HD, LANES), F32))

    def step(t, sa, a_next):
        vt = v_ref[t]

        def kbody(kk, acc):
            o_acc, sa_acc = acc
            sk = (S[kk] * w_ref[t, pl.ds(kk, 1), :] + sa * b_ref[t, pl.ds(kk, 1), :]
                  + vt * k_ref[t, pl.ds(kk, 1), :])
            S[kk] = sk
            return (o_acc + sk * r_ref[t, pl.ds(kk, 1), :], sa_acc + sk * a_next(kk))

        zero = jnp.zeros((HD, LANES), F32)
        o_acc, sa_next = lax.fori_loop(0, HD, kbody, (zero, zero), unroll=8)
        o_ref[t] = o_acc
        return sa_next

    sa = lax.fori_loop(0, tc - 1, lambda t, sa: step(t, sa, lambda kk: an_ref[t + 1, pl.ds(kk, 1), :]),
                       sa_buf[...])
    sa_buf[...] = step(tc - 1, sa, lambda kk: nxt_ref[0, pl.ds(kk, 1), :])

    @pl.when(c == nc - 1)
    def _():
        st_ref[...] = S[...]


def _scan(r, w, k, v, an, b, s0, *, tc):
    t, _, lanes = r.shape
    ng, nc = lanes // LANES, t // tc
    seq = pl.BlockSpec((tc, HD, LANES), lambda g, c: (c, 0, g))
    nxt = pl.BlockSpec((1, HD, LANES), lambda g, c: (jnp.minimum((c + 1) * tc, t - 1), 0, g))
    return pl.pallas_call(
        functools.partial(_scan_kernel, tc=tc, nc=nc), grid=(ng, nc),
        in_specs=[seq] * 6 + [nxt, pl.BlockSpec((HD, HD, LANES), lambda g, c: (0, 0, g))],
        out_specs=[seq, pl.BlockSpec((HD, HD, LANES), lambda g, c: (0, 0, g))],
        out_shape=[jax.ShapeDtypeStruct((t, HD, lanes), F32), jax.ShapeDtypeStruct((HD, HD, lanes), F32)],
        scratch_shapes=[pltpu.VMEM((HD, HD, LANES), F32), pltpu.VMEM((HD, LANES), F32)],
        name=f"wkv_scan_t{t}", compiler_params=_cp(("arbitrary", "arbitrary")))(r, w, k, v, an, b, an, s0)


N_SCAN_IN = 6


def _to_scan_kernel(*refs, tc):
    low = lax.broadcasted_iota(I32, (HD, LANES), 1) < HD

    def pair(i, carry):
        t = i * 2
        for src, dst in zip(refs[:N_SCAN_IN], refs[N_SCAN_IN:]):
            y0, y1 = src[:, t, :], src[:, t + 1, :]
            m = jnp.concatenate([y[:, LANES * p:LANES * (p + 1)] for y in (y0, y1) for p in range(8)], axis=0).T
            top, bot = m[:HD], m[HD:]
            dst[t] = jnp.where(low, top, pltpu.roll(bot, HD, 1))
            dst[t + 1] = jnp.where(low, pltpu.roll(top, HD, 1), bot)
        return carry

    lax.fori_loop(0, tc // 2, pair, 0, unroll=2)


def _to_scan_call(arrs, *, tc):
    nb, t, _ = arrs[0].shape
    ng, nc = nb // 8, t // tc
    return pl.pallas_call(
        functools.partial(_to_scan_kernel, tc=tc), grid=(ng, nc),
        in_specs=[pl.BlockSpec((8, tc, D), lambda g, c: (g, c, 0))] * N_SCAN_IN,
        out_specs=[pl.BlockSpec((tc, HD, LANES), lambda g, c: (c, 0, g))] * N_SCAN_IN,
        out_shape=[jax.ShapeDtypeStruct((t, HD, ng * LANES), F32)] * N_SCAN_IN, name=f"wkv_relayout_t{t}",
        compiler_params=_cp(("arbitrary", "arbitrary")))(*arrs)


def _post_kernel(o_ref, r_ref, k_ref, v_ref, xg_ref, xb_ref, rk_ref, y_ref, *, tc):
    low = lax.broadcasted_iota(I32, (HD, LANES), 1) < HD

    def norm(t):
        o = o_ref[t]
        d = o - jnp.mean(o, axis=0, keepdims=True)
        vo = jnp.mean(d * d, axis=0, keepdims=True)
        bonus = jnp.sum(r_ref[t] * k_ref[t] * rk_ref[...], axis=0, keepdims=True) * v_ref[t]
        return d * lax.rsqrt(vo + LNX_EPS) * xg_ref[...] + xb_ref[...] + bonus

    def pair(i, carry):
        t = i * 2
        z0, z1 = norm(t), norm(t + 1)
        top = jnp.where(low, z0, pltpu.roll(z1, HD, 1))
        bot = jnp.where(low, pltpu.roll(z0, HD, 1), z1)
        m = jnp.concatenate([top, bot], axis=0).T
        y_ref[t] = jnp.concatenate([m[8 * p:8 * p + 8] for p in range(8)], axis=1)
        y_ref[t + 1] = jnp.concatenate([m[HD + 8 * p:HD + 8 * p + 8] for p in range(8)], axis=1)
        return carry

    lax.fori_loop(0, tc // 2, pair, 0, unroll=4)


def _post(o, r, k, v, xg, xb, rk, *, tc):
    t, _, lanes = o.shape
    ng, nc = lanes // LANES, t // tc
    seq = pl.BlockSpec((tc, HD, LANES), lambda g, c: (c, 0, g))
    const = pl.BlockSpec((HD, LANES), lambda g, c: (0, 0))
    return pl.pallas_call(
        functools.partial(_post_kernel, tc=tc), grid=(ng, nc),
        in_specs=[seq] * 4 + [const] * 3,
        out_specs=pl.BlockSpec((tc, 8, D), lambda g, c: (c, g, 0)),
        out_shape=jax.ShapeDtypeStruct((t, ng * 8, D), F32), name=f"wkv_post_t{t}",
        compiler_params=_cp(("arbitrary", "arbitrary")))(o, r, k, v, xg, xb, rk)


def _rw3_kernel(y_ref, g_ref, x_ref, wo_ref, lg_ref, lb_ref, out_ref):
    mix = _bdot(y_ref[...] * g_ref[...], wo_ref[...])
    out_ref[...] = _ln(ALPHA * x_ref[...] + mix, lg_ref[...], lb_ref[...])


def _head_major_spec(tt):
    return pl.BlockSpec((HEADS, tt, HD), lambda i: (0, i, 0))


def _tok_call(body, name, n, tt, toks, consts, outs):
    in_specs = [_head_major_spec(tt) if a.ndim == 3 else pl.BlockSpec((tt, a.shape[1]), lambda i: (i, 0))
                for a in toks]
    in_specs += [pl.BlockSpec(a.shape, lambda i, _n=a.ndim: (0,) * _n) for a in consts]
    out_specs = [_head_major_spec(tt) if w == 'heads' else pl.BlockSpec((tt, w), lambda i: (i, 0))
                 for w, _ in outs]
    out_shape = [jax.ShapeDtypeStruct((HEADS, n, HD) if w == 'heads' else (n, w), dt) for w, dt in outs]
    return pl.pallas_call(
        body, grid=(n // tt,), in_specs=in_specs, out_specs=out_specs, out_shape=out_shape, name=name,
        compiler_params=_cp(("arbitrary",)))(*toks, *consts)


def _qkv_kernel(x_ref, w_ref, b_ref, q_ref, k_ref, v_ref):
    h = _bdot(x_ref[...], w_ref[...]) + b_ref[...]
    for hd in range(HEADS):
        q_ref[hd] = h[:, hd * HD:(hd + 1) * HD].astype(BF16)
    k_ref[...] = h[:, HEADS * HD:HEADS * HD + KVH * HD]
    v_ref[...] = h[:, HEADS * HD + KVH * HD:]


def _attn_kernel(q_ref, kp_ref, vp_ref, bias_ref, sink_ref, prev_ref, o_ref, *, nq, nsub, kb, mask_lo):
    del prev_ref
    c = pl.program_id(1)
    start = pl.multiple_of(c * (nsub * nq), 8)
    span = kb + (nsub - 1) * nq
    kall = kp_ref[0, pl.ds(start, span), :].astype(BF16)
    vall = vp_ref[0, pl.ds(start, span), :].astype(BF16)
    for s in range(nsub):
        rows = slice(s * nq, (s + 1) * nq)
        valid = (start + s * nq + lax.broadcasted_iota(I32, (1, kb), 1)) >= mask_lo
        for g in range(KVH):
            hs = slice(g * GROUP, (g + 1) * GROUP)
            qg = q_ref[hs, rows, :].reshape(GROUP * nq, HD)
            kh = kall[s * nq:s * nq + kb, g * HD:(g + 1) * HD]
            vh = vall[s * nq:s * nq + kb, g * HD:(g + 1) * HD]
            logits = lax.dot_general(qg, kh, (((1,), (1,)), ((), ())), preferred_element_type=F32) * HD ** -0.5
            logits = jnp.where(valid, logits + bias_ref[hs].reshape(GROUP * nq, kb), -1e30)
            sink = sink_ref[hs].reshape(GROUP * nq, 1)
            m = jnp.maximum(jnp.max(logits, axis=-1, keepdims=True), sink)
            p = jnp.exp(logits - m)
            p = p / (jnp.sum(p, axis=-1, keepdims=True) + jnp.exp(sink - m))
            og = jnp.dot(p.astype(BF16), vh, preferred_element_type=F32)
            o_ref[hs, rows, :] = og.reshape(GROUP, nq, HD).astype(BF16)


def _attn(q, kp, vp, bias, sinks, *, nb, nt, nq, nsub, kb, mask_lo, row_off, prev):
    off = row_off // (nq * nsub)
    n_total = q.shape[1]
    heads = pl.BlockSpec((HEADS, nq * nsub, HD), lambda b, c: (0, off + b * nt + c, 0))
    const = lambda a: pl.BlockSpec(a.shape, lambda b, c, _n=a.ndim: (0,) * _n)
    bat = lambda a: pl.BlockSpec((1,) + a.shape[1:], lambda b, c: (b, 0, 0))
    sink_tab = jnp.broadcast_to(sinks.reshape(HEADS, 1, 1), (HEADS, nq, 1))
    if prev is None:
        prev = jnp.zeros((HEADS, n_total, HD), BF16)
    return pl.pallas_call(
        functools.partial(_attn_kernel, nq=nq, nsub=nsub, kb=kb, mask_lo=mask_lo), grid=(nb, nt),
        in_specs=[heads, bat(kp), bat(vp), const(bias), const(sink_tab), pl.BlockSpec(memory_space=pl.ANY)],
        out_specs=heads, out_shape=jax.ShapeDtypeStruct((HEADS, n_total, HD), BF16),
        input_output_aliases={5: 0}, name=f"attn_q{nq}",
        compiler_params=_cp(("arbitrary", "arbitrary")))(q, kp, vp, bias, sink_tab, prev)


def _oproj_kernel(o_ref, x_ref, wo_ref, bo_ref, lg_ref, lb_ref, out_ref):
    acc = jnp.dot(o_ref[0], wo_ref[0:HD, :], preferred_element_type=F32)
    for hd in range(1, HEADS):
        acc = acc + jnp.dot(o_ref[hd], wo_ref[hd * HD:(hd + 1) * HD, :], preferred_element_type=F32)
    out_ref[...] = _ln(ALPHA * x_ref[...] + acc + bo_ref[...], lg_ref[...], lb_ref[...])


def _t5_bucket(rel):
    half = N_BUCKETS // 2
    max_exact = half // 2
    ret = jnp.where(rel > 0, half, 0)
    n = jnp.abs(rel)
    nf = jnp.maximum(n, 1).astype(F32)
    large = max_exact + (jnp.log(nf / max_exact) / math.log(MAX_DISTANCE / max_exact)
                         * (half - max_exact)).astype(I32)
    large = jnp.minimum(large, half - 1)
    return ret + jnp.where(n < max_exact, n, large)


def _t5_bias(rel_bias, n_q, n_k):
    rel = jnp.arange(n_k)[None, :] - WINDOW - jnp.arange(n_q)[:, None]
    onehot = (_t5_bucket(rel)[..., None] == jnp.arange(N_BUCKETS)).astype(F32)
    return jnp.einsum('qkn,nh->hqk', onehot, rel_bias, precision=lax.Precision.HIGHEST)


def _route_kernel(x_ref, wr_ref, br_ref, idx_ref, gate_ref, rank_ref, cnt_ref, carry, *, tt):
    i = pl.program_id(0)

    @pl.when(i == 0)
    def _():
        carry[...] = jnp.zeros_like(carry)

    logits = lax.dot_general(wr_ref[...].astype(BF16), x_ref[...].astype(BF16), (((1,), (1,)), ((), ())),
                             preferred_element_type=F32) + br_ref[...]
    sub = lax.broadcasted_iota(I32, (N_EXP, tt), 0)
    out_row = lax.broadcasted_iota(I32, (8, tt), 0)
    vals, sels = [], []
    idx_out = jnp.zeros((8, tt), I32)
    work = logits
    for k in range(TOP_K):
        m = jnp.max(work, axis=0, keepdims=True)
        ik = jnp.min(jnp.where(work == m, sub, N_EXP), axis=0, keepdims=True)
        sel = sub == ik
        vals.append(m)
        sels.append(sel)
        idx_out = jnp.where(out_row == k, ik, idx_out)
        work = jnp.where(sel, -jnp.inf, work)
    es = [jnp.exp(v - vals[0]) for v in vals]
    den = es[0] + es[1] + es[2] + es[3]
    gate_out = jnp.zeros((8, tt), F32)
    for k in range(TOP_K):
        gate_out = jnp.where(out_row == k, es[k] / den, gate_out)
    onehot = jnp.zeros((N_EXP, tt), F32)
    for sel in sels:
        onehot = onehot + sel.astype(F32)
    before = (lax.broadcasted_iota(I32, (tt, tt), 0) < lax.broadcasted_iota(I32, (tt, tt), 1)).astype(BF16)
    base = carry[...] + jnp.dot(onehot.astype(BF16), before, preferred_element_type=F32)
    rank_out = jnp.zeros((8, tt), I32)
    for k in range(TOP_K):
        rk = jnp.sum(jnp.where(sels[k], base, 0.0), axis=0, keepdims=True)
        rank_out = jnp.where(out_row == k, rk.astype(I32), rank_out)
    carry[...] = carry[...] + jnp.sum(onehot, axis=1, keepdims=True)
    idx_ref[...] = idx_out
    gate_ref[...] = gate_out
    rank_ref[...] = rank_out
    cnt_ref[...] = carry[...]


def _route(x1, wr_t, br_col, *, tt):
    n = x1.shape[0]
    rows = pl.BlockSpec((8, tt), lambda i: (0, i))
    const = lambda a: pl.BlockSpec(a.shape, lambda i, _n=a.ndim: (0,) * _n)
    return pl.pallas_call(
        functools.partial(_route_kernel, tt=tt), grid=(n // tt,),
        in_specs=[pl.BlockSpec((tt, D), lambda i: (i, 0)), const(wr_t), const(br_col)],
        out_specs=[rows, rows, rows, pl.BlockSpec((N_EXP, 1), lambda i: (0, 0))],
        out_shape=[jax.ShapeDtypeStruct((8, n), I32), jax.ShapeDtypeStruct((8, n), F32),
                   jax.ShapeDtypeStruct((8, n), I32), jax.ShapeDtypeStruct((N_EXP, 1), F32)],
        scratch_shapes=[pltpu.VMEM((N_EXP, 1), F32)], name="moe_route",
        compiler_params=_cp(("arbitrary",)))(x1, wr_t, br_col)


SUB = D // LANES
assert SUB == 8


def _rows_to_tiles(dst_ref, x, n):
    for s in range(SUB):
        dst_ref[pl.ds(s, n, stride=SUB), :] = x[:, LANES * s:LANES * (s + 1)]


def _tiles_to_rows(src_ref, n):
    return jnp.concatenate([src_ref[pl.ds(s, n, stride=SUB), :] for s in range(SUB)], axis=1)


def _tile_copy(src, s, dst, d, sem):
    return pltpu.make_async_copy(src.at[pl.ds(pl.multiple_of(s * SUB, SUB), SUB), :],
                                 dst.at[pl.ds(pl.multiple_of(d * SUB, SUB), SUB), :], sem)


ROW_UNROLL = 16


def _drain_rows(src, dst, sem, tt):
    def drain(r, carry):
        for k in range(TOP_K):
            _tile_copy(src, 0, dst, 0, sem).wait()
        return carry

    lax.fori_loop(0, tt, drain, 0, unroll=ROW_UNROLL)


def _disp_kernel(dest_ref, x_ref, xs_in, xs_ref, xt, sem, *, tt, nt):
    del xs_in
    i = pl.program_id(0)
    slot = lax.rem(i, 2)
    stage = xt.at[slot]
    _rows_to_tiles(stage, x_ref[...], tt)

    def issue(r, carry):
        for k in range(TOP_K):
            _tile_copy(stage, r, xs_ref, dest_ref[r * TOP_K + k], sem.at[slot]).start(priority=k % 2)
        return carry

    lax.fori_loop(0, tt, issue, 0, unroll=ROW_UNROLL)

    @pl.when(i > 0)
    def _():
        _drain_rows(xt.at[1 - slot], xs_ref, sem.at[1 - slot], tt)

    @pl.when(i == nt - 1)
    def _():
        _drain_rows(stage, xs_ref, sem.at[slot], tt)


def _dispatch(dest, x1, xs_prev, *, tt):
    n = x1.shape[0]
    smem_tok = pl.BlockSpec((tt * TOP_K,), lambda i: (i,), memory_space=pltpu.SMEM)
    return pl.pallas_call(
        functools.partial(_disp_kernel, tt=tt, nt=n // tt), grid=(n // tt,),
        in_specs=[smem_tok, pl.BlockSpec((tt, D), lambda i: (i, 0)), pl.BlockSpec(memory_space=pl.ANY)],
        out_specs=pl.BlockSpec(memory_space=pl.ANY),
        out_shape=jax.ShapeDtypeStruct(xs_prev.shape, F32),
        scratch_shapes=[pltpu.VMEM((2, tt * SUB, LANES), F32), pltpu.SemaphoreType.DMA((2,))],
        input_output_aliases={2: 0}, name="moe_dispatch",
        compiler_params=_cp(("arbitrary",)))(dest, x1, xs_prev)


def _expert_kernel(te_ref, nv_ref, xs_ref, w1_ref, b1_ref, w2_ref, b2_ref, y_ref, w1b, w2b):
    i = pl.program_id(0)
    valid = i < nv_ref[0]
    changed = jnp.logical_or(i == 0, te_ref[i] != te_ref[jnp.maximum(i - 1, 0)])

    @pl.when(jnp.logical_and(valid, changed))
    def _():
        for r0 in range(0, D, 256):
            w1b[r0:r0 + 256, :] = w1_ref[0, 0, r0:r0 + 256, :].astype(BF16)
            w2b[r0:r0 + 256, :] = w2_ref[0, 0, r0:r0 + 256, :].astype(BF16)

    @pl.when(valid)
    def _():
        x = _tiles_to_rows(xs_ref, EXP_TILE).astype(BF16)
        h = jnp.dot(x, w1b[...], preferred_element_type=F32) + b1_ref[0, 0]
        glu = jnp.minimum(h[:, :D], SWIGLU_LIMIT)
        lin = jnp.clip(h[:, D:], -SWIGLU_LIMIT, SWIGLU_LIMIT)
        act = glu * _sigmoid(SWIGLU_ALPHA * glu) * (lin + 1.0)
        y = jnp.dot(act.astype(BF16), w2b[...], preferred_element_type=F32) + b2_ref[0, 0]
        _rows_to_tiles(y_ref, y, EXP_TILE)

    @pl.when(jnp.logical_not(valid))
    def _():
        y_ref[...] = jnp.zeros_like(y_ref)


def _experts(tile_expert, n_valid, xs, w1, b1, w2, b2, layer):
    n_tiles = xs.shape[0] // (EXP_TILE * SUB)
    grid_spec = pltpu.PrefetchScalarGridSpec(
        num_scalar_prefetch=2, grid=(n_tiles,),
        in_specs=[pl.BlockSpec((EXP_TILE * SUB, LANES), lambda i, te, nv: (i, 0)),
                  pl.BlockSpec((1, 1, D, 2 * D), lambda i, te, nv: (layer, te[i], 0, 0)),
                  pl.BlockSpec((1, 1, 1, 2 * D), lambda i, te, nv: (layer, te[i], 0, 0)),
                  pl.BlockSpec((1, 1, D, D), lambda i, te, nv: (layer, te[i], 0, 0)),
                  pl.BlockSpec((1, 1, 1, D), lambda i, te, nv: (layer, te[i], 0, 0))],
        out_specs=pl.BlockSpec((EXP_TILE * SUB, LANES), lambda i, te, nv: (i, 0)),
        scratch_shapes=[pltpu.VMEM((D, 2 * D), BF16), pltpu.VMEM((D, D), BF16)])
    return pl.pallas_call(
        _expert_kernel, grid_spec=grid_spec, out_shape=jax.ShapeDtypeStruct(xs.shape, F32), name="moe_experts",
        compiler_params=_cp(("arbitrary",)))(tile_expert, n_valid, xs, w1, b1, w2, b2)


def _comb_kernel(dest_ref, next_ref, gate_ref, x1_ref, p_ref, y_ref, lg_ref, lb_ref, wg_ref, bg_ref,
                 wp_ref, o_ref, buf, sem, *, tt, nt):
    i = pl.program_id(0)
    slot = lax.rem(i, 2)

    def gather(idx_ref, s):
        def issue(r, carry):
            for k in range(TOP_K):
                _tile_copy(y_ref, idx_ref[r * TOP_K + k], buf.at[s, k], r, sem.at[s]).start(priority=k % 2)
            return carry

        lax.fori_loop(0, tt, issue, 0, unroll=ROW_UNROLL)

    @pl.when(i == 0)
    def _():
        gather(dest_ref, slot)

    @pl.when(i + 1 < nt)
    def _():
        gather(next_ref, 1 - slot)

    _drain_rows(y_ref, buf.at[slot, 0], sem.at[slot], tt)
    gate = gate_ref[...]
    moe = gate[:, 0:1] * _tiles_to_rows(buf.at[slot, 0], tt)
    for k in range(1, TOP_K):
        moe = moe + gate[:, k:k + 1] * _tiles_to_rows(buf.at[slot, k], tt)
    x2 = _ln(ALPHA * x1_ref[...] + moe, lg_ref[...], lb_ref[...])
    gt = _sigmoid(_bdot(x2, wg_ref[...]) + bg_ref[...])
    o_ref[...] = x2 + gt * _bdot(p_ref[...], wp_ref[...])


def _combine(dest, gate, x1, p, y, lnp, wg, bg, wp, *, tt):
    n = x1.shape[0]
    smem_tok = pl.BlockSpec((tt * TOP_K,), lambda i: (i,), memory_space=pltpu.SMEM)
    tok = lambda w: pl.BlockSpec((tt, w), lambda i: (i, 0))
    const = lambda a: pl.BlockSpec(a.shape, lambda i, _n=a.ndim: (0,) * _n)
    nt = n // tt
    smem_next = pl.BlockSpec((tt * TOP_K,), lambda i: (jnp.minimum(i + 1, nt - 1),), memory_space=pltpu.SMEM)
    return pl.pallas_call(
        functools.partial(_comb_kernel, tt=tt, nt=nt), grid=(nt,),
        in_specs=[smem_tok, smem_next, tok(TOP_K), tok(D), tok(p.shape[1]), pl.BlockSpec(memory_space=pl.ANY),
                  const(lnp[0]), const(lnp[1]), const(wg), const(bg), const(wp)],
        out_specs=tok(D), out_shape=jax.ShapeDtypeStruct((n, D), F32),
        scratch_shapes=[pltpu.VMEM((2, TOP_K, tt * SUB, LANES), F32), pltpu.SemaphoreType.DMA((2,))],
        name="moe_combine_ple",
        compiler_params=_cp(("arbitrary",)))(dest, dest, gate, x1, p, y, lnp[0], lnp[1], wg, bg, wp)


def _moe_ple(x1, p, xs_buf, wr_t, br_col, w1, b1, w2, b2, layer, lnp, wg, bg, wp, *, tt):
    n = x1.shape[0]
    idx, gate, rank, counts = _route(x1, wr_t, br_col, tt=512 if n % 512 == 0 else tt)
    counts = counts[:, 0].astype(I32)
    padded = (counts + EXP_TILE - 1) // EXP_TILE * EXP_TILE
    pad_end = jnp.cumsum(padded)
    pad_start = pad_end - padded
    n_tiles = xs_buf.shape[0] // (EXP_TILE * SUB)
    n_valid = (pad_end[-1] // EXP_TILE).astype(I32)
    tiles = jnp.minimum(jnp.arange(n_tiles, dtype=I32), n_valid - 1) * EXP_TILE
    tile_expert = jnp.minimum(jnp.sum((tiles[:, None] >= pad_end[None, :]).astype(I32), axis=1), N_EXP - 1)
    experts = jnp.arange(N_EXP, dtype=I32)
    start_of = jnp.sum(jnp.where(idx[:TOP_K, :, None] == experts, pad_start, 0), axis=-1)
    dest = (start_of + rank[:TOP_K]).T.reshape(-1)
    xs = _dispatch(dest, x1, xs_buf, tt=tt)
    y = _experts(tile_expert, n_valid.reshape(1), xs, w1, b1, w2, b2, layer)
    return _combine(dest, gate[:TOP_K].T, x1, p, y, lnp, wg, bg, wp, tt=tt), xs


def _state_to_scan(s, nb):
    s = s.reshape(nb // 8, 8, HEADS // 2, 2, HD, HD).transpose(5, 4, 0, 3, 2, 1)
    return s.reshape(HD, HD, nb * HEADS)


def _state_from_scan(s, nb):
    s = s.reshape(HD, HD, nb // 8, 2, HEADS // 2, 8).transpose(2, 5, 4, 3, 1, 0)
    return s.reshape(nb, HEADS, HD, HD)


def _head_vec_to_scan(a):
    a = a.reshape(HEADS // 2, 2, HD).transpose(2, 1, 0)
    return jnp.broadcast_to(a[..., None], (HD, 2, HEADS // 2, 8)).reshape(HD, LANES)


def _row2(a):
    return a.reshape(1, -1)


def _tile(n):
    for tt in (512, 256, 128, 64, 32, 16, 8):
        if n % tt == 0:
            return tt
    raise ValueError(n)


def kernel(x_prompt, x_sample, p_prompt, p_sample, cache_conv, state_rwkv_shift, state_rwkv_wkv, cache_swa_k, cache_swa_v, conv_w_in, conv_b_in, conv_w_dw, conv_b_dw, conv_ln_g, conv_ln_b, conv_w_out, conv_b_out, rwkv_mu, rwkv_w_rkv, rwkv_w0, rwkv_w1, rwkv_w2, rwkv_a0, rwkv_a1, rwkv_a2, rwkv_g1, rwkv_g2, rwkv_k_k, rwkv_k_a, rwkv_r_k, rwkv_lnx_g, rwkv_lnx_b, rwkv_w_o, attn_w_qkv, attn_b_qkv, attn_sinks, attn_w_o, attn_b_o, rel_bias, ln_g, ln_b, moe_w_router, moe_b_router, moe_w1, moe_b1, moe_w2, moe_b2, ple_w_proj, ple_w_gate, ple_b_gate):
    bp, tp, _ = x_prompt.shape
    bs, ts, _ = x_sample.shape
    n_p, n_s = bp * tp, bs * ts
    n = n_p + n_s
    assert tp % 128 == 0 and n_p % ts == 0 and ts % 8 == 0 and HALO <= ts <= CHUNK
    tt_tok = _tile(n)
    tt_p = 256 if tp % 256 == 0 else 128
    x = None
    p_all = jnp.concatenate([p_prompt.reshape(DEPTH, n_p, -1), p_sample.reshape(DEPTH, n_s, -1)], axis=1)
    n_rows = (-(-n * TOP_K // EXP_TILE) + N_EXP) * EXP_TILE
    xs_buf = jnp.zeros((n_rows * SUB, LANES), F32)
    head_sel = (jnp.arange(D)[:, None] // HD == jnp.arange(LANES)[None, :]).astype(BF16)
    head_sel_t = head_sel.T
    conv_p, conv_s, shift_p, shift_s, wkv_p, wkv_s = [], [], [], [], [], []
    swa_kp, swa_vp, swa_ks, swa_vs = [], [], [], []
    for i in range(DEPTH):
        kind, j = i % 3, i // 3
        lnp = (_row2(ln_g[i, 0]), _row2(ln_b[i, 0]))
        if kind == 0:
            cw = (conv_w_in[j].astype(BF16), _row2(conv_b_in[j]), conv_w_dw[j].astype(BF16).astype(F32),
                  _row2(conv_b_dw[j]),
                  _row2(conv_ln_g[j]), _row2(conv_ln_b[j]), conv_w_out[j].astype(BF16), _row2(conv_b_out[j]))
            st_p = jnp.zeros((bp, HALO, D), F32)
            st_s = jnp.pad(cache_conv[j], ((0, 0), (HALO - (CONV_W - 1), 0), (0, 0)))
            xin_p, xin_s = (x_prompt.reshape(n_p, D), x_sample.reshape(n_s, D)) if i == 0 else (x, x)
            x1, so_p = _conv_mixer(xin_p, st_p, cw, lnp, nb=bp, t=tp, tt=tt_p, row_off=0, n_total=n, prev=None)
            x1, so_s = _conv_mixer(xin_s, st_s, cw, lnp, nb=bs, t=ts, tt=ts, row_off=n_p, n_total=n, prev=[x1])
            conv_p.append(so_p[:, HALO - (CONV_W - 1):])
            conv_s.append(so_s[:, HALO - (CONV_W - 1):])
        elif kind == 1:
            rw = (rwkv_mu[j], rwkv_w_rkv[j, 0].astype(BF16), rwkv_w_rkv[j, 1].astype(BF16),
                  rwkv_w_rkv[j, 2].astype(BF16), _row2(rwkv_w0[j]), rwkv_w1[j].astype(BF16),
                  rwkv_w2[j].astype(BF16), _row2(rwkv_a0[j]), rwkv_a1[j].astype(BF16), rwkv_a2[j].astype(BF16),
                  rwkv_g1[j].astype(BF16), rwkv_g2[j].astype(BF16), _row2(rwkv_k_k[j]), _row2(rwkv_k_a[j]))
            sh_p = jnp.zeros((bp, 1, D), F32)
            sh_s = state_rwkv_shift[j].reshape(bs, 1, D)
            post_c = [_head_vec_to_scan(a) for a in (rwkv_lnx_g[j], rwkv_lnx_b[j], rwkv_r_k[j].reshape(-1))]
            out_c = [('const', rwkv_w_o[j].astype(BF16)), ('const', lnp[0]), ('const', lnp[1])]
            x1, states = None, []
            for (lo, nb_, t_, tt_, sh, s0, tc) in ((0, bp, tp, tt_p, sh_p, None, 64),
                                                   (n_p, bs, ts, ts, sh_s, state_rwkv_wkv[j], ts)):
                r, w, k, v, an, b, g = _rw1(x, sh, rw, head_sel, head_sel_t, nb=nb_, t=t_, tt=tt_, row_off=lo)
                rs, ws, ks, vs, ans, bs_ = _to_scan_call([a.reshape(nb_, t_, D) for a in (r, w, k, v, an, b)],
                                                         tc=min(tc, 32))
                s0l = jnp.zeros((HD, HD, nb_ * HEADS), F32) if s0 is None else _state_to_scan(s0, nb_)
                o_l, s_l = _scan(rs, ws, ks, vs, ans, bs_, s0l, tc=tc)
                y = _post(o_l, rs, ks, vs, *post_c, tc=tc).reshape(t_, nb_ * D)
                x1, = _seq_call(_rw3_kernel, name=f"rwkv_out_t{t_}", nb=nb_, nt=t_ // tt_, tt=tt_, row_off=lo,
                                n_total=n, ins=[('tm', y), ('own', g), ('tok', x)] + out_c,
                                outs=[('tok', (D,), F32)], scratch=[], prev=None if x1 is None else [x1])
                states.append(_state_from_scan(s_l, nb_))
            shift_p.append(x[tp - 1:n_p:tp])
            shift_s.append(x[n_p + ts - 1::ts])
            wkv_p.append(states[0])
            wkv_s.append(states[1])
        else:
            q, kx, vx = _tok_call(_qkv_kernel, "attn_qkv", n, tt_tok, [x],
                                  [attn_w_qkv[j].astype(BF16), _row2(attn_b_qkv[j])],
                                  [('heads', BF16), (KVH * HD, F32), (KVH * HD, F32)])
            k_p = kx[:n_p].reshape(bp, tp, KVH * HD)
            v_p = vx[:n_p].reshape(bp, tp, KVH * HD)
            zpad = jnp.zeros((bp, WINDOW, KVH * HD), F32)
            nc = tp // CHUNK
            band = WINDOW + CHUNK
            o = _attn(q, jnp.concatenate([zpad, k_p], axis=1), jnp.concatenate([zpad, v_p], axis=1),
                      _t5_bias(rel_bias, CHUNK, band), attn_sinks[j], nb=bp, nt=nc, nq=CHUNK, nsub=1, kb=band,
                      mask_lo=WINDOW, row_off=0, prev=None)
            k_all = jnp.concatenate([cache_swa_k[j].reshape(bs, WINDOW, KVH * HD),
                                     kx[n_p:].reshape(bs, ts, KVH * HD)], axis=1)
            v_all = jnp.concatenate([cache_swa_v[j].reshape(bs, WINDOW, KVH * HD),
                                     vx[n_p:].reshape(bs, ts, KVH * HD)], axis=1)
            o = _attn(q, k_all, v_all, _t5_bias(rel_bias, ts, WINDOW + ts), attn_sinks[j],
                      nb=bs, nt=1, nq=ts, nsub=1, kb=WINDOW + ts, mask_lo=0, row_off=n_p, prev=o)
            x1, = _tok_call(_oproj_kernel, "attn_out", n, tt_tok, [o, x],
                            [attn_w_o[j].astype(BF16), _row2(attn_b_o[j]), lnp[0], lnp[1]], [(D, F32)])
            swa_kp.append(k_p[:, -WINDOW:].reshape(bp, WINDOW, KVH, HD))
            swa_vp.append(v_p[:, -WINDOW:].reshape(bp, WINDOW, KVH, HD))
            swa_ks.append(k_all[:, -WINDOW:].reshape(bs, WINDOW, KVH, HD))
            swa_vs.append(v_all[:, -WINDOW:].reshape(bs, WINDOW, KVH, HD))
        x, xs_buf = _moe_ple(x1, p_all[i], xs_buf, moe_w_router[i].T, moe_b_router[i].reshape(N_EXP, 1), moe_w1,
                             moe_b1.reshape(DEPTH, N_EXP, 1, 2 * D), moe_w2, moe_b2.reshape(DEPTH, N_EXP, 1, D), i,
                             (_row2(ln_g[i, 1]), _row2(ln_b[i, 1])),
                             ple_w_gate[i].astype(BF16), _row2(ple_b_gate[i]), ple_w_proj[i].astype(BF16), tt=tt_tok)
    return (x[:n_p].reshape(bp, tp, D), x[n_p:].reshape(bs, ts, D), jnp.stack(conv_p), jnp.stack(conv_s),
            jnp.stack(shift_p), jnp.stack(shift_s), jnp.stack(wkv_p), jnp.stack(wkv_s), jnp.stack(swa_kp),
            jnp.stack(swa_vp), jnp.stack(swa_ks), jnp.stack(swa_vs))
```

```python
import functools
import math

import jax
import jax.numpy as jnp
from jax import lax
from jax.experimental import pallas as pl
from jax.experimental.pallas import tpu as pltpu

F32 = jnp.float32
BF16 = jnp.bfloat16
I32 = jnp.int32

D = 1024
DEPTH = 4
CONV_W = 31
HALO = 32
HEADS = 16
HD = 64
KVH = 2
GROUP = HEADS // KVH
WINDOW = 128
CHUNK = 64
N_BUCKETS = 32
MAX_DISTANCE = 128
N_EXP = 32
TOP_K = 4
EXP_TILE = 512
CAST_ROWS = 256
LANES = 128
SUB = 8
HEAD_PAIRS = HEADS // 2
GROUP_ROWS = LANES // HEADS
SCAN_CHUNK = 64
RELAYOUT_CHUNK = 32
LNX_EPS = 64e-5
LN_EPS = 1e-5
ALPHA = (2 * DEPTH) ** 0.25
SWIGLU_ALPHA = 1.702
SWIGLU_LIMIT = 7.0
VMEM_LIMIT = 56 * 1024 * 1024


def _cp(sem):
    return pltpu.CompilerParams(dimension_semantics=sem, vmem_limit_bytes=VMEM_LIMIT)


def _bdot(a, b):
    return jnp.dot(a.astype(BF16), b.astype(BF16), preferred_element_type=F32)


def _split(a):
    hi = a.astype(BF16)
    lo = (a - hi.astype(F32)).astype(BF16)
    return hi, lo


def _split_dot(a, b_exact):
    hi, lo = _split(a)
    return (jnp.dot(hi, b_exact, preferred_element_type=F32)
            + jnp.dot(lo, b_exact, preferred_element_type=F32))


def _ln(x, g, b, eps=LN_EPS):
    mu = jnp.mean(x, axis=-1, keepdims=True)
    xc = x - mu
    var = jnp.mean(xc * xc, axis=-1, keepdims=True)
    return xc * lax.rsqrt(var + eps) * g + b


def _sigmoid(x):
    return 1.0 / (1.0 + jnp.exp(-x))


def _seq_call(body, *, name, nb, nt, tt, row_off, n_total, ins, outs, scratch, prev=None):
    off = row_off // tt
    in_specs, args = [], []
    for kind, a in ins:
        if kind == 'tok':
            in_specs.append(pl.BlockSpec((tt, a.shape[1]), lambda b, j: (off + b * nt + j, 0)))
        elif kind == 'own':
            in_specs.append(pl.BlockSpec((tt, a.shape[1]), lambda b, j: (b * nt + j, 0)))
        elif kind == 'tm':
            in_specs.append(pl.BlockSpec((tt, a.shape[1] // nb), lambda b, j: (j, b)))
        elif kind == 'bat':
            in_specs.append(pl.BlockSpec((1,) + a.shape[1:], lambda b, j: (b, 0, 0)))
        else:
            in_specs.append(pl.BlockSpec(a.shape, lambda b, j, _n=a.ndim: (0,) * _n))
        args.append(a)
    out_specs, out_shapes = [], []
    for kind, tail, dt in outs:
        if kind == 'tok':
            out_specs.append(pl.BlockSpec((tt, tail[0]), lambda b, j: (off + b * nt + j, 0)))
            out_shapes.append(jax.ShapeDtypeStruct((n_total, tail[0]), dt))
        elif kind == 'own':
            out_specs.append(pl.BlockSpec((tt, tail[0]), lambda b, j: (b * nt + j, 0)))
            out_shapes.append(jax.ShapeDtypeStruct((nb * nt * tt, tail[0]), dt))
        elif kind == 'tm':
            out_specs.append(pl.BlockSpec((tt, tail[0]), lambda b, j: (j, b)))
            out_shapes.append(jax.ShapeDtypeStruct((nt * tt, nb * tail[0]), dt))
        else:
            out_specs.append(pl.BlockSpec((1,) + tuple(tail), lambda b, j: (b, 0, 0)))
            out_shapes.append(jax.ShapeDtypeStruct((nb,) + tuple(tail), dt))
    aliases = {}
    n_prev = 0
    tok_out = [i for i, o in enumerate(outs) if o[0] == 'tok']
    if prev is None:
        prev = [jnp.zeros((n_total, outs[i][1][0]), outs[i][2]) for i in tok_out]
    for p, oi in zip(prev, tok_out):
        aliases[len(args)] = oi
        in_specs.append(pl.BlockSpec(memory_space=pl.ANY))
        args.append(p)
        n_prev += 1
    n_in = len(ins)

    def wrapped(*refs):
        body(*refs[:n_in], *refs[n_in + n_prev:])

    return pl.pallas_call(
        wrapped, grid=(nb, nt), in_specs=in_specs, out_specs=out_specs, out_shape=out_shapes,
        scratch_shapes=scratch, input_output_aliases=aliases, name=name,
        compiler_params=_cp(("arbitrary", "arbitrary")))(*args)


CONV_RC = 64
CONV_LC = 256


def _conv_kernel(x_ref, st_ref, win_ref, bin_ref, wdw_ref, bdw_ref, cg_ref, cb_ref, wout_ref,
                 bout_ref, lg_ref, lb_ref, o_ref, so_ref, ubuf, ybuf, *, tt, nt):
    j = pl.program_id(1)

    @pl.when(j == 0)
    def _():
        ubuf[0:HALO, :] = st_ref[0].astype(BF16).astype(F32)

    @pl.when(j > 0)
    def _():
        ubuf[0:HALO, :] = ubuf[tt:tt + HALO, :]

    x = x_ref[...]
    h = _bdot(x, win_ref[...]) + bin_ref[...]
    u = h[:, :D] * _sigmoid(h[:, D:])
    ubuf[HALO:HALO + tt, :] = u.astype(BF16).astype(F32)

    @pl.when(j == nt - 1)
    def _():
        so_ref[0] = u[tt - HALO:tt, :]
    first = HALO - (CONV_W - 1)
    rc = min(CONV_RC, tt)
    for r0 in range(0, tt, rc):
        for c0 in range(0, D, CONV_LC):
            y = jnp.zeros((rc, CONV_LC), F32) + bdw_ref[:, c0:c0 + CONV_LC]
            for off in range(SUB):
                rows = rc + (SUB if off else 0)
                acc = None
                for tap in range(CONV_W):
                    if (first + tap) % SUB != off:
                        continue
                    base = r0 + (first + tap) // SUB * SUB
                    term = wdw_ref[tap:tap + 1, c0:c0 + CONV_LC] * ubuf[base:base + rows, c0:c0 + CONV_LC]
                    acc = term if acc is None else acc + term
                y = y + acc[off:off + rc]
            ybuf[r0:r0 + rc, c0:c0 + CONV_LC] = y
    z = _ln(ybuf[...], cg_ref[...], cb_ref[...])
    z = z * _sigmoid(z)
    mix = _bdot(z, wout_ref[...]) + bout_ref[...]
    o_ref[...] = _ln(ALPHA * x + mix, lg_ref[...], lb_ref[...])


def _conv_mixer(x, state, w, lnp, *, nb, t, tt, row_off, n_total, prev):
    nt = t // tt
    win, bin_, wdw, bdw, cg, cb, wout, bout = w
    ins = [('tok' if x.shape[0] == n_total else 'own', x), ('bat', state), ('const', win), ('const', bin_),
           ('const', wdw), ('const', bdw),
           ('const', cg), ('const', cb), ('const', wout), ('const', bout), ('const', lnp[0]), ('const', lnp[1])]
    outs = [('tok', (D,), F32), ('bat', (HALO, D), F32)]
    scratch = [pltpu.VMEM((HALO + tt, D), F32), pltpu.VMEM((tt, D), F32)]
    return _seq_call(functools.partial(_conv_kernel, tt=tt, nt=nt), name=f"conv_t{t}", nb=nb, nt=nt, tt=tt,
                     row_off=row_off,
                     n_total=n_total, ins=ins, outs=outs, scratch=scratch, prev=prev)


def _head_sum(y, hs_ref, hst_ref):
    s = _split_dot(y, hs_ref[...])
    return _split_dot(s, hst_ref[...])


def _rw1_kernel(x_ref, sh_ref, mu_ref, wr_ref, wk_ref, wv_ref, w0_ref, w1_ref, w2_ref, a0_ref, a1_ref,
                a2_ref, g1_ref, g2_ref, kk_ref, ka_ref, hs_ref, hst_ref,
                r_ref, w_ref, k_ref, v_ref, an_ref, b_ref, g_ref, xbuf, *, tt):
    j = pl.program_id(1)

    @pl.when(j == 0)
    def _():
        xbuf[SUB - 1:SUB, :] = sh_ref[0]

    @pl.when(j > 0)
    def _():
        xbuf[SUB - 1:SUB, :] = xbuf[SUB - 1 + tt:SUB + tt, :]

    x = x_ref[...]
    xbuf[SUB:SUB + tt, :] = x
    xx = xbuf[SUB - 1:SUB - 1 + tt, :] - x
    mu = mu_ref[...]
    xr = x + xx * mu[0:1]
    xw = x + xx * mu[1:2]
    xk = x + xx * mu[2:3]
    xv = x + xx * mu[3:4]
    xa = x + xx * mu[4:5]
    xg = x + xx * mu[5:6]
    r = _bdot(xr, wr_ref[...])
    k = _bdot(xk, wk_ref[...])
    v = _bdot(xv, wv_ref[...])
    lw = w0_ref[...] + _bdot(jnp.tanh(_bdot(xw, w1_ref[...])), w2_ref[...])
    z = -lw
    log_w = -(jnp.maximum(z, 0.0) + jnp.log(1.0 + jnp.exp(-jnp.abs(z)))) - 0.5
    a = _sigmoid(a0_ref[...] + _bdot(_bdot(xa, a1_ref[...]), a2_ref[...]))
    g = _bdot(_sigmoid(_bdot(xg, g1_ref[...])), g2_ref[...])
    kk = k * kk_ref[...]
    ss = _head_sum(kk * kk, hs_ref, hst_ref)
    kk = kk * lax.rsqrt(jnp.maximum(ss, 1e-24))
    r_ref[...] = r
    w_ref[...] = jnp.exp(-jnp.exp(log_w))
    k_ref[...] = k * (1.0 + (a - 1.0) * ka_ref[...])
    v_ref[...] = v
    an_ref[...] = -kk
    b_ref[...] = kk * a
    g_ref[...] = g


def _rw1(x, shift, w, hs, hst, *, nb, t, tt, row_off):
    nt = t // tt
    prev = []
    ins = [('tok', x), ('bat', shift)] + [('const', a) for a in w] + [('const', hs), ('const', hst)]
    outs = [('own', (D,), F32)] * (N_SCAN_IN + 1)
    scratch = [pltpu.VMEM((SUB + tt, D), F32)]
    return _seq_call(functools.partial(_rw1_kernel, tt=tt), name=f"rwkv_proj_t{t}", nb=nb, nt=nt, tt=tt,
                     row_off=row_off,
                     n_total=x.shape[0], ins=ins, outs=outs, scratch=scratch, prev=prev)


def _scan_kernel(r_ref, w_ref, k_ref, v_ref, an_ref, b_ref, nxt_ref, s0_ref, o_ref, st_ref, S, sa_buf,
                 *, tc, nc):
    c = pl.program_id(1)

    @pl.when(c == 0)
    def _():
        S[...] = s0_ref[...]

        def init(kk, acc):
            return acc + S[kk] * an_ref[0, pl.ds(kk, 1), :]

        sa_buf[...] = lax.fori_loop(0, HD, init, jnp.zeros((HD, LANES), F32))

    def step(t, sa, a_next):
        vt = v_ref[t]

        def kbody(kk, acc):
            o_acc, sa_acc = acc
            sk = (S[kk] * w_ref[t, pl.ds(kk, 1), :] + sa * b_ref[t, pl.ds(kk, 1), :]
                  + vt * k_ref[t, pl.ds(kk, 1), :])
            S[kk] = sk
            return (o_acc + sk * r_ref[t, pl.ds(kk, 1), :], sa_acc + sk * a_next(kk))

        zero = jnp.zeros((HD, LANES), F32)
        o_acc, sa_next = lax.fori_loop(0, HD, kbody, (zero, zero), unroll=8)
        o_ref[t] = o_acc
        return sa_next

    sa = lax.fori_loop(0, tc - 1, lambda t, sa: step(t, sa, lambda kk: an_ref[t + 1, pl.ds(kk, 1), :]),
                       sa_buf[...])
    sa_buf[...] = step(tc - 1, sa, lambda kk: nxt_ref[0, pl.ds(kk, 1), :])

    @pl.when(c == nc - 1)
    def _():
        st_ref[...] = S[...]


def _scan(r, w, k, v, an, b, s0, *, tc):
    t, _, lanes = r.shape
    ng, nc = lanes // LANES, t // tc
    seq = pl.BlockSpec((tc, HD, LANES), lambda g, c: (c, 0, g))
    nxt = pl.BlockSpec((1, HD, LANES), lambda g, c: (jnp.minimum((c + 1) * tc, t - 1), 0, g))
    return pl.pallas_call(
        functools.partial(_scan_kernel, tc=tc, nc=nc), grid=(ng, nc),
        in_specs=[seq] * 6 + [nxt, pl.BlockSpec((HD, HD, LANES), lambda g, c: (0, 0, g))],
        out_specs=[seq, pl.BlockSpec((HD, HD, LANES), lambda g, c: (0, 0, g))],
        out_shape=[jax.ShapeDtypeStruct((t, HD, lanes), F32), jax.ShapeDtypeStruct((HD, HD, lanes), F32)],
        scratch_shapes=[pltpu.VMEM((HD, HD, LANES), F32), pltpu.VMEM((HD, LANES), F32)],
        name=f"wkv_scan_t{t}", compiler_params=_cp(("arbitrary", "arbitrary")))(r, w, k, v, an, b, an, s0)


N_SCAN_IN = 6


def _to_scan_kernel(*refs, tc):
    low = lax.broadcasted_iota(I32, (HD, LANES), 1) < HD

    def pair(i, carry):
        t = i * 2
        for src, dst in zip(refs[:N_SCAN_IN], refs[N_SCAN_IN:]):
            y0, y1 = src[:, t, :], src[:, t + 1, :]
            m = jnp.concatenate([y[:, LANES * p:LANES * (p + 1)] for y in (y0, y1) for p in range(HEAD_PAIRS)],
                                axis=0).T
            top, bot = m[:HD], m[HD:]
            dst[t] = jnp.where(low, top, pltpu.roll(bot, HD, 1))
            dst[t + 1] = jnp.where(low, pltpu.roll(top, HD, 1), bot)
        return carry

    lax.fori_loop(0, tc // 2, pair, 0, unroll=2)


def _to_scan_call(arrs, *, tc):
    nb, t, _ = arrs[0].shape
    ng, nc = nb // GROUP_ROWS, t // tc
    return pl.pallas_call(
        functools.partial(_to_scan_kernel, tc=tc), grid=(ng, nc),
        in_specs=[pl.BlockSpec((GROUP_ROWS, tc, D), lambda g, c: (g, c, 0))] * N_SCAN_IN,
        out_specs=[pl.BlockSpec((tc, HD, LANES), lambda g, c: (c, 0, g))] * N_SCAN_IN,
        out_shape=[jax.ShapeDtypeStruct((t, HD, ng * LANES), F32)] * N_SCAN_IN, name=f"wkv_relayout_t{t}",
        compiler_params=_cp(("arbitrary", "arbitrary")))(*arrs)


def _post_kernel(o_ref, r_ref, k_ref, v_ref, xg_ref, xb_ref, rk_ref, y_ref, *, tc):
    low = lax.broadcasted_iota(I32, (HD, LANES), 1) < HD

    def norm(t):
        o = o_ref[t]
        d = o - jnp.mean(o, axis=0, keepdims=True)
        vo = jnp.mean(d * d, axis=0, keepdims=True)
        bonus = jnp.sum(r_ref[t] * k_ref[t] * rk_ref[...], axis=0, keepdims=True) * v_ref[t]
        return d * lax.rsqrt(vo + LNX_EPS) * xg_ref[...] + xb_ref[...] + bonus

    def pair(i, carry):
        t = i * 2
        z0, z1 = norm(t), norm(t + 1)
        top = jnp.where(low, z0, pltpu.roll(z1, HD, 1))
        bot = jnp.where(low, pltpu.roll(z0, HD, 1), z1)
        m = jnp.concatenate([top, bot], axis=0).T
        rows = lambda base, p: m[base + GROUP_ROWS * p:base + GROUP_ROWS * (p + 1)]
        y_ref[t] = jnp.concatenate([rows(0, p) for p in range(HEAD_PAIRS)], axis=1)
        y_ref[t + 1] = jnp.concatenate([rows(HD, p) for p in range(HEAD_PAIRS)], axis=1)
        return carry

    lax.fori_loop(0, tc // 2, pair, 0, unroll=4)


def _post(o, r, k, v, xg, xb, rk, *, tc):
    t, _, lanes = o.shape
    ng, nc = lanes // LANES, t // tc
    seq = pl.BlockSpec((tc, HD, LANES), lambda g, c: (c, 0, g))
    const = pl.BlockSpec((HD, LANES), lambda g, c: (0, 0))
    return pl.pallas_call(
        functools.partial(_post_kernel, tc=tc), grid=(ng, nc),
        in_specs=[seq] * 4 + [const] * 3,
        out_specs=pl.BlockSpec((tc, GROUP_ROWS, D), lambda g, c: (c, g, 0)),
        out_shape=jax.ShapeDtypeStruct((t, ng * GROUP_ROWS, D), F32), name=f"wkv_post_t{t}",
        compiler_params=_cp(("arbitrary", "arbitrary")))(o, r, k, v, xg, xb, rk)


def _rw3_kernel(y_ref, g_ref, x_ref, wo_ref, lg_ref, lb_ref, out_ref):
    mix = _bdot(y_ref[...] * g_ref[...], wo_ref[...])
    out_ref[...] = _ln(ALPHA * x_ref[...] + mix, lg_ref[...], lb_ref[...])


def _head_major_spec(tt):
    return pl.BlockSpec((HEADS, tt, HD), lambda i: (0, i, 0))


def _tok_call(body, name, n, tt, toks, consts, outs):
    in_specs = [_head_major_spec(tt) if a.ndim == 3 else pl.BlockSpec((tt, a.shape[1]), lambda i: (i, 0))
                for a in toks]
    in_specs += [pl.BlockSpec(a.shape, lambda i, _n=a.ndim: (0,) * _n) for a in consts]
    out_specs = [_head_major_spec(tt) if w == 'heads' else pl.BlockSpec((tt, w), lambda i: (i, 0))
                 for w, _ in outs]
    out_shape = [jax.ShapeDtypeStruct((HEADS, n, HD) if w == 'heads' else (n, w), dt) for w, dt in outs]
    return pl.pallas_call(
        body, grid=(n // tt,), in_specs=in_specs, out_specs=out_specs, out_shape=out_shape, name=name,
        compiler_params=_cp(("arbitrary",)))(*toks, *consts)


def _qkv_kernel(x_ref, w_ref, b_ref, q_ref, k_ref, v_ref):
    h = _bdot(x_ref[...], w_ref[...]) + b_ref[...]
    for hd in range(HEADS):
        q_ref[hd] = h[:, hd * HD:(hd + 1) * HD].astype(BF16)
    k_ref[...] = h[:, HEADS * HD:HEADS * HD + KVH * HD]
    v_ref[...] = h[:, HEADS * HD + KVH * HD:]


def _attn_kernel(q_ref, kp_ref, vp_ref, bias_ref, sink_ref, prev_ref, o_ref, *, nq, nsub, kb, mask_lo):
    del prev_ref
    c = pl.program_id(1)
    start = pl.multiple_of(c * (nsub * nq), SUB)
    span = kb + (nsub - 1) * nq
    kall = kp_ref[0, pl.ds(start, span), :].astype(BF16)
    vall = vp_ref[0, pl.ds(start, span), :].astype(BF16)
    for s in range(nsub):
        rows = slice(s * nq, (s + 1) * nq)
        valid = (start + s * nq + lax.broadcasted_iota(I32, (1, kb), 1)) >= mask_lo
        for g in range(KVH):
            hs = slice(g * GROUP, (g + 1) * GROUP)
            qg = q_ref[hs, rows, :].reshape(GROUP * nq, HD)
            kh = kall[s * nq:s * nq + kb, g * HD:(g + 1) * HD]
            vh = vall[s * nq:s * nq + kb, g * HD:(g + 1) * HD]
            logits = lax.dot_general(qg, kh, (((1,), (1,)), ((), ())), preferred_element_type=F32) * HD ** -0.5
            logits = jnp.where(valid, logits + bias_ref[hs].reshape(GROUP * nq, kb), -1e30)
            sink = sink_ref[hs].reshape(GROUP * nq, 1)
            m = jnp.maximum(jnp.max(logits, axis=-1, keepdims=True), sink)
            p = jnp.exp(logits - m)
            p = p / (jnp.sum(p, axis=-1, keepdims=True) + jnp.exp(sink - m))
            og = jnp.dot(p.astype(BF16), vh, preferred_element_type=F32)
            o_ref[hs, rows, :] = og.reshape(GROUP, nq, HD).astype(BF16)


def _attn(q, kp, vp, bias, sinks, *, nb, nt, nq, nsub, kb, mask_lo, row_off, prev):
    off = row_off // (nq * nsub)
    n_total = q.shape[1]
    heads = pl.BlockSpec((HEADS, nq * nsub, HD), lambda b, c: (0, off + b * nt + c, 0))
    const = lambda a: pl.BlockSpec(a.shape, lambda b, c, _n=a.ndim: (0,) * _n)
    bat = lambda a: pl.BlockSpec((1,) + a.shape[1:], lambda b, c: (b, 0, 0))
    sink_tab = jnp.broadcast_to(sinks.reshape(HEADS, 1, 1), (HEADS, nq, 1))
    if prev is None:
        prev = jnp.zeros((HEADS, n_total, HD), BF16)
    return pl.pallas_call(
        functools.partial(_attn_kernel, nq=nq, nsub=nsub, kb=kb, mask_lo=mask_lo), grid=(nb, nt),
        in_specs=[heads, bat(kp), bat(vp), const(bias), const(sink_tab), pl.BlockSpec(memory_space=pl.ANY)],
        out_specs=heads, out_shape=jax.ShapeDtypeStruct((HEADS, n_total, HD), BF16),
        input_output_aliases={5: 0}, name=f"attn_q{nq}",
        compiler_params=_cp(("arbitrary", "arbitrary")))(q, kp, vp, bias, sink_tab, prev)


def _oproj_kernel(o_ref, x_ref, wo_ref, bo_ref, lg_ref, lb_ref, out_ref):
    acc = jnp.dot(o_ref[0], wo_ref[0:HD, :], preferred_element_type=F32)
    for hd in range(1, HEADS):
        acc = acc + jnp.dot(o_ref[hd], wo_ref[hd * HD:(hd + 1) * HD, :], preferred_element_type=F32)
    out_ref[...] = _ln(ALPHA * x_ref[...] + acc + bo_ref[...], lg_ref[...], lb_ref[...])


def _t5_bucket(rel):
    half = N_BUCKETS // 2
    max_exact = half // 2
    ret = jnp.where(rel > 0, half, 0)
    n = jnp.abs(rel)
    nf = jnp.maximum(n, 1).astype(F32)
    large = max_exact + (jnp.log(nf / max_exact) / math.log(MAX_DISTANCE / max_exact)
                         * (half - max_exact)).astype(I32)
    large = jnp.minimum(large, half - 1)
    return ret + jnp.where(n < max_exact, n, large)


def _t5_bias(rel_bias, n_q, n_k):
    rel = jnp.arange(n_k)[None, :] - WINDOW - jnp.arange(n_q)[:, None]
    onehot = (_t5_bucket(rel)[..., None] == jnp.arange(N_BUCKETS)).astype(F32)
    return jnp.einsum('qkn,nh->hqk', onehot, rel_bias, precision=lax.Precision.HIGHEST)


def _route_kernel(x_ref, wr_ref, br_ref, idx_ref, gate_ref, rank_ref, cnt_ref, carry, *, tt):
    i = pl.program_id(0)

    @pl.when(i == 0)
    def _():
        carry[...] = jnp.zeros_like(carry)

    logits = lax.dot_general(wr_ref[...].astype(BF16), x_ref[...].astype(BF16), (((1,), (1,)), ((), ())),
                             preferred_element_type=F32) + br_ref[...]
    sub = lax.broadcasted_iota(I32, (N_EXP, tt), 0)
    out_row = lax.broadcasted_iota(I32, (8, tt), 0)
    vals, sels = [], []
    idx_out = jnp.zeros((8, tt), I32)
    work = logits
    for k in range(TOP_K):
        m = jnp.max(work, axis=0, keepdims=True)
        ik = jnp.min(jnp.where(work == m, sub, N_EXP), axis=0, keepdims=True)
        sel = sub == ik
        vals.append(m)
        sels.append(sel)
        idx_out = jnp.where(out_row == k, ik, idx_out)
        work = jnp.where(sel, -jnp.inf, work)
    es = [jnp.exp(v - vals[0]) for v in vals]
    den = es[0] + es[1] + es[2] + es[3]
    gate_out = jnp.zeros((8, tt), F32)
    for k in range(TOP_K):
        gate_out = jnp.where(out_row == k, es[k] / den, gate_out)
    onehot = jnp.zeros((N_EXP, tt), F32)
    for sel in sels:
        onehot = onehot + sel.astype(F32)
    before = (lax.broadcasted_iota(I32, (tt, tt), 0) < lax.broadcasted_iota(I32, (tt, tt), 1)).astype(BF16)
    base = carry[...] + jnp.dot(onehot.astype(BF16), before, preferred_element_type=F32)
    rank_out = jnp.zeros((8, tt), I32)
    for k in range(TOP_K):
        rk = jnp.sum(jnp.where(sels[k], base, 0.0), axis=0, keepdims=True)
        rank_out = jnp.where(out_row == k, rk.astype(I32), rank_out)
    carry[...] = carry[...] + jnp.sum(onehot, axis=1, keepdims=True)
    idx_ref[...] = idx_out
    gate_ref[...] = gate_out
    rank_ref[...] = rank_out
    cnt_ref[...] = carry[...]


def _route(x1, wr_t, br_col, *, tt):
    n = x1.shape[0]
    rows = pl.BlockSpec((8, tt), lambda i: (0, i))
    const = lambda a: pl.BlockSpec(a.shape, lambda i, _n=a.ndim: (0,) * _n)
    return pl.pallas_call(
        functools.partial(_route_kernel, tt=tt), grid=(n // tt,),
        in_specs=[pl.BlockSpec((tt, D), lambda i: (i, 0)), const(wr_t), const(br_col)],
        out_specs=[rows, rows, rows, pl.BlockSpec((N_EXP, 1), lambda i: (0, 0))],
        out_shape=[jax.ShapeDtypeStruct((8, n), I32), jax.ShapeDtypeStruct((8, n), F32),
                   jax.ShapeDtypeStruct((8, n), I32), jax.ShapeDtypeStruct((N_EXP, 1), F32)],
        scratch_shapes=[pltpu.VMEM((N_EXP, 1), F32)], name="moe_route",
        compiler_params=_cp(("arbitrary",)))(x1, wr_t, br_col)


assert D == SUB * LANES


def _rows_to_tiles(dst_ref, x, n):
    for s in range(SUB):
        dst_ref[pl.ds(s, n, stride=SUB), :] = x[:, LANES * s:LANES * (s + 1)]


def _tiles_to_rows(src_ref, n):
    return jnp.concatenate([src_ref[pl.ds(s, n, stride=SUB), :] for s in range(SUB)], axis=1)


def _tile_copy(src, s, dst, d, sem):
    return pltpu.make_async_copy(src.at[pl.ds(pl.multiple_of(s * SUB, SUB), SUB), :],
                                 dst.at[pl.ds(pl.multiple_of(d * SUB, SUB), SUB), :], sem)


ROW_UNROLL = 16


def _drain_rows(src, dst, sem, tt):
    def drain(r, carry):
        for k in range(TOP_K):
            _tile_copy(src, 0, dst, 0, sem).wait()
        return carry

    lax.fori_loop(0, tt, drain, 0, unroll=ROW_UNROLL)


def _disp_kernel(dest_ref, x_ref, xs_in, xs_ref, xt, sem, *, tt, nt):
    del xs_in
    i = pl.program_id(0)
    slot = lax.rem(i, 2)
    stage = xt.at[slot]
    _rows_to_tiles(stage, x_ref[...], tt)

    def issue(r, carry):
        for k in range(TOP_K):
            _tile_copy(stage, r, xs_ref, dest_ref[r * TOP_K + k], sem.at[slot]).start(priority=k % 2)
        return carry

    lax.fori_loop(0, tt, issue, 0, unroll=ROW_UNROLL)

    @pl.when(i > 0)
    def _():
        _drain_rows(xt.at[1 - slot], xs_ref, sem.at[1 - slot], tt)

    @pl.when(i == nt - 1)
    def _():
        _drain_rows(stage, xs_ref, sem.at[slot], tt)


def _dispatch(dest, x1, xs_prev, *, tt):
    n = x1.shape[0]
    smem_tok = pl.BlockSpec((tt * TOP_K,), lambda i: (i,), memory_space=pltpu.SMEM)
    return pl.pallas_call(
        functools.partial(_disp_kernel, tt=tt, nt=n // tt), grid=(n // tt,),
        in_specs=[smem_tok, pl.BlockSpec((tt, D), lambda i: (i, 0)), pl.BlockSpec(memory_space=pl.ANY)],
        out_specs=pl.BlockSpec(memory_space=pl.ANY),
        out_shape=jax.ShapeDtypeStruct(xs_prev.shape, F32),
        scratch_shapes=[pltpu.VMEM((2, tt * SUB, LANES), F32), pltpu.SemaphoreType.DMA((2,))],
        input_output_aliases={2: 0}, name="moe_dispatch",
        compiler_params=_cp(("arbitrary",)))(dest, x1, xs_prev)


def _expert_kernel(te_ref, nv_ref, xs_ref, w1_ref, b1_ref, w2_ref, b2_ref, y_ref, w1b, w2b):
    i = pl.program_id(0)
    valid = i < nv_ref[0]
    changed = jnp.logical_or(i == 0, te_ref[i] != te_ref[jnp.maximum(i - 1, 0)])

    @pl.when(jnp.logical_and(valid, changed))
    def _():
        for r0 in range(0, D, CAST_ROWS):
            w1b[r0:r0 + CAST_ROWS, :] = w1_ref[0, 0, r0:r0 + CAST_ROWS, :].astype(BF16)
            w2b[r0:r0 + CAST_ROWS, :] = w2_ref[0, 0, r0:r0 + CAST_ROWS, :].astype(BF16)

    @pl.when(valid)
    def _():
        x = _tiles_to_rows(xs_ref, EXP_TILE).astype(BF16)
        h = jnp.dot(x, w1b[...], preferred_element_type=F32) + b1_ref[0, 0]
        glu = jnp.minimum(h[:, :D], SWIGLU_LIMIT)
        lin = jnp.clip(h[:, D:], -SWIGLU_LIMIT, SWIGLU_LIMIT)
        act = glu * _sigmoid(SWIGLU_ALPHA * glu) * (lin + 1.0)
        y = jnp.dot(act.astype(BF16), w2b[...], preferred_element_type=F32) + b2_ref[0, 0]
        _rows_to_tiles(y_ref, y, EXP_TILE)

    @pl.when(jnp.logical_not(valid))
    def _():
        y_ref[...] = jnp.zeros_like(y_ref)


def _experts(tile_expert, n_valid, xs, w1, b1, w2, b2, layer):
    n_tiles = xs.shape[0] // (EXP_TILE * SUB)
    grid_spec = pltpu.PrefetchScalarGridSpec(
        num_scalar_prefetch=2, grid=(n_tiles,),
        in_specs=[pl.BlockSpec((EXP_TILE * SUB, LANES), lambda i, te, nv: (i, 0)),
                  pl.BlockSpec((1, 1, D, 2 * D), lambda i, te, nv: (layer, te[i], 0, 0)),
                  pl.BlockSpec((1, 1, 1, 2 * D), lambda i, te, nv: (layer, te[i], 0, 0)),
                  pl.BlockSpec((1, 1, D, D), lambda i, te, nv: (layer, te[i], 0, 0)),
                  pl.BlockSpec((1, 1, 1, D), lambda i, te, nv: (layer, te[i], 0, 0))],
        out_specs=pl.BlockSpec((EXP_TILE * SUB, LANES), lambda i, te, nv: (i, 0)),
        scratch_shapes=[pltpu.VMEM((D, 2 * D), BF16), pltpu.VMEM((D, D), BF16)])
    return pl.pallas_call(
        _expert_kernel, grid_spec=grid_spec, out_shape=jax.ShapeDtypeStruct(xs.shape, F32), name="moe_experts",
        compiler_params=_cp(("arbitrary",)))(tile_expert, n_valid, xs, w1, b1, w2, b2)


def _comb_kernel(dest_ref, next_ref, gate_ref, x1_ref, p_ref, y_ref, lg_ref, lb_ref, wg_ref, bg_ref,
                 wp_ref, o_ref, buf, sem, *, tt, nt):
    i = pl.program_id(0)
    slot = lax.rem(i, 2)

    def gather(idx_ref, s):
        def issue(r, carry):
            for k in range(TOP_K):
                _tile_copy(y_ref, idx_ref[r * TOP_K + k], buf.at[s, k], r, sem.at[s]).start(priority=k % 2)
            return carry

        lax.fori_loop(0, tt, issue, 0, unroll=ROW_UNROLL)

    @pl.when(i == 0)
    def _():
        gather(dest_ref, slot)

    @pl.when(i + 1 < nt)
    def _():
        gather(next_ref, 1 - slot)

    _drain_rows(y_ref, buf.at[slot, 0], sem.at[slot], tt)
    gate = gate_ref[...]
    moe = gate[:, 0:1] * _tiles_to_rows(buf.at[slot, 0], tt)
    for k in range(1, TOP_K):
        moe = moe + gate[:, k:k + 1] * _tiles_to_rows(buf.at[slot, k], tt)
    x2 = _ln(ALPHA * x1_ref[...] + moe, lg_ref[...], lb_ref[...])
    gt = _sigmoid(_bdot(x2, wg_ref[...]) + bg_ref[...])
    o_ref[...] = x2 + gt * _bdot(p_ref[...], wp_ref[...])


def _combine(dest, gate, x1, p, y, lnp, wg, bg, wp, *, tt):
    n = x1.shape[0]
    smem_tok = pl.BlockSpec((tt * TOP_K,), lambda i: (i,), memory_space=pltpu.SMEM)
    tok = lambda w: pl.BlockSpec((tt, w), lambda i: (i, 0))
    const = lambda a: pl.BlockSpec(a.shape, lambda i, _n=a.ndim: (0,) * _n)
    nt = n // tt
    smem_next = pl.BlockSpec((tt * TOP_K,), lambda i: (jnp.minimum(i + 1, nt - 1),), memory_space=pltpu.SMEM)
    return pl.pallas_call(
        functools.partial(_comb_kernel, tt=tt, nt=nt), grid=(nt,),
        in_specs=[smem_tok, smem_next, tok(TOP_K), tok(D), tok(p.shape[1]), pl.BlockSpec(memory_space=pl.ANY),
                  const(lnp[0]), const(lnp[1]), const(wg), const(bg), const(wp)],
        out_specs=tok(D), out_shape=jax.ShapeDtypeStruct((n, D), F32),
        scratch_shapes=[pltpu.VMEM((2, TOP_K, tt * SUB, LANES), F32), pltpu.SemaphoreType.DMA((2,))],
        name="moe_combine_ple",
        compiler_params=_cp(("arbitrary",)))(dest, dest, gate, x1, p, y, lnp[0], lnp[1], wg, bg, wp)


def _moe_ple(x1, p, xs_buf, wr_t, br_col, w1, b1, w2, b2, layer, lnp, wg, bg, wp, *, tt):
    n = x1.shape[0]
    idx, gate, rank, counts = _route(x1, wr_t, br_col, tt=512 if n % 512 == 0 else tt)
    counts = counts[:, 0].astype(I32)
    padded = (counts + EXP_TILE - 1) // EXP_TILE * EXP_TILE
    pad_end = jnp.cumsum(padded)
    pad_start = pad_end - padded
    n_tiles = xs_buf.shape[0] // (EXP_TILE * SUB)
    n_valid = (pad_end[-1] // EXP_TILE).astype(I32)
    tiles = jnp.minimum(jnp.arange(n_tiles, dtype=I32), n_valid - 1) * EXP_TILE
    tile_expert = jnp.minimum(jnp.sum((tiles[:, None] >= pad_end[None, :]).astype(I32), axis=1), N_EXP - 1)
    experts = jnp.arange(N_EXP, dtype=I32)
    start_of = jnp.sum(jnp.where(idx[:TOP_K, :, None] == experts, pad_start, 0), axis=-1)
    dest = (start_of + rank[:TOP_K]).T.reshape(-1)
    xs = _dispatch(dest, x1, xs_buf, tt=tt)
    y = _experts(tile_expert, n_valid.reshape(1), xs, w1, b1, w2, b2, layer)
    return _combine(dest, gate[:TOP_K].T, x1, p, y, lnp, wg, bg, wp, tt=tt), xs


def _state_to_scan(s, nb):
    s = s.reshape(nb // GROUP_ROWS, GROUP_ROWS, HEAD_PAIRS, 2, HD, HD).transpose(5, 4, 0, 3, 2, 1)
    return s.reshape(HD, HD, nb * HEADS)


def _state_from_scan(s, nb):
    s = s.reshape(HD, HD, nb // GROUP_ROWS, 2, HEAD_PAIRS, GROUP_ROWS).transpose(2, 5, 4, 3, 1, 0)
    return s.reshape(nb, HEADS, HD, HD)


def _head_vec_to_scan(a):
    a = a.reshape(HEAD_PAIRS, 2, HD).transpose(2, 1, 0)
    return jnp.broadcast_to(a[..., None], (HD, 2, HEAD_PAIRS, GROUP_ROWS)).reshape(HD, LANES)


def _row2(a):
    return a.reshape(1, -1)


def _tile(n):
    for tt in (256, 128, 64, 32, 16, 8):
        if n % tt == 0:
            return tt
    raise ValueError(n)


def kernel(x_prompt, x_sample, p_prompt, p_sample, cache_conv, state_rwkv_shift, state_rwkv_wkv, cache_swa_k, cache_swa_v, conv_w_in, conv_b_in, conv_w_dw, conv_b_dw, conv_ln_g, conv_ln_b, conv_w_out, conv_b_out, rwkv_mu, rwkv_w_rkv, rwkv_w0, rwkv_w1, rwkv_w2, rwkv_a0, rwkv_a1, rwkv_a2, rwkv_g1, rwkv_g2, rwkv_k_k, rwkv_k_a, rwkv_r_k, rwkv_lnx_g, rwkv_lnx_b, rwkv_w_o, attn_w_qkv, attn_b_qkv, attn_sinks, attn_w_o, attn_b_o, rel_bias, ln_g, ln_b, moe_w_router, moe_b_router, moe_w1, moe_b1, moe_w2, moe_b2, ple_w_proj, ple_w_gate, ple_b_gate):
    bp, tp, _ = x_prompt.shape
    bs, ts, _ = x_sample.shape
    n_p, n_s = bp * tp, bs * ts
    n = n_p + n_s
    assert tp % 128 == 0 and n_p % ts == 0 and ts % SUB == 0 and HALO <= ts <= CHUNK
    assert bp % GROUP_ROWS == 0 and bs % GROUP_ROWS == 0
    tt_tok = _tile(n)
    tt_dense = 512 if n % 512 == 0 else tt_tok
    tt_p = 256 if tp % 256 == 0 else 128
    x = None
    p_all = jnp.concatenate([p_prompt.reshape(DEPTH, n_p, -1), p_sample.reshape(DEPTH, n_s, -1)], axis=1)
    n_rows = (-(-n * TOP_K // EXP_TILE) + N_EXP) * EXP_TILE
    xs_buf = jnp.zeros((n_rows * SUB, LANES), F32)
    head_sel = (jnp.arange(D)[:, None] // HD == jnp.arange(LANES)[None, :]).astype(BF16)
    head_sel_t = head_sel.T
    conv_p, conv_s, shift_p, shift_s, wkv_p, wkv_s = [], [], [], [], [], []
    swa_kp, swa_vp, swa_ks, swa_vs = [], [], [], []
    for i in range(DEPTH):
        kind, j = i % 3, i // 3
        lnp = (_row2(ln_g[i, 0]), _row2(ln_b[i, 0]))
        if kind == 0:
            cw = (conv_w_in[j].astype(BF16), _row2(conv_b_in[j]), conv_w_dw[j].astype(BF16).astype(F32),
                  _row2(conv_b_dw[j]),
                  _row2(conv_ln_g[j]), _row2(conv_ln_b[j]), conv_w_out[j].astype(BF16), _row2(conv_b_out[j]))
            st_p = jnp.zeros((bp, HALO, D), F32)
            st_s = jnp.pad(cache_conv[j], ((0, 0), (HALO - (CONV_W - 1), 0), (0, 0)))
            xin_p, xin_s = (x_prompt.reshape(n_p, D), x_sample.reshape(n_s, D)) if i == 0 else (x, x)
            x1, so_p = _conv_mixer(xin_p, st_p, cw, lnp, nb=bp, t=tp, tt=tt_p, row_off=0, n_total=n, prev=None)
            x1, so_s = _conv_mixer(xin_s, st_s, cw, lnp, nb=bs, t=ts, tt=ts, row_off=n_p, n_total=n, prev=[x1])
            conv_p.append(so_p[:, HALO - (CONV_W - 1):])
            conv_s.append(so_s[:, HALO - (CONV_W - 1):])
        elif kind == 1:
            rw = (rwkv_mu[j], rwkv_w_rkv[j, 0].astype(BF16), rwkv_w_rkv[j, 1].astype(BF16),
                  rwkv_w_rkv[j, 2].astype(BF16), _row2(rwkv_w0[j]), rwkv_w1[j].astype(BF16),
                  rwkv_w2[j].astype(BF16), _row2(rwkv_a0[j]), rwkv_a1[j].astype(BF16), rwkv_a2[j].astype(BF16),
                  rwkv_g1[j].astype(BF16), rwkv_g2[j].astype(BF16), _row2(rwkv_k_k[j]), _row2(rwkv_k_a[j]))
            sh_p = jnp.zeros((bp, 1, D), F32)
            sh_s = state_rwkv_shift[j].reshape(bs, 1, D)
            post_c = [_head_vec_to_scan(a) for a in (rwkv_lnx_g[j], rwkv_lnx_b[j], rwkv_r_k[j].reshape(-1))]
            out_c = [('const', rwkv_w_o[j].astype(BF16)), ('const', lnp[0]), ('const', lnp[1])]
            x1, states = None, []
            for (lo, nb_, t_, tt_, sh, s0, tc) in ((0, bp, tp, tt_p, sh_p, None, SCAN_CHUNK),
                                                   (n_p, bs, ts, ts, sh_s, state_rwkv_wkv[j], ts)):
                r, w, k, v, an, b, g = _rw1(x, sh, rw, head_sel, head_sel_t, nb=nb_, t=t_, tt=tt_, row_off=lo)
                rs, ws, ks, vs, ans, bs_ = _to_scan_call([a.reshape(nb_, t_, D) for a in (r, w, k, v, an, b)],
                                                         tc=min(tc, RELAYOUT_CHUNK))
                s0l = jnp.zeros((HD, HD, nb_ * HEADS), F32) if s0 is None else _state_to_scan(s0, nb_)
                o_l, s_l = _scan(rs, ws, ks, vs, ans, bs_, s0l, tc=tc)
                y = _post(o_l, rs, ks, vs, *post_c, tc=tc).reshape(t_, nb_ * D)
                x1, = _seq_call(_rw3_kernel, name=f"rwkv_out_t{t_}", nb=nb_, nt=t_ // tt_, tt=tt_, row_off=lo,
                                n_total=n, ins=[('tm', y), ('own', g), ('tok', x)] + out_c,
                                outs=[('tok', (D,), F32)], scratch=[], prev=None if x1 is None else [x1])
                states.append(_state_from_scan(s_l, nb_))
            shift_p.append(x[tp - 1:n_p:tp])
            shift_s.append(x[n_p + ts - 1::ts])
            wkv_p.append(states[0])
            wkv_s.append(states[1])
        else:
            q, kx, vx = _tok_call(_qkv_kernel, "attn_qkv", n, tt_dense, [x],
                                  [attn_w_qkv[j].astype(BF16), _row2(attn_b_qkv[j])],
                                  [('heads', BF16), (KVH * HD, F32), (KVH * HD, F32)])
            k_p = kx[:n_p].reshape(bp, tp, KVH * HD)
            v_p = vx[:n_p].reshape(bp, tp, KVH * HD)
            zpad = jnp.zeros((bp, WINDOW, KVH * HD), F32)
            nc = tp // CHUNK
            band = WINDOW + CHUNK
            o = _attn(q, jnp.concatenate([zpad, k_p], axis=1), jnp.concatenate([zpad, v_p], axis=1),
                      _t5_bias(rel_bias, CHUNK, band), attn_sinks[j], nb=bp, nt=nc, nq=CHUNK, nsub=1, kb=band,
                      mask_lo=WINDOW, row_off=0, prev=None)
            k_all = jnp.concatenate([cache_swa_k[j].reshape(bs, WINDOW, KVH * HD),
                                     kx[n_p:].reshape(bs, ts, KVH * HD)], axis=1)
            v_all = jnp.concatenate([cache_swa_v[j].reshape(bs, WINDOW, KVH * HD),
                                     vx[n_p:].reshape(bs, ts, KVH * HD)], axis=1)
            o = _attn(q, k_all, v_all, _t5_bias(rel_bias, ts, WINDOW + ts), attn_sinks[j],
                      nb=bs, nt=1, nq=ts, nsub=1, kb=WINDOW + ts, mask_lo=0, row_off=n_p, prev=o)
            x1, = _tok_call(_oproj_kernel, "attn_out", n, tt_dense, [o, x],
                            [attn_w_o[j].astype(BF16), _row2(attn_b_o[j]), lnp[0], lnp[1]], [(D, F32)])
            swa_kp.append(k_p[:, -WINDOW:].reshape(bp, WINDOW, KVH, HD))
            swa_vp.append(v_p[:, -WINDOW:].reshape(bp, WINDOW, KVH, HD))
            swa_ks.append(k_all[:, -WINDOW:].reshape(bs, WINDOW, KVH, HD))
            swa_vs.append(v_all[:, -WINDOW:].reshape(bs, WINDOW, KVH, HD))
        x, xs_buf = _moe_ple(x1, p_all[i], xs_buf, moe_w_router[i].T, moe_b_router[i].reshape(N_EXP, 1), moe_w1,
                             moe_b1.reshape(DEPTH, N_EXP, 1, 2 * D), moe_w2, moe_b2.reshape(DEPTH, N_EXP, 1, D), i,
                             (_row2(ln_g[i, 1]), _row2(ln_b[i, 1])),
                             ple_w_gate[i].astype(BF16), _row2(ple_b_gate[i]), ple_w_proj[i].astype(BF16), tt=tt_tok)
    return (x[:n_p].reshape(bp, tp, D), x[n_p:].reshape(bs, ts, D), jnp.stack(conv_p), jnp.stack(conv_s),
            jnp.stack(shift_p), jnp.stack(shift_s), jnp.stack(wkv_p), jnp.stack(wkv_s), jnp.stack(swa_kp),
            jnp.stack(swa_vp), jnp.stack(swa_ks), jnp.stack(swa_vs))
```

```python
import functools
import math

import jax
import jax.numpy as jnp
from jax import lax
from jax.experimental import pallas as pl
from jax.experimental.pallas import tpu as pltpu

F32 = jnp.float32
BF16 = jnp.bfloat16
I32 = jnp.int32

D = 1024
DEPTH = 4
CONV_W = 31
HALO = 32
HEADS = 16
HD = 64
KVH = 2
GROUP = HEADS // KVH
WINDOW = 128
CHUNK = 64
N_BUCKETS = 32
MAX_DISTANCE = 128
N_EXP = 32
TOP_K = 4
EXP_TILE = 512
CAST_ROWS = 256
LANES = 128
SUB = 8
HEAD_PAIRS = HEADS // 2
GROUP_ROWS = LANES // HEADS
SCAN_CHUNK = 64
RELAYOUT_CHUNK = 32
LNX_EPS = 64e-5
LN_EPS = 1e-5
ALPHA = (2 * DEPTH) ** 0.25
SWIGLU_ALPHA = 1.702
SWIGLU_LIMIT = 7.0
VMEM_LIMIT = 56 * 1024 * 1024


def _cp(sem):
    return pltpu.CompilerParams(dimension_semantics=sem, vmem_limit_bytes=VMEM_LIMIT)


def _bdot(a, b):
    return jnp.dot(a.astype(BF16), b.astype(BF16), preferred_element_type=F32)


def _split(a):
    hi = a.astype(BF16)
    lo = (a - hi.astype(F32)).astype(BF16)
    return hi, lo


def _split_dot(a, b_exact):
    hi, lo = _split(a)
    return (jnp.dot(hi, b_exact, preferred_element_type=F32)
            + jnp.dot(lo, b_exact, preferred_element_type=F32))


def _ln(x, g, b, eps=LN_EPS):
    mu = jnp.mean(x, axis=-1, keepdims=True)
    xc = x - mu
    var = jnp.mean(xc * xc, axis=-1, keepdims=True)
    return xc * lax.rsqrt(var + eps) * g + b


def _sigmoid(x):
    return 1.0 / (1.0 + jnp.exp(-x))


def _seq_call(body, *, name, nb, nt, tt, row_off, n_total, ins, outs, scratch, prev=None):
    off = row_off // tt
    in_specs, args = [], []
    for kind, a in ins:
        if kind == 'tok':
            in_specs.append(pl.BlockSpec((tt, a.shape[1]), lambda b, j: (off + b * nt + j, 0)))
        elif kind == 'own':
            in_specs.append(pl.BlockSpec((tt, a.shape[1]), lambda b, j: (b * nt + j, 0)))
        elif kind == 'tm':
            in_specs.append(pl.BlockSpec((tt, a.shape[1] // nb), lambda b, j: (j, b)))
        elif kind == 'bat':
            in_specs.append(pl.BlockSpec((1,) + a.shape[1:], lambda b, j: (b, 0, 0)))
        else:
            in_specs.append(pl.BlockSpec(a.shape, lambda b, j, _n=a.ndim: (0,) * _n))
        args.append(a)
    out_specs, out_shapes = [], []
    for kind, tail, dt in outs:
        if kind == 'tok':
            out_specs.append(pl.BlockSpec((tt, tail[0]), lambda b, j: (off + b * nt + j, 0)))
            out_shapes.append(jax.ShapeDtypeStruct((n_total, tail[0]), dt))
        elif kind == 'own':
            out_specs.append(pl.BlockSpec((tt, tail[0]), lambda b, j: (b * nt + j, 0)))
            out_shapes.append(jax.ShapeDtypeStruct((nb * nt * tt, tail[0]), dt))
        elif kind == 'tm':
            out_specs.append(pl.BlockSpec((tt, tail[0]), lambda b, j: (j, b)))
            out_shapes.append(jax.ShapeDtypeStruct((nt * tt, nb * tail[0]), dt))
        else:
            out_specs.append(pl.BlockSpec((1,) + tuple(tail), lambda b, j: (b, 0, 0)))
            out_shapes.append(jax.ShapeDtypeStruct((nb,) + tuple(tail), dt))
    aliases = {}
    n_prev = 0
    tok_out = [i for i, o in enumerate(outs) if o[0] == 'tok']
    if prev is None:
        prev = [jnp.zeros((n_total, outs[i][1][0]), outs[i][2]) for i in tok_out]
    for p, oi in zip(prev, tok_out):
        aliases[len(args)] = oi
        in_specs.append(pl.BlockSpec(memory_space=pl.ANY))
        args.append(p)
        n_prev += 1
    n_in = len(ins)

    def wrapped(*refs):
        body(*refs[:n_in], *refs[n_in + n_prev:])

    return pl.pallas_call(
        wrapped, grid=(nb, nt), in_specs=in_specs, out_specs=out_specs, out_shape=out_shapes,
        scratch_shapes=scratch, input_output_aliases=aliases, name=name,
        compiler_params=_cp(("arbitrary", "arbitrary")))(*args)


CONV_RC = 64
CONV_LC = 256


def _conv_kernel(x_ref, st_ref, win_ref, bin_ref, wdw_ref, bdw_ref, cg_ref, cb_ref, wout_ref,
                 bout_ref, lg_ref, lb_ref, o_ref, so_ref, ubuf, ybuf, *, tt, nt):
    j = pl.program_id(1)

    @pl.when(j == 0)
    def _():
        ubuf[0:HALO, :] = st_ref[0].astype(BF16).astype(F32)

    @pl.when(j > 0)
    def _():
        ubuf[0:HALO, :] = ubuf[tt:tt + HALO, :]

    x = x_ref[...]
    h = _bdot(x, win_ref[...]) + bin_ref[...]
    u = h[:, :D] * _sigmoid(h[:, D:])
    ubuf[HALO:HALO + tt, :] = u.astype(BF16).astype(F32)

    @pl.when(j == nt - 1)
    def _():
        so_ref[0] = u[tt - HALO:tt, :]
    first = HALO - (CONV_W - 1)
    rc = min(CONV_RC, tt)
    for r0 in range(0, tt, rc):
        for c0 in range(0, D, CONV_LC):
            y = jnp.zeros((rc, CONV_LC), F32) + bdw_ref[:, c0:c0 + CONV_LC]
            for off in range(SUB):
                rows = rc + (SUB if off else 0)
                acc = None
                for tap in range(CONV_W):
                    if (first + tap) % SUB != off:
                        continue
                    base = r0 + (first + tap) // SUB * SUB
                    term = wdw_ref[tap:tap + 1, c0:c0 + CONV_LC] * ubuf[base:base + rows, c0:c0 + CONV_LC]
                    acc = term if acc is None else acc + term
                y = y + acc[off:off + rc]
            ybuf[r0:r0 + rc, c0:c0 + CONV_LC] = y
    z = _ln(ybuf[...], cg_ref[...], cb_ref[...])
    z = z * _sigmoid(z)
    mix = _bdot(z, wout_ref[...]) + bout_ref[...]
    o_ref[...] = _ln(ALPHA * x + mix, lg_ref[...], lb_ref[...])


def _conv_mixer(x, state, w, lnp, *, nb, t, tt, row_off, n_total, prev):
    nt = t // tt
    win, bin_, wdw, bdw, cg, cb, wout, bout = w
    ins = [('tok' if x.shape[0] == n_total else 'own', x), ('bat', state), ('const', win), ('const', bin_),
           ('const', wdw), ('const', bdw),
           ('const', cg), ('const', cb), ('const', wout), ('const', bout), ('const', lnp[0]), ('const', lnp[1])]
    outs = [('tok', (D,), F32), ('bat', (HALO, D), F32)]
    scratch = [pltpu.VMEM((HALO + tt, D), F32), pltpu.VMEM((tt, D), F32)]
    return _seq_call(functools.partial(_conv_kernel, tt=tt, nt=nt), name=f"conv_t{t}", nb=nb, nt=nt, tt=tt,
                     row_off=row_off,
                     n_total=n_total, ins=ins, outs=outs, scratch=scratch, prev=prev)


def _head_sum(y, hs_ref, hst_ref):
    s = _split_dot(y, hs_ref[...])
    return _split_dot(s, hst_ref[...])


def _rw1_kernel(x_ref, sh_ref, mu_ref, wr_ref, wk_ref, wv_ref, w0_ref, w1_ref, w2_ref, a0_ref, a1_ref,
                a2_ref, g1_ref, g2_ref, kk_ref, ka_ref, hs_ref, hst_ref,
                r_ref, w_ref, k_ref, v_ref, an_ref, b_ref, g_ref, xbuf, *, tt):
    j = pl.program_id(1)

    @pl.when(j == 0)
    def _():
        xbuf[SUB - 1:SUB, :] = sh_ref[0]

    @pl.when(j > 0)
    def _():
        xbuf[SUB - 1:SUB, :] = xbuf[SUB - 1 + tt:SUB + tt, :]

    x = x_ref[...]
    xbuf[SUB:SUB + tt, :] = x
    xx = xbuf[SUB - 1:SUB - 1 + tt, :] - x
    mu = mu_ref[...]
    xr = x + xx * mu[0:1]
    xw = x + xx * mu[1:2]
    xk = x + xx * mu[2:3]
    xv = x + xx * mu[3:4]
    xa = x + xx * mu[4:5]
    xg = x + xx * mu[5:6]
    r = _bdot(xr, wr_ref[...])
    k = _bdot(xk, wk_ref[...])
    v = _bdot(xv, wv_ref[...])
    lw = w0_ref[...] + _bdot(jnp.tanh(_bdot(xw, w1_ref[...])), w2_ref[...])
    z = -lw
    log_w = -(jnp.maximum(z, 0.0) + jnp.log(1.0 + jnp.exp(-jnp.abs(z)))) - 0.5
    a = _sigmoid(a0_ref[...] + _bdot(_bdot(xa, a1_ref[...]), a2_ref[...]))
    g = _bdot(_sigmoid(_bdot(xg, g1_ref[...])), g2_ref[...])
    kk = k * kk_ref[...]
    ss = _head_sum(kk * kk, hs_ref, hst_ref)
    kk = kk * lax.rsqrt(jnp.maximum(ss, 1e-24))
    r_ref[...] = r
    w_ref[...] = jnp.exp(-jnp.exp(log_w))
    k_ref[...] = k * (1.0 + (a - 1.0) * ka_ref[...])
    v_ref[...] = v
    an_ref[...] = -kk
    b_ref[...] = kk * a
    g_ref[...] = g


def _rw1(x, shift, w, hs, hst, *, nb, t, tt, row_off):
    nt = t // tt
    prev = []
    ins = [('tok', x), ('bat', shift)] + [('const', a) for a in w] + [('const', hs), ('const', hst)]
    outs = [('own', (D,), F32)] * (N_SCAN_IN + 1)
    scratch = [pltpu.VMEM((SUB + tt, D), F32)]
    return _seq_call(functools.partial(_rw1_kernel, tt=tt), name=f"rwkv_proj_t{t}", nb=nb, nt=nt, tt=tt,
                     row_off=row_off,
                     n_total=x.shape[0], ins=ins, outs=outs, scratch=scratch, prev=prev)


def _scan_kernel(r_ref, w_ref, k_ref, v_ref, an_ref, b_ref, nxt_ref, s0_ref, o_ref, st_ref, S, sa_buf,
                 *, tc, nc):
    c = pl.program_id(1)

    @pl.when(c == 0)
    def _():
        S[...] = s0_ref[...]

        def init(kk, acc):
            return acc + S[kk] * an_ref[0, pl.ds(kk, 1), :]

        sa_buf[...] = lax.fori_loop(0, HD, init, jnp.zeros((HD, LANES), F32))

    def step(t, sa, a_next):
        vt = v_ref[t]

        def kbody(kk, acc):
            o_acc, sa_acc = acc
            sk = (S[kk] * w_ref[t, pl.ds(kk, 1), :] + sa * b_ref[t, pl.ds(kk, 1), :]
                  + vt * k_ref[t, pl.ds(kk, 1), :])
            S[kk] = sk
            return (o_acc + sk * r_ref[t, pl.ds(kk, 1), :], sa_acc + sk * a_next(kk))

        zero = jnp.zeros((HD, LANES), F32)
        o_acc, sa_next = lax.fori_loop(0, HD, kbody, (zero, zero), unroll=16)
        o_ref[t] = o_acc
        return sa_next

    sa = lax.fori_loop(0, tc - 1, lambda t, sa: step(t, sa, lambda kk: an_ref[t + 1, pl.ds(kk, 1), :]),
                       sa_buf[...])
    sa_buf[...] = step(tc - 1, sa, lambda kk: nxt_ref[0, pl.ds(kk, 1), :])

    @pl.when(c == nc - 1)
    def _():
        st_ref[...] = S[...]


def _scan(r, w, k, v, an, b, s0, *, tc):
    t, _, lanes = r.shape
    ng, nc = lanes // LANES, t // tc
    seq = pl.BlockSpec((tc, HD, LANES), lambda g, c: (c, 0, g))
    nxt = pl.BlockSpec((1, HD, LANES), lambda g, c: (jnp.minimum((c + 1) * tc, t - 1), 0, g))
    return pl.pallas_call(
        functools.partial(_scan_kernel, tc=tc, nc=nc), grid=(ng, nc),
        in_specs=[seq] * 6 + [nxt, pl.BlockSpec((HD, HD, LANES), lambda g, c: (0, 0, g))],
        out_specs=[seq, pl.BlockSpec((HD, HD, LANES), lambda g, c: (0, 0, g))],
        out_shape=[jax.ShapeDtypeStruct((t, HD, lanes), F32), jax.ShapeDtypeStruct((HD, HD, lanes), F32)],
        scratch_shapes=[pltpu.VMEM((HD, HD, LANES), F32), pltpu.VMEM((HD, LANES), F32)],
        name=f"wkv_scan_t{t}", compiler_params=_cp(("arbitrary", "arbitrary")))(r, w, k, v, an, b, an, s0)


N_SCAN_IN = 6


def _to_scan_kernel(*refs, tc):
    low = lax.broadcasted_iota(I32, (HD, LANES), 1) < HD

    def pair(i, carry):
        t = i * 2
        for src, dst in zip(refs[:N_SCAN_IN], refs[N_SCAN_IN:]):
            y0, y1 = src[:, t, :], src[:, t + 1, :]
            m = jnp.concatenate([y[:, LANES * p:LANES * (p + 1)] for y in (y0, y1) for p in range(HEAD_PAIRS)],
                                axis=0).T
            top, bot = m[:HD], m[HD:]
            dst[t] = jnp.where(low, top, pltpu.roll(bot, HD, 1))
            dst[t + 1] = jnp.where(low, pltpu.roll(top, HD, 1), bot)
        return carry

    lax.fori_loop(0, tc // 2, pair, 0, unroll=2)


def _to_scan_call(arrs, *, tc):
    nb, t, _ = arrs[0].shape
    ng, nc = nb // GROUP_ROWS, t // tc
    return pl.pallas_call(
        functools.partial(_to_scan_kernel, tc=tc), grid=(ng, nc),
        in_specs=[pl.BlockSpec((GROUP_ROWS, tc, D), lambda g, c: (g, c, 0))] * N_SCAN_IN,
        out_specs=[pl.BlockSpec((tc, HD, LANES), lambda g, c: (c, 0, g))] * N_SCAN_IN,
        out_shape=[jax.ShapeDtypeStruct((t, HD, ng * LANES), F32)] * N_SCAN_IN, name=f"wkv_relayout_t{t}",
        compiler_params=_cp(("arbitrary", "arbitrary")))(*arrs)


def _post_kernel(o_ref, r_ref, k_ref, v_ref, xg_ref, xb_ref, rk_ref, y_ref, *, tc):
    low = lax.broadcasted_iota(I32, (HD, LANES), 1) < HD

    def norm(t):
        o = o_ref[t]
        d = o - jnp.mean(o, axis=0, keepdims=True)
        vo = jnp.mean(d * d, axis=0, keepdims=True)
        bonus = jnp.sum(r_ref[t] * k_ref[t] * rk_ref[...], axis=0, keepdims=True) * v_ref[t]
        return d * lax.rsqrt(vo + LNX_EPS) * xg_ref[...] + xb_ref[...] + bonus

    def pair(i, carry):
        t = i * 2
        z0, z1 = norm(t), norm(t + 1)
        top = jnp.where(low, z0, pltpu.roll(z1, HD, 1))
        bot = jnp.where(low, pltpu.roll(z0, HD, 1), z1)
        m = jnp.concatenate([top, bot], axis=0).T
        rows = lambda base, p: m[base + GROUP_ROWS * p:base + GROUP_ROWS * (p + 1)]
        y_ref[t] = jnp.concatenate([rows(0, p) for p in range(HEAD_PAIRS)], axis=1)
        y_ref[t + 1] = jnp.concatenate([rows(HD, p) for p in range(HEAD_PAIRS)], axis=1)
        return carry

    lax.fori_loop(0, tc // 2, pair, 0, unroll=4)


def _post(o, r, k, v, xg, xb, rk, *, tc):
    t, _, lanes = o.shape
    ng, nc = lanes // LANES, t // tc
    seq = pl.BlockSpec((tc, HD, LANES), lambda g, c: (c, 0, g))
    const = pl.BlockSpec((HD, LANES), lambda g, c: (0, 0))
    return pl.pallas_call(
        functools.partial(_post_kernel, tc=tc), grid=(ng, nc),
        in_specs=[seq] * 4 + [const] * 3,
        out_specs=pl.BlockSpec((tc, GROUP_ROWS, D), lambda g, c: (c, g, 0)),
        out_shape=jax.ShapeDtypeStruct((t, ng * GROUP_ROWS, D), F32), name=f"wkv_post_t{t}",
        compiler_params=_cp(("arbitrary", "arbitrary")))(o, r, k, v, xg, xb, rk)


def _rw3_kernel(y_ref, g_ref, x_ref, wo_ref, lg_ref, lb_ref, out_ref):
    mix = _bdot(y_ref[...] * g_ref[...], wo_ref[...])
    out_ref[...] = _ln(ALPHA * x_ref[...] + mix, lg_ref[...], lb_ref[...])


def _head_major_spec(tt):
    return pl.BlockSpec((HEADS, tt, HD), lambda i: (0, i, 0))


def _tok_call(body, name, n, tt, toks, consts, outs):
    in_specs = [_head_major_spec(tt) if a.ndim == 3 else pl.BlockSpec((tt, a.shape[1]), lambda i: (i, 0))
                for a in toks]
    in_specs += [pl.BlockSpec(a.shape, lambda i, _n=a.ndim: (0,) * _n) for a in consts]
    out_specs = [_head_major_spec(tt) if w == 'heads' else pl.BlockSpec((tt, w), lambda i: (i, 0))
                 for w, _ in outs]
    out_shape = [jax.ShapeDtypeStruct((HEADS, n, HD) if w == 'heads' else (n, w), dt) for w, dt in outs]
    return pl.pallas_call(
        body, grid=(n // tt,), in_specs=in_specs, out_specs=out_specs, out_shape=out_shape, name=name,
        compiler_params=_cp(("arbitrary",)))(*toks, *consts)


def _qkv_kernel(x_ref, w_ref, b_ref, q_ref, k_ref, v_ref):
    h = _bdot(x_ref[...], w_ref[...]) + b_ref[...]
    for hd in range(HEADS):
        q_ref[hd] = h[:, hd * HD:(hd + 1) * HD].astype(BF16)
    k_ref[...] = h[:, HEADS * HD:HEADS * HD + KVH * HD]
    v_ref[...] = h[:, HEADS * HD + KVH * HD:]


def _attn_kernel(q_ref, kp_ref, vp_ref, bias_ref, sink_ref, prev_ref, o_ref, *, nq, nsub, kb, mask_lo):
    del prev_ref
    c = pl.program_id(1)
    start = pl.multiple_of(c * (nsub * nq), SUB)
    span = kb + (nsub - 1) * nq
    kall = kp_ref[0, pl.ds(start, span), :].astype(BF16)
    vall = vp_ref[0, pl.ds(start, span), :].astype(BF16)
    for s in range(nsub):
        rows = slice(s * nq, (s + 1) * nq)
        valid = (start + s * nq + lax.broadcasted_iota(I32, (1, kb), 1)) >= mask_lo
        for g in range(KVH):
            hs = slice(g * GROUP, (g + 1) * GROUP)
            qg = q_ref[hs, rows, :].reshape(GROUP * nq, HD)
            kh = kall[s * nq:s * nq + kb, g * HD:(g + 1) * HD]
            vh = vall[s * nq:s * nq + kb, g * HD:(g + 1) * HD]
            logits = lax.dot_general(qg, kh, (((1,), (1,)), ((), ())), preferred_element_type=F32) * HD ** -0.5
            logits = jnp.where(valid, logits + bias_ref[hs].reshape(GROUP * nq, kb), -1e30)
            sink = sink_ref[hs].reshape(GROUP * nq, 1)
            m = jnp.maximum(jnp.max(logits, axis=-1, keepdims=True), sink)
            p = jnp.exp(logits - m)
            p = p / (jnp.sum(p, axis=-1, keepdims=True) + jnp.exp(sink - m))
            og = jnp.dot(p.astype(BF16), vh, preferred_element_type=F32)
            o_ref[hs, rows, :] = og.reshape(GROUP, nq, HD).astype(BF16)


def _attn(q, kp, vp, bias, sinks, *, nb, nt, nq, nsub, kb, mask_lo, row_off, prev):
    off = row_off // (nq * nsub)
    n_total = q.shape[1]
    heads = pl.BlockSpec((HEADS, nq * nsub, HD), lambda b, c: (0, off + b * nt + c, 0))
    const = lambda a: pl.BlockSpec(a.shape, lambda b, c, _n=a.ndim: (0,) * _n)
    bat = lambda a: pl.BlockSpec((1,) + a.shape[1:], lambda b, c: (b, 0, 0))
    sink_tab = jnp.broadcast_to(sinks.reshape(HEADS, 1, 1), (HEADS, nq, 1))
    if prev is None:
        prev = jnp.zeros((HEADS, n_total, HD), BF16)
    return pl.pallas_call(
        functools.partial(_attn_kernel, nq=nq, nsub=nsub, kb=kb, mask_lo=mask_lo), grid=(nb, nt),
        in_specs=[heads, bat(kp), bat(vp), const(bias), const(sink_tab), pl.BlockSpec(memory_space=pl.ANY)],
        out_specs=heads, out_shape=jax.ShapeDtypeStruct((HEADS, n_total, HD), BF16),
        input_output_aliases={5: 0}, name=f"attn_q{nq}",
        compiler_params=_cp(("arbitrary", "arbitrary")))(q, kp, vp, bias, sink_tab, prev)


def _oproj_kernel(o_ref, x_ref, wo_ref, bo_ref, lg_ref, lb_ref, out_ref):
    acc = jnp.dot(o_ref[0], wo_ref[0:HD, :], preferred_element_type=F32)
    for hd in range(1, HEADS):
        acc = acc + jnp.dot(o_ref[hd], wo_ref[hd * HD:(hd + 1) * HD, :], preferred_element_type=F32)
    out_ref[...] = _ln(ALPHA * x_ref[...] + acc + bo_ref[...], lg_ref[...], lb_ref[...])


def _t5_bucket(rel):
    half = N_BUCKETS // 2
    max_exact = half // 2
    ret = jnp.where(rel > 0, half, 0)
    n = jnp.abs(rel)
    nf = jnp.maximum(n, 1).astype(F32)
    large = max_exact + (jnp.log(nf / max_exact) / math.log(MAX_DISTANCE / max_exact)
                         * (half - max_exact)).astype(I32)
    large = jnp.minimum(large, half - 1)
    return ret + jnp.where(n < max_exact, n, large)


def _t5_bias(rel_bias, n_q, n_k):
    rel = jnp.arange(n_k)[None, :] - WINDOW - jnp.arange(n_q)[:, None]
    onehot = (_t5_bucket(rel)[..., None] == jnp.arange(N_BUCKETS)).astype(F32)
    return jnp.einsum('qkn,nh->hqk', onehot, rel_bias, precision=lax.Precision.HIGHEST)


def _route_kernel(x_ref, wr_ref, br_ref, idx_ref, gate_ref, rank_ref, cnt_ref, carry, *, tt):
    i = pl.program_id(0)

    @pl.when(i == 0)
    def _():
        carry[...] = jnp.zeros_like(carry)

    logits = lax.dot_general(wr_ref[...].astype(BF16), x_ref[...].astype(BF16), (((1,), (1,)), ((), ())),
                             preferred_element_type=F32) + br_ref[...]
    sub = lax.broadcasted_iota(I32, (N_EXP, tt), 0)
    out_row = lax.broadcasted_iota(I32, (8, tt), 0)
    vals, sels = [], []
    idx_out = jnp.zeros((8, tt), I32)
    work = logits
    for k in range(TOP_K):
        m = jnp.max(work, axis=0, keepdims=True)
        ik = jnp.min(jnp.where(work == m, sub, N_EXP), axis=0, keepdims=True)
        sel = sub == ik
        vals.append(m)
        sels.append(sel)
        idx_out = jnp.where(out_row == k, ik, idx_out)
        work = jnp.where(sel, -jnp.inf, work)
    es = [jnp.exp(v - vals[0]) for v in vals]
    den = es[0] + es[1] + es[2] + es[3]
    gate_out = jnp.zeros((8, tt), F32)
    for k in range(TOP_K):
        gate_out = jnp.where(out_row == k, es[k] / den, gate_out)
    onehot = jnp.zeros((N_EXP, tt), F32)
    for sel in sels:
        onehot = onehot + sel.astype(F32)
    before = (lax.broadcasted_iota(I32, (tt, tt), 0) < lax.broadcasted_iota(I32, (tt, tt), 1)).astype(BF16)
    base = carry[...] + jnp.dot(onehot.astype(BF16), before, preferred_element_type=F32)
    rank_out = jnp.zeros((8, tt), I32)
    for k in range(TOP_K):
        rk = jnp.sum(jnp.where(sels[k], base, 0.0), axis=0, keepdims=True)
        rank_out = jnp.where(out_row == k, rk.astype(I32), rank_out)
    carry[...] = carry[...] + jnp.sum(onehot, axis=1, keepdims=True)
    idx_ref[...] = idx_out
    gate_ref[...] = gate_out
    rank_ref[...] = rank_out
    cnt_ref[...] = carry[...]


def _route(x1, wr_t, br_col, *, tt):
    n = x1.shape[0]
    rows = pl.BlockSpec((8, tt), lambda i: (0, i))
    const = lambda a: pl.BlockSpec(a.shape, lambda i, _n=a.ndim: (0,) * _n)
    return pl.pallas_call(
        functools.partial(_route_kernel, tt=tt), grid=(n // tt,),
        in_specs=[pl.BlockSpec((tt, D), lambda i: (i, 0)), const(wr_t), const(br_col)],
        out_specs=[rows, rows, rows, pl.BlockSpec((N_EXP, 1), lambda i: (0, 0))],
        out_shape=[jax.ShapeDtypeStruct((8, n), I32), jax.ShapeDtypeStruct((8, n), F32),
                   jax.ShapeDtypeStruct((8, n), I32), jax.ShapeDtypeStruct((N_EXP, 1), F32)],
        scratch_shapes=[pltpu.VMEM((N_EXP, 1), F32)], name="moe_route",
        compiler_params=_cp(("arbitrary",)))(x1, wr_t, br_col)


assert D == SUB * LANES


def _rows_to_tiles(dst_ref, x, n):
    for s in range(SUB):
        dst_ref[pl.ds(s, n, stride=SUB), :] = x[:, LANES * s:LANES * (s + 1)]


def _tiles_to_rows(src_ref, n):
    return jnp.concatenate([src_ref[pl.ds(s, n, stride=SUB), :] for s in range(SUB)], axis=1)


def _tile_copy(src, s, dst, d, sem):
    return pltpu.make_async_copy(src.at[pl.ds(pl.multiple_of(s * SUB, SUB), SUB), :],
                                 dst.at[pl.ds(pl.multiple_of(d * SUB, SUB), SUB), :], sem)


ROW_UNROLL = 16


def _drain_rows(src, dst, sem, tt):
    def drain(r, carry):
        for k in range(TOP_K):
            _tile_copy(src, 0, dst, 0, sem).wait()
        return carry

    lax.fori_loop(0, tt, drain, 0, unroll=ROW_UNROLL)


def _disp_kernel(dest_ref, x_ref, xs_in, xs_ref, xt, sem, *, tt, nt):
    del xs_in
    i = pl.program_id(0)
    slot = lax.rem(i, 2)
    stage = xt.at[slot]
    _rows_to_tiles(stage, x_ref[...], tt)

    def issue(r, carry):
        for k in range(TOP_K):
            _tile_copy(stage, r, xs_ref, dest_ref[r * TOP_K + k], sem.at[slot]).start(priority=k % 2)
        return carry

    lax.fori_loop(0, tt, issue, 0, unroll=ROW_UNROLL)

    @pl.when(i > 0)
    def _():
        _drain_rows(xt.at[1 - slot], xs_ref, sem.at[1 - slot], tt)

    @pl.when(i == nt - 1)
    def _():
        _drain_rows(stage, xs_ref, sem.at[slot], tt)


def _dispatch(dest, x1, xs_prev, *, tt):
    n = x1.shape[0]
    smem_tok = pl.BlockSpec((tt * TOP_K,), lambda i: (i,), memory_space=pltpu.SMEM)
    return pl.pallas_call(
        functools.partial(_disp_kernel, tt=tt, nt=n // tt), grid=(n // tt,),
        in_specs=[smem_tok, pl.BlockSpec((tt, D), lambda i: (i, 0)), pl.BlockSpec(memory_space=pl.ANY)],
        out_specs=pl.BlockSpec(memory_space=pl.ANY),
        out_shape=jax.ShapeDtypeStruct(xs_prev.shape, F32),
        scratch_shapes=[pltpu.VMEM((2, tt * SUB, LANES), F32), pltpu.SemaphoreType.DMA((2,))],
        input_output_aliases={2: 0}, name="moe_dispatch",
        compiler_params=_cp(("arbitrary",)))(dest, x1, xs_prev)


def _expert_kernel(te_ref, nv_ref, xs_ref, w1_ref, b1_ref, w2_ref, b2_ref, y_ref, w1b, w2b):
    i = pl.program_id(0)
    valid = i < nv_ref[0]
    changed = jnp.logical_or(i == 0, te_ref[i] != te_ref[jnp.maximum(i - 1, 0)])

    @pl.when(jnp.logical_and(valid, changed))
    def _():
        for r0 in range(0, D, CAST_ROWS):
            w1b[r0:r0 + CAST_ROWS, :] = w1_ref[0, 0, r0:r0 + CAST_ROWS, :].astype(BF16)
            w2b[r0:r0 + CAST_ROWS, :] = w2_ref[0, 0, r0:r0 + CAST_ROWS, :].astype(BF16)

    @pl.when(valid)
    def _():
        x = _tiles_to_rows(xs_ref, EXP_TILE).astype(BF16)
        h = jnp.dot(x, w1b[...], preferred_element_type=F32) + b1_ref[0, 0]
        glu = jnp.minimum(h[:, :D], SWIGLU_LIMIT)
        lin = jnp.clip(h[:, D:], -SWIGLU_LIMIT, SWIGLU_LIMIT)
        act = glu * _sigmoid(SWIGLU_ALPHA * glu) * (lin + 1.0)
        y = jnp.dot(act.astype(BF16), w2b[...], preferred_element_type=F32) + b2_ref[0, 0]
        _rows_to_tiles(y_ref, y, EXP_TILE)

    @pl.when(jnp.logical_not(valid))
    def _():
        y_ref[...] = jnp.zeros_like(y_ref)


def _experts(tile_expert, n_valid, xs, w1, b1, w2, b2, layer):
    n_tiles = xs.shape[0] // (EXP_TILE * SUB)
    grid_spec = pltpu.PrefetchScalarGridSpec(
        num_scalar_prefetch=2, grid=(n_tiles,),
        in_specs=[pl.BlockSpec((EXP_TILE * SUB, LANES), lambda i, te, nv: (i, 0)),
                  pl.BlockSpec((1, 1, D, 2 * D), lambda i, te, nv: (layer, te[i], 0, 0)),
                  pl.BlockSpec((1, 1, 1, 2 * D), lambda i, te, nv: (layer, te[i], 0, 0)),
                  pl.BlockSpec((1, 1, D, D), lambda i, te, nv: (layer, te[i], 0, 0)),
                  pl.BlockSpec((1, 1, 1, D), lambda i, te, nv: (layer, te[i], 0, 0))],
        out_specs=pl.BlockSpec((EXP_TILE * SUB, LANES), lambda i, te, nv: (i, 0)),
        scratch_shapes=[pltpu.VMEM((D, 2 * D), BF16), pltpu.VMEM((D, D), BF16)])
    return pl.pallas_call(
        _expert_kernel, grid_spec=grid_spec, out_shape=jax.ShapeDtypeStruct(xs.shape, F32), name="moe_experts",
        compiler_params=_cp(("arbitrary",)))(tile_expert, n_valid, xs, w1, b1, w2, b2)


def _comb_kernel(dest_ref, next_ref, gate_ref, x1_ref, p_ref, y_ref, lg_ref, lb_ref, wg_ref, bg_ref,
                 wp_ref, o_ref, buf, sem, *, tt, nt):
    i = pl.program_id(0)
    slot = lax.rem(i, 2)

    def gather(idx_ref, s):
        def issue(r, carry):
            for k in range(TOP_K):
                _tile_copy(y_ref, idx_ref[r * TOP_K + k], buf.at[s, k], r, sem.at[s]).start(priority=k % 2)
            return carry

        lax.fori_loop(0, tt, issue, 0, unroll=ROW_UNROLL)

    @pl.when(i == 0)
    def _():
        gather(dest_ref, slot)

    @pl.when(i + 1 < nt)
    def _():
        gather(next_ref, 1 - slot)

    _drain_rows(y_ref, buf.at[slot, 0], sem.at[slot], tt)
    gate = gate_ref[...]
    moe = gate[:, 0:1] * _tiles_to_rows(buf.at[slot, 0], tt)
    for k in range(1, TOP_K):
        moe = moe + gate[:, k:k + 1] * _tiles_to_rows(buf.at[slot, k], tt)
    x2 = _ln(ALPHA * x1_ref[...] + moe, lg_ref[...], lb_ref[...])
    gt = _sigmoid(_bdot(x2, wg_ref[...]) + bg_ref[...])
    o_ref[...] = x2 + gt * _bdot(p_ref[...], wp_ref[...])


def _combine(dest, gate, x1, p, y, lnp, wg, bg, wp, *, tt):
    n = x1.shape[0]
    smem_tok = pl.BlockSpec((tt * TOP_K,), lambda i: (i,), memory_space=pltpu.SMEM)
    tok = lambda w: pl.BlockSpec((tt, w), lambda i: (i, 0))
    const = lambda a: pl.BlockSpec(a.shape, lambda i, _n=a.ndim: (0,) * _n)
    nt = n // tt
    smem_next = pl.BlockSpec((tt * TOP_K,), lambda i: (jnp.minimum(i + 1, nt - 1),), memory_space=pltpu.SMEM)
    return pl.pallas_call(
        functools.partial(_comb_kernel, tt=tt, nt=nt), grid=(nt,),
        in_specs=[smem_tok, smem_next, tok(TOP_K), tok(D), tok(p.shape[1]), pl.BlockSpec(memory_space=pl.ANY),
                  const(lnp[0]), const(lnp[1]), const(wg), const(bg), const(wp)],
        out_specs=tok(D), out_shape=jax.ShapeDtypeStruct((n, D), F32),
        scratch_shapes=[pltpu.VMEM((2, TOP_K, tt * SUB, LANES), F32), pltpu.SemaphoreType.DMA((2,))],
        name="moe_combine_ple",
        compiler_params=_cp(("arbitrary",)))(dest, dest, gate, x1, p, y, lnp[0], lnp[1], wg, bg, wp)


def _moe_ple(x1, p, xs_buf, wr_t, br_col, w1, b1, w2, b2, layer, lnp, wg, bg, wp, *, tt):
    n = x1.shape[0]
    idx, gate, rank, counts = _route(x1, wr_t, br_col, tt=512 if n % 512 == 0 else tt)
    counts = counts[:, 0].astype(I32)
    padded = (counts + EXP_TILE - 1) // EXP_TILE * EXP_TILE
    pad_end = jnp.cumsum(padded)
    pad_start = pad_end - padded
    n_tiles = xs_buf.shape[0] // (EXP_TILE * SUB)
    n_valid = (pad_end[-1] // EXP_TILE).astype(I32)
    tiles = jnp.minimum(jnp.arange(n_tiles, dtype=I32), n_valid - 1) * EXP_TILE
    tile_expert = jnp.minimum(jnp.sum((tiles[:, None] >= pad_end[None, :]).astype(I32), axis=1), N_EXP - 1)
    experts = jnp.arange(N_EXP, dtype=I32)
    start_of = jnp.sum(jnp.where(idx[:TOP_K, :, None] == experts, pad_start, 0), axis=-1)
    dest = (start_of + rank[:TOP_K]).T.reshape(-1)
    xs = _dispatch(dest, x1, xs_buf, tt=tt)
    y = _experts(tile_expert, n_valid.reshape(1), xs, w1, b1, w2, b2, layer)
    return _combine(dest, gate[:TOP_K].T, x1, p, y, lnp, wg, bg, wp, tt=tt), xs


def _state_to_scan(s, nb):
    s = s.reshape(nb // GROUP_ROWS, GROUP_ROWS, HEAD_PAIRS, 2, HD, HD).transpose(5, 4, 0, 3, 2, 1)
    return s.reshape(HD, HD, nb * HEADS)


def _state_from_scan(s, nb):
    s = s.reshape(HD, HD, nb // GROUP_ROWS, 2, HEAD_PAIRS, GROUP_ROWS).transpose(2, 5, 4, 3, 1, 0)
    return s.reshape(nb, HEADS, HD, HD)


def _head_vec_to_scan(a):
    a = a.reshape(HEAD_PAIRS, 2, HD).transpose(2, 1, 0)
    return jnp.broadcast_to(a[..., None], (HD, 2, HEAD_PAIRS, GROUP_ROWS)).reshape(HD, LANES)


def _row2(a):
    return a.reshape(1, -1)


def _tile(n):
    for tt in (256, 128, 64, 32, 16, 8):
        if n % tt == 0:
            return tt
    raise ValueError(n)


def kernel(x_prompt, x_sample, p_prompt, p_sample, cache_conv, state_rwkv_shift, state_rwkv_wkv, cache_swa_k, cache_swa_v, conv_w_in, conv_b_in, conv_w_dw, conv_b_dw, conv_ln_g, conv_ln_b, conv_w_out, conv_b_out, rwkv_mu, rwkv_w_rkv, rwkv_w0, rwkv_w1, rwkv_w2, rwkv_a0, rwkv_a1, rwkv_a2, rwkv_g1, rwkv_g2, rwkv_k_k, rwkv_k_a, rwkv_r_k, rwkv_lnx_g, rwkv_lnx_b, rwkv_w_o, attn_w_qkv, attn_b_qkv, attn_sinks, attn_w_o, attn_b_o, rel_bias, ln_g, ln_b, moe_w_router, moe_b_router, moe_w1, moe_b1, moe_w2, moe_b2, ple_w_proj, ple_w_gate, ple_b_gate):
    bp, tp, _ = x_prompt.shape
    bs, ts, _ = x_sample.shape
    n_p, n_s = bp * tp, bs * ts
    n = n_p + n_s
    assert tp % 128 == 0 and n_p % ts == 0 and ts % SUB == 0 and HALO <= ts <= CHUNK
    assert bp % GROUP_ROWS == 0 and bs % GROUP_ROWS == 0
    tt_tok = _tile(n)
    tt_dense = 512 if n % 512 == 0 else tt_tok
    tt_p = 256 if tp % 256 == 0 else 128
    tt_conv = 512 if tp % 512 == 0 else tt_p
    x = None
    p_all = jnp.concatenate([p_prompt.reshape(DEPTH, n_p, -1), p_sample.reshape(DEPTH, n_s, -1)], axis=1)
    n_rows = (-(-n * TOP_K // EXP_TILE) + N_EXP) * EXP_TILE
    xs_buf = jnp.zeros((n_rows * SUB, LANES), F32)
    head_sel = (jnp.arange(D)[:, None] // HD == jnp.arange(LANES)[None, :]).astype(BF16)
    head_sel_t = head_sel.T
    conv_p, conv_s, shift_p, shift_s, wkv_p, wkv_s = [], [], [], [], [], []
    swa_kp, swa_vp, swa_ks, swa_vs = [], [], [], []
    for i in range(DEPTH):
        kind, j = i % 3, i // 3
        lnp = (_row2(ln_g[i, 0]), _row2(ln_b[i, 0]))
        if kind == 0:
            cw = (conv_w_in[j].astype(BF16), _row2(conv_b_in[j]), conv_w_dw[j].astype(BF16).astype(F32),
                  _row2(conv_b_dw[j]),
                  _row2(conv_ln_g[j]), _row2(conv_ln_b[j]), conv_w_out[j].astype(BF16), _row2(conv_b_out[j]))
            st_p = jnp.zeros((bp, HALO, D), F32)
            st_s = jnp.pad(cache_conv[j], ((0, 0), (HALO - (CONV_W - 1), 0), (0, 0)))
            xin_p, xin_s = (x_prompt.reshape(n_p, D), x_sample.reshape(n_s, D)) if i == 0 else (x, x)
            x1, so_p = _conv_mixer(xin_p, st_p, cw, lnp, nb=bp, t=tp, tt=tt_conv, row_off=0, n_total=n, prev=None)
            x1, so_s = _conv_mixer(xin_s, st_s, cw, lnp, nb=bs, t=ts, tt=ts, row_off=n_p, n_total=n, prev=[x1])
            conv_p.append(so_p[:, HALO - (CONV_W - 1):])
            conv_s.append(so_s[:, HALO - (CONV_W - 1):])
        elif kind == 1:
            rw = (rwkv_mu[j], rwkv_w_rkv[j, 0].astype(BF16), rwkv_w_rkv[j, 1].astype(BF16),
                  rwkv_w_rkv[j, 2].astype(BF16), _row2(rwkv_w0[j]), rwkv_w1[j].astype(BF16),
                  rwkv_w2[j].astype(BF16), _row2(rwkv_a0[j]), rwkv_a1[j].astype(BF16), rwkv_a2[j].astype(BF16),
                  rwkv_g1[j].astype(BF16), rwkv_g2[j].astype(BF16), _row2(rwkv_k_k[j]), _row2(rwkv_k_a[j]))
            sh_p = jnp.zeros((bp, 1, D), F32)
            sh_s = state_rwkv_shift[j].reshape(bs, 1, D)
            post_c = [_head_vec_to_scan(a) for a in (rwkv_lnx_g[j], rwkv_lnx_b[j], rwkv_r_k[j].reshape(-1))]
            out_c = [('const', rwkv_w_o[j].astype(BF16)), ('const', lnp[0]), ('const', lnp[1])]
            x1, states = None, []
            for (lo, nb_, t_, tt_, sh, s0, tc) in ((0, bp, tp, tt_p, sh_p, None, SCAN_CHUNK),
                                                   (n_p, bs, ts, ts, sh_s, state_rwkv_wkv[j], ts)):
                r, w, k, v, an, b, g = _rw1(x, sh, rw, head_sel, head_sel_t, nb=nb_, t=t_, tt=tt_, row_off=lo)
                rs, ws, ks, vs, ans, bs_ = _to_scan_call([a.reshape(nb_, t_, D) for a in (r, w, k, v, an, b)],
                                                         tc=min(tc, RELAYOUT_CHUNK))
                s0l = jnp.zeros((HD, HD, nb_ * HEADS), F32) if s0 is None else _state_to_scan(s0, nb_)
                o_l, s_l = _scan(rs, ws, ks, vs, ans, bs_, s0l, tc=tc)
                y = _post(o_l, rs, ks, vs, *post_c, tc=tc).reshape(t_, nb_ * D)
                x1, = _seq_call(_rw3_kernel, name=f"rwkv_out_t{t_}", nb=nb_, nt=t_ // tt_, tt=tt_, row_off=lo,
                                n_total=n, ins=[('tm', y), ('own', g), ('tok', x)] + out_c,
                                outs=[('tok', (D,), F32)], scratch=[], prev=None if x1 is None else [x1])
                states.append(_state_from_scan(s_l, nb_))
            shift_p.append(x[tp - 1:n_p:tp])
            shift_s.append(x[n_p + ts - 1::ts])
            wkv_p.append(states[0])
            wkv_s.append(states[1])
        else:
            q, kx, vx = _tok_call(_qkv_kernel, "attn_qkv", n, tt_dense, [x],
                                  [attn_w_qkv[j].astype(BF16), _row2(attn_b_qkv[j])],
                                  [('heads', BF16), (KVH * HD, F32), (KVH * HD, F32)])
            k_p = kx[:n_p].reshape(bp, tp, KVH * HD)
            v_p = vx[:n_p].reshape(bp, tp, KVH * HD)
            zpad = jnp.zeros((bp, WINDOW, KVH * HD), F32)
            nc = tp // CHUNK
            band = WINDOW + CHUNK
            o = _attn(q, jnp.concatenate([zpad, k_p], axis=1), jnp.concatenate([zpad, v_p], axis=1),
                      _t5_bias(rel_bias, CHUNK, band), attn_sinks[j], nb=bp, nt=nc, nq=CHUNK, nsub=1, kb=band,
                      mask_lo=WINDOW, row_off=0, prev=None)
            k_all = jnp.concatenate([cache_swa_k[j].reshape(bs, WINDOW, KVH * HD),
                                     kx[n_p:].reshape(bs, ts, KVH * HD)], axis=1)
            v_all = jnp.concatenate([cache_swa_v[j].reshape(bs, WINDOW, KVH * HD),
                                     vx[n_p:].reshape(bs, ts, KVH * HD)], axis=1)
            o = _attn(q, k_all, v_all, _t5_bias(rel_bias, ts, WINDOW + ts), attn_sinks[j],
                      nb=bs, nt=1, nq=ts, nsub=1, kb=WINDOW + ts, mask_lo=0, row_off=n_p, prev=o)
            x1, = _tok_call(_oproj_kernel, "attn_out", n, tt_dense, [o, x],
                            [attn_w_o[j].astype(BF16), _row2(attn_b_o[j]), lnp[0], lnp[1]], [(D, F32)])
            swa_kp.append(k_p[:, -WINDOW:].reshape(bp, WINDOW, KVH, HD))
            swa_vp.append(v_p[:, -WINDOW:].reshape(bp, WINDOW, KVH, HD))
            swa_ks.append(k_all[:, -WINDOW:].reshape(bs, WINDOW, KVH, HD))
            swa_vs.append(v_all[:, -WINDOW:].reshape(bs, WINDOW, KVH, HD))
        x, xs_buf = _moe_ple(x1, p_all[i], xs_buf, moe_w_router[i].T, moe_b_router[i].reshape(N_EXP, 1), moe_w1,
                             moe_b1.reshape(DEPTH, N_EXP, 1, 2 * D), moe_w2, moe_b2.reshape(DEPTH, N_EXP, 1, D), i,
                             (_row2(ln_g[i, 1]), _row2(ln_b[i, 1])),
                             ple_w_gate[i].astype(BF16), _row2(ple_b_gate[i]), ple_w_proj[i].astype(BF16), tt=tt_tok)
    return (x[:n_p].reshape(bp, tp, D), x[n_p:].reshape(bs, ts, D), jnp.stack(conv_p), jnp.stack(conv_s),
            jnp.stack(shift_p), jnp.stack(shift_s), jnp.stack(wkv_p), jnp.stack(wkv_s), jnp.stack(swa_kp),
            jnp.stack(swa_vp), jnp.stack(swa_ks), jnp.stack(swa_vs))
```

```python
import functools
import math

import jax
import jax.numpy as jnp
from jax import lax
from jax.experimental import pallas as pl
from jax.experimental.pallas import tpu as pltpu

F32 = jnp.float32
BF16 = jnp.bfloat16
I32 = jnp.int32

D = 1024
DEPTH = 4
CONV_W = 31
HALO = 32
HEADS = 16
HD = 64
KVH = 2
GROUP = HEADS // KVH
WINDOW = 128
CHUNK = 64
N_BUCKETS = 32
MAX_DISTANCE = 128
N_EXP = 32
TOP_K = 4
EXP_TILE = 512
CAST_ROWS = 256
LANES = 128
SUB = 8
HEAD_PAIRS = HEADS // 2
GROUP_ROWS = LANES // HEADS
SCAN_CHUNK = 64
RELAYOUT_CHUNK = 32
LNX_EPS = 64e-5
LN_EPS = 1e-5
ALPHA = (2 * DEPTH) ** 0.25
SWIGLU_ALPHA = 1.702
SWIGLU_LIMIT = 7.0
VMEM_LIMIT = 56 * 1024 * 1024


def _cp(sem):
    return pltpu.CompilerParams(dimension_semantics=sem, vmem_limit_bytes=VMEM_LIMIT)


def _bdot(a, b):
    return jnp.dot(a.astype(BF16), b.astype(BF16), preferred_element_type=F32)


def _split(a):
    hi = a.astype(BF16)
    lo = (a - hi.astype(F32)).astype(BF16)
    return hi, lo


def _split_dot(a, b_exact):
    hi, lo = _split(a)
    return (jnp.dot(hi, b_exact, preferred_element_type=F32)
            + jnp.dot(lo, b_exact, preferred_element_type=F32))


def _ln(x, g, b, eps=LN_EPS):
    mu = jnp.mean(x, axis=-1, keepdims=True)
    xc = x - mu
    var = jnp.mean(xc * xc, axis=-1, keepdims=True)
    return xc * lax.rsqrt(var + eps) * g + b


def _sigmoid(x):
    return 1.0 / (1.0 + jnp.exp(-x))


def _seq_call(body, *, name, nb, nt, tt, row_off, n_total, ins, outs, scratch, prev=None):
    off = row_off // tt
    in_specs, args = [], []
    for kind, a in ins:
        if kind == 'tok':
            in_specs.append(pl.BlockSpec((tt, a.shape[1]), lambda b, j: (off + b * nt + j, 0)))
        elif kind == 'own':
            in_specs.append(pl.BlockSpec((tt, a.shape[1]), lambda b, j: (b * nt + j, 0)))
        elif kind == 'tm':
            in_specs.append(pl.BlockSpec((tt, a.shape[1] // nb), lambda b, j: (j, b)))
        elif kind == 'bat':
            in_specs.append(pl.BlockSpec((1,) + a.shape[1:], lambda b, j: (b, 0, 0)))
        else:
            in_specs.append(pl.BlockSpec(a.shape, lambda b, j, _n=a.ndim: (0,) * _n))
        args.append(a)
    out_specs, out_shapes = [], []
    for kind, tail, dt in outs:
        if kind == 'tok':
            out_specs.append(pl.BlockSpec((tt, tail[0]), lambda b, j: (off + b * nt + j, 0)))
            out_shapes.append(jax.ShapeDtypeStruct((n_total, tail[0]), dt))
        elif kind == 'own':
            out_specs.append(pl.BlockSpec((tt, tail[0]), lambda b, j: (b * nt + j, 0)))
            out_shapes.append(jax.ShapeDtypeStruct((nb * nt * tt, tail[0]), dt))
        elif kind == 'tm':
            out_specs.append(pl.BlockSpec((tt, tail[0]), lambda b, j: (j, b)))
            out_shapes.append(jax.ShapeDtypeStruct((nt * tt, nb * tail[0]), dt))
        else:
            out_specs.append(pl.BlockSpec((1,) + tuple(tail), lambda b, j: (b, 0, 0)))
            out_shapes.append(jax.ShapeDtypeStruct((nb,) + tuple(tail), dt))
    aliases = {}
    n_prev = 0
    tok_out = [i for i, o in enumerate(outs) if o[0] == 'tok']
    if prev is None:
        prev = [jnp.zeros((n_total, outs[i][1][0]), outs[i][2]) for i in tok_out]
    for p, oi in zip(prev, tok_out):
        aliases[len(args)] = oi
        in_specs.append(pl.BlockSpec(memory_space=pl.ANY))
        args.append(p)
        n_prev += 1
    n_in = len(ins)

    def wrapped(*refs):
        body(*refs[:n_in], *refs[n_in + n_prev:])

    return pl.pallas_call(
        wrapped, grid=(nb, nt), in_specs=in_specs, out_specs=out_specs, out_shape=out_shapes,
        scratch_shapes=scratch, input_output_aliases=aliases, name=name,
        compiler_params=_cp(("arbitrary", "arbitrary")))(*args)


CONV_RC = 64
CONV_LC = 256


def _conv_kernel(x_ref, st_ref, win_ref, bin_ref, wdw_ref, bdw_ref, cg_ref, cb_ref, wout_ref,
                 bout_ref, lg_ref, lb_ref, o_ref, so_ref, ubuf, ybuf, *, tt, nt):
    j = pl.program_id(1)

    @pl.when(j == 0)
    def _():
        ubuf[0:HALO, :] = st_ref[0].astype(BF16).astype(F32)

    @pl.when(j > 0)
    def _():
        ubuf[0:HALO, :] = ubuf[tt:tt + HALO, :]

    x = x_ref[...]
    h = _bdot(x, win_ref[...]) + bin_ref[...]
    u = h[:, :D] * _sigmoid(h[:, D:])
    ubuf[HALO:HALO + tt, :] = u.astype(BF16).astype(F32)

    @pl.when(j == nt - 1)
    def _():
        so_ref[0] = u[tt - HALO:tt, :]
    first = HALO - (CONV_W - 1)
    rc = min(CONV_RC, tt)
    for r0 in range(0, tt, rc):
        for c0 in range(0, D, CONV_LC):
            y = jnp.zeros((rc, CONV_LC), F32) + bdw_ref[:, c0:c0 + CONV_LC]
            for off in range(SUB):
                rows = rc + (SUB if off else 0)
                acc = None
                for tap in range(CONV_W):
                    if (first + tap) % SUB != off:
                        continue
                    base = r0 + (first + tap) // SUB * SUB
                    term = wdw_ref[tap:tap + 1, c0:c0 + CONV_LC] * ubuf[base:base + rows, c0:c0 + CONV_LC]
                    acc = term if acc is None else acc + term
                y = y + acc[off:off + rc]
            ybuf[r0:r0 + rc, c0:c0 + CONV_LC] = y
    z = _ln(ybuf[...], cg_ref[...], cb_ref[...])
    z = z * _sigmoid(z)
    mix = _bdot(z, wout_ref[...]) + bout_ref[...]
    o_ref[...] = _ln(ALPHA * x + mix, lg_ref[...], lb_ref[...])


def _conv_mixer(x, state, w, lnp, *, nb, t, tt, row_off, n_total, prev):
    nt = t // tt
    win, bin_, wdw, bdw, cg, cb, wout, bout = w
    ins = [('tok' if x.shape[0] == n_total else 'own', x), ('bat', state), ('const', win), ('const', bin_),
           ('const', wdw), ('const', bdw),
           ('const', cg), ('const', cb), ('const', wout), ('const', bout), ('const', lnp[0]), ('const', lnp[1])]
    outs = [('tok', (D,), F32), ('bat', (HALO, D), F32)]
    scratch = [pltpu.VMEM((HALO + tt, D), F32), pltpu.VMEM((tt, D), F32)]
    return _seq_call(functools.partial(_conv_kernel, tt=tt, nt=nt), name=f"conv_t{t}", nb=nb, nt=nt, tt=tt,
                     row_off=row_off,
                     n_total=n_total, ins=ins, outs=outs, scratch=scratch, prev=prev)


def _head_sum(y, hs_ref, hst_ref):
    s = _split_dot(y, hs_ref[...])
    return _split_dot(s, hst_ref[...])


def _rw1_kernel(x_ref, sh_ref, mu_ref, wr_ref, wk_ref, wv_ref, w0_ref, w1_ref, w2_ref, a0_ref, a1_ref,
                a2_ref, g1_ref, g2_ref, kk_ref, ka_ref, hs_ref, hst_ref,
                r_ref, w_ref, k_ref, v_ref, an_ref, b_ref, g_ref, xbuf, *, tt):
    j = pl.program_id(1)

    @pl.when(j == 0)
    def _():
        xbuf[SUB - 1:SUB, :] = sh_ref[0]

    @pl.when(j > 0)
    def _():
        xbuf[SUB - 1:SUB, :] = xbuf[SUB - 1 + tt:SUB + tt, :]

    x = x_ref[...]
    xbuf[SUB:SUB + tt, :] = x
    xx = xbuf[SUB - 1:SUB - 1 + tt, :] - x
    mu = mu_ref[...]
    xr = x + xx * mu[0:1]
    xw = x + xx * mu[1:2]
    xk = x + xx * mu[2:3]
    xv = x + xx * mu[3:4]
    xa = x + xx * mu[4:5]
    xg = x + xx * mu[5:6]
    r = _bdot(xr, wr_ref[...])
    k = _bdot(xk, wk_ref[...])
    v = _bdot(xv, wv_ref[...])
    lw = w0_ref[...] + _bdot(jnp.tanh(_bdot(xw, w1_ref[...])), w2_ref[...])
    z = -lw
    log_w = -(jnp.maximum(z, 0.0) + jnp.log(1.0 + jnp.exp(-jnp.abs(z)))) - 0.5
    a = _sigmoid(a0_ref[...] + _bdot(_bdot(xa, a1_ref[...]), a2_ref[...]))
    g = _bdot(_sigmoid(_bdot(xg, g1_ref[...])), g2_ref[...])
    kk = k * kk_ref[...]
    ss = _head_sum(kk * kk, hs_ref, hst_ref)
    kk = kk * lax.rsqrt(jnp.maximum(ss, 1e-24))
    r_ref[...] = r
    w_ref[...] = jnp.exp(-jnp.exp(log_w))
    k_ref[...] = k * (1.0 + (a - 1.0) * ka_ref[...])
    v_ref[...] = v
    an_ref[...] = -kk
    b_ref[...] = kk * a
    g_ref[...] = g


def _rw1(x, shift, w, hs, hst, *, nb, t, tt, row_off):
    nt = t // tt
    prev = []
    ins = [('tok', x), ('bat', shift)] + [('const', a) for a in w] + [('const', hs), ('const', hst)]
    outs = [('own', (D,), F32)] * (N_SCAN_IN + 1)
    scratch = [pltpu.VMEM((SUB + tt, D), F32)]
    return _seq_call(functools.partial(_rw1_kernel, tt=tt), name=f"rwkv_proj_t{t}", nb=nb, nt=nt, tt=tt,
                     row_off=row_off,
                     n_total=x.shape[0], ins=ins, outs=outs, scratch=scratch, prev=prev)


def _scan_kernel(r_ref, w_ref, k_ref, v_ref, an_ref, b_ref, nxt_ref, s0_ref, o_ref, st_ref, S, sa_buf,
                 *, tc, nc):
    c = pl.program_id(1)

    @pl.when(c == 0)
    def _():
        S[...] = s0_ref[...]

        def init(kk, acc):
            return acc + S[kk] * an_ref[0, pl.ds(kk, 1), :]

        sa_buf[...] = lax.fori_loop(0, HD, init, jnp.zeros((HD, LANES), F32))

    def step(t, sa, a_next):
        vt = v_ref[t]

        def kbody(kk, acc):
            o_acc, sa_acc = acc
            sk = (S[kk] * w_ref[t, pl.ds(kk, 1), :] + sa * b_ref[t, pl.ds(kk, 1), :]
                  + vt * k_ref[t, pl.ds(kk, 1), :])
            S[kk] = sk
            return (o_acc + sk * r_ref[t, pl.ds(kk, 1), :], sa_acc + sk * a_next(kk))

        zero = jnp.zeros((HD, LANES), F32)
        o_acc, sa_next = lax.fori_loop(0, HD, kbody, (zero, zero), unroll=32)
        o_ref[t] = o_acc
        return sa_next

    sa = lax.fori_loop(0, tc - 1, lambda t, sa: step(t, sa, lambda kk: an_ref[t + 1, pl.ds(kk, 1), :]),
                       sa_buf[...])
    sa_buf[...] = step(tc - 1, sa, lambda kk: nxt_ref[0, pl.ds(kk, 1), :])

    @pl.when(c == nc - 1)
    def _():
        st_ref[...] = S[...]


def _scan(r, w, k, v, an, b, s0, *, tc):
    t, _, lanes = r.shape
    ng, nc = lanes // LANES, t // tc
    seq = pl.BlockSpec((tc, HD, LANES), lambda g, c: (c, 0, g))
    nxt = pl.BlockSpec((1, HD, LANES), lambda g, c: (jnp.minimum((c + 1) * tc, t - 1), 0, g))
    return pl.pallas_call(
        functools.partial(_scan_kernel, tc=tc, nc=nc), grid=(ng, nc),
        in_specs=[seq] * 6 + [nxt, pl.BlockSpec((HD, HD, LANES), lambda g, c: (0, 0, g))],
        out_specs=[seq, pl.BlockSpec((HD, HD, LANES), lambda g, c: (0, 0, g))],
        out_shape=[jax.ShapeDtypeStruct((t, HD, lanes), F32), jax.ShapeDtypeStruct((HD, HD, lanes), F32)],
        scratch_shapes=[pltpu.VMEM((HD, HD, LANES), F32), pltpu.VMEM((HD, LANES), F32)],
        name=f"wkv_scan_t{t}", compiler_params=_cp(("arbitrary", "arbitrary")))(r, w, k, v, an, b, an, s0)


N_SCAN_IN = 6


def _to_scan_kernel(*refs, tc):
    low = lax.broadcasted_iota(I32, (HD, LANES), 1) < HD

    def pair(i, carry):
        t = i * 2
        for src, dst in zip(refs[:N_SCAN_IN], refs[N_SCAN_IN:]):
            y0, y1 = src[:, t, :], src[:, t + 1, :]
            m = jnp.concatenate([y[:, LANES * p:LANES * (p + 1)] for y in (y0, y1) for p in range(HEAD_PAIRS)],
                                axis=0).T
            top, bot = m[:HD], m[HD:]
            dst[t] = jnp.where(low, top, pltpu.roll(bot, HD, 1))
            dst[t + 1] = jnp.where(low, pltpu.roll(top, HD, 1), bot)
        return carry

    lax.fori_loop(0, tc // 2, pair, 0, unroll=2)


def _to_scan_call(arrs, *, tc):
    nb, t, _ = arrs[0].shape
    ng, nc = nb // GROUP_ROWS, t // tc
    return pl.pallas_call(
        functools.partial(_to_scan_kernel, tc=tc), grid=(ng, nc),
        in_specs=[pl.BlockSpec((GROUP_ROWS, tc, D), lambda g, c: (g, c, 0))] * N_SCAN_IN,
        out_specs=[pl.BlockSpec((tc, HD, LANES), lambda g, c: (c, 0, g))] * N_SCAN_IN,
        out_shape=[jax.ShapeDtypeStruct((t, HD, ng * LANES), F32)] * N_SCAN_IN, name=f"wkv_relayout_t{t}",
        compiler_params=_cp(("arbitrary", "arbitrary")))(*arrs)


def _post_kernel(o_ref, r_ref, k_ref, v_ref, xg_ref, xb_ref, rk_ref, y_ref, *, tc):
    low = lax.broadcasted_iota(I32, (HD, LANES), 1) < HD

    def norm(t):
        o = o_ref[t]
        d = o - jnp.mean(o, axis=0, keepdims=True)
        vo = jnp.mean(d * d, axis=0, keepdims=True)
        bonus = jnp.sum(r_ref[t] * k_ref[t] * rk_ref[...], axis=0, keepdims=True) * v_ref[t]
        return d * lax.rsqrt(vo + LNX_EPS) * xg_ref[...] + xb_ref[...] + bonus

    def pair(i, carry):
        t = i * 2
        z0, z1 = norm(t), norm(t + 1)
        top = jnp.where(low, z0, pltpu.roll(z1, HD, 1))
        bot = jnp.where(low, pltpu.roll(z0, HD, 1), z1)
        m = jnp.concatenate([top, bot], axis=0).T
        rows = lambda base, p: m[base + GROUP_ROWS * p:base + GROUP_ROWS * (p + 1)]
        y_ref[t] = jnp.concatenate([rows(0, p) for p in range(HEAD_PAIRS)], axis=1)
        y_ref[t + 1] = jnp.concatenate([rows(HD, p) for p in range(HEAD_PAIRS)], axis=1)
        return carry

    lax.fori_loop(0, tc // 2, pair, 0, unroll=4)


def _post(o, r, k, v, xg, xb, rk, *, tc):
    t, _, lanes = o.shape
    ng, nc = lanes // LANES, t // tc
    seq = pl.BlockSpec((tc, HD, LANES), lambda g, c: (c, 0, g))
    const = pl.BlockSpec((HD, LANES), lambda g, c: (0, 0))
    return pl.pallas_call(
        functools.partial(_post_kernel, tc=tc), grid=(ng, nc),
        in_specs=[seq] * 4 + [const] * 3,
        out_specs=pl.BlockSpec((tc, GROUP_ROWS, D), lambda g, c: (c, g, 0)),
        out_shape=jax.ShapeDtypeStruct((t, ng * GROUP_ROWS, D), F32), name=f"wkv_post_t{t}",
        compiler_params=_cp(("arbitrary", "arbitrary")))(o, r, k, v, xg, xb, rk)


def _rw3_kernel(y_ref, g_ref, x_ref, wo_ref, lg_ref, lb_ref, out_ref):
    mix = _bdot(y_ref[...] * g_ref[...], wo_ref[...])
    out_ref[...] = _ln(ALPHA * x_ref[...] + mix, lg_ref[...], lb_ref[...])


def _head_major_spec(tt):
    return pl.BlockSpec((HEADS, tt, HD), lambda i: (0, i, 0))


def _tok_call(body, name, n, tt, toks, consts, outs):
    in_specs = [_head_major_spec(tt) if a.ndim == 3 else pl.BlockSpec((tt, a.shape[1]), lambda i: (i, 0))
                for a in toks]
    in_specs += [pl.BlockSpec(a.shape, lambda i, _n=a.ndim: (0,) * _n) for a in consts]
    out_specs = [_head_major_spec(tt) if w == 'heads' else pl.BlockSpec((tt, w), lambda i: (i, 0))
                 for w, _ in outs]
    out_shape = [jax.ShapeDtypeStruct((HEADS, n, HD) if w == 'heads' else (n, w), dt) for w, dt in outs]
    return pl.pallas_call(
        body, grid=(n // tt,), in_specs=in_specs, out_specs=out_specs, out_shape=out_shape, name=name,
        compiler_params=_cp(("arbitrary",)))(*toks, *consts)


def _qkv_kernel(x_ref, w_ref, b_ref, q_ref, k_ref, v_ref):
    h = _bdot(x_ref[...], w_ref[...]) + b_ref[...]
    for hd in range(HEADS):
        q_ref[hd] = h[:, hd * HD:(hd + 1) * HD].astype(BF16)
    k_ref[...] = h[:, HEADS * HD:HEADS * HD + KVH * HD]
    v_ref[...] = h[:, HEADS * HD + KVH * HD:]


def _attn_kernel(q_ref, kp_ref, vp_ref, bias_ref, sink_ref, prev_ref, o_ref, *, nq, nsub, kb, mask_lo):
    del prev_ref
    c = pl.program_id(1)
    start = pl.multiple_of(c * (nsub * nq), SUB)
    span = kb + (nsub - 1) * nq
    kall = kp_ref[0, pl.ds(start, span), :].astype(BF16)
    vall = vp_ref[0, pl.ds(start, span), :].astype(BF16)
    for s in range(nsub):
        rows = slice(s * nq, (s + 1) * nq)
        valid = (start + s * nq + lax.broadcasted_iota(I32, (1, kb), 1)) >= mask_lo
        for g in range(KVH):
            hs = slice(g * GROUP, (g + 1) * GROUP)
            qg = q_ref[hs, rows, :].reshape(GROUP * nq, HD)
            kh = kall[s * nq:s * nq + kb, g * HD:(g + 1) * HD]
            vh = vall[s * nq:s * nq + kb, g * HD:(g + 1) * HD]
            logits = lax.dot_general(qg, kh, (((1,), (1,)), ((), ())), preferred_element_type=F32) * HD ** -0.5
            logits = jnp.where(valid, logits + bias_ref[hs].reshape(GROUP * nq, kb), -1e30)
            sink = sink_ref[hs].reshape(GROUP * nq, 1)
            m = jnp.maximum(jnp.max(logits, axis=-1, keepdims=True), sink)
            p = jnp.exp(logits - m)
            p = p / (jnp.sum(p, axis=-1, keepdims=True) + jnp.exp(sink - m))
            og = jnp.dot(p.astype(BF16), vh, preferred_element_type=F32)
            o_ref[hs, rows, :] = og.reshape(GROUP, nq, HD).astype(BF16)


def _attn(q, kp, vp, bias, sinks, *, nb, nt, nq, nsub, kb, mask_lo, row_off, prev):
    off = row_off // (nq * nsub)
    n_total = q.shape[1]
    heads = pl.BlockSpec((HEADS, nq * nsub, HD), lambda b, c: (0, off + b * nt + c, 0))
    const = lambda a: pl.BlockSpec(a.shape, lambda b, c, _n=a.ndim: (0,) * _n)
    bat = lambda a: pl.BlockSpec((1,) + a.shape[1:], lambda b, c: (b, 0, 0))
    sink_tab = jnp.broadcast_to(sinks.reshape(HEADS, 1, 1), (HEADS, nq, 1))
    if prev is None:
        prev = jnp.zeros((HEADS, n_total, HD), BF16)
    return pl.pallas_call(
        functools.partial(_attn_kernel, nq=nq, nsub=nsub, kb=kb, mask_lo=mask_lo), grid=(nb, nt),
        in_specs=[heads, bat(kp), bat(vp), const(bias), const(sink_tab), pl.BlockSpec(memory_space=pl.ANY)],
        out_specs=heads, out_shape=jax.ShapeDtypeStruct((HEADS, n_total, HD), BF16),
        input_output_aliases={5: 0}, name=f"attn_q{nq}",
        compiler_params=_cp(("arbitrary", "arbitrary")))(q, kp, vp, bias, sink_tab, prev)


def _oproj_kernel(o_ref, x_ref, wo_ref, bo_ref, lg_ref, lb_ref, out_ref):
    acc = jnp.dot(o_ref[0], wo_ref[0:HD, :], preferred_element_type=F32)
    for hd in range(1, HEADS):
        acc = acc + jnp.dot(o_ref[hd], wo_ref[hd * HD:(hd + 1) * HD, :], preferred_element_type=F32)
    out_ref[...] = _ln(ALPHA * x_ref[...] + acc + bo_ref[...], lg_ref[...], lb_ref[...])


def _t5_bucket(rel):
    half = N_BUCKETS // 2
    max_exact = half // 2
    ret = jnp.where(rel > 0, half, 0)
    n = jnp.abs(rel)
    nf = jnp.maximum(n, 1).astype(F32)
    large = max_exact + (jnp.log(nf / max_exact) / math.log(MAX_DISTANCE / max_exact)
                         * (half - max_exact)).astype(I32)
    large = jnp.minimum(large, half - 1)
    return ret + jnp.where(n < max_exact, n, large)


def _t5_bias(rel_bias, n_q, n_k):
    rel = jnp.arange(n_k)[None, :] - WINDOW - jnp.arange(n_q)[:, None]
    onehot = (_t5_bucket(rel)[..., None] == jnp.arange(N_BUCKETS)).astype(F32)
    return jnp.einsum('qkn,nh->hqk', onehot, rel_bias, precision=lax.Precision.HIGHEST)


def _route_kernel(x_ref, wr_ref, br_ref, idx_ref, gate_ref, rank_ref, cnt_ref, carry, *, tt):
    i = pl.program_id(0)

    @pl.when(i == 0)
    def _():
        carry[...] = jnp.zeros_like(carry)

    logits = lax.dot_general(wr_ref[...].astype(BF16), x_ref[...].astype(BF16), (((1,), (1,)), ((), ())),
                             preferred_element_type=F32) + br_ref[...]
    sub = lax.broadcasted_iota(I32, (N_EXP, tt), 0)
    out_row = lax.broadcasted_iota(I32, (8, tt), 0)
    vals, sels = [], []
    idx_out = jnp.zeros((8, tt), I32)
    work = logits
    for k in range(TOP_K):
        m = jnp.max(work, axis=0, keepdims=True)
        ik = jnp.min(jnp.where(work == m, sub, N_EXP), axis=0, keepdims=True)
        sel = sub == ik
        vals.append(m)
        sels.append(sel)
        idx_out = jnp.where(out_row == k, ik, idx_out)
        work = jnp.where(sel, -jnp.inf, work)
    es = [jnp.exp(v - vals[0]) for v in vals]
    den = es[0] + es[1] + es[2] + es[3]
    gate_out = jnp.zeros((8, tt), F32)
    for k in range(TOP_K):
        gate_out = jnp.where(out_row == k, es[k] / den, gate_out)
    onehot = jnp.zeros((N_EXP, tt), F32)
    for sel in sels:
        onehot = onehot + sel.astype(F32)
    before = (lax.broadcasted_iota(I32, (tt, tt), 0) < lax.broadcasted_iota(I32, (tt, tt), 1)).astype(BF16)
    base = carry[...] + jnp.dot(onehot.astype(BF16), before, preferred_element_type=F32)
    rank_out = jnp.zeros((8, tt), I32)
    for k in range(TOP_K):
        rk = jnp.sum(jnp.where(sels[k], base, 0.0), axis=0, keepdims=True)
        rank_out = jnp.where(out_row == k, rk.astype(I32), rank_out)
    carry[...] = carry[...] + jnp.sum(onehot, axis=1, keepdims=True)
    idx_ref[...] = idx_out
    gate_ref[...] = gate_out
    rank_ref[...] = rank_out
    cnt_ref[...] = carry[...]


def _route(x1, wr_t, br_col, *, tt):
    n = x1.shape[0]
    rows = pl.BlockSpec((8, tt), lambda i: (0, i))
    const = lambda a: pl.BlockSpec(a.shape, lambda i, _n=a.ndim: (0,) * _n)
    return pl.pallas_call(
        functools.partial(_route_kernel, tt=tt), grid=(n // tt,),
        in_specs=[pl.BlockSpec((tt, D), lambda i: (i, 0)), const(wr_t), const(br_col)],
        out_specs=[rows, rows, rows, pl.BlockSpec((N_EXP, 1), lambda i: (0, 0))],
        out_shape=[jax.ShapeDtypeStruct((8, n), I32), jax.ShapeDtypeStruct((8, n), F32),
                   jax.ShapeDtypeStruct((8, n), I32), jax.ShapeDtypeStruct((N_EXP, 1), F32)],
        scratch_shapes=[pltpu.VMEM((N_EXP, 1), F32)], name="moe_route",
        compiler_params=_cp(("arbitrary",)))(x1, wr_t, br_col)


assert D == SUB * LANES


def _rows_to_tiles(dst_ref, x, n):
    for s in range(SUB):
        dst_ref[pl.ds(s, n, stride=SUB), :] = x[:, LANES * s:LANES * (s + 1)]


def _tiles_to_rows(src_ref, n):
    return jnp.concatenate([src_ref[pl.ds(s, n, stride=SUB), :] for s in range(SUB)], axis=1)


def _tile_copy(src, s, dst, d, sem):
    return pltpu.make_async_copy(src.at[pl.ds(pl.multiple_of(s * SUB, SUB), SUB), :],
                                 dst.at[pl.ds(pl.multiple_of(d * SUB, SUB), SUB), :], sem)


ROW_UNROLL = 16


def _drain_rows(src, dst, sem, tt):
    def drain(r, carry):
        for k in range(TOP_K):
            _tile_copy(src, 0, dst, 0, sem).wait()
        return carry

    lax.fori_loop(0, tt, drain, 0, unroll=ROW_UNROLL)


def _disp_kernel(dest_ref, x_ref, xs_in, xs_ref, xt, sem, *, tt, nt):
    del xs_in
    i = pl.program_id(0)
    slot = lax.rem(i, 2)
    stage = xt.at[slot]
    _rows_to_tiles(stage, x_ref[...], tt)

    def issue(r, carry):
        for k in range(TOP_K):
            _tile_copy(stage, r, xs_ref, dest_ref[r * TOP_K + k], sem.at[slot]).start(priority=k % 2)
        return carry

    lax.fori_loop(0, tt, issue, 0, unroll=ROW_UNROLL)

    @pl.when(i > 0)
    def _():
        _drain_rows(xt.at[1 - slot], xs_ref, sem.at[1 - slot], tt)

    @pl.when(i == nt - 1)
    def _():
        _drain_rows(stage, xs_ref, sem.at[slot], tt)


def _dispatch(dest, x1, xs_prev, *, tt):
    n = x1.shape[0]
    smem_tok = pl.BlockSpec((tt * TOP_K,), lambda i: (i,), memory_space=pltpu.SMEM)
    return pl.pallas_call(
        functools.partial(_disp_kernel, tt=tt, nt=n // tt), grid=(n // tt,),
        in_specs=[smem_tok, pl.BlockSpec((tt, D), lambda i: (i, 0)), pl.BlockSpec(memory_space=pl.ANY)],
        out_specs=pl.BlockSpec(memory_space=pl.ANY),
        out_shape=jax.ShapeDtypeStruct(xs_prev.shape, F32),
        scratch_shapes=[pltpu.VMEM((2, tt * SUB, LANES), F32), pltpu.SemaphoreType.DMA((2,))],
        input_output_aliases={2: 0}, name="moe_dispatch",
        compiler_params=_cp(("arbitrary",)))(dest, x1, xs_prev)


def _expert_kernel(te_ref, nv_ref, xs_ref, w1_ref, b1_ref, w2_ref, b2_ref, y_ref, w1b, w2b):
    i = pl.program_id(0)
    valid = i < nv_ref[0]
    changed = jnp.logical_or(i == 0, te_ref[i] != te_ref[jnp.maximum(i - 1, 0)])

    @pl.when(jnp.logical_and(valid, changed))
    def _():
        for r0 in range(0, D, CAST_ROWS):
            w1b[r0:r0 + CAST_ROWS, :] = w1_ref[0, 0, r0:r0 + CAST_ROWS, :].astype(BF16)
            w2b[r0:r0 + CAST_ROWS, :] = w2_ref[0, 0, r0:r0 + CAST_ROWS, :].astype(BF16)

    @pl.when(valid)
    def _():
        x = _tiles_to_rows(xs_ref, EXP_TILE).astype(BF16)
        h = jnp.dot(x, w1b[...], preferred_element_type=F32) + b1_ref[0, 0]
        glu = jnp.minimum(h[:, :D], SWIGLU_LIMIT)
        lin = jnp.clip(h[:, D:], -SWIGLU_LIMIT, SWIGLU_LIMIT)
        act = glu * _sigmoid(SWIGLU_ALPHA * glu) * (lin + 1.0)
        y = jnp.dot(act.astype(BF16), w2b[...], preferred_element_type=F32) + b2_ref[0, 0]
        _rows_to_tiles(y_ref, y, EXP_TILE)

    @pl.when(jnp.logical_not(valid))
    def _():
        y_ref[...] = jnp.zeros_like(y_ref)


def _experts(tile_expert, n_valid, xs, w1, b1, w2, b2, layer):
    n_tiles = xs.shape[0] // (EXP_TILE * SUB)
    grid_spec = pltpu.PrefetchScalarGridSpec(
        num_scalar_prefetch=2, grid=(n_tiles,),
        in_specs=[pl.BlockSpec((EXP_TILE * SUB, LANES), lambda i, te, nv: (i, 0)),
                  pl.BlockSpec((1, 1, D, 2 * D), lambda i, te, nv: (layer, te[i], 0, 0)),
                  pl.BlockSpec((1, 1, 1, 2 * D), lambda i, te, nv: (layer, te[i], 0, 0)),
                  pl.BlockSpec((1, 1, D, D), lambda i, te, nv: (layer, te[i], 0, 0)),
                  pl.BlockSpec((1, 1, 1, D), lambda i, te, nv: (layer, te[i], 0, 0))],
        out_specs=pl.BlockSpec((EXP_TILE * SUB, LANES), lambda i, te, nv: (i, 0)),
        scratch_shapes=[pltpu.VMEM((D, 2 * D), BF16), pltpu.VMEM((D, D), BF16)])
    return pl.pallas_call(
        _expert_kernel, grid_spec=grid_spec, out_shape=jax.ShapeDtypeStruct(xs.shape, F32), name="moe_experts",
        compiler_params=_cp(("arbitrary",)))(tile_expert, n_valid, xs, w1, b1, w2, b2)


def _comb_kernel(dest_ref, next_ref, gate_ref, x1_ref, p_ref, y_ref, lg_ref, lb_ref, wg_ref, bg_ref,
                 wp_ref, o_ref, buf, sem, *, tt, nt):
    i = pl.program_id(0)
    slot = lax.rem(i, 2)

    def gather(idx_ref, s):
        def issue(r, carry):
            for k in range(TOP_K):
                _tile_copy(y_ref, idx_ref[r * TOP_K + k], buf.at[s, k], r, sem.at[s]).start(priority=k % 2)
            return carry

        lax.fori_loop(0, tt, issue, 0, unroll=ROW_UNROLL)

    @pl.when(i == 0)
    def _():
        gather(dest_ref, slot)

    @pl.when(i + 1 < nt)
    def _():
        gather(next_ref, 1 - slot)

    _drain_rows(y_ref, buf.at[slot, 0], sem.at[slot], tt)
    gate = gate_ref[...]
    moe = gate[:, 0:1] * _tiles_to_rows(buf.at[slot, 0], tt)
    for k in range(1, TOP_K):
        moe = moe + gate[:, k:k + 1] * _tiles_to_rows(buf.at[slot, k], tt)
    x2 = _ln(ALPHA * x1_ref[...] + moe, lg_ref[...], lb_ref[...])
    gt = _sigmoid(_bdot(x2, wg_ref[...]) + bg_ref[...])
    o_ref[...] = x2 + gt * _bdot(p_ref[...], wp_ref[...])


def _combine(dest, gate, x1, p, y, lnp, wg, bg, wp, *, tt):
    n = x1.shape[0]
    smem_tok = pl.BlockSpec((tt * TOP_K,), lambda i: (i,), memory_space=pltpu.SMEM)
    tok = lambda w: pl.BlockSpec((tt, w), lambda i: (i, 0))
    const = lambda a: pl.BlockSpec(a.shape, lambda i, _n=a.ndim: (0,) * _n)
    nt = n // tt
    smem_next = pl.BlockSpec((tt * TOP_K,), lambda i: (jnp.minimum(i + 1, nt - 1),), memory_space=pltpu.SMEM)
    return pl.pallas_call(
        functools.partial(_comb_kernel, tt=tt, nt=nt), grid=(nt,),
        in_specs=[smem_tok, smem_next, tok(TOP_K), tok(D), tok(p.shape[1]), pl.BlockSpec(memory_space=pl.ANY),
                  const(lnp[0]), const(lnp[1]), const(wg), const(bg), const(wp)],
        out_specs=tok(D), out_shape=jax.ShapeDtypeStruct((n, D), F32),
        scratch_shapes=[pltpu.VMEM((2, TOP_K, tt * SUB, LANES), F32), pltpu.SemaphoreType.DMA((2,))],
        name="moe_combine_ple",
        compiler_params=_cp(("arbitrary",)))(dest, dest, gate, x1, p, y, lnp[0], lnp[1], wg, bg, wp)


def _moe_ple(x1, p, xs_buf, wr_t, br_col, w1, b1, w2, b2, layer, lnp, wg, bg, wp, *, tt):
    n = x1.shape[0]
    idx, gate, rank, counts = _route(x1, wr_t, br_col, tt=512 if n % 512 == 0 else tt)
    counts = counts[:, 0].astype(I32)
    padded = (counts + EXP_TILE - 1) // EXP_TILE * EXP_TILE
    pad_end = jnp.cumsum(padded)
    pad_start = pad_end - padded
    n_tiles = xs_buf.shape[0] // (EXP_TILE * SUB)
    n_valid = (pad_end[-1] // EXP_TILE).astype(I32)
    tiles = jnp.minimum(jnp.arange(n_tiles, dtype=I32), n_valid - 1) * EXP_TILE
    tile_expert = jnp.minimum(jnp.sum((tiles[:, None] >= pad_end[None, :]).astype(I32), axis=1), N_EXP - 1)
    experts = jnp.arange(N_EXP, dtype=I32)
    start_of = jnp.sum(jnp.where(idx[:TOP_K, :, None] == experts, pad_start, 0), axis=-1)
    dest = (start_of + rank[:TOP_K]).T.reshape(-1)
    xs = _dispatch(dest, x1, xs_buf, tt=tt)
    y = _experts(tile_expert, n_valid.reshape(1), xs, w1, b1, w2, b2, layer)
    return _combine(dest, gate[:TOP_K].T, x1, p, y, lnp, wg, bg, wp, tt=tt), xs


def _state_to_scan(s, nb):
    s = s.reshape(nb // GROUP_ROWS, GROUP_ROWS, HEAD_PAIRS, 2, HD, HD).transpose(5, 4, 0, 3, 2, 1)
    return s.reshape(HD, HD, nb * HEADS)


def _state_from_scan(s, nb):
    s = s.reshape(HD, HD, nb // GROUP_ROWS, 2, HEAD_PAIRS, GROUP_ROWS).transpose(2, 5, 4, 3, 1, 0)
    return s.reshape(nb, HEADS, HD, HD)


def _head_vec_to_scan(a):
    a = a.reshape(HEAD_PAIRS, 2, HD).transpose(2, 1, 0)
    return jnp.broadcast_to(a[..., None], (HD, 2, HEAD_PAIRS, GROUP_ROWS)).reshape(HD, LANES)


def _row2(a):
    return a.reshape(1, -1)


def _tile(n):
    for tt in (256, 128, 64, 32, 16, 8):
        if n % tt == 0:
            return tt
    raise ValueError(n)


def kernel(x_prompt, x_sample, p_prompt, p_sample, cache_conv, state_rwkv_shift, state_rwkv_wkv, cache_swa_k, cache_swa_v, conv_w_in, conv_b_in, conv_w_dw, conv_b_dw, conv_ln_g, conv_ln_b, conv_w_out, conv_b_out, rwkv_mu, rwkv_w_rkv, rwkv_w0, rwkv_w1, rwkv_w2, rwkv_a0, rwkv_a1, rwkv_a2, rwkv_g1, rwkv_g2, rwkv_k_k, rwkv_k_a, rwkv_r_k, rwkv_lnx_g, rwkv_lnx_b, rwkv_w_o, attn_w_qkv, attn_b_qkv, attn_sinks, attn_w_o, attn_b_o, rel_bias, ln_g, ln_b, moe_w_router, moe_b_router, moe_w1, moe_b1, moe_w2, moe_b2, ple_w_proj, ple_w_gate, ple_b_gate):
    bp, tp, _ = x_prompt.shape
    bs, ts, _ = x_sample.shape
    n_p, n_s = bp * tp, bs * ts
    n = n_p + n_s
    assert tp % 128 == 0 and n_p % ts == 0 and ts % SUB == 0 and HALO <= ts <= CHUNK
    assert bp % GROUP_ROWS == 0 and bs % GROUP_ROWS == 0
    tt_tok = _tile(n)
    tt_dense = 512 if n % 512 == 0 else tt_tok
    tt_p = 256 if tp % 256 == 0 else 128
    tt_conv = 512 if tp % 512 == 0 else tt_p
    x = None
    p_all = jnp.concatenate([p_prompt.reshape(DEPTH, n_p, -1), p_sample.reshape(DEPTH, n_s, -1)], axis=1)
    n_rows = (-(-n * TOP_K // EXP_TILE) + N_EXP) * EXP_TILE
    xs_buf = jnp.zeros((n_rows * SUB, LANES), F32)
    head_sel = (jnp.arange(D)[:, None] // HD == jnp.arange(LANES)[None, :]).astype(BF16)
    head_sel_t = head_sel.T
    conv_p, conv_s, shift_p, shift_s, wkv_p, wkv_s = [], [], [], [], [], []
    swa_kp, swa_vp, swa_ks, swa_vs = [], [], [], []
    for i in range(DEPTH):
        kind, j = i % 3, i // 3
        lnp = (_row2(ln_g[i, 0]), _row2(ln_b[i, 0]))
        if kind == 0:
            cw = (conv_w_in[j].astype(BF16), _row2(conv_b_in[j]), conv_w_dw[j].astype(BF16).astype(F32),
                  _row2(conv_b_dw[j]),
                  _row2(conv_ln_g[j]), _row2(conv_ln_b[j]), conv_w_out[j].astype(BF16), _row2(conv_b_out[j]))
            st_p = jnp.zeros((bp, HALO, D), F32)
            st_s = jnp.pad(cache_conv[j], ((0, 0), (HALO - (CONV_W - 1), 0), (0, 0)))
            xin_p, xin_s = (x_prompt.reshape(n_p, D), x_sample.reshape(n_s, D)) if i == 0 else (x, x)
            x1, so_p = _conv_mixer(xin_p, st_p, cw, lnp, nb=bp, t=tp, tt=tt_conv, row_off=0, n_total=n, prev=None)
            x1, so_s = _conv_mixer(xin_s, st_s, cw, lnp, nb=bs, t=ts, tt=ts, row_off=n_p, n_total=n, prev=[x1])
            conv_p.append(so_p[:, HALO - (CONV_W - 1):])
            conv_s.append(so_s[:, HALO - (CONV_W - 1):])
        elif kind == 1:
            rw = (rwkv_mu[j], rwkv_w_rkv[j, 0].astype(BF16), rwkv_w_rkv[j, 1].astype(BF16),
                  rwkv_w_rkv[j, 2].astype(BF16), _row2(rwkv_w0[j]), rwkv_w1[j].astype(BF16),
                  rwkv_w2[j].astype(BF16), _row2(rwkv_a0[j]), rwkv_a1[j].astype(BF16), rwkv_a2[j].astype(BF16),
                  rwkv_g1[j].astype(BF16), rwkv_g2[j].astype(BF16), _row2(rwkv_k_k[j]), _row2(rwkv_k_a[j]))
            sh_p = jnp.zeros((bp, 1, D), F32)
            sh_s = state_rwkv_shift[j].reshape(bs, 1, D)
            post_c = [_head_vec_to_scan(a) for a in (rwkv_lnx_g[j], rwkv_lnx_b[j], rwkv_r_k[j].reshape(-1))]
            out_c = [('const', rwkv_w_o[j].astype(BF16)), ('const', lnp[0]), ('const', lnp[1])]
            x1, states = None, []
            for (lo, nb_, t_, tt_, sh, s0, tc) in ((0, bp, tp, tt_p, sh_p, None, SCAN_CHUNK),
                                                   (n_p, bs, ts, ts, sh_s, state_rwkv_wkv[j], ts)):
                r, w, k, v, an, b, g = _rw1(x, sh, rw, head_sel, head_sel_t, nb=nb_, t=t_, tt=tt_, row_off=lo)
                rs, ws, ks, vs, ans, bs_ = _to_scan_call([a.reshape(nb_, t_, D) for a in (r, w, k, v, an, b)],
                                                         tc=min(tc, RELAYOUT_CHUNK))
                s0l = jnp.zeros((HD, HD, nb_ * HEADS), F32) if s0 is None else _state_to_scan(s0, nb_)
                o_l, s_l = _scan(rs, ws, ks, vs, ans, bs_, s0l, tc=tc)
                y = _post(o_l, rs, ks, vs, *post_c, tc=tc).reshape(t_, nb_ * D)
                x1, = _seq_call(_rw3_kernel, name=f"rwkv_out_t{t_}", nb=nb_, nt=t_ // tt_, tt=tt_, row_off=lo,
                                n_total=n, ins=[('tm', y), ('own', g), ('tok', x)] + out_c,
                                outs=[('tok', (D,), F32)], scratch=[], prev=None if x1 is None else [x1])
                states.append(_state_from_scan(s_l, nb_))
            shift_p.append(x[tp - 1:n_p:tp])
            shift_s.append(x[n_p + ts - 1::ts])
            wkv_p.append(states[0])
            wkv_s.append(states[1])
        else:
            q, kx, vx = _tok_call(_qkv_kernel, "attn_qkv", n, tt_dense, [x],
                                  [attn_w_qkv[j].astype(BF16), _row2(attn_b_qkv[j])],
                                  [('heads', BF16), (KVH * HD, F32), (KVH * HD, F32)])
            k_p = kx[:n_p].reshape(bp, tp, KVH * HD)
            v_p = vx[:n_p].reshape(bp, tp, KVH * HD)
            zpad = jnp.zeros((bp, WINDOW, KVH * HD), F32)
            nc = tp // CHUNK
            band = WINDOW + CHUNK
            o = _attn(q, jnp.concatenate([zpad, k_p], axis=1), jnp.concatenate([zpad, v_p], axis=1),
                      _t5_bias(rel_bias, CHUNK, band), attn_sinks[j], nb=bp, nt=nc, nq=CHUNK, nsub=1, kb=band,
                      mask_lo=WINDOW, row_off=0, prev=None)
            k_all = jnp.concatenate([cache_swa_k[j].reshape(bs, WINDOW, KVH * HD),
                                     kx[n_p:].reshape(bs, ts, KVH * HD)], axis=1)
            v_all = jnp.concatenate([cache_swa_v[j].reshape(bs, WINDOW, KVH * HD),
                                     vx[n_p:].reshape(bs, ts, KVH * HD)], axis=1)
            o = _attn(q, k_all, v_all, _t5_bias(rel_bias, ts, WINDOW + ts), attn_sinks[j],
                      nb=bs, nt=1, nq=ts, nsub=1, kb=WINDOW + ts, mask_lo=0, row_off=n_p, prev=o)
            x1, = _tok_call(_oproj_kernel, "attn_out", n, tt_dense, [o, x],
                            [attn_w_o[j].astype(BF16), _row2(attn_b_o[j]), lnp[0], lnp[1]], [(D, F32)])
            swa_kp.append(k_p[:, -WINDOW:].reshape(bp, WINDOW, KVH, HD))
            swa_vp.append(v_p[:, -WINDOW:].reshape(bp, WINDOW, KVH, HD))
            swa_ks.append(k_all[:, -WINDOW:].reshape(bs, WINDOW, KVH, HD))
            swa_vs.append(v_all[:, -WINDOW:].reshape(bs, WINDOW, KVH, HD))
        x, xs_buf = _moe_ple(x1, p_all[i], xs_buf, moe_w_router[i].T, moe_b_router[i].reshape(N_EXP, 1), moe_w1,
                             moe_b1.reshape(DEPTH, N_EXP, 1, 2 * D), moe_w2, moe_b2.reshape(DEPTH, N_EXP, 1, D), i,
                             (_row2(ln_g[i, 1]), _row2(ln_b[i, 1])),
                             ple_w_gate[i].astype(BF16), _row2(ple_b_gate[i]), ple_w_proj[i].astype(BF16), tt=tt_tok)
    return (x[:n_p].reshape(bp, tp, D), x[n_p:].reshape(bs, ts, D), jnp.stack(conv_p), jnp.stack(conv_s),
            jnp.stack(shift_p), jnp.stack(shift_s), jnp.stack(wkv_p), jnp.stack(wkv_s), jnp.stack(swa_kp),
            jnp.stack(swa_vp), jnp.stack(swa_ks), jnp.stack(swa_vs))
```

```python
import functools
import math

import jax
import jax.numpy as jnp
from jax import lax
from jax.experimental import pallas as pl
from jax.experimental.pallas import tpu as pltpu

F32 = jnp.float32
BF16 = jnp.bfloat16
I32 = jnp.int32

D = 1024
DEPTH = 4
CONV_W = 31
HALO = 32
HEADS = 16
HD = 64
KVH = 2
GROUP = HEADS // KVH
WINDOW = 128
CHUNK = 64
N_BUCKETS = 32
MAX_DISTANCE = 128
N_EXP = 32
TOP_K = 4
EXP_TILE = 512
CAST_ROWS = 256
LANES = 128
SUB = 8
HEAD_PAIRS = HEADS // 2
GROUP_ROWS = LANES // HEADS
SCAN_CHUNK = 64
RELAYOUT_CHUNK = 32
LNX_EPS = 64e-5
LN_EPS = 1e-5
ALPHA = (2 * DEPTH) ** 0.25
SWIGLU_ALPHA = 1.702
SWIGLU_LIMIT = 7.0
VMEM_LIMIT = 56 * 1024 * 1024


def _cp(sem):
    return pltpu.CompilerParams(dimension_semantics=sem, vmem_limit_bytes=VMEM_LIMIT)


def _bdot(a, b):
    return jnp.dot(a.astype(BF16), b.astype(BF16), preferred_element_type=F32)


def _split(a):
    hi = a.astype(BF16)
    lo = (a - hi.astype(F32)).astype(BF16)
    return hi, lo


def _split_dot(a, b_exact):
    hi, lo = _split(a)
    return (jnp.dot(hi, b_exact, preferred_element_type=F32)
            + jnp.dot(lo, b_exact, preferred_element_type=F32))


def _ln(x, g, b, eps=LN_EPS):
    mu = jnp.mean(x, axis=-1, keepdims=True)
    xc = x - mu
    var = jnp.mean(xc * xc, axis=-1, keepdims=True)
    return xc * lax.rsqrt(var + eps) * g + b


def _sigmoid(x):
    return 1.0 / (1.0 + jnp.exp(-x))


def _seq_call(body, *, name, nb, nt, tt, row_off, n_total, ins, outs, scratch, prev=None):
    off = row_off // tt
    in_specs, args = [], []
    for kind, a in ins:
        if kind == 'tok':
            in_specs.append(pl.BlockSpec((tt, a.shape[1]), lambda b, j: (off + b * nt + j, 0)))
        elif kind == 'own':
            in_specs.append(pl.BlockSpec((tt, a.shape[1]), lambda b, j: (b * nt + j, 0)))
        elif kind == 'tm':
            in_specs.append(pl.BlockSpec((tt, a.shape[1] // nb), lambda b, j: (j, b)))
        elif kind == 'bat':
            in_specs.append(pl.BlockSpec((1,) + a.shape[1:], lambda b, j: (b, 0, 0)))
        else:
            in_specs.append(pl.BlockSpec(a.shape, lambda b, j, _n=a.ndim: (0,) * _n))
        args.append(a)
    out_specs, out_shapes = [], []
    for kind, tail, dt in outs:
        if kind == 'tok':
            out_specs.append(pl.BlockSpec((tt, tail[0]), lambda b, j: (off + b * nt + j, 0)))
            out_shapes.append(jax.ShapeDtypeStruct((n_total, tail[0]), dt))
        elif kind == 'own':
            out_specs.append(pl.BlockSpec((tt, tail[0]), lambda b, j: (b * nt + j, 0)))
            out_shapes.append(jax.ShapeDtypeStruct((nb * nt * tt, tail[0]), dt))
        elif kind == 'tm':
            out_specs.append(pl.BlockSpec((tt, tail[0]), lambda b, j: (j, b)))
            out_shapes.append(jax.ShapeDtypeStruct((nt * tt, nb * tail[0]), dt))
        else:
            out_specs.append(pl.BlockSpec((1,) + tuple(tail), lambda b, j: (b, 0, 0)))
            out_shapes.append(jax.ShapeDtypeStruct((nb,) + tuple(tail), dt))
    aliases = {}
    n_prev = 0
    tok_out = [i for i, o in enumerate(outs) if o[0] == 'tok']
    if prev is None:
        prev = [jnp.zeros((n_total, outs[i][1][0]), outs[i][2]) for i in tok_out]
    for p, oi in zip(prev, tok_out):
        aliases[len(args)] = oi
        in_specs.append(pl.BlockSpec(memory_space=pl.ANY))
        args.append(p)
        n_prev += 1
    n_in = len(ins)

    def wrapped(*refs):
        body(*refs[:n_in], *refs[n_in + n_prev:])

    return pl.pallas_call(
        wrapped, grid=(nb, nt), in_specs=in_specs, out_specs=out_specs, out_shape=out_shapes,
        scratch_shapes=scratch, input_output_aliases=aliases, name=name,
        compiler_params=_cp(("arbitrary", "arbitrary")))(*args)


CONV_RC = 64
CONV_LC = 256


def _conv_kernel(x_ref, st_ref, win_ref, bin_ref, wdw_ref, bdw_ref, cg_ref, cb_ref, wout_ref,
                 bout_ref, lg_ref, lb_ref, o_ref, so_ref, ubuf, ybuf, *, tt, nt):
    j = pl.program_id(1)

    @pl.when(j == 0)
    def _():
        ubuf[0:HALO, :] = st_ref[0].astype(BF16).astype(F32)

    @pl.when(j > 0)
    def _():
        ubuf[0:HALO, :] = ubuf[tt:tt + HALO, :]

    x = x_ref[...]
    h = _bdot(x, win_ref[...]) + bin_ref[...]
    u = h[:, :D] * _sigmoid(h[:, D:])
    ubuf[HALO:HALO + tt, :] = u.astype(BF16).astype(F32)

    @pl.when(j == nt - 1)
    def _():
        so_ref[0] = u[tt - HALO:tt, :]
    first = HALO - (CONV_W - 1)
    rc = min(CONV_RC, tt)
    for r0 in range(0, tt, rc):
        for c0 in range(0, D, CONV_LC):
            y = jnp.zeros((rc, CONV_LC), F32) + bdw_ref[:, c0:c0 + CONV_LC]
            for off in range(SUB):
                rows = rc + (SUB if off else 0)
                acc = None
                for tap in range(CONV_W):
                    if (first + tap) % SUB != off:
                        continue
                    base = r0 + (first + tap) // SUB * SUB
                    term = wdw_ref[tap:tap + 1, c0:c0 + CONV_LC] * ubuf[base:base + rows, c0:c0 + CONV_LC]
                    acc = term if acc is None else acc + term
                y = y + acc[off:off + rc]
            ybuf[r0:r0 + rc, c0:c0 + CONV_LC] = y
    z = _ln(ybuf[...], cg_ref[...], cb_ref[...])
    z = z * _sigmoid(z)
    mix = _bdot(z, wout_ref[...]) + bout_ref[...]
    o_ref[...] = _ln(ALPHA * x + mix, lg_ref[...], lb_ref[...])


def _conv_mixer(x, state, w, lnp, *, nb, t, tt, row_off, n_total, prev):
    nt = t // tt
    win, bin_, wdw, bdw, cg, cb, wout, bout = w
    ins = [('tok' if x.shape[0] == n_total else 'own', x), ('bat', state), ('const', win), ('const', bin_),
           ('const', wdw), ('const', bdw),
           ('const', cg), ('const', cb), ('const', wout), ('const', bout), ('const', lnp[0]), ('const', lnp[1])]
    outs = [('tok', (D,), F32), ('bat', (HALO, D), F32)]
    scratch = [pltpu.VMEM((HALO + tt, D), F32), pltpu.VMEM((tt, D), F32)]
    return _seq_call(functools.partial(_conv_kernel, tt=tt, nt=nt), name=f"conv_t{t}", nb=nb, nt=nt, tt=tt,
                     row_off=row_off,
                     n_total=n_total, ins=ins, outs=outs, scratch=scratch, prev=prev)


def _head_sum(y, hs_ref, hst_ref):
    s = _split_dot(y, hs_ref[...])
    return _split_dot(s, hst_ref[...])


def _rw1_kernel(x_ref, sh_ref, mu_ref, wr_ref, wk_ref, wv_ref, w0_ref, w1_ref, w2_ref, a0_ref, a1_ref,
                a2_ref, g1_ref, g2_ref, kk_ref, ka_ref, hs_ref, hst_ref,
                r_ref, w_ref, k_ref, v_ref, an_ref, b_ref, g_ref, xbuf, *, tt):
    j = pl.program_id(1)

    @pl.when(j == 0)
    def _():
        xbuf[SUB - 1:SUB, :] = sh_ref[0]

    @pl.when(j > 0)
    def _():
        xbuf[SUB - 1:SUB, :] = xbuf[SUB - 1 + tt:SUB + tt, :]

    x = x_ref[...]
    xbuf[SUB:SUB + tt, :] = x
    xx = xbuf[SUB - 1:SUB - 1 + tt, :] - x
    mu = mu_ref[...]
    xr = x + xx * mu[0:1]
    xw = x + xx * mu[1:2]
    xk = x + xx * mu[2:3]
    xv = x + xx * mu[3:4]
    xa = x + xx * mu[4:5]
    xg = x + xx * mu[5:6]
    r = _bdot(xr, wr_ref[...])
    k = _bdot(xk, wk_ref[...])
    v = _bdot(xv, wv_ref[...])
    lw = w0_ref[...] + _bdot(jnp.tanh(_bdot(xw, w1_ref[...])), w2_ref[...])
    z = -lw
    log_w = -(jnp.maximum(z, 0.0) + jnp.log(1.0 + jnp.exp(-jnp.abs(z)))) - 0.5
    a = _sigmoid(a0_ref[...] + _bdot(_bdot(xa, a1_ref[...]), a2_ref[...]))
    g = _bdot(_sigmoid(_bdot(xg, g1_ref[...])), g2_ref[...])
    kk = k * kk_ref[...]
    ss = _head_sum(kk * kk, hs_ref, hst_ref)
    kk = kk * lax.rsqrt(jnp.maximum(ss, 1e-24))
    r_ref[...] = r
    w_ref[...] = jnp.exp(-jnp.exp(log_w))
    k_ref[...] = k * (1.0 + (a - 1.0) * ka_ref[...])
    v_ref[...] = v
    an_ref[...] = -kk
    b_ref[...] = kk * a
    g_ref[...] = g


def _rw1(x, shift, w, hs, hst, *, nb, t, tt, row_off):
    nt = t // tt
    prev = []
    ins = [('tok', x), ('bat', shift)] + [('const', a) for a in w] + [('const', hs), ('const', hst)]
    outs = [('own', (D,), F32)] * (N_SCAN_IN + 1)
    scratch = [pltpu.VMEM((SUB + tt, D), F32)]
    return _seq_call(functools.partial(_rw1_kernel, tt=tt), name=f"rwkv_proj_t{t}", nb=nb, nt=nt, tt=tt,
                     row_off=row_off,
                     n_total=x.shape[0], ins=ins, outs=outs, scratch=scratch, prev=prev)


def _scan_kernel(r_ref, w_ref, k_ref, v_ref, an_ref, b_ref, nxt_ref, s0_ref, o_ref, st_ref, S, sa_buf,
                 *, tc, nc):
    c = pl.program_id(1)

    @pl.when(c == 0)
    def _():
        S[...] = s0_ref[...]

        def init(kk, acc):
            return acc + S[kk] * an_ref[0, pl.ds(kk, 1), :]

        sa_buf[...] = lax.fori_loop(0, HD, init, jnp.zeros((HD, LANES), F32))

    def step(t, sa, a_next):
        vt = v_ref[t]

        def kbody(kk, acc):
            o_acc, sa_acc = acc
            sk = (S[kk] * w_ref[t, pl.ds(kk, 1), :] + sa * b_ref[t, pl.ds(kk, 1), :]
                  + vt * k_ref[t, pl.ds(kk, 1), :])
            S[kk] = sk
            return (o_acc + sk * r_ref[t, pl.ds(kk, 1), :], sa_acc + sk * a_next(kk))

        zero = jnp.zeros((HD, LANES), F32)
        o_acc, sa_next = lax.fori_loop(0, HD, kbody, (zero, zero), unroll=32)
        o_ref[t] = o_acc
        return sa_next

    sa = lax.fori_loop(0, tc - 1, lambda t, sa: step(t, sa, lambda kk: an_ref[t + 1, pl.ds(kk, 1), :]),
                       sa_buf[...])
    sa_buf[...] = step(tc - 1, sa, lambda kk: nxt_ref[0, pl.ds(kk, 1), :])

    @pl.when(c == nc - 1)
    def _():
        st_ref[...] = S[...]


def _scan(r, w, k, v, an, b, s0, *, tc):
    t, _, lanes = r.shape
    ng, nc = lanes // LANES, t // tc
    seq = pl.BlockSpec((tc, HD, LANES), lambda g, c: (c, 0, g))
    nxt = pl.BlockSpec((1, HD, LANES), lambda g, c: (jnp.minimum((c + 1) * tc, t - 1), 0, g))
    return pl.pallas_call(
        functools.partial(_scan_kernel, tc=tc, nc=nc), grid=(ng, nc),
        in_specs=[seq] * 6 + [nxt, pl.BlockSpec((HD, HD, LANES), lambda g, c: (0, 0, g))],
        out_specs=[seq, pl.BlockSpec((HD, HD, LANES), lambda g, c: (0, 0, g))],
        out_shape=[jax.ShapeDtypeStruct((t, HD, lanes), F32), jax.ShapeDtypeStruct((HD, HD, lanes), F32)],
        scratch_shapes=[pltpu.VMEM((HD, HD, LANES), F32), pltpu.VMEM((HD, LANES), F32)],
        name=f"wkv_scan_t{t}", compiler_params=_cp(("arbitrary", "arbitrary")))(r, w, k, v, an, b, an, s0)


N_SCAN_IN = 6


def _to_scan_kernel(*refs, tc):
    low = lax.broadcasted_iota(I32, (HD, LANES), 1) < HD

    def pair(i, carry):
        t = i * 2
        for src, dst in zip(refs[:N_SCAN_IN], refs[N_SCAN_IN:]):
            y0, y1 = src[:, t, :], src[:, t + 1, :]
            m = jnp.concatenate([y[:, LANES * p:LANES * (p + 1)] for y in (y0, y1) for p in range(HEAD_PAIRS)],
                                axis=0).T
            top, bot = m[:HD], m[HD:]
            dst[t] = jnp.where(low, top, pltpu.roll(bot, HD, 1))
            dst[t + 1] = jnp.where(low, pltpu.roll(top, HD, 1), bot)
        return carry

    lax.fori_loop(0, tc // 2, pair, 0, unroll=2)


def _to_scan_call(arrs, *, tc):
    nb, t, _ = arrs[0].shape
    ng, nc = nb // GROUP_ROWS, t // tc
    return pl.pallas_call(
        functools.partial(_to_scan_kernel, tc=tc), grid=(ng, nc),
        in_specs=[pl.BlockSpec((GROUP_ROWS, tc, D), lambda g, c: (g, c, 0))] * N_SCAN_IN,
        out_specs=[pl.BlockSpec((tc, HD, LANES), lambda g, c: (c, 0, g))] * N_SCAN_IN,
        out_shape=[jax.ShapeDtypeStruct((t, HD, ng * LANES), F32)] * N_SCAN_IN, name=f"wkv_relayout_t{t}",
        compiler_params=_cp(("arbitrary", "arbitrary")))(*arrs)


def _post_kernel(o_ref, r_ref, k_ref, v_ref, xg_ref, xb_ref, rk_ref, y_ref, *, tc):
    low = lax.broadcasted_iota(I32, (HD, LANES), 1) < HD

    def norm(t):
        o = o_ref[t]
        d = o - jnp.mean(o, axis=0, keepdims=True)
        vo = jnp.mean(d * d, axis=0, keepdims=True)
        bonus = jnp.sum(r_ref[t] * k_ref[t] * rk_ref[...], axis=0, keepdims=True) * v_ref[t]
        return d * lax.rsqrt(vo + LNX_EPS) * xg_ref[...] + xb_ref[...] + bonus

    def pair(i, carry):
        t = i * 2
        z0, z1 = norm(t), norm(t + 1)
        top = jnp.where(low, z0, pltpu.roll(z1, HD, 1))
        bot = jnp.where(low, pltpu.roll(z0, HD, 1), z1)
        m = jnp.concatenate([top, bot], axis=0).T
        rows = lambda base, p: m[base + GROUP_ROWS * p:base + GROUP_ROWS * (p + 1)]
        y_ref[t] = jnp.concatenate([rows(0, p) for p in range(HEAD_PAIRS)], axis=1)
        y_ref[t + 1] = jnp.concatenate([rows(HD, p) for p in range(HEAD_PAIRS)], axis=1)
        return carry

    lax.fori_loop(0, tc // 2, pair, 0, unroll=4)


def _post(o, r, k, v, xg, xb, rk, *, tc):
    t, _, lanes = o.shape
    ng, nc = lanes // LANES, t // tc
    seq = pl.BlockSpec((tc, HD, LANES), lambda g, c: (c, 0, g))
    const = pl.BlockSpec((HD, LANES), lambda g, c: (0, 0))
    return pl.pallas_call(
        functools.partial(_post_kernel, tc=tc), grid=(ng, nc),
        in_specs=[seq] * 4 + [const] * 3,
        out_specs=pl.BlockSpec((tc, GROUP_ROWS, D), lambda g, c: (c, g, 0)),
        out_shape=jax.ShapeDtypeStruct((t, ng * GROUP_ROWS, D), F32), name=f"wkv_post_t{t}",
        compiler_params=_cp(("arbitrary", "arbitrary")))(o, r, k, v, xg, xb, rk)


def _rw3_kernel(y_ref, g_ref, x_ref, wo_ref, lg_ref, lb_ref, out_ref):
    mix = _bdot(y_ref[...] * g_ref[...], wo_ref[...])
    out_ref[...] = _ln(ALPHA * x_ref[...] + mix, lg_ref[...], lb_ref[...])


def _head_major_spec(tt):
    return pl.BlockSpec((HEADS, tt, HD), lambda i: (0, i, 0))


def _tok_call(body, name, n, tt, toks, consts, outs):
    in_specs = [_head_major_spec(tt) if a.ndim == 3 else pl.BlockSpec((tt, a.shape[1]), lambda i: (i, 0))
                for a in toks]
    in_specs += [pl.BlockSpec(a.shape, lambda i, _n=a.ndim: (0,) * _n) for a in consts]
    out_specs = [_head_major_spec(tt) if w == 'heads' else pl.BlockSpec((tt, w), lambda i: (i, 0))
                 for w, _ in outs]
    out_shape = [jax.ShapeDtypeStruct((HEADS, n, HD) if w == 'heads' else (n, w), dt) for w, dt in outs]
    return pl.pallas_call(
        body, grid=(n // tt,), in_specs=in_specs, out_specs=out_specs, out_shape=out_shape, name=name,
        compiler_params=_cp(("arbitrary",)))(*toks, *consts)


def _qkv_kernel(x_ref, w_ref, b_ref, q_ref, k_ref, v_ref):
    h = _bdot(x_ref[...], w_ref[...]) + b_ref[...]
    for hd in range(HEADS):
        q_ref[hd] = h[:, hd * HD:(hd + 1) * HD].astype(BF16)
    k_ref[...] = h[:, HEADS * HD:HEADS * HD + KVH * HD]
    v_ref[...] = h[:, HEADS * HD + KVH * HD:]


def _attn_kernel(q_ref, kp_ref, vp_ref, bias_ref, sink_ref, prev_ref, o_ref, *, nq, nsub, kb, mask_lo):
    del prev_ref
    c = pl.program_id(1)
    start = pl.multiple_of(c * (nsub * nq), SUB)
    span = kb + (nsub - 1) * nq
    kall = kp_ref[0, pl.ds(start, span), :].astype(BF16)
    vall = vp_ref[0, pl.ds(start, span), :].astype(BF16)
    for s in range(nsub):
        rows = slice(s * nq, (s + 1) * nq)
        valid = (start + s * nq + lax.broadcasted_iota(I32, (1, kb), 1)) >= mask_lo
        for g in range(KVH):
            hs = slice(g * GROUP, (g + 1) * GROUP)
            qg = q_ref[hs, rows, :].reshape(GROUP * nq, HD)
            kh = kall[s * nq:s * nq + kb, g * HD:(g + 1) * HD]
            vh = vall[s * nq:s * nq + kb, g * HD:(g + 1) * HD]
            logits = lax.dot_general(qg, kh, (((1,), (1,)), ((), ())), preferred_element_type=F32) * HD ** -0.5
            logits = jnp.where(valid, logits + bias_ref[hs].reshape(GROUP * nq, kb), -1e30)
            sink = sink_ref[hs].reshape(GROUP * nq, 1)
            m = jnp.maximum(jnp.max(logits, axis=-1, keepdims=True), sink)
            p = jnp.exp(logits - m)
            p = p / (jnp.sum(p, axis=-1, keepdims=True) + jnp.exp(sink - m))
            og = jnp.dot(p.astype(BF16), vh, preferred_element_type=F32)
            o_ref[hs, rows, :] = og.reshape(GROUP, nq, HD).astype(BF16)


def _attn(q, kp, vp, bias, sinks, *, nb, nt, nq, nsub, kb, mask_lo, row_off, prev):
    off = row_off // (nq * nsub)
    n_total = q.shape[1]
    heads = pl.BlockSpec((HEADS, nq * nsub, HD), lambda b, c: (0, off + b * nt + c, 0))
    const = lambda a: pl.BlockSpec(a.shape, lambda b, c, _n=a.ndim: (0,) * _n)
    bat = lambda a: pl.BlockSpec((1,) + a.shape[1:], lambda b, c: (b, 0, 0))
    sink_tab = jnp.broadcast_to(sinks.reshape(HEADS, 1, 1), (HEADS, nq, 1))
    if prev is None:
        prev = jnp.zeros((HEADS, n_total, HD), BF16)
    return pl.pallas_call(
        functools.partial(_attn_kernel, nq=nq, nsub=nsub, kb=kb, mask_lo=mask_lo), grid=(nb, nt),
        in_specs=[heads, bat(kp), bat(vp), const(bias), const(sink_tab), pl.BlockSpec(memory_space=pl.ANY)],
        out_specs=heads, out_shape=jax.ShapeDtypeStruct((HEADS, n_total, HD), BF16),
        input_output_aliases={5: 0}, name=f"attn_q{nq}",
        compiler_params=_cp(("arbitrary", "arbitrary")))(q, kp, vp, bias, sink_tab, prev)


def _oproj_kernel(o_ref, x_ref, wo_ref, bo_ref, lg_ref, lb_ref, out_ref):
    acc = jnp.dot(o_ref[0], wo_ref[0:HD, :], preferred_element_type=F32)
    for hd in range(1, HEADS):
        acc = acc + jnp.dot(o_ref[hd], wo_ref[hd * HD:(hd + 1) * HD, :], preferred_element_type=F32)
    out_ref[...] = _ln(ALPHA * x_ref[...] + acc + bo_ref[...], lg_ref[...], lb_ref[...])


def _t5_bucket(rel):
    half = N_BUCKETS // 2
    max_exact = half // 2
    ret = jnp.where(rel > 0, half, 0)
    n = jnp.abs(rel)
    nf = jnp.maximum(n, 1).astype(F32)
    large = max_exact + (jnp.log(nf / max_exact) / math.log(MAX_DISTANCE / max_exact)
                         * (half - max_exact)).astype(I32)
    large = jnp.minimum(large, half - 1)
    return ret + jnp.where(n < max_exact, n, large)


def _t5_bias(rel_bias, n_q, n_k):
    rel = jnp.arange(n_k)[None, :] - WINDOW - jnp.arange(n_q)[:, None]
    onehot = (_t5_bucket(rel)[..., None] == jnp.arange(N_BUCKETS)).astype(F32)
    return jnp.einsum('qkn,nh->hqk', onehot, rel_bias, precision=lax.Precision.HIGHEST)


def _route_kernel(x_ref, wr_ref, br_ref, idx_ref, gate_ref, rank_ref, cnt_ref, carry, *, tt):
    i = pl.program_id(0)

    @pl.when(i == 0)
    def _():
        carry[...] = jnp.zeros_like(carry)

    logits = lax.dot_general(wr_ref[...].astype(BF16), x_ref[...].astype(BF16), (((1,), (1,)), ((), ())),
                             preferred_element_type=F32) + br_ref[...]
    sub = lax.broadcasted_iota(I32, (N_EXP, tt), 0)
    out_row = lax.broadcasted_iota(I32, (8, tt), 0)
    vals, sels = [], []
    idx_out = jnp.zeros((8, tt), I32)
    work = logits
    for k in range(TOP_K):
        m = jnp.max(work, axis=0, keepdims=True)
        ik = jnp.min(jnp.where(work == m, sub, N_EXP), axis=0, keepdims=True)
        sel = sub == ik
        vals.append(m)
        sels.append(sel)
        idx_out = jnp.where(out_row == k, ik, idx_out)
        work = jnp.where(sel, -jnp.inf, work)
    es = [jnp.exp(v - vals[0]) for v in vals]
    den = es[0] + es[1] + es[2] + es[3]
    gate_out = jnp.zeros((8, tt), F32)
    for k in range(TOP_K):
        gate_out = jnp.where(out_row == k, es[k] / den, gate_out)
    onehot = jnp.zeros((N_EXP, tt), F32)
    for sel in sels:
        onehot = onehot + sel.astype(F32)
    before = (lax.broadcasted_iota(I32, (tt, tt), 0) < lax.broadcasted_iota(I32, (tt, tt), 1)).astype(BF16)
    base = carry[...] + jnp.dot(onehot.astype(BF16), before, preferred_element_type=F32)
    rank_out = jnp.zeros((8, tt), I32)
    for k in range(TOP_K):
        rk = jnp.sum(jnp.where(sels[k], base, 0.0), axis=0, keepdims=True)
        rank_out = jnp.where(out_row == k, rk.astype(I32), rank_out)
    carry[...] = carry[...] + jnp.sum(onehot, axis=1, keepdims=True)
    idx_ref[...] = idx_out
    gate_ref[...] = gate_out
    rank_ref[...] = rank_out
    cnt_ref[...] = carry[...]


def _route(x1, wr_t, br_col, *, tt):
    n = x1.shape[0]
    rows = pl.BlockSpec((8, tt), lambda i: (0, i))
    const = lambda a: pl.BlockSpec(a.shape, lambda i, _n=a.ndim: (0,) * _n)
    return pl.pallas_call(
        functools.partial(_route_kernel, tt=tt), grid=(n // tt,),
        in_specs=[pl.BlockSpec((tt, D), lambda i: (i, 0)), const(wr_t), const(br_col)],
        out_specs=[rows, rows, rows, pl.BlockSpec((N_EXP, 1), lambda i: (0, 0))],
        out_shape=[jax.ShapeDtypeStruct((8, n), I32), jax.ShapeDtypeStruct((8, n), F32),
                   jax.ShapeDtypeStruct((8, n), I32), jax.ShapeDtypeStruct((N_EXP, 1), F32)],
        scratch_shapes=[pltpu.VMEM((N_EXP, 1), F32)], name="moe_route",
        compiler_params=_cp(("arbitrary",)))(x1, wr_t, br_col)


assert D == SUB * LANES


def _rows_to_tiles(dst_ref, x, n):
    for s in range(SUB):
        dst_ref[pl.ds(s, n, stride=SUB), :] = x[:, LANES * s:LANES * (s + 1)]


def _tiles_to_rows(src_ref, n):
    return jnp.concatenate([src_ref[pl.ds(s, n, stride=SUB), :] for s in range(SUB)], axis=1)


def _tile_copy(src, s, dst, d, sem):
    return pltpu.make_async_copy(src.at[pl.ds(pl.multiple_of(s * SUB, SUB), SUB), :],
                                 dst.at[pl.ds(pl.multiple_of(d * SUB, SUB), SUB), :], sem)


ROW_UNROLL = 16


def _drain_rows(src, dst, sem, tt):
    def drain(r, carry):
        for k in range(TOP_K):
            _tile_copy(src, 0, dst, 0, sem).wait()
        return carry

    lax.fori_loop(0, tt, drain, 0, unroll=ROW_UNROLL)


def _disp_kernel(dest_ref, x_ref, xs_in, xs_ref, xt, sem, *, tt, nt):
    del xs_in
    i = pl.program_id(0)
    slot = lax.rem(i, 2)
    stage = xt.at[slot]
    _rows_to_tiles(stage, x_ref[...], tt)

    def issue(r, carry):
        for k in range(TOP_K):
            _tile_copy(stage, r, xs_ref, dest_ref[r * TOP_K + k], sem.at[slot]).start(priority=k % 2)
        return carry

    lax.fori_loop(0, tt, issue, 0, unroll=ROW_UNROLL)

    @pl.when(i > 0)
    def _():
        _drain_rows(xt.at[1 - slot], xs_ref, sem.at[1 - slot], tt)

    @pl.when(i == nt - 1)
    def _():
        _drain_rows(stage, xs_ref, sem.at[slot], tt)


def _dispatch(dest, x1, xs_prev, *, tt):
    n = x1.shape[0]
    smem_tok = pl.BlockSpec((tt * TOP_K,), lambda i: (i,), memory_space=pltpu.SMEM)
    return pl.pallas_call(
        functools.partial(_disp_kernel, tt=tt, nt=n // tt), grid=(n // tt,),
        in_specs=[smem_tok, pl.BlockSpec((tt, D), lambda i: (i, 0)), pl.BlockSpec(memory_space=pl.ANY)],
        out_specs=pl.BlockSpec(memory_space=pl.ANY),
        out_shape=jax.ShapeDtypeStruct(xs_prev.shape, F32),
        scratch_shapes=[pltpu.VMEM((2, tt * SUB, LANES), F32), pltpu.SemaphoreType.DMA((2,))],
        input_output_aliases={2: 0}, name="moe_dispatch",
        compiler_params=_cp(("arbitrary",)))(dest, x1, xs_prev)


def _expert_kernel(te_ref, nv_ref, xs_ref, w1_ref, b1_ref, w2_ref, b2_ref, y_ref, w1b, w2b):
    i = pl.program_id(0)
    valid = i < nv_ref[0]
    changed = jnp.logical_or(i == 0, te_ref[i] != te_ref[jnp.maximum(i - 1, 0)])

    @pl.when(jnp.logical_and(valid, changed))
    def _():
        for r0 in range(0, D, CAST_ROWS):
            w1b[r0:r0 + CAST_ROWS, :] = w1_ref[0, 0, r0:r0 + CAST_ROWS, :].astype(BF16)
            w2b[r0:r0 + CAST_ROWS, :] = w2_ref[0, 0, r0:r0 + CAST_ROWS, :].astype(BF16)

    @pl.when(valid)
    def _():
        half = EXP_TILE // 2
        for r in range(2):
            rows = pl.ds(r * half * SUB, half * SUB)
            x = _tiles_to_rows(xs_ref.at[rows], half).astype(BF16)
            h = jnp.dot(x, w1b[...], preferred_element_type=F32) + b1_ref[0, 0]
            glu = jnp.minimum(h[:, :D], SWIGLU_LIMIT)
            lin = jnp.clip(h[:, D:], -SWIGLU_LIMIT, SWIGLU_LIMIT)
            act = glu * _sigmoid(SWIGLU_ALPHA * glu) * (lin + 1.0)
            y = jnp.dot(act.astype(BF16), w2b[...], preferred_element_type=F32) + b2_ref[0, 0]
            _rows_to_tiles(y_ref.at[rows], y, half)

    @pl.when(jnp.logical_not(valid))
    def _():
        y_ref[...] = jnp.zeros_like(y_ref)


def _experts(tile_expert, n_valid, xs, w1, b1, w2, b2, layer):
    n_tiles = xs.shape[0] // (EXP_TILE * SUB)
    grid_spec = pltpu.PrefetchScalarGridSpec(
        num_scalar_prefetch=2, grid=(n_tiles,),
        in_specs=[pl.BlockSpec((EXP_TILE * SUB, LANES), lambda i, te, nv: (i, 0)),
                  pl.BlockSpec((1, 1, D, 2 * D), lambda i, te, nv: (layer, te[i], 0, 0)),
                  pl.BlockSpec((1, 1, 1, 2 * D), lambda i, te, nv: (layer, te[i], 0, 0)),
                  pl.BlockSpec((1, 1, D, D), lambda i, te, nv: (layer, te[i], 0, 0)),
                  pl.BlockSpec((1, 1, 1, D), lambda i, te, nv: (layer, te[i], 0, 0))],
        out_specs=pl.BlockSpec((EXP_TILE * SUB, LANES), lambda i, te, nv: (i, 0)),
        scratch_shapes=[pltpu.VMEM((D, 2 * D), BF16), pltpu.VMEM((D, D), BF16)])
    return pl.pallas_call(
        _expert_kernel, grid_spec=grid_spec, out_shape=jax.ShapeDtypeStruct(xs.shape, F32), name="moe_experts",
        compiler_params=_cp(("arbitrary",)))(tile_expert, n_valid, xs, w1, b1, w2, b2)


def _comb_kernel(dest_ref, next_ref, gate_ref, x1_ref, p_ref, y_ref, lg_ref, lb_ref, wg_ref, bg_ref,
                 wp_ref, o_ref, buf, sem, *, tt, nt):
    i = pl.program_id(0)
    slot = lax.rem(i, 2)

    def gather(idx_ref, s):
        def issue(r, carry):
            for k in range(TOP_K):
                _tile_copy(y_ref, idx_ref[r * TOP_K + k], buf.at[s, k], r, sem.at[s]).start(priority=k % 2)
            return carry

        lax.fori_loop(0, tt, issue, 0, unroll=ROW_UNROLL)

    @pl.when(i == 0)
    def _():
        gather(dest_ref, slot)

    @pl.when(i + 1 < nt)
    def _():
        gather(next_ref, 1 - slot)

    _drain_rows(y_ref, buf.at[slot, 0], sem.at[slot], tt)
    gate = gate_ref[...]
    moe = gate[:, 0:1] * _tiles_to_rows(buf.at[slot, 0], tt)
    for k in range(1, TOP_K):
        moe = moe + gate[:, k:k + 1] * _tiles_to_rows(buf.at[slot, k], tt)
    x2 = _ln(ALPHA * x1_ref[...] + moe, lg_ref[...], lb_ref[...])
    gt = _sigmoid(_bdot(x2, wg_ref[...]) + bg_ref[...])
    o_ref[...] = x2 + gt * _bdot(p_ref[...], wp_ref[...])


def _combine(dest, gate, x1, p, y, lnp, wg, bg, wp, *, tt):
    n = x1.shape[0]
    smem_tok = pl.BlockSpec((tt * TOP_K,), lambda i: (i,), memory_space=pltpu.SMEM)
    tok = lambda w: pl.BlockSpec((tt, w), lambda i: (i, 0))
    const = lambda a: pl.BlockSpec(a.shape, lambda i, _n=a.ndim: (0,) * _n)
    nt = n // tt
    smem_next = pl.BlockSpec((tt * TOP_K,), lambda i: (jnp.minimum(i + 1, nt - 1),), memory_space=pltpu.SMEM)
    return pl.pallas_call(
        functools.partial(_comb_kernel, tt=tt, nt=nt), grid=(nt,),
        in_specs=[smem_tok, smem_next, tok(TOP_K), tok(D), tok(p.shape[1]), pl.BlockSpec(memory_space=pl.ANY),
                  const(lnp[0]), const(lnp[1]), const(wg), const(bg), const(wp)],
        out_specs=tok(D), out_shape=jax.ShapeDtypeStruct((n, D), F32),
        scratch_shapes=[pltpu.VMEM((2, TOP_K, tt * SUB, LANES), F32), pltpu.SemaphoreType.DMA((2,))],
        name="moe_combine_ple",
        compiler_params=_cp(("arbitrary",)))(dest, dest, gate, x1, p, y, lnp[0], lnp[1], wg, bg, wp)


def _moe_ple(x1, p, xs_buf, wr_t, br_col, w1, b1, w2, b2, layer, lnp, wg, bg, wp, *, tt):
    n = x1.shape[0]
    idx, gate, rank, counts = _route(x1, wr_t, br_col, tt=512 if n % 512 == 0 else tt)
    counts = counts[:, 0].astype(I32)
    padded = (counts + EXP_TILE - 1) // EXP_TILE * EXP_TILE
    pad_end = jnp.cumsum(padded)
    pad_start = pad_end - padded
    n_tiles = xs_buf.shape[0] // (EXP_TILE * SUB)
    n_valid = (pad_end[-1] // EXP_TILE).astype(I32)
    tiles = jnp.minimum(jnp.arange(n_tiles, dtype=I32), n_valid - 1) * EXP_TILE
    tile_expert = jnp.minimum(jnp.sum((tiles[:, None] >= pad_end[None, :]).astype(I32), axis=1), N_EXP - 1)
    experts = jnp.arange(N_EXP, dtype=I32)
    start_of = jnp.sum(jnp.where(idx[:TOP_K, :, None] == experts, pad_start, 0), axis=-1)
    dest = (start_of + rank[:TOP_K]).T.reshape(-1)
    xs = _dispatch(dest, x1, xs_buf, tt=tt)
    y = _experts(tile_expert, n_valid.reshape(1), xs, w1, b1, w2, b2, layer)
    return _combine(dest, gate[:TOP_K].T, x1, p, y, lnp, wg, bg, wp, tt=tt), xs


def _state_to_scan(s, nb):
    s = s.reshape(nb // GROUP_ROWS, GROUP_ROWS, HEAD_PAIRS, 2, HD, HD).transpose(5, 4, 0, 3, 2, 1)
    return s.reshape(HD, HD, nb * HEADS)


def _state_from_scan(s, nb):
    s = s.reshape(HD, HD, nb // GROUP_ROWS, 2, HEAD_PAIRS, GROUP_ROWS).transpose(2, 5, 4, 3, 1, 0)
    return s.reshape(nb, HEADS, HD, HD)


def _head_vec_to_scan(a):
    a = a.reshape(HEAD_PAIRS, 2, HD).transpose(2, 1, 0)
    return jnp.broadcast_to(a[..., None], (HD, 2, HEAD_PAIRS, GROUP_ROWS)).reshape(HD, LANES)


def _row2(a):
    return a.reshape(1, -1)


def _tile(n):
    for tt in (256, 128, 64, 32, 16, 8):
        if n % tt == 0:
            return tt
    raise ValueError(n)


def kernel(x_prompt, x_sample, p_prompt, p_sample, cache_conv, state_rwkv_shift, state_rwkv_wkv, cache_swa_k, cache_swa_v, conv_w_in, conv_b_in, conv_w_dw, conv_b_dw, conv_ln_g, conv_ln_b, conv_w_out, conv_b_out, rwkv_mu, rwkv_w_rkv, rwkv_w0, rwkv_w1, rwkv_w2, rwkv_a0, rwkv_a1, rwkv_a2, rwkv_g1, rwkv_g2, rwkv_k_k, rwkv_k_a, rwkv_r_k, rwkv_lnx_g, rwkv_lnx_b, rwkv_w_o, attn_w_qkv, attn_b_qkv, attn_sinks, attn_w_o, attn_b_o, rel_bias, ln_g, ln_b, moe_w_router, moe_b_router, moe_w1, moe_b1, moe_w2, moe_b2, ple_w_proj, ple_w_gate, ple_b_gate):
    bp, tp, _ = x_prompt.shape
    bs, ts, _ = x_sample.shape
    n_p, n_s = bp * tp, bs * ts
    n = n_p + n_s
    assert tp % 128 == 0 and n_p % ts == 0 and ts % SUB == 0 and HALO <= ts <= CHUNK
    assert bp % GROUP_ROWS == 0 and bs % GROUP_ROWS == 0
    tt_tok = _tile(n)
    tt_dense = 512 if n % 512 == 0 else tt_tok
    tt_p = 256 if tp % 256 == 0 else 128
    tt_conv = 512 if tp % 512 == 0 else tt_p
    x = None
    p_all = jnp.concatenate([p_prompt.reshape(DEPTH, n_p, -1), p_sample.reshape(DEPTH, n_s, -1)], axis=1)
    n_rows = (-(-n * TOP_K // EXP_TILE) + N_EXP) * EXP_TILE
    xs_buf = jnp.zeros((n_rows * SUB, LANES), F32)
    head_sel = (jnp.arange(D)[:, None] // HD == jnp.arange(LANES)[None, :]).astype(BF16)
    head_sel_t = head_sel.T
    conv_p, conv_s, shift_p, shift_s, wkv_p, wkv_s = [], [], [], [], [], []
    swa_kp, swa_vp, swa_ks, swa_vs = [], [], [], []
    for i in range(DEPTH):
        kind, j = i % 3, i // 3
        lnp = (_row2(ln_g[i, 0]), _row2(ln_b[i, 0]))
        if kind == 0:
            cw = (conv_w_in[j].astype(BF16), _row2(conv_b_in[j]), conv_w_dw[j].astype(BF16).astype(F32),
                  _row2(conv_b_dw[j]),
                  _row2(conv_ln_g[j]), _row2(conv_ln_b[j]), conv_w_out[j].astype(BF16), _row2(conv_b_out[j]))
            st_p = jnp.zeros((bp, HALO, D), F32)
            st_s = jnp.pad(cache_conv[j], ((0, 0), (HALO - (CONV_W - 1), 0), (0, 0)))
            xin_p, xin_s = (x_prompt.reshape(n_p, D), x_sample.reshape(n_s, D)) if i == 0 else (x, x)
            x1, so_p = _conv_mixer(xin_p, st_p, cw, lnp, nb=bp, t=tp, tt=tt_conv, row_off=0, n_total=n, prev=None)
            x1, so_s = _conv_mixer(xin_s, st_s, cw, lnp, nb=bs, t=ts, tt=ts, row_off=n_p, n_total=n, prev=[x1])
            conv_p.append(so_p[:, HALO - (CONV_W - 1):])
            conv_s.append(so_s[:, HALO - (CONV_W - 1):])
        elif kind == 1:
            rw = (rwkv_mu[j], rwkv_w_rkv[j, 0].astype(BF16), rwkv_w_rkv[j, 1].astype(BF16),
                  rwkv_w_rkv[j, 2].astype(BF16), _row2(rwkv_w0[j]), rwkv_w1[j].astype(BF16),
                  rwkv_w2[j].astype(BF16), _row2(rwkv_a0[j]), rwkv_a1[j].astype(BF16), rwkv_a2[j].astype(BF16),
                  rwkv_g1[j].astype(BF16), rwkv_g2[j].astype(BF16), _row2(rwkv_k_k[j]), _row2(rwkv_k_a[j]))
            sh_p = jnp.zeros((bp, 1, D), F32)
            sh_s = state_rwkv_shift[j].reshape(bs, 1, D)
            post_c = [_head_vec_to_scan(a) for a in (rwkv_lnx_g[j], rwkv_lnx_b[j], rwkv_r_k[j].reshape(-1))]
            out_c = [('const', rwkv_w_o[j].astype(BF16)), ('const', lnp[0]), ('const', lnp[1])]
            x1, states = None, []
            for (lo, nb_, t_, tt_, sh, s0, tc) in ((0, bp, tp, tt_p, sh_p, None, SCAN_CHUNK),
                                                   (n_p, bs, ts, ts, sh_s, state_rwkv_wkv[j], ts)):
                r, w, k, v, an, b, g = _rw1(x, sh, rw, head_sel, head_sel_t, nb=nb_, t=t_, tt=tt_, row_off=lo)
                rs, ws, ks, vs, ans, bs_ = _to_scan_call([a.reshape(nb_, t_, D) for a in (r, w, k, v, an, b)],
                                                         tc=min(tc, RELAYOUT_CHUNK))
                s0l = jnp.zeros((HD, HD, nb_ * HEADS), F32) if s0 is None else _state_to_scan(s0, nb_)
                o_l, s_l = _scan(rs, ws, ks, vs, ans, bs_, s0l, tc=tc)
                y = _post(o_l, rs, ks, vs, *post_c, tc=tc).reshape(t_, nb_ * D)
                x1, = _seq_call(_rw3_kernel, name=f"rwkv_out_t{t_}", nb=nb_, nt=t_ // tt_, tt=tt_, row_off=lo,
                                n_total=n, ins=[('tm', y), ('own', g), ('tok', x)] + out_c,
                                outs=[('tok', (D,), F32)], scratch=[], prev=None if x1 is None else [x1])
                states.append(_state_from_scan(s_l, nb_))
            shift_p.append(x[tp - 1:n_p:tp])
            shift_s.append(x[n_p + ts - 1::ts])
            wkv_p.append(states[0])
            wkv_s.append(states[1])
        else:
            q, kx, vx = _tok_call(_qkv_kernel, "attn_qkv", n, tt_dense, [x],
                                  [attn_w_qkv[j].astype(BF16), _row2(attn_b_qkv[j])],
                                  [('heads', BF16), (KVH * HD, F32), (KVH * HD, F32)])
            k_p = kx[:n_p].reshape(bp, tp, KVH * HD)
            v_p = vx[:n_p].reshape(bp, tp, KVH * HD)
            zpad = jnp.zeros((bp, WINDOW, KVH * HD), F32)
            nc = tp // CHUNK
            band = WINDOW + CHUNK
            o = _attn(q, jnp.concatenate([zpad, k_p], axis=1), jnp.concatenate([zpad, v_p], axis=1),
                      _t5_bias(rel_bias, CHUNK, band), attn_sinks[j], nb=bp, nt=nc, nq=CHUNK, nsub=1, kb=band,
                      mask_lo=WINDOW, row_off=0, prev=None)
            k_all = jnp.concatenate([cache_swa_k[j].reshape(bs, WINDOW, KVH * HD),
                                     kx[n_p:].reshape(bs, ts, KVH * HD)], axis=1)
            v_all = jnp.concatenate([cache_swa_v[j].reshape(bs, WINDOW, KVH * HD),
                                     vx[n_p:].reshape(bs, ts, KVH * HD)], axis=1)
            o = _attn(q, k_all, v_all, _t5_bias(rel_bias, ts, WINDOW + ts), attn_sinks[j],
                      nb=bs, nt=1, nq=ts, nsub=1, kb=WINDOW + ts, mask_lo=0, row_off=n_p, prev=o)
            x1, = _tok_call(_oproj_kernel, "attn_out", n, tt_dense, [o, x],
                            [attn_w_o[j].astype(BF16), _row2(attn_b_o[j]), lnp[0], lnp[1]], [(D, F32)])
            swa_kp.append(k_p[:, -WINDOW:].reshape(bp, WINDOW, KVH, HD))
            swa_vp.append(v_p[:, -WINDOW:].reshape(bp, WINDOW, KVH, HD))
            swa_ks.append(k_all[:, -WINDOW:].reshape(bs, WINDOW, KVH, HD))
            swa_vs.append(v_all[:, -WINDOW:].reshape(bs, WINDOW, KVH, HD))
        x, xs_buf = _moe_ple(x1, p_all[i], xs_buf, moe_w_router[i].T, moe_b_router[i].reshape(N_EXP, 1), moe_w1,
                             moe_b1.reshape(DEPTH, N_EXP, 1, 2 * D), moe_w2, moe_b2.reshape(DEPTH, N_EXP, 1, D), i,
                             (_row2(ln_g[i, 1]), _row2(ln_b[i, 1])),
                             ple_w_gate[i].astype(BF16), _row2(ple_b_gate[i]), ple_w_proj[i].astype(BF16), tt=tt_tok)
    return (x[:n_p].reshape(bp, tp, D), x[n_p:].reshape(bs, ts, D), jnp.stack(conv_p), jnp.stack(conv_s),
            jnp.stack(shift_p), jnp.stack(shift_s), jnp.stack(wkv_p), jnp.stack(wkv_s), jnp.stack(swa_kp),
            jnp.stack(swa_vp), jnp.stack(swa_ks), jnp.stack(swa_vs))
```
